```python
import math
import jax, jax.numpy as jnp
from jax import lax
import numpy as np

D_MODEL = 1024
BATCH = 8
SEQ = 4096
DEPTH = 4

GLA_HEADS = 4
GLA_DK = 32
GLA_DV = 64
GLA_GATE_RANK = 16
GLA_TAU = 16.0
GLA_CHUNK = 64
DIFF_HEADS = 4
DIFF_D = 64
DIFF_QBLOCK = 128
RET_HEADS = 4
RET_DK = 64
RET_DV = 64
RET_CHUNK = 128
T5_BUCKETS = 32
T5_MAX_DIST = 128
N_EXPERTS = 32
TOP_K = 4
D_FF = D_MODEL
SWIGLU_LIMIT = 7.0
SWIGLU_ALPHA = 1.702
MOE_BLOCK = 256
PLE_DIM = 256
DEEPNORM_ALPHA = (2 * DEPTH) ** 0.25
DEEPNORM_BETA = (8 * DEPTH) ** -0.25
LN_EPS = 1e-5
HEAD_NORM_EPS = 1e-5

MIX_WIDTH = GLA_HEADS * GLA_DV + DIFF_HEADS * 2 * DIFF_D + RET_HEADS * RET_DV
IN_SPLITS = (GLA_HEADS * GLA_DK, GLA_HEADS * GLA_DK, GLA_HEADS * GLA_DV, GLA_GATE_RANK, GLA_HEADS * GLA_DV,
             DIFF_HEADS * 2 * DIFF_D, DIFF_HEADS * 2 * DIFF_D, DIFF_HEADS * 2 * DIFF_D,
             RET_HEADS * RET_DK, RET_HEADS * RET_DK, RET_HEADS * RET_DV, RET_HEADS * RET_DV)
IN_WIDTH = sum(IN_SPLITS)
IN_OFFSETS = tuple(int(o) for o in np.cumsum(IN_SPLITS)[:-1])

kernel_name = "hymba_gla_diff_retnet_moe_deepnorm"


def layer_norm(x, g, b):
    xf = x.astype(jnp.float32)
    mu = jnp.mean(xf, axis=-1, keepdims=True)
    var = jnp.mean(jnp.square(xf - mu), axis=-1, keepdims=True)
    y = (xf - mu) * lax.rsqrt(var + LN_EPS) * g.astype(jnp.float32) + b.astype(jnp.float32)
    return y.astype(x.dtype)


def head_rms_norm(x, g=None):
    xf = x.astype(jnp.float32)
    y = xf * lax.rsqrt(jnp.mean(jnp.square(xf), axis=-1, keepdims=True) + HEAD_NORM_EPS)
    if g is not None:
        y = y * g.astype(jnp.float32)
    return y


def to_chunks(t, c):
    b, s, h, d = t.shape
    return t.reshape(b, s // c, c, h, d).transpose(1, 0, 3, 2, 4)


def from_chunks(t):
    n, b, h, c, d = t.shape
    return t.transpose(1, 0, 3, 2, 4).reshape(b, n * c, h, d)


def t5_bucket(rel):
    n = jnp.maximum(-rel, 0)
    max_exact = T5_BUCKETS // 2
    nf = jnp.maximum(n, 1).astype(jnp.float32)
    large = max_exact + (jnp.log(nf / max_exact) / math.log(T5_MAX_DIST / max_exact)
                         * (T5_BUCKETS - max_exact)).astype(jnp.int32)
    large = jnp.minimum(large, T5_BUCKETS - 1)
    return jnp.where(n < max_exact, n, large)


def gla_group(q, k, v, gate_lr, out_gate, w_gate, b_gate, norm_g):
    b_, s_ = q.shape[:2]
    log_a = jax.nn.log_sigmoid((gate_lr @ w_gate + b_gate).astype(jnp.float32)) / GLA_TAU
    log_a = log_a.reshape(b_, s_, GLA_HEADS, GLA_DK)
    qc = to_chunks(q.astype(jnp.float32) * GLA_DK ** -0.5, GLA_CHUNK)
    kc = to_chunks(k.astype(jnp.float32), GLA_CHUNK)
    vc = to_chunks(v.astype(jnp.float32), GLA_CHUNK)
    lc = to_chunks(log_a, GLA_CHUNK)
    causal = jnp.tril(jnp.ones((GLA_CHUNK, GLA_CHUNK), dtype=bool))

    def step(state, inp):
        q_c, k_c, v_c, la_c = inp
        cum = jnp.cumsum(la_c, axis=2)
        diff = cum[:, :, :, None, :] - cum[:, :, None, :, :]
        decay = jnp.exp(jnp.where(causal[:, :, None], diff, -jnp.inf))
        scores = jnp.einsum('bhtc,bhsc,bhtsc->bhts', q_c, k_c, decay)
        o = scores @ v_c + jnp.einsum('bhtc,bhcv->bhtv', q_c * jnp.exp(cum), state)
        last = cum[:, :, -1:, :]
        state = jnp.exp(last[:, :, 0, :])[..., None] * state + \
            jnp.einsum('bhsc,bhsv->bhcv', k_c * jnp.exp(last - cum), v_c)
        return state, o

    s0 = jnp.zeros((b_, GLA_HEADS, GLA_DK, GLA_DV), jnp.float32)
    _, o = lax.scan(step, s0, (qc, kc, vc, lc))
    o = head_rms_norm(from_chunks(o), norm_g).reshape(b_, s_, GLA_HEADS * GLA_DV)
    return o * jax.nn.silu(out_gate.astype(jnp.float32))


def diff_group(q, k, v, lam, lam_init, norm_g, rel_bias):
    b_, s_ = q.shape[:2]
    nq = s_ // DIFF_QBLOCK
    qh = q.astype(jnp.float32).transpose(0, 2, 3, 1, 4) * DIFF_D ** -0.5
    kh = k.astype(jnp.float32).transpose(0, 2, 3, 1, 4)
    vh = v.astype(jnp.float32).transpose(0, 2, 1, 3)
    qb = qh.reshape(b_, DIFF_HEADS, 2, nq, DIFF_QBLOCK, DIFF_D).transpose(3, 0, 1, 2, 4, 5)
    kpos = jnp.arange(s_)

    def block(args):
        q_blk, blk = args
        qpos = blk * DIFF_QBLOCK + jnp.arange(DIFF_QBLOCK)
        rel = kpos[None, :] - qpos[:, None]
        bias = rel_bias[t5_bucket(rel)].astype(jnp.float32).transpose(2, 0, 1)
        logits = jnp.einsum('bhmqd,bhmkd->bhmqk', q_blk, kh) + bias[None, :, None]
        logits = jnp.where((rel <= 0)[None, None, None], logits, -jnp.inf)
        probs = jax.nn.softmax(logits, axis=-1)
        attn = probs[:, :, 0] - lam * probs[:, :, 1]
        return jnp.einsum('bhqk,bhke->bhqe', attn, vh)

    o = lax.map(block, (qb, jnp.arange(nq)))
    o = o.transpose(1, 0, 3, 2, 4).reshape(b_, s_, DIFF_HEADS, 2 * DIFF_D)
    o = head_rms_norm(o, norm_g) * (1.0 - lam_init)
    return o.reshape(b_, s_, DIFF_HEADS * 2 * DIFF_D)


def rotate_every_two(t):
    t1 = t[..., ::2]
    t2 = t[..., 1::2]
    return jnp.stack((-t2, t1), axis=-1).reshape(t.shape)


def retention_group(q, k, v, gate):
    b_, s_ = q.shape[:2]
    pos = jnp.arange(s_, dtype=jnp.float32)
    angle = 1.0 / (10000.0 ** jnp.linspace(0.0, 1.0, RET_DK // 2, dtype=jnp.float32))
    angle = jnp.repeat(angle, 2)
    sin = jnp.sin(pos[:, None] * angle)[None, :, None, :]
    cos = jnp.cos(pos[:, None] * angle)[None, :, None, :]
    qf = q.astype(jnp.float32)
    kf = k.astype(jnp.float32)
    qf = qf * cos + rotate_every_two(qf) * sin
    kf = (kf * cos + rotate_every_two(kf) * sin) * RET_DK ** -0.5
    log_g = jnp.log1p(-jnp.exp2(-5.0 - jnp.arange(RET_HEADS, dtype=jnp.float32)))
    idx = jnp.arange(RET_CHUNK, dtype=jnp.float32)
    rel = idx[:, None] - idx[None, :]
    inner_decay = jnp.where(rel[None] >= 0, jnp.exp(jnp.maximum(rel, 0.0)[None] * log_g[:, None, None]), 0.0)
    cross_decay = jnp.exp((idx + 1.0)[None] * log_g[:, None])
    state_decay = jnp.exp((RET_CHUNK - 1.0 - idx)[None] * log_g[:, None])
    chunk_decay = jnp.exp(RET_CHUNK * log_g)

    def step(state, inp):
        q_c, k_c, v_c = inp
        scores = jnp.einsum('bhtd,bhsd->bhts', q_c, k_c) * inner_decay
        o = scores @ v_c + jnp.einsum('bhtd,bhdv->bhtv', q_c, state) * cross_decay[:, :, None]
        state = state * chunk_decay[:, None, None] + \
            jnp.einsum('bhsd,bhsv->bhdv', k_c * state_decay[:, :, None], v_c)
        return state, o

    s0 = jnp.zeros((b_, RET_HEADS, RET_DK, RET_DV), jnp.float32)
    _, o = lax.scan(step, s0, (to_chunks(qf, RET_CHUNK), to_chunks(kf, RET_CHUNK),
                               to_chunks(v.astype(jnp.float32), RET_CHUNK)))
    o = head_rms_norm(from_chunks(o)).reshape(b_, s_, RET_HEADS * RET_DV)
    return o * jax.nn.silu(gate.astype(jnp.float32))


def hybrid_mixer(x, w_in, w_gla_gate, b_gla_gate, gla_norm_g, diff_lambda, diff_norm_g,
                 w_out, rel_bias, lam_init):
    b_, s_, _ = x.shape
    z = x @ w_in
    (gq, gk, gv, g_lr, g_out, dq, dk, dv, rq, rk, rv, rg) = jnp.split(z, IN_OFFSETS, axis=-1)
    gla_o = gla_group(gq.reshape(b_, s_, GLA_HEADS, GLA_DK), gk.reshape(b_, s_, GLA_HEADS, GLA_DK),
                      gv.reshape(b_, s_, GLA_HEADS, GLA_DV), g_lr, g_out,
                      w_gla_gate, b_gla_gate, gla_norm_g)
    lf = diff_lambda.astype(jnp.float32)
    lam = jnp.exp(jnp.sum(lf[0] * lf[1])) - jnp.exp(jnp.sum(lf[2] * lf[3])) + lam_init
    diff_o = diff_group(dq.reshape(b_, s_, DIFF_HEADS, 2, DIFF_D), dk.reshape(b_, s_, DIFF_HEADS, 2, DIFF_D),
                        dv.reshape(b_, s_, DIFF_HEADS, 2 * DIFF_D), lam, lam_init, diff_norm_g, rel_bias)
    ret_o = retention_group(rq.reshape(b_, s_, RET_HEADS, RET_DK), rk.reshape(b_, s_, RET_HEADS, RET_DK),
                            rv.reshape(b_, s_, RET_HEADS, RET_DV), rg)
    mixed = jnp.concatenate([gla_o, diff_o, ret_o], axis=-1).astype(x.dtype)
    return mixed @ w_out


def moe_ffn(h, w_router, b_router, w_gate_up, b_gate_up, w_down, b_down):
    n_tok, d = h.shape
    logits = (h @ w_router + b_router).astype(jnp.float32)
    top_v, top_i = lax.top_k(logits, TOP_K)
    gates = jax.nn.softmax(top_v, axis=-1)
    n_assign = n_tok * TOP_K
    e_flat = top_i.reshape(-1)
    tok_flat = jnp.repeat(jnp.arange(n_tok, dtype=jnp.int32), TOP_K)
    g_flat = gates.reshape(-1)
    order = jnp.argsort(e_flat)
    se, stok, sg = e_flat[order], tok_flat[order], g_flat[order]
    sizes = jnp.bincount(e_flat, length=N_EXPERTS)
    starts = jnp.cumsum(sizes) - sizes
    padded = ((sizes + MOE_BLOCK - 1) // MOE_BLOCK) * MOE_BLOCK
    pends = jnp.cumsum(padded)
    pstarts = pends - padded
    dest = pstarts[se] + (jnp.arange(n_assign) - starts[se])
    buf_len = ((n_assign + N_EXPERTS * (MOE_BLOCK - 1) + MOE_BLOCK - 1) // MOE_BLOCK) * MOE_BLOCK
    n_blk = buf_len // MOE_BLOCK
    buf_tok = jnp.full((buf_len,), n_tok, jnp.int32).at[dest].set(stok)
    buf_g = jnp.zeros((buf_len,), jnp.float32).at[dest].set(sg)
    blk_e = jnp.clip(jnp.searchsorted(pends, jnp.arange(n_blk) * MOE_BLOCK, side='right'), 0, N_EXPERTS - 1)
    h_pad = jnp.concatenate([h, jnp.zeros((1, d), h.dtype)], axis=0)

    def expert_block(args):
        tok, e = args
        hb = h_pad[tok]
        gu = hb @ w_gate_up[e] + b_gate_up[e]
        g = jnp.minimum(gu[:, ::2], SWIGLU_LIMIT)
        u = jnp.clip(gu[:, 1::2], -SWIGLU_LIMIT, SWIGLU_LIMIT)
        act = (u + 1.0) * (g * jax.nn.sigmoid(g * SWIGLU_ALPHA))
        return act @ w_down[e] + b_down[e]

    yb = lax.map(expert_block, (buf_tok.reshape(n_blk, MOE_BLOCK), blk_e))
    contrib = yb.reshape(buf_len, d).astype(jnp.float32) * buf_g[:, None]
    y = jax.ops.segment_sum(contrib, buf_tok, num_segments=n_tok + 1)[:n_tok]
    return y.astype(h.dtype)


def setup_inputs(seed: int = 0) -> dict:
    key = jax.random.key(seed)
    ks = jax.random.split(key, 24)
    f32 = jnp.float32

    def nrm(k, shape, scale):
        return jax.random.normal(k, shape, f32) * scale

    return {
        "x": nrm(ks[0], (BATCH, SEQ, D_MODEL), 1.0),
        "p": nrm(ks[1], (DEPTH, BATCH, SEQ, PLE_DIM), 1.0),
        "w_in": nrm(ks[2], (DEPTH, D_MODEL, IN_WIDTH), D_MODEL ** -0.5),
        "w_gla_gate": nrm(ks[3], (DEPTH, GLA_GATE_RANK, GLA_HEADS * GLA_DK), GLA_GATE_RANK ** -0.5),
        "b_gla_gate": nrm(ks[4], (DEPTH, GLA_HEADS * GLA_DK), 0.1),
        "gla_norm_g": 1.0 + nrm(ks[5], (DEPTH, GLA_DV), 0.02),
        "diff_lambda": nrm(ks[6], (DEPTH, 4, DIFF_D), 0.1),
        "diff_norm_g": 1.0 + nrm(ks[7], (DEPTH, 2 * DIFF_D), 0.02),
        "w_out": nrm(ks[8], (DEPTH, MIX_WIDTH, D_MODEL), MIX_WIDTH ** -0.5 * DEEPNORM_BETA),
        "rel_bias": nrm(ks[9], (T5_BUCKETS, DIFF_HEADS), 0.5),
        "ln1_g": 1.0 + nrm(ks[10], (DEPTH, D_MODEL), 0.02),
        "ln1_b": nrm(ks[11], (DEPTH, D_MODEL), 0.02),
        "w_router": nrm(ks[12], (DEPTH, D_MODEL, N_EXPERTS), D_MODEL ** -0.5),
        "b_router": nrm(ks[13], (DEPTH, N_EXPERTS), 0.01),
        "w_gate_up": nrm(ks[14], (DEPTH, N_EXPERTS, D_MODEL, 2 * D_FF), D_MODEL ** -0.5),
        "b_gate_up": nrm(ks[15], (DEPTH, N_EXPERTS, 2 * D_FF), 0.01),
        "w_down": nrm(ks[16], (DEPTH, N_EXPERTS, D_FF, D_MODEL), D_FF ** -0.5 * DEEPNORM_BETA),
        "b_down": nrm(ks[17], (DEPTH, N_EXPERTS, D_MODEL), 0.01),
        "w_ple_gate": nrm(ks[18], (DEPTH, D_MODEL, D_MODEL), D_MODEL ** -0.5),
        "b_ple_gate": nrm(ks[19], (DEPTH, D_MODEL), 0.01),
        "w_ple_proj": nrm(ks[20], (DEPTH, PLE_DIM, D_MODEL), PLE_DIM ** -0.5 * DEEPNORM_BETA),
        "ln2_g": 1.0 + nrm(ks[21], (DEPTH, D_MODEL), 0.02),
        "ln2_b": nrm(ks[22], (DEPTH, D_MODEL), 0.02),
    }


def reference(x, p, w_in, w_gla_gate, b_gla_gate, gla_norm_g, diff_lambda, diff_norm_g, w_out,
              rel_bias, ln1_g, ln1_b, w_router, b_router, w_gate_up, b_gate_up, w_down, b_down,
              w_ple_gate, b_ple_gate, w_ple_proj, ln2_g, ln2_b):
    b_, s_, d = x.shape
    for i in range(DEPTH):
        lam_init = 0.8 - 0.6 * math.exp(-0.3 * i)
        a = hybrid_mixer(x, w_in[i], w_gla_gate[i], b_gla_gate[i], gla_norm_g[i], diff_lambda[i],
                         diff_norm_g[i], w_out[i], rel_bias, lam_init)
        x = layer_norm(DEEPNORM_ALPHA * x + a, ln1_g[i], ln1_b[i])
        m = moe_ffn(x.reshape(b_ * s_, d), w_router[i], b_router[i], w_gate_up[i], b_gate_up[i],
                    w_down[i], b_down[i]).reshape(b_, s_, d)
        e = jax.nn.sigmoid(x @ w_ple_gate[i] + b_ple_gate[i]) * (p[i] @ w_ple_proj[i])
        x = layer_norm(DEEPNORM_ALPHA * x + m + e, ln2_g[i], ln2_b[i])
    return x
```

```python
import functools
import math

import numpy as np
import jax
import jax.numpy as jnp
from jax import lax
from jax.experimental import pallas as pl
from jax.experimental.pallas import tpu as pltpu

F32 = jnp.float32
BF16 = jnp.bfloat16

GLA_HEADS = 4
GLA_DK = 32
GLA_DV = 64
GLA_GATE_RANK = 16
GLA_TAU = 16.0
GLA_CHUNK = 64
DIFF_HEADS = 4
DIFF_D = 64
RET_HEADS = 4
RET_DK = 64
RET_DV = 64
RET_CHUNK = 128
T5_BUCKETS = 32
T5_MAX_DIST = 128
N_EXPERTS = 32
TOP_K = 4
SWIGLU_LIMIT = 7.0
SWIGLU_ALPHA = 1.702
LN_EPS = 1e-5
HEAD_NORM_EPS = 1e-5

LANE = 128
VMEM_LIMIT = 56 * 1024 * 1024

_GQ, _GK, _GV, _GO = 0, 128, 256, 512
_DQ, _DK, _DV = 768, 1280, 1792
_RQ, _RK, _RV, _RG = 2304, 2560, 2816, 3072
_LR = 3328
Z_WIDTH = 3456

GLA_FAST_MAX_DECAY = 60.0


def _cparams(sem):
    return pltpu.CompilerParams(dimension_semantics=sem, vmem_limit_bytes=VMEM_LIMIT)


def _split_hi_lo(a):
    hi = a.astype(BF16)
    lo = (a - hi.astype(F32)).astype(BF16)
    return hi, lo


def _dot(a, b):
    return jnp.dot(a, b, preferred_element_type=F32)


def _dot_nt(a, b):
    return lax.dot_general(a, b, (((1,), (1,)), ((), ())), preferred_element_type=F32)


def _dot_tn(a, b):
    return lax.dot_general(a, b, (((0,), (0,)), ((), ())), preferred_element_type=F32)


def _seg_sum(a, ind):
    hi, lo = _split_hi_lo(a)
    return _dot(hi, ind) + _dot(lo, ind)


def _mm_kernel(x_ref, w_ref, o_ref):
    o_ref[...] = _dot(x_ref[...], w_ref[...]).astype(o_ref.dtype)


def _matmul(x, w, tm, tn, out_dtype):
    m, k = x.shape
    n = w.shape[1]
    return pl.pallas_call(
        _mm_kernel,
        grid=(n // tn, m // tm),
        in_specs=[pl.BlockSpec((tm, k), lambda j, i: (i, 0)),
                  pl.BlockSpec((k, tn), lambda j, i: (0, j))],
        out_specs=pl.BlockSpec((tm, tn), lambda j, i: (i, j)),
        out_shape=jax.ShapeDtypeStruct((m, n), out_dtype),
        compiler_params=_cparams(("arbitrary", "arbitrary")),
        name="in_proj",
    )(x, w)


def _gla_kernel(q_ref, k_ref, v_ref, og_ref, lr_ref, wg_ref, bg_ref, ng_ref, o_ref,
                state_ref, la_ref, cum_ref, kf_ref, vf_ref, oacc_ref, intra_ref, *, n_chunks, fast_max_decay):
    c_len = GLA_CHUNK
    hk = GLA_HEADS * GLA_DK
    hv = GLA_HEADS * GLA_DV

    @pl.when(pl.program_id(1) == 0)
    def _():
        state_ref[...] = jnp.zeros_like(state_ref)

    zg = _dot(lr_ref[...], wg_ref[...]) + bg_ref[...]
    la = (jnp.minimum(zg, 0.0) - jnp.log1p(jnp.exp(-jnp.abs(zg)))) / GLA_TAU
    la_ref[...] = la
    chunk_tot = jnp.sum(la.reshape(n_chunks, c_len, hk), axis=1)
    fast = jnp.min(chunk_tot) > -fast_max_decay

    row = lax.broadcasted_iota(jnp.int32, (c_len, c_len), 0)
    col = lax.broadcasted_iota(jnp.int32, (c_len, c_len), 1)
    tril = (row >= col).astype(BF16)
    sr = lax.broadcasted_iota(jnp.int32, (hv, hk), 0)
    sc = lax.broadcasted_iota(jnp.int32, (hv, hk), 1)
    state_mask = ((sr // GLA_DV) == (sc // GLA_DK)).astype(F32)
    lane_k = lax.broadcasted_iota(jnp.int32, (1, hk), 1) // GLA_DK
    lane_v = lax.broadcasted_iota(jnp.int32, (1, hv), 1) // GLA_DV
    r4 = lax.broadcasted_iota(jnp.int32, (GLA_HEADS * c_len, c_len), 0) % c_len
    c4 = lax.broadcasted_iota(jnp.int32, (GLA_HEADS * c_len, c_len), 1)
    causal4 = r4 >= c4
    er = lax.broadcasted_iota(jnp.int32, (hk, hv), 0) // GLA_DK
    ec = lax.broadcasted_iota(jnp.int32, (hk, hv), 1) // GLA_DV
    head_expand = (er == ec).astype(BF16)
    trow = lax.broadcasted_iota(jnp.int32, (c_len, hv), 0)

    def chunk(c, carry):
        r0 = pl.multiple_of(c * c_len, c_len)
        rows = pl.ds(r0, c_len)
        q = q_ref[rows, :].astype(F32) * (GLA_DK ** -0.5)
        k = k_ref[rows, :].astype(F32)
        v = v_ref[rows, :]
        la_c = la_ref[rows, :]
        la_hi, la_lo = _split_hi_lo(la_c)
        cum = _dot(tril, la_hi) + _dot(tril, la_lo)
        last = cum[c_len - 1:c_len, :]

        @pl.when(fast)
        def _():
            mid = cum[c_len // 2:c_len // 2 + 1, :]
            qa = q * jnp.exp(cum - mid)
            kb = (k * jnp.exp(mid - cum)).astype(BF16)
            qa4 = jnp.concatenate(
                [jnp.where(lane_k == h, qa, 0.0) for h in range(GLA_HEADS)], axis=0).astype(BF16)
            s4 = jnp.where(causal4, _dot_nt(qa4, kb), 0.0)
            pv = _dot(s4.astype(BF16), v)
            acc = jnp.zeros((c_len, hv), F32)
            for h in range(GLA_HEADS):
                acc = acc + jnp.where(lane_v == h, pv[h * c_len:(h + 1) * c_len, :], 0.0)
            intra_ref[...] = acc

        @pl.when(jnp.logical_not(fast))
        def _():
            cum_ref[...] = cum
            kf_ref[...] = k
            vf_ref[...] = v.astype(F32)

            def one_key(s, acc):
                ks = kf_ref[pl.ds(s, 1), :]
                vs = vf_ref[pl.ds(s, 1), :]
                cs = cum_ref[pl.ds(s, 1), :]
                w = q * ks * jnp.exp(jnp.minimum(cum - cs, 0.0))
                wsum = _dot(w.astype(BF16), head_expand)
                return acc + jnp.where(trow >= s, wsum, 0.0) * vs

            intra_ref[...] = lax.fori_loop(0, c_len, one_key, jnp.zeros((c_len, hv), F32))

        st = state_ref[...]
        qg = (q * jnp.exp(cum)).astype(BF16)
        inter = _dot_nt(qg, st.astype(BF16))
        oacc_ref[rows, :] = intra_ref[...] + inter
        kg = (k * jnp.exp(last - cum)).astype(BF16)
        upd = _dot_tn(v, kg)
        state_ref[...] = (st * jnp.exp(last) + upd) * state_mask
        return carry

    lax.fori_loop(0, n_chunks, chunk, 0)

    o = oacc_ref[...]
    vr = lax.broadcasted_iota(jnp.int32, (hv, hv), 0) // GLA_DV
    vc = lax.broadcasted_iota(jnp.int32, (hv, hv), 1) // GLA_DV
    seg = (vr == vc).astype(BF16)
    ms = _seg_sum(o * o, seg) * (1.0 / GLA_DV)
    y = o * lax.rsqrt(ms + HEAD_NORM_EPS) * ng_ref[...]
    g = og_ref[...].astype(F32)
    o_ref[...] = (y * (g * jax.nn.sigmoid(g))).astype(o_ref.dtype)


def _gla(z, w_gate, b_gate, norm_g, batch, seq, tg=512, fast_max_decay=GLA_FAST_MAX_DECAY):
    tg = min(tg, seq)
    nt = seq // tg
    hk = GLA_HEADS * GLA_DK
    hv = GLA_HEADS * GLA_DV

    def rowmap(cb):
        return lambda b, t: (b * nt + t, cb)

    const = lambda b, t: (0, 0)
    return pl.pallas_call(
        functools.partial(_gla_kernel, n_chunks=tg // GLA_CHUNK, fast_max_decay=fast_max_decay),
        grid=(batch, nt),
        in_specs=[pl.BlockSpec((tg, hk), rowmap(_GQ // hk)),
                  pl.BlockSpec((tg, hk), rowmap(_GK // hk)),
                  pl.BlockSpec((tg, hv), rowmap(_GV // hv)),
                  pl.BlockSpec((tg, hv), rowmap(_GO // hv)),
                  pl.BlockSpec((tg, LANE), rowmap(_LR // LANE)),
                  pl.BlockSpec((LANE, hk), const),
                  pl.BlockSpec((1, hk), const),
                  pl.BlockSpec((1, hv), const)],
        out_specs=pl.BlockSpec((tg, hv), lambda b, t: (b * nt + t, 0)),
        out_shape=jax.ShapeDtypeStruct((batch * seq, hv), BF16),
        scratch_shapes=[pltpu.VMEM((hv, hk), F32),
                        pltpu.VMEM((tg, hk), F32),
                        pltpu.VMEM((GLA_CHUNK, hk), F32),
                        pltpu.VMEM((GLA_CHUNK, hk), F32),
                        pltpu.VMEM((GLA_CHUNK, hv), F32),
                        pltpu.VMEM((tg, hv), F32),
                        pltpu.VMEM((GLA_CHUNK, hv), F32)],
        compiler_params=_cparams(("arbitrary", "arbitrary")),
        name="gla",
    )(z, z, z, z, z, w_gate, b_gate, norm_g)


def _diff_kernel(lam_ref, q_ref, k_ref, v_ref, bias_ref, ng_ref, o_ref,
                 m_ref, l_ref, acc_ref, *, blk, lam_init):
    qi = pl.program_id(2)
    d = DIFF_D
    q = q_ref[...] * (DIFF_D ** -0.5)
    qm = (q[:, :d], q[:, d:])
    m_ref[...] = jnp.full_like(m_ref, -jnp.inf)
    l_ref[...] = jnp.zeros_like(l_ref)
    acc_ref[...] = jnp.zeros_like(acc_ref)

    def kv_step(j, carry):
        rows = pl.ds(pl.multiple_of(j * blk, blk), blk)
        kj = k_ref[rows, :]
        vj = v_ref[rows, :]
        bias = bias_ref[0, jnp.minimum(qi - j, 2)]
        for m in range(2):
            s = _dot_nt(qm[m], kj[:, m * d:(m + 1) * d]) + bias
            m_old = m_ref[m]
            m_new = jnp.maximum(m_old, jnp.max(s, axis=-1, keepdims=True))
            alpha = jnp.exp(m_old - m_new)
            p = jnp.exp(s - m_new)
            l_ref[m] = alpha * l_ref[m] + jnp.sum(p, axis=-1, keepdims=True)
            acc_ref[m] = alpha * acc_ref[m] + _dot(p.astype(BF16), vj)
            m_ref[m] = m_new
        return carry

    lax.fori_loop(0, qi + 1, kv_step, 0)

    lf = lam_ref[...]
    lam = (jnp.exp(jnp.sum(lf[0:1] * lf[1:2], axis=-1, keepdims=True))
           - jnp.exp(jnp.sum(lf[2:3] * lf[3:4], axis=-1, keepdims=True)) + lam_init)
    o = acc_ref[0] / l_ref[0] - lam * (acc_ref[1] / l_ref[1])
    ms = jnp.mean(o * o, axis=-1, keepdims=True)
    y = o * lax.rsqrt(ms + HEAD_NORM_EPS) * ng_ref[...] * (1.0 - lam_init)
    o_ref[...] = y.astype(o_ref.dtype)


def _t5_bucket(rel):
    n = jnp.maximum(-rel, 0)
    max_exact = T5_BUCKETS // 2
    nf = jnp.maximum(n, 1).astype(F32)
    large = max_exact + (jnp.log(nf / max_exact) / math.log(T5_MAX_DIST / max_exact)
                         * (T5_BUCKETS - max_exact)).astype(jnp.int32)
    large = jnp.minimum(large, T5_BUCKETS - 1)
    return jnp.where(n < max_exact, n, large)


def _bias_tiles(rel_bias, blk):
    assert blk + 1 >= T5_MAX_DIST
    i = jnp.arange(blk)[:, None]
    j = jnp.arange(blk)[None, :]
    tiles = []
    for delta in range(3):
        rel = j - (i + delta * blk)
        t = rel_bias[_t5_bucket(rel)].astype(F32)
        if delta == 0:
            t = jnp.where((rel <= 0)[:, :, None], t, -jnp.inf)
        tiles.append(jnp.transpose(t, (2, 0, 1)))
    return jnp.stack(tiles, axis=1)


def _diff(z, diff_lambda, norm_g, bias_tiles, batch, seq, lam_init, blk=256):
    blk = min(blk, seq)
    nq = seq // blk
    w = 2 * DIFF_D
    return pl.pallas_call(
        functools.partial(_diff_kernel, blk=blk, lam_init=lam_init),
        grid=(batch, DIFF_HEADS, nq),
        in_specs=[pl.BlockSpec((4, DIFF_D), lambda b, h, i: (0, 0)),
                  pl.BlockSpec((blk, w), lambda b, h, i: (b * nq + i, _DQ // w + h)),
                  pl.BlockSpec((seq, w), lambda b, h, i: (b, _DK // w + h)),
                  pl.BlockSpec((seq, w), lambda b, h, i: (b, _DV // w + h)),
                  pl.BlockSpec((1, 3, blk, blk), lambda b, h, i: (h, 0, 0, 0)),
                  pl.BlockSpec((1, w), lambda b, h, i: (0, 0))],
        out_specs=pl.BlockSpec((blk, w), lambda b, h, i: (b * nq + i, h)),
        out_shape=jax.ShapeDtypeStruct((batch * seq, DIFF_HEADS * w), BF16),
        scratch_shapes=[pltpu.VMEM((2, blk, 1), F32),
                        pltpu.VMEM((2, blk, 1), F32),
                        pltpu.VMEM((2, blk, w), F32)],
        compiler_params=_cparams(("arbitrary", "arbitrary", "arbitrary")),
        name="diff_attn",
    )(diff_lambda, z, z, z, bias_tiles, norm_g)


def _ret_kernel(q_ref, k_ref, v_ref, g_ref, sin_ref, cos_ref, inner_ref, cross_ref, sdec_ref, cdec_ref,
                o_ref, state_ref):
    c_len = RET_CHUNK
    hk = RET_HEADS * RET_DK
    hv = RET_HEADS * RET_DV

    @pl.when(pl.program_id(1) == 0)
    def _():
        state_ref[...] = jnp.zeros_like(state_ref)

    sin = sin_ref[...]
    cos = cos_ref[...]
    even = (lax.broadcasted_iota(jnp.int32, (1, hk), 1) % 2) == 0

    def rotary(t):
        nxt = pltpu.roll(t, hk - 1, 1)
        prv = pltpu.roll(t, 1, 1)
        return t * cos + jnp.where(even, -nxt, prv) * sin

    qf = rotary(q_ref[...].astype(F32))
    kf = rotary(k_ref[...].astype(F32)) * (RET_DK ** -0.5)
    v = v_ref[...]

    lane_k = lax.broadcasted_iota(jnp.int32, (1, hk), 1) // RET_DK
    lane_v = lax.broadcasted_iota(jnp.int32, (1, hv), 1) // RET_DV
    q4 = jnp.concatenate([jnp.where(lane_k == h, qf, 0.0) for h in range(RET_HEADS)], axis=0).astype(BF16)
    s4 = _dot_nt(q4, kf.astype(BF16)) * inner_ref[...]
    pv = _dot(s4.astype(BF16), v)
    acc = jnp.zeros((c_len, hv), F32)
    for h in range(RET_HEADS):
        acc = acc + jnp.where(lane_v == h, pv[h * c_len:(h + 1) * c_len, :], 0.0)

    st = state_ref[...]
    o = acc + _dot_nt(qf.astype(BF16), st.astype(BF16)) * cross_ref[...]
    ks = (kf * sdec_ref[...]).astype(BF16)
    sr = lax.broadcasted_iota(jnp.int32, (hv, hk), 0) // RET_DV
    sc = lax.broadcasted_iota(jnp.int32, (hv, hk), 1) // RET_DK
    state_ref[...] = jnp.where(sr == sc, st * cdec_ref[...] + _dot_tn(v, ks), 0.0)

    vr = lax.broadcasted_iota(jnp.int32, (hv, hv), 0) // RET_DV
    vc = lax.broadcasted_iota(jnp.int32, (hv, hv), 1) // RET_DV
    seg = (vr == vc).astype(BF16)
    ms = _seg_sum(o * o, seg) * (1.0 / RET_DV)
    g = g_ref[...].astype(F32)
    o_ref[...] = (o * lax.rsqrt(ms + HEAD_NORM_EPS) * (g * jax.nn.sigmoid(g))).astype(o_ref.dtype)


def _ret_tables(seq):
    c_len = RET_CHUNK
    pos = jnp.arange(seq, dtype=F32)
    angle = 1.0 / (10000.0 ** jnp.linspace(0.0, 1.0, RET_DK // 2, dtype=F32))
    angle = jnp.repeat(angle, 2)
    sin = jnp.tile(jnp.sin(pos[:, None] * angle), (1, RET_HEADS))
    cos = jnp.tile(jnp.cos(pos[:, None] * angle), (1, RET_HEADS))
    log_g = jnp.log1p(-jnp.exp2(-5.0 - jnp.arange(RET_HEADS, dtype=F32)))
    idx = jnp.arange(c_len, dtype=F32)
    rel = idx[:, None] - idx[None, :]
    inner = jnp.where(rel[None] >= 0, jnp.exp(jnp.maximum(rel, 0.0)[None] * log_g[:, None, None]), 0.0)
    inner = inner.reshape(RET_HEADS * c_len, c_len)
    cross = jnp.exp((idx + 1.0)[None] * log_g[:, None])
    cross = jnp.repeat(cross.T, RET_DV, axis=1)
    sdec = jnp.exp((c_len - 1.0 - idx)[None] * log_g[:, None])
    sdec = jnp.repeat(sdec.T, RET_DK, axis=1)
    cdec = jnp.repeat(jnp.exp(c_len * log_g), RET_DK)[None, :]
    return sin, cos, inner, cross, sdec, cdec


def _ret(z, tables, batch, seq):
    c_len = RET_CHUNK
    nt = seq // c_len
    hk = RET_HEADS * RET_DK
    hv = RET_HEADS * RET_DV
    sin, cos, inner, cross, sdec, cdec = tables

    def rowmap(cb):
        return lambda b, t: (b * nt + t, cb)

    const = lambda b, t: (0, 0)
    return pl.pallas_call(
        _ret_kernel,
        grid=(batch, nt),
        in_specs=[pl.BlockSpec((c_len, hk), rowmap(_RQ // hk)),
                  pl.BlockSpec((c_len, hk), rowmap(_RK // hk)),
                  pl.BlockSpec((c_len, hv), rowmap(_RV // hv)),
                  pl.BlockSpec((c_len, hv), rowmap(_RG // hv)),
                  pl.BlockSpec((c_len, hk), lambda b, t: (t, 0)),
                  pl.BlockSpec((c_len, hk), lambda b, t: (t, 0)),
                  pl.BlockSpec((RET_HEADS * c_len, c_len), const),
                  pl.BlockSpec((c_len, hv), const),
                  pl.BlockSpec((c_len, hk), const),
                  pl.BlockSpec((1, hk), const)],
        out_specs=pl.BlockSpec((c_len, hv), lambda b, t: (b * nt + t, 0)),
        out_shape=jax.ShapeDtypeStruct((batch * seq, hv), BF16),
        scratch_shapes=[pltpu.VMEM((hv, hk), F32)],
        compiler_params=_cparams(("arbitrary", "arbitrary")),
        name="retention",
    )(z, z, z, z, sin, cos, inner, cross, sdec, cdec)


def _layer_norm(t, g, b):
    mu = jnp.mean(t, axis=-1, keepdims=True)
    var = jnp.mean(jnp.square(t - mu), axis=-1, keepdims=True)
    return (t - mu) * lax.rsqrt(var + LN_EPS) * g + b


def _out_ln_kernel(gla_ref, dif_ref, ret_ref, x_ref, wo_ref, g_ref, b_ref, wr_hi_ref, wr_lo_ref, br_ref,
                   x1_ref, x1b_ref, gate_ref, idx_ref, *, alpha):
    w0 = gla_ref.shape[1]
    w1 = w0 + dif_ref.shape[1]
    a = (_dot(gla_ref[...], wo_ref[0:w0, :]) + _dot(dif_ref[...], wo_ref[w0:w1, :])
         + _dot(ret_ref[...], wo_ref[w1:, :]))
    x1 = _layer_norm(alpha * x_ref[...] + a, g_ref[...], b_ref[...])
    x1_ref[...] = x1
    x1b_ref[...] = x1.astype(BF16)

    hi, lo = _split_hi_lo(x1)
    logits = (_dot(hi, wr_hi_ref[...]) + _dot(lo, wr_hi_ref[...]) + _dot(hi, wr_lo_ref[...])) + br_ref[...]
    lane = lax.broadcasted_iota(jnp.int32, logits.shape, 1)
    work = jnp.where(lane < N_EXPERTS, logits, -jnp.inf)
    vals = jnp.zeros(logits.shape, F32)
    idxs = jnp.zeros(logits.shape, jnp.int32)
    top0 = None
    for r in range(TOP_K):
        mx = jnp.max(work, axis=-1, keepdims=True)
        am = jnp.min(jnp.where(work == mx, lane, LANE), axis=-1, keepdims=True)
        if r == 0:
            top0 = mx
        vals = jnp.where(lane == r, jnp.exp(mx - top0), vals)
        idxs = jnp.where(lane == r, am, idxs)
        work = jnp.where(lane == am, -jnp.inf, work)
    gate_ref[...] = vals / jnp.sum(vals, axis=-1, keepdims=True)
    idx_ref[...] = idxs


def _out_ln(gla_o, dif_o, ret_o, x, w_out, ln_g, ln_b, wr_hi, wr_lo, b_router, alpha, tm=512):
    n, d = x.shape
    tm = min(tm, n)
    row = lambda i: (i, 0)
    const = lambda i: (0, 0)
    return pl.pallas_call(
        functools.partial(_out_ln_kernel, alpha=alpha),
        grid=(n // tm,),
        in_specs=[pl.BlockSpec((tm, gla_o.shape[1]), row),
                  pl.BlockSpec((tm, dif_o.shape[1]), row),
                  pl.BlockSpec((tm, ret_o.shape[1]), row),
                  pl.BlockSpec((tm, d), row),
                  pl.BlockSpec(w_out.shape, const),
                  pl.BlockSpec((1, d), const),
                  pl.BlockSpec((1, d), const),
                  pl.BlockSpec((d, LANE), const),
                  pl.BlockSpec((d, LANE), const),
                  pl.BlockSpec((1, LANE), const)],
        out_specs=[pl.BlockSpec((tm, d), row),
                   pl.BlockSpec((tm, d), row),
                   pl.BlockSpec((tm, LANE), row),
                   pl.BlockSpec((tm, LANE), row)],
        out_shape=[jax.ShapeDtypeStruct((n, d), F32),
                   jax.ShapeDtypeStruct((n, d), BF16),
                   jax.ShapeDtypeStruct((n, LANE), F32),
                   jax.ShapeDtypeStruct((n, LANE), jnp.int32)],
        compiler_params=_cparams(("arbitrary",)),
        name="out_ln_router",
    )(gla_o, dif_o, ret_o, x, w_out, ln_g, ln_b, wr_hi, wr_lo, b_router)


def _moe_kernel(blk_e_ref, n_used_ref, x_ref, wgu_ref, bgu_ref, wdn_ref, bdn_ref, o_ref):
    i = pl.program_id(0)
    f = wdn_ref.shape[0]

    @pl.when(i < n_used_ref[0])
    def _():
        gu = _dot(x_ref[...], wgu_ref[...]) + bgu_ref[...]
        g = jnp.minimum(gu[:, :f], SWIGLU_LIMIT)
        u = jnp.clip(gu[:, f:], -SWIGLU_LIMIT, SWIGLU_LIMIT)
        act = (u + 1.0) * (g * jax.nn.sigmoid(g * SWIGLU_ALPHA))
        o_ref[...] = (_dot(act.astype(BF16), wdn_ref[...]) + bdn_ref[...]).astype(o_ref.dtype)

    @pl.when(i >= n_used_ref[0])
    def _():
        o_ref[...] = jnp.zeros_like(o_ref)


def _moe(xs, blk_e, n_used, w_gu, b_gu, w_dn, b_dn, blk):
    rows, d = xs.shape
    f2 = w_gu.shape[2]
    f = w_dn.shape[1]
    grid_spec = pltpu.PrefetchScalarGridSpec(
        num_scalar_prefetch=2,
        grid=(rows // blk,),
        in_specs=[pl.BlockSpec((blk, d), lambda i, be, nu: (i, 0)),
                  pl.BlockSpec((None, d, f2), lambda i, be, nu: (be[i], 0, 0)),
                  pl.BlockSpec((None, 1, f2), lambda i, be, nu: (be[i], 0, 0)),
                  pl.BlockSpec((None, f, d), lambda i, be, nu: (be[i], 0, 0)),
                  pl.BlockSpec((None, 1, d), lambda i, be, nu: (be[i], 0, 0))],
        out_specs=pl.BlockSpec((blk, d), lambda i, be, nu: (i, 0)),
    )
    return pl.pallas_call(
        _moe_kernel,
        grid_spec=grid_spec,
        out_shape=jax.ShapeDtypeStruct((rows, d), BF16),
        compiler_params=_cparams(("arbitrary",)),
        name="moe_experts",
    )(blk_e, n_used, xs, w_gu, b_gu, w_dn, b_dn)


def _combine_kernel(yg_ref, gate_ref, x1_ref, x1b_ref, p_ref, wg_ref, bg_ref, wp_ref, g_ref, b_ref,
                    x2_ref, x2b_ref, *, alpha):
    d = x1_ref.shape[1]
    gates = gate_ref[...]
    m = jnp.zeros(x1_ref.shape, F32)
    for r in range(TOP_K):
        m = m + yg_ref[:, r * d:(r + 1) * d].astype(F32) * gates[:, r:r + 1]
    e = (jax.nn.sigmoid(_dot(x1b_ref[...], wg_ref[...]) + bg_ref[...])
         * _dot(p_ref[...].astype(BF16), wp_ref[...]))
    x2 = _layer_norm(alpha * x1_ref[...] + m + e, g_ref[...], b_ref[...])
    x2_ref[...] = x2
    x2b_ref[...] = x2.astype(BF16)


def _combine(yg, gates, x1, x1b, p, w_pg, b_pg, w_pp, ln_g, ln_b, alpha, tm=512):
    n, d = x1.shape
    tm = min(tm, n)
    row = lambda i: (i, 0)
    const = lambda i: (0, 0)
    return pl.pallas_call(
        functools.partial(_combine_kernel, alpha=alpha),
        grid=(n // tm,),
        in_specs=[pl.BlockSpec((tm, TOP_K * d), row),
                  pl.BlockSpec((tm, LANE), row),
                  pl.BlockSpec((tm, d), row),
                  pl.BlockSpec((tm, d), row),
                  pl.BlockSpec((tm, p.shape[1]), row),
                  pl.BlockSpec(w_pg.shape, const),
                  pl.BlockSpec((1, d), const),
                  pl.BlockSpec(w_pp.shape, const),
                  pl.BlockSpec((1, d), const),
                  pl.BlockSpec((1, d), const)],
        out_specs=[pl.BlockSpec((tm, d), row), pl.BlockSpec((tm, d), row)],
        out_shape=[jax.ShapeDtypeStruct((n, d), F32), jax.ShapeDtypeStruct((n, d), BF16)],
        compiler_params=_cparams(("arbitrary",)),
        name="combine_ln",
    )(yg, gates, x1, x1b, p, w_pg, b_pg, w_pp, ln_g, ln_b)


def _route(idx, n_tok, blk):
    e_flat = idx.reshape(-1)
    n_assign = e_flat.shape[0]
    onehot = (e_flat[:, None] == jnp.arange(N_EXPERTS, dtype=jnp.int32)[None, :]).astype(jnp.int32)
    rank = jnp.take_along_axis(jnp.cumsum(onehot, axis=0), e_flat[:, None], axis=1)[:, 0] - 1
    sizes = jnp.sum(onehot, axis=0)
    padded = ((sizes + blk - 1) // blk) * blk
    pends = jnp.cumsum(padded)
    pstarts = pends - padded
    dest = pstarts[e_flat] + rank
    buf_len = ((n_assign + N_EXPERTS * (blk - 1) + blk - 1) // blk) * blk
    n_blk = buf_len // blk
    tok = jnp.arange(n_assign, dtype=jnp.int32) // TOP_K
    buf_tok = jnp.zeros((buf_len,), jnp.int32).at[dest].set(tok)
    blk_e = jnp.clip(jnp.searchsorted(pends, jnp.arange(n_blk, dtype=jnp.int32) * blk, side='right'),
                     0, N_EXPERTS - 1).astype(jnp.int32)
    n_used = (pends[-1] // blk).astype(jnp.int32).reshape(1)
    return buf_tok, dest.reshape(n_tok, TOP_K), blk_e, n_used


def _regroup_w_in(w_in):
    gq, gk, gv, glr, gout, rest = jnp.split(w_in, [128, 256, 512, 528, 784], axis=-1)
    pad = jnp.zeros(w_in.shape[:-1] + (Z_WIDTH - w_in.shape[-1],), w_in.dtype)
    return jnp.concatenate([gq, gk, gv, gout, rest, glr, pad], axis=-1)


def kernel(x, p, w_in, w_gla_gate, b_gla_gate, gla_norm_g, diff_lambda, diff_norm_g, w_out, rel_bias, ln1_g, ln1_b, w_router, b_router, w_gate_up, b_gate_up, w_down, b_down, w_ple_gate, b_ple_gate, w_ple_proj, ln2_g, ln2_b):
    batch, seq, d = x.shape
    depth = w_in.shape[0]
    n = batch * seq
    alpha = (2 * depth) ** 0.25
    moe_blk = 256
    diff_blk = min(256, seq)

    w_in_b = _regroup_w_in(w_in).astype(BF16)
    wg = jnp.zeros((depth, LANE, GLA_HEADS * GLA_DK), F32).at[:, :GLA_GATE_RANK, :].set(w_gla_gate).astype(BF16)
    gla_g = jnp.tile(gla_norm_g, (1, GLA_HEADS))[:, None, :]
    w_out_b = w_out.astype(BF16)
    wr = jnp.zeros((depth, d, LANE), F32).at[:, :, :N_EXPERTS].set(w_router)
    wr_hi = wr.astype(BF16)
    wr_lo = (wr - wr_hi.astype(F32)).astype(BF16)
    br = jnp.zeros((depth, 1, LANE), F32).at[:, 0, :N_EXPERTS].set(b_router)
    w_gu_b = jnp.concatenate([w_gate_up[..., 0::2], w_gate_up[..., 1::2]], axis=-1).astype(BF16)
    b_gu = jnp.concatenate([b_gate_up[..., 0::2], b_gate_up[..., 1::2]], axis=-1)[:, :, None, :]
    w_dn_b = w_down.astype(BF16)
    b_dn = b_down[:, :, None, :]
    w_pg_b = w_ple_gate.astype(BF16)
    w_pp_b = w_ple_proj.astype(BF16)
    bias_tiles = _bias_tiles(rel_bias, diff_blk)
    ret_tables = _ret_tables(seq)

    xf = x.reshape(n, d)
    xb = xf.astype(BF16)
    pf = p.reshape(depth, n, p.shape[-1])
    for i in range(depth):
        lam_init = 0.8 - 0.6 * math.exp(-0.3 * i)
        z = _matmul(xb, w_in_b[i], min(1024, n), Z_WIDTH // 3, BF16)
        gla_o = _gla(z, wg[i], b_gla_gate[i][None, :], gla_g[i], batch, seq)
        dif_o = _diff(z, diff_lambda[i], diff_norm_g[i][None, :], bias_tiles, batch, seq, lam_init, diff_blk)
        ret_o = _ret(z, ret_tables, batch, seq)
        x1, x1b, gates, idx = _out_ln(gla_o, dif_o, ret_o, xf, w_out_b[i], ln1_g[i][None, :], ln1_b[i][None, :],
                                      wr_hi[i], wr_lo[i], br[i], alpha)
        buf_tok, slot, blk_e, n_used = _route(idx[:, :TOP_K], n, moe_blk)
        xs = x1b[buf_tok]
        yb = _moe(xs, blk_e, n_used, w_gu_b[i], b_gu[i], w_dn_b[i], b_dn[i], moe_blk)
        yg = yb[slot].reshape(n, TOP_K * d)
        xf, xb = _combine(yg, gates, x1, x1b, pf[i], w_pg_b[i], b_ple_gate[i][None, :], w_pp_b[i],
                          ln2_g[i][None, :], ln2_b[i][None, :], alpha)
    return xf.reshape(batch, seq, d)
```

```python
import functools
import math

import numpy as np
import jax
import jax.numpy as jnp
from jax import lax
from jax.experimental import pallas as pl
from jax.experimental.pallas import tpu as pltpu

F32 = jnp.float32
BF16 = jnp.bfloat16

GLA_HEADS = 4
GLA_DK = 32
GLA_DV = 64
GLA_GATE_RANK = 16
GLA_TAU = 16.0
GLA_CHUNK = 64
DIFF_HEADS = 4
DIFF_D = 64
RET_HEADS = 4
RET_DK = 64
RET_DV = 64
RET_CHUNK = 128
T5_BUCKETS = 32
T5_MAX_DIST = 128
N_EXPERTS = 32
TOP_K = 4
SWIGLU_LIMIT = 7.0
SWIGLU_ALPHA = 1.702
LN_EPS = 1e-5
HEAD_NORM_EPS = 1e-5

LANE = 128
VMEM_LIMIT = 56 * 1024 * 1024

_GQ, _GK, _GV, _GO = 0, 128, 256, 512
_DQ, _DK, _DV = 768, 1280, 1792
_RQ, _RK, _RV, _RG = 2304, 2560, 2816, 3072
_LR = 3328
Z_WIDTH = 3456

GLA_FAST_MAX_DECAY = 60.0


def _cparams(sem):
    return pltpu.CompilerParams(dimension_semantics=sem, vmem_limit_bytes=VMEM_LIMIT)


def _split_hi_lo(a):
    hi = a.astype(BF16)
    lo = (a - hi.astype(F32)).astype(BF16)
    return hi, lo


def _dot(a, b):
    return jnp.dot(a, b, preferred_element_type=F32)


def _dot_nt(a, b):
    return lax.dot_general(a, b, (((1,), (1,)), ((), ())), preferred_element_type=F32)


def _dot_tn(a, b):
    return lax.dot_general(a, b, (((0,), (0,)), ((), ())), preferred_element_type=F32)


def _seg_sum(a, ind):
    hi, lo = _split_hi_lo(a)
    return _dot(hi, ind) + _dot(lo, ind)


def _mm_kernel(x_ref, w_ref, o_ref):
    o_ref[...] = _dot(x_ref[...], w_ref[...]).astype(o_ref.dtype)


def _matmul(x, w, tm, tn, out_dtype):
    m, k = x.shape
    n = w.shape[1]
    return pl.pallas_call(
        _mm_kernel,
        grid=(n // tn, m // tm),
        in_specs=[pl.BlockSpec((tm, k), lambda j, i: (i, 0)),
                  pl.BlockSpec((k, tn), lambda j, i: (0, j))],
        out_specs=pl.BlockSpec((tm, tn), lambda j, i: (i, j)),
        out_shape=jax.ShapeDtypeStruct((m, n), out_dtype),
        compiler_params=_cparams(("arbitrary", "arbitrary")),
        name="in_proj",
    )(x, w)


def _gla_kernel(q_ref, k_ref, v_ref, og_ref, lr_ref, wg_ref, bg_ref, ng_ref, o_ref,
                state_ref, la_ref, cum_ref, kf_ref, vf_ref, oacc_ref, intra_ref, *, n_chunks, fast_max_decay):
    c_len = GLA_CHUNK
    hk = GLA_HEADS * GLA_DK
    hv = GLA_HEADS * GLA_DV

    @pl.when(pl.program_id(1) == 0)
    def _():
        state_ref[...] = jnp.zeros_like(state_ref)

    zg = _dot(lr_ref[...], wg_ref[...]) + bg_ref[...]
    la = (jnp.minimum(zg, 0.0) - jnp.log1p(jnp.exp(-jnp.abs(zg)))) / GLA_TAU
    la_ref[...] = la
    chunk_tot = jnp.sum(la.reshape(n_chunks, c_len, hk), axis=1)
    fast = jnp.min(chunk_tot) > -fast_max_decay

    row = lax.broadcasted_iota(jnp.int32, (c_len, c_len), 0)
    col = lax.broadcasted_iota(jnp.int32, (c_len, c_len), 1)
    tril = (row >= col).astype(BF16)
    sr = lax.broadcasted_iota(jnp.int32, (hv, hk), 0)
    sc = lax.broadcasted_iota(jnp.int32, (hv, hk), 1)
    state_mask = ((sr // GLA_DV) == (sc // GLA_DK)).astype(F32)
    lane_k = lax.broadcasted_iota(jnp.int32, (1, hk), 1) // GLA_DK
    lane_v = lax.broadcasted_iota(jnp.int32, (1, hv), 1) // GLA_DV
    r4 = lax.broadcasted_iota(jnp.int32, (GLA_HEADS * c_len, c_len), 0) % c_len
    c4 = lax.broadcasted_iota(jnp.int32, (GLA_HEADS * c_len, c_len), 1)
    causal4 = r4 >= c4
    er = lax.broadcasted_iota(jnp.int32, (hk, hv), 0) // GLA_DK
    ec = lax.broadcasted_iota(jnp.int32, (hk, hv), 1) // GLA_DV
    head_expand = (er == ec).astype(BF16)
    trow = lax.broadcasted_iota(jnp.int32, (c_len, hv), 0)

    def chunk(c, carry):
        r0 = pl.multiple_of(c * c_len, c_len)
        rows = pl.ds(r0, c_len)
        q = q_ref[rows, :].astype(F32) * (GLA_DK ** -0.5)
        k = k_ref[rows, :].astype(F32)
        v = v_ref[rows, :]
        la_c = la_ref[rows, :]
        la_hi, la_lo = _split_hi_lo(la_c)
        cum = _dot(tril, la_hi) + _dot(tril, la_lo)
        last = cum[c_len - 1:c_len, :]

        @pl.when(fast)
        def _():
            mid = cum[c_len // 2:c_len // 2 + 1, :]
            qa = q * jnp.exp(cum - mid)
            kb = (k * jnp.exp(mid - cum)).astype(BF16)
            qa4 = jnp.concatenate(
                [jnp.where(lane_k == h, qa, 0.0) for h in range(GLA_HEADS)], axis=0).astype(BF16)
            s4 = jnp.where(causal4, _dot_nt(qa4, kb), 0.0)
            pv = _dot(s4.astype(BF16), v)
            acc = jnp.zeros((c_len, hv), F32)
            for h in range(GLA_HEADS):
                acc = acc + jnp.where(lane_v == h, pv[h * c_len:(h + 1) * c_len, :], 0.0)
            intra_ref[...] = acc

        @pl.when(jnp.logical_not(fast))
        def _():
            cum_ref[...] = cum
            kf_ref[...] = k
            vf_ref[...] = v.astype(F32)

            def one_key(s, acc):
                ks = kf_ref[pl.ds(s, 1), :]
                vs = vf_ref[pl.ds(s, 1), :]
                cs = cum_ref[pl.ds(s, 1), :]
                w = q * ks * jnp.exp(jnp.minimum(cum - cs, 0.0))
                wsum = _dot(w.astype(BF16), head_expand)
                return acc + jnp.where(trow >= s, wsum, 0.0) * vs

            intra_ref[...] = lax.fori_loop(0, c_len, one_key, jnp.zeros((c_len, hv), F32))

        st = state_ref[...]
        qg = (q * jnp.exp(cum)).astype(BF16)
        inter = _dot_nt(qg, st.astype(BF16))
        oacc_ref[rows, :] = intra_ref[...] + inter
        kg = (k * jnp.exp(last - cum)).astype(BF16)
        upd = _dot_tn(v, kg)
        state_ref[...] = (st * jnp.exp(last) + upd) * state_mask
        return carry

    lax.fori_loop(0, n_chunks, chunk, 0)

    o = oacc_ref[...]
    vr = lax.broadcasted_iota(jnp.int32, (hv, hv), 0) // GLA_DV
    vc = lax.broadcasted_iota(jnp.int32, (hv, hv), 1) // GLA_DV
    seg = (vr == vc).astype(BF16)
    ms = _seg_sum(o * o, seg) * (1.0 / GLA_DV)
    y = o * lax.rsqrt(ms + HEAD_NORM_EPS) * ng_ref[...]
    g = og_ref[...].astype(F32)
    o_ref[...] = (y * (g * jax.nn.sigmoid(g))).astype(o_ref.dtype)


def _gla(z, w_gate, b_gate, norm_g, batch, seq, tg=512, fast_max_decay=GLA_FAST_MAX_DECAY):
    tg = min(tg, seq)
    nt = seq // tg
    hk = GLA_HEADS * GLA_DK
    hv = GLA_HEADS * GLA_DV

    def rowmap(cb):
        return lambda b, t: (b * nt + t, cb)

    const = lambda b, t: (0, 0)
    return pl.pallas_call(
        functools.partial(_gla_kernel, n_chunks=tg // GLA_CHUNK, fast_max_decay=fast_max_decay),
        grid=(batch, nt),
        in_specs=[pl.BlockSpec((tg, hk), rowmap(_GQ // hk)),
                  pl.BlockSpec((tg, hk), rowmap(_GK // hk)),
                  pl.BlockSpec((tg, hv), rowmap(_GV // hv)),
                  pl.BlockSpec((tg, hv), rowmap(_GO // hv)),
                  pl.BlockSpec((tg, LANE), rowmap(_LR // LANE)),
                  pl.BlockSpec((LANE, hk), const),
                  pl.BlockSpec((1, hk), const),
                  pl.BlockSpec((1, hv), const)],
        out_specs=pl.BlockSpec((tg, hv), lambda b, t: (b * nt + t, 0)),
        out_shape=jax.ShapeDtypeStruct((batch * seq, hv), BF16),
        scratch_shapes=[pltpu.VMEM((hv, hk), F32),
                        pltpu.VMEM((tg, hk), F32),
                        pltpu.VMEM((GLA_CHUNK, hk), F32),
                        pltpu.VMEM((GLA_CHUNK, hk), F32),
                        pltpu.VMEM((GLA_CHUNK, hv), F32),
                        pltpu.VMEM((tg, hv), F32),
                        pltpu.VMEM((GLA_CHUNK, hv), F32)],
        compiler_params=_cparams(("arbitrary", "arbitrary")),
        name="gla",
    )(z, z, z, z, z, w_gate, b_gate, norm_g)


def _diff_kernel(far_ref, lam_ref, q_ref, k_ref, v_ref, bias_ref, ng_ref, o_ref,
                 m_ref, l_ref, acc_ref, *, blk, lam_init):
    qi = pl.program_id(2)
    d = DIFF_D
    q = q_ref[...] * (DIFF_D ** -0.5)
    qm = (q[:, :d], q[:, d:])
    m_ref[...] = jnp.full_like(m_ref, -jnp.inf)
    l_ref[...] = jnp.zeros_like(l_ref)
    acc_ref[...] = jnp.zeros_like(acc_ref)

    def kv_step(j, bias, shift):
        rows = pl.ds(pl.multiple_of(j * blk, blk), blk)
        kj = k_ref[rows, :]
        vj = v_ref[rows, :]
        for m in range(2):
            s = _dot_nt(qm[m], kj[:, m * d:(m + 1) * d])
            if bias is not None:
                s = s + bias
            m_cur = jnp.max(s, axis=-1, keepdims=True)
            if shift is not None:
                m_cur = m_cur + shift
            m_old = m_ref[m]
            m_new = jnp.maximum(m_old, m_cur)
            alpha = jnp.exp(m_old - m_new)
            sub = m_new if shift is None else m_new - shift
            p = jnp.exp(s - pltpu.repeat(sub, blk // LANE, axis=1))
            l_ref[m] = alpha * l_ref[m] + jnp.sum(p, axis=-1, keepdims=True)
            acc_ref[m] = alpha * acc_ref[m] + _dot(p.astype(BF16), vj)
            m_ref[m] = m_new

    far = far_ref[pl.program_id(1)]

    def far_step(j, carry):
        kv_step(j, None, far)
        return carry

    lax.fori_loop(0, jnp.maximum(qi - 1, 0), far_step, 0)

    @pl.when(qi >= 1)
    def _():
        kv_step(qi - 1, bias_ref[0, 1], None)

    kv_step(qi, bias_ref[0, 0], None)

    lf = lam_ref[...]
    lam = (jnp.exp(jnp.sum(lf[0:1] * lf[1:2], axis=-1, keepdims=True))
           - jnp.exp(jnp.sum(lf[2:3] * lf[3:4], axis=-1, keepdims=True)) + lam_init)
    o = acc_ref[0] / l_ref[0] - lam * (acc_ref[1] / l_ref[1])
    ms = jnp.mean(o * o, axis=-1, keepdims=True)
    y = o * lax.rsqrt(ms + HEAD_NORM_EPS) * ng_ref[...] * (1.0 - lam_init)
    o_ref[...] = y.astype(o_ref.dtype)


def _t5_bucket(rel):
    n = jnp.maximum(-rel, 0)
    max_exact = T5_BUCKETS // 2
    nf = jnp.maximum(n, 1).astype(F32)
    large = max_exact + (jnp.log(nf / max_exact) / math.log(T5_MAX_DIST / max_exact)
                         * (T5_BUCKETS - max_exact)).astype(jnp.int32)
    large = jnp.minimum(large, T5_BUCKETS - 1)
    return jnp.where(n < max_exact, n, large)


def _bias_tiles(rel_bias, blk):
    assert blk + 1 >= T5_MAX_DIST
    i = jnp.arange(blk)[:, None]
    j = jnp.arange(blk)[None, :]
    tiles = []
    for delta in range(3):
        rel = j - (i + delta * blk)
        t = rel_bias[_t5_bucket(rel)].astype(F32)
        if delta == 0:
            t = jnp.where((rel <= 0)[:, :, None], t, -jnp.inf)
        tiles.append(jnp.transpose(t, (2, 0, 1)))
    return jnp.stack(tiles[:2], axis=1), tiles[2][:, 0, 0]


def _diff(z, diff_lambda, norm_g, bias, batch, seq, lam_init, blk=256):
    blk = min(blk, seq)
    nq = seq // blk
    w = 2 * DIFF_D
    bias_tiles, bias_far = bias
    return pl.pallas_call(
        functools.partial(_diff_kernel, blk=blk, lam_init=lam_init),
        grid=(batch, DIFF_HEADS, nq),
        in_specs=[pl.BlockSpec(memory_space=pltpu.SMEM),
                  pl.BlockSpec((4, DIFF_D), lambda b, h, i: (0, 0)),
                  pl.BlockSpec((blk, w), lambda b, h, i: (b * nq + i, _DQ // w + h)),
                  pl.BlockSpec((seq, w), lambda b, h, i: (b, _DK // w + h)),
                  pl.BlockSpec((seq, w), lambda b, h, i: (b, _DV // w + h)),
                  pl.BlockSpec((1, 2, blk, blk), lambda b, h, i: (h, 0, 0, 0)),
                  pl.BlockSpec((1, w), lambda b, h, i: (0, 0))],
        out_specs=pl.BlockSpec((blk, w), lambda b, h, i: (b * nq + i, h)),
        out_shape=jax.ShapeDtypeStruct((batch * seq, DIFF_HEADS * w), BF16),
        scratch_shapes=[pltpu.VMEM((2, blk, LANE), F32),
                        pltpu.VMEM((2, blk, LANE), F32),
                        pltpu.VMEM((2, blk, w), F32)],
        compiler_params=_cparams(("arbitrary", "arbitrary", "arbitrary")),
        name="diff_attn",
    )(bias_far, diff_lambda, z, z, z, bias_tiles, norm_g)


def _ret_kernel(q_ref, k_ref, v_ref, g_ref, sin_ref, cos_ref, inner_ref, cross_ref, sdec_ref, cdec_ref,
                o_ref, state_ref):
    c_len = RET_CHUNK
    hk = RET_HEADS * RET_DK
    hv = RET_HEADS * RET_DV

    @pl.when(pl.program_id(1) == 0)
    def _():
        state_ref[...] = jnp.zeros_like(state_ref)

    sin = sin_ref[...]
    cos = cos_ref[...]
    even = (lax.broadcasted_iota(jnp.int32, (1, hk), 1) % 2) == 0

    def rotary(t):
        nxt = pltpu.roll(t, hk - 1, 1)
        prv = pltpu.roll(t, 1, 1)
        return t * cos + jnp.where(even, -nxt, prv) * sin

    qf = rotary(q_ref[...].astype(F32))
    kf = rotary(k_ref[...].astype(F32)) * (RET_DK ** -0.5)
    v = v_ref[...]

    lane_k = lax.broadcasted_iota(jnp.int32, (1, hk), 1) // RET_DK
    lane_v = lax.broadcasted_iota(jnp.int32, (1, hv), 1) // RET_DV
    q4 = jnp.concatenate([jnp.where(lane_k == h, qf, 0.0) for h in range(RET_HEADS)], axis=0).astype(BF16)
    s4 = _dot_nt(q4, kf.astype(BF16)) * inner_ref[...]
    pv = _dot(s4.astype(BF16), v)
    acc = jnp.zeros((c_len, hv), F32)
    for h in range(RET_HEADS):
        acc = acc + jnp.where(lane_v == h, pv[h * c_len:(h + 1) * c_len, :], 0.0)

    st = state_ref[...]
    o = acc + _dot_nt(qf.astype(BF16), st.astype(BF16)) * cross_ref[...]
    ks = (kf * sdec_ref[...]).astype(BF16)
    sr = lax.broadcasted_iota(jnp.int32, (hv, hk), 0) // RET_DV
    sc = lax.broadcasted_iota(jnp.int32, (hv, hk), 1) // RET_DK
    state_ref[...] = jnp.where(sr == sc, st * cdec_ref[...] + _dot_tn(v, ks), 0.0)

    vr = lax.broadcasted_iota(jnp.int32, (hv, hv), 0) // RET_DV
    vc = lax.broadcasted_iota(jnp.int32, (hv, hv), 1) // RET_DV
    seg = (vr == vc).astype(BF16)
    ms = _seg_sum(o * o, seg) * (1.0 / RET_DV)
    g = g_ref[...].astype(F32)
    o_ref[...] = (o * lax.rsqrt(ms + HEAD_NORM_EPS) * (g * jax.nn.sigmoid(g))).astype(o_ref.dtype)


def _ret_tables(seq):
    c_len = RET_CHUNK
    pos = jnp.arange(seq, dtype=F32)
    angle = 1.0 / (10000.0 ** jnp.linspace(0.0, 1.0, RET_DK // 2, dtype=F32))
    angle = jnp.repeat(angle, 2)
    sin = jnp.tile(jnp.sin(pos[:, None] * angle), (1, RET_HEADS))
    cos = jnp.tile(jnp.cos(pos[:, None] * angle), (1, RET_HEADS))
    log_g = jnp.log1p(-jnp.exp2(-5.0 - jnp.arange(RET_HEADS, dtype=F32)))
    idx = jnp.arange(c_len, dtype=F32)
    rel = idx[:, None] - idx[None, :]
    inner = jnp.where(rel[None] >= 0, jnp.exp(jnp.maximum(rel, 0.0)[None] * log_g[:, None, None]), 0.0)
    inner = inner.reshape(RET_HEADS * c_len, c_len)
    cross = jnp.exp((idx + 1.0)[None] * log_g[:, None])
    cross = jnp.repeat(cross.T, RET_DV, axis=1)
    sdec = jnp.exp((c_len - 1.0 - idx)[None] * log_g[:, None])
    sdec = jnp.repeat(sdec.T, RET_DK, axis=1)
    cdec = jnp.repeat(jnp.exp(c_len * log_g), RET_DK)[None, :]
    return sin, cos, inner, cross, sdec, cdec


def _ret(z, tables, batch, seq):
    c_len = RET_CHUNK
    nt = seq // c_len
    hk = RET_HEADS * RET_DK
    hv = RET_HEADS * RET_DV
    sin, cos, inner, cross, sdec, cdec = tables

    def rowmap(cb):
        return lambda b, t: (b * nt + t, cb)

    const = lambda b, t: (0, 0)
    return pl.pallas_call(
        _ret_kernel,
        grid=(batch, nt),
        in_specs=[pl.BlockSpec((c_len, hk), rowmap(_RQ // hk)),
                  pl.BlockSpec((c_len, hk), rowmap(_RK // hk)),
                  pl.BlockSpec((c_len, hv), rowmap(_RV // hv)),
                  pl.BlockSpec((c_len, hv), rowmap(_RG // hv)),
                  pl.BlockSpec((c_len, hk), lambda b, t: (t, 0)),
                  pl.BlockSpec((c_len, hk), lambda b, t: (t, 0)),
                  pl.BlockSpec((RET_HEADS * c_len, c_len), const),
                  pl.BlockSpec((c_len, hv), const),
                  pl.BlockSpec((c_len, hk), const),
                  pl.BlockSpec((1, hk), const)],
        out_specs=pl.BlockSpec((c_len, hv), lambda b, t: (b * nt + t, 0)),
        out_shape=jax.ShapeDtypeStruct((batch * seq, hv), BF16),
        scratch_shapes=[pltpu.VMEM((hv, hk), F32)],
        compiler_params=_cparams(("arbitrary", "arbitrary")),
        name="retention",
    )(z, z, z, z, sin, cos, inner, cross, sdec, cdec)


def _layer_norm(t, g, b):
    mu = jnp.mean(t, axis=-1, keepdims=True)
    var = jnp.mean(jnp.square(t - mu), axis=-1, keepdims=True)
    return (t - mu) * lax.rsqrt(var + LN_EPS) * g + b


def _out_ln_kernel(gla_ref, dif_ref, ret_ref, x_ref, wo_ref, g_ref, b_ref, wr_hi_ref, wr_lo_ref, br_ref,
                   x1_ref, x1b_ref, gate_ref, idx_ref, *, alpha):
    w0 = gla_ref.shape[1]
    w1 = w0 + dif_ref.shape[1]
    a = (_dot(gla_ref[...], wo_ref[0:w0, :]) + _dot(dif_ref[...], wo_ref[w0:w1, :])
         + _dot(ret_ref[...], wo_ref[w1:, :]))
    x1 = _layer_norm(alpha * x_ref[...] + a, g_ref[...], b_ref[...])
    x1_ref[...] = x1
    x1b_ref[...] = x1.astype(BF16)

    hi, lo = _split_hi_lo(x1)
    logits = (_dot(hi, wr_hi_ref[...]) + _dot(lo, wr_hi_ref[...]) + _dot(hi, wr_lo_ref[...])) + br_ref[...]
    lane = lax.broadcasted_iota(jnp.int32, logits.shape, 1)
    work = jnp.where(lane < N_EXPERTS, logits, -jnp.inf)
    vals = jnp.zeros(logits.shape, F32)
    idxs = jnp.zeros(logits.shape, jnp.int32)
    top0 = None
    for r in range(TOP_K):
        mx = jnp.max(work, axis=-1, keepdims=True)
        am = jnp.min(jnp.where(work == mx, lane, LANE), axis=-1, keepdims=True)
        if r == 0:
            top0 = mx
        vals = jnp.where(lane == r, jnp.exp(mx - top0), vals)
        idxs = jnp.where(lane == r, am, idxs)
        work = jnp.where(lane == am, -jnp.inf, work)
    gate_ref[...] = vals / jnp.sum(vals, axis=-1, keepdims=True)
    idx_ref[...] = idxs


def _out_ln(gla_o, dif_o, ret_o, x, w_out, ln_g, ln_b, wr_hi, wr_lo, b_router, alpha, tm=512):
    n, d = x.shape
    tm = min(tm, n)
    row = lambda i: (i, 0)
    const = lambda i: (0, 0)
    return pl.pallas_call(
        functools.partial(_out_ln_kernel, alpha=alpha),
        grid=(n // tm,),
        in_specs=[pl.BlockSpec((tm, gla_o.shape[1]), row),
                  pl.BlockSpec((tm, dif_o.shape[1]), row),
                  pl.BlockSpec((tm, ret_o.shape[1]), row),
                  pl.BlockSpec((tm, d), row),
                  pl.BlockSpec(w_out.shape, const),
                  pl.BlockSpec((1, d), const),
                  pl.BlockSpec((1, d), const),
                  pl.BlockSpec((d, LANE), const),
                  pl.BlockSpec((d, LANE), const),
                  pl.BlockSpec((1, LANE), const)],
        out_specs=[pl.BlockSpec((tm, d), row),
                   pl.BlockSpec((tm, d), row),
                   pl.BlockSpec((tm, LANE), row),
                   pl.BlockSpec((tm, LANE), row)],
        out_shape=[jax.ShapeDtypeStruct((n, d), F32),
                   jax.ShapeDtypeStruct((n, d), BF16),
                   jax.ShapeDtypeStruct((n, LANE), F32),
                   jax.ShapeDtypeStruct((n, LANE), jnp.int32)],
        compiler_params=_cparams(("arbitrary",)),
        name="out_ln_router",
    )(gla_o, dif_o, ret_o, x, w_out, ln_g, ln_b, wr_hi, wr_lo, b_router)


GU_GROUP = 2 * LANE


def _deinterleave_kernel(w_ref, perm_ref, o_ref):
    w = w_ref[...].astype(BF16)
    for g in range(w.shape[1] // GU_GROUP):
        cols = slice(g * GU_GROUP, (g + 1) * GU_GROUP)
        o_ref[:, cols] = _dot(w[:, cols], perm_ref[...]).astype(o_ref.dtype)


def _deinterleave_perm():
    perm = np.zeros((GU_GROUP, GU_GROUP), np.float32)
    half = np.arange(LANE)
    perm[2 * half, half] = 1.0
    perm[2 * half + 1, LANE + half] = 1.0
    return jnp.asarray(perm, BF16)


def _deinterleave_gate_up(w, tm=512):
    rows = math.prod(w.shape[:-1])
    cols = w.shape[-1]
    out = pl.pallas_call(
        _deinterleave_kernel,
        grid=(rows // tm,),
        in_specs=[pl.BlockSpec((tm, cols), lambda i: (i, 0)),
                  pl.BlockSpec((GU_GROUP, GU_GROUP), lambda i: (0, 0))],
        out_specs=pl.BlockSpec((tm, cols), lambda i: (i, 0)),
        out_shape=jax.ShapeDtypeStruct((rows, cols), BF16),
        compiler_params=_cparams(("arbitrary",)),
        name="deinterleave_gate_up",
    )(w.reshape(rows, cols), _deinterleave_perm())
    return out.reshape(w.shape)


def _moe_kernel(blk_e_ref, n_used_ref, x_ref, wgu_ref, bgu_ref, wdn_ref, bdn_ref, o_ref):
    i = pl.program_id(0)

    @pl.when(i < n_used_ref[0])
    def _():
        gu = _dot(x_ref[...], wgu_ref[...]) + bgu_ref[...]
        acts = []
        for grp in range(gu.shape[1] // GU_GROUP):
            g = jnp.minimum(gu[:, grp * GU_GROUP:grp * GU_GROUP + LANE], SWIGLU_LIMIT)
            u = jnp.clip(gu[:, grp * GU_GROUP + LANE:(grp + 1) * GU_GROUP], -SWIGLU_LIMIT, SWIGLU_LIMIT)
            acts.append(((u + 1.0) * (g * jax.nn.sigmoid(g * SWIGLU_ALPHA))).astype(BF16))
        act = jnp.concatenate(acts, axis=1)
        o_ref[...] = (_dot(act, wdn_ref[...]) + bdn_ref[...]).astype(o_ref.dtype)

    @pl.when(i >= n_used_ref[0])
    def _():
        o_ref[...] = jnp.zeros_like(o_ref)


def _moe(xs, blk_e, n_used, w_gu, b_gu, w_dn, b_dn, blk):
    rows, d = xs.shape
    f2 = w_gu.shape[2]
    f = w_dn.shape[1]
    grid_spec = pltpu.PrefetchScalarGridSpec(
        num_scalar_prefetch=2,
        grid=(rows // blk,),
        in_specs=[pl.BlockSpec((blk, d), lambda i, be, nu: (i, 0)),
                  pl.BlockSpec((None, d, f2), lambda i, be, nu: (be[i], 0, 0)),
                  pl.BlockSpec((None, 1, f2), lambda i, be, nu: (be[i], 0, 0)),
                  pl.BlockSpec((None, f, d), lambda i, be, nu: (be[i], 0, 0)),
                  pl.BlockSpec((None, 1, d), lambda i, be, nu: (be[i], 0, 0))],
        out_specs=pl.BlockSpec((blk, d), lambda i, be, nu: (i, 0)),
    )
    return pl.pallas_call(
        _moe_kernel,
        grid_spec=grid_spec,
        out_shape=jax.ShapeDtypeStruct((rows, d), BF16),
        compiler_params=_cparams(("arbitrary",)),
        name="moe_experts",
    )(blk_e, n_used, xs, w_gu, b_gu, w_dn, b_dn)


def _combine_kernel(yg_ref, gate_ref, x1_ref, x1b_ref, p_ref, wg_ref, bg_ref, wp_ref, g_ref, b_ref,
                    x2_ref, x2b_ref, *, alpha):
    d = x1_ref.shape[1]
    gates = gate_ref[...]
    m = jnp.zeros(x1_ref.shape, F32)
    for r in range(TOP_K):
        m = m + yg_ref[r].astype(F32) * gates[:, r:r + 1]
    e = (jax.nn.sigmoid(_dot(x1b_ref[...], wg_ref[...]) + bg_ref[...])
         * _dot(p_ref[...].astype(BF16), wp_ref[...]))
    x2 = _layer_norm(alpha * x1_ref[...] + m + e, g_ref[...], b_ref[...])
    x2_ref[...] = x2
    x2b_ref[...] = x2.astype(BF16)


def _combine(yg, gates, x1, x1b, p, w_pg, b_pg, w_pp, ln_g, ln_b, alpha, tm=512):
    n, d = x1.shape
    tm = min(tm, n)
    row = lambda i: (i, 0)
    const = lambda i: (0, 0)
    return pl.pallas_call(
        functools.partial(_combine_kernel, alpha=alpha),
        grid=(n // tm,),
        in_specs=[pl.BlockSpec((TOP_K, tm, d), lambda i: (0, i, 0)),
                  pl.BlockSpec((tm, LANE), row),
                  pl.BlockSpec((tm, d), row),
                  pl.BlockSpec((tm, d), row),
                  pl.BlockSpec((tm, p.shape[1]), row),
                  pl.BlockSpec(w_pg.shape, const),
                  pl.BlockSpec((1, d), const),
                  pl.BlockSpec(w_pp.shape, const),
                  pl.BlockSpec((1, d), const),
                  pl.BlockSpec((1, d), const)],
        out_specs=[pl.BlockSpec((tm, d), row), pl.BlockSpec((tm, d), row)],
        out_shape=[jax.ShapeDtypeStruct((n, d), F32), jax.ShapeDtypeStruct((n, d), BF16)],
        compiler_params=_cparams(("arbitrary",)),
        name="combine_ln",
    )(yg, gates, x1, x1b, p, w_pg, b_pg, w_pp, ln_g, ln_b)


def _route(idx, n_tok, blk):
    e_flat = idx.reshape(-1)
    n_assign = e_flat.shape[0]
    onehot = (e_flat[:, None] == jnp.arange(N_EXPERTS, dtype=jnp.int32)[None, :]).astype(jnp.int32)
    rank = jnp.take_along_axis(jnp.cumsum(onehot, axis=0), e_flat[:, None], axis=1)[:, 0] - 1
    sizes = jnp.sum(onehot, axis=0)
    padded = ((sizes + blk - 1) // blk) * blk
    pends = jnp.cumsum(padded)
    pstarts = pends - padded
    dest = pstarts[e_flat] + rank
    buf_len = ((n_assign + N_EXPERTS * (blk - 1) + blk - 1) // blk) * blk
    n_blk = buf_len // blk
    tok = jnp.arange(n_assign, dtype=jnp.int32) // TOP_K
    buf_tok = jnp.zeros((buf_len,), jnp.int32).at[dest].set(tok)
    blk_e = jnp.clip(jnp.searchsorted(pends, jnp.arange(n_blk, dtype=jnp.int32) * blk, side='right'),
                     0, N_EXPERTS - 1).astype(jnp.int32)
    n_used = (pends[-1] // blk).astype(jnp.int32).reshape(1)
    return buf_tok, dest.reshape(n_tok, TOP_K), blk_e, n_used


def _regroup_w_in(w_in):
    gq, gk, gv, glr, gout, rest = jnp.split(w_in, [128, 256, 512, 528, 784], axis=-1)
    pad = jnp.zeros(w_in.shape[:-1] + (Z_WIDTH - w_in.shape[-1],), w_in.dtype)
    return jnp.concatenate([gq, gk, gv, gout, rest, glr, pad], axis=-1)


def kernel(x, p, w_in, w_gla_gate, b_gla_gate, gla_norm_g, diff_lambda, diff_norm_g, w_out, rel_bias, ln1_g, ln1_b, w_router, b_router, w_gate_up, b_gate_up, w_down, b_down, w_ple_gate, b_ple_gate, w_ple_proj, ln2_g, ln2_b):
    batch, seq, d = x.shape
    depth = w_in.shape[0]
    n = batch * seq
    alpha = (2 * depth) ** 0.25
    moe_blk = 256
    diff_blk = min(256, seq)

    w_in_b = _regroup_w_in(w_in).astype(BF16)
    wg = jnp.zeros((depth, LANE, GLA_HEADS * GLA_DK), F32).at[:, :GLA_GATE_RANK, :].set(w_gla_gate).astype(BF16)
    gla_g = jnp.tile(gla_norm_g, (1, GLA_HEADS))[:, None, :]
    w_out_b = w_out.astype(BF16)
    wr = jnp.zeros((depth, d, LANE), F32).at[:, :, :N_EXPERTS].set(w_router)
    wr_hi = wr.astype(BF16)
    wr_lo = (wr - wr_hi.astype(F32)).astype(BF16)
    br = jnp.zeros((depth, 1, LANE), F32).at[:, 0, :N_EXPERTS].set(b_router)
    w_gu_b = _deinterleave_gate_up(w_gate_up)
    b_gu = b_gate_up.reshape(depth, N_EXPERTS, -1, LANE, 2).swapaxes(-1, -2).reshape(depth, N_EXPERTS, 1, -1)
    w_dn_b = w_down.astype(BF16)
    b_dn = b_down[:, :, None, :]
    w_pg_b = w_ple_gate.astype(BF16)
    w_pp_b = w_ple_proj.astype(BF16)
    bias_tiles = _bias_tiles(rel_bias, diff_blk)
    ret_tables = _ret_tables(seq)

    xf = x.reshape(n, d)
    xb = xf.astype(BF16)
    pf = p.reshape(depth, n, p.shape[-1])
    for i in range(depth):
        lam_init = 0.8 - 0.6 * math.exp(-0.3 * i)
        z = _matmul(xb, w_in_b[i], min(1024, n), Z_WIDTH // 3, BF16)
        gla_o = _gla(z, wg[i], b_gla_gate[i][None, :], gla_g[i], batch, seq)
        dif_o = _diff(z, diff_lambda[i], diff_norm_g[i][None, :], bias_tiles, batch, seq, lam_init, diff_blk)
        ret_o = _ret(z, ret_tables, batch, seq)
        x1, x1b, gates, idx = _out_ln(gla_o, dif_o, ret_o, xf, w_out_b[i], ln1_g[i][None, :], ln1_b[i][None, :],
                                      wr_hi[i], wr_lo[i], br[i], alpha)
        buf_tok, slot, blk_e, n_used = _route(idx[:, :TOP_K], n, moe_blk)
        xs = x1b[buf_tok]
        yb = _moe(xs, blk_e, n_used, w_gu_b[i], b_gu[i], w_dn_b[i], b_dn[i], moe_blk)
        yg = yb[slot.T]
        xf, xb = _combine(yg, gates, x1, x1b, pf[i], w_pg_b[i], b_ple_gate[i][None, :], w_pp_b[i],
                          ln2_g[i][None, :], ln2_b[i][None, :], alpha)
    return xf.reshape(batch, seq, d)
```

```python
import functools
import math

import numpy as np
import jax
import jax.numpy as jnp
from jax import lax
from jax.experimental import pallas as pl
from jax.experimental.pallas import tpu as pltpu

F32 = jnp.float32
BF16 = jnp.bfloat16

GLA_HEADS = 4
GLA_DK = 32
GLA_DV = 64
GLA_GATE_RANK = 16
GLA_TAU = 16.0
GLA_CHUNK = 64
DIFF_HEADS = 4
DIFF_D = 64
RET_HEADS = 4
RET_DK = 64
RET_DV = 64
RET_CHUNK = 128
T5_BUCKETS = 32
T5_MAX_DIST = 128
N_EXPERTS = 32
TOP_K = 4
SWIGLU_LIMIT = 7.0
SWIGLU_ALPHA = 1.702
LN_EPS = 1e-5
HEAD_NORM_EPS = 1e-5

LANE = 128
VMEM_LIMIT = 56 * 1024 * 1024

_GQ, _GK, _GV, _GO = 0, 128, 256, 512
_DQ, _DK, _DV = 768, 1280, 1792
_RQ, _RK, _RV, _RG = 2304, 2560, 2816, 3072
_LR = 3328
Z_WIDTH = 3456

GLA_FAST_MAX_DECAY = 60.0


def _cparams(sem):
    return pltpu.CompilerParams(dimension_semantics=sem, vmem_limit_bytes=VMEM_LIMIT)


def _split_hi_lo(a):
    hi = a.astype(BF16)
    lo = (a - hi.astype(F32)).astype(BF16)
    return hi, lo


def _dot(a, b):
    return jnp.dot(a, b, preferred_element_type=F32)


def _dot_nt(a, b):
    return lax.dot_general(a, b, (((1,), (1,)), ((), ())), preferred_element_type=F32)


def _dot_tn(a, b):
    return lax.dot_general(a, b, (((0,), (0,)), ((), ())), preferred_element_type=F32)


def _seg_sum(a, ind):
    hi, lo = _split_hi_lo(a)
    return _dot(hi, ind) + _dot(lo, ind)


def _mm_kernel(x_ref, w_ref, o_ref):
    o_ref[...] = _dot(x_ref[...], w_ref[...]).astype(o_ref.dtype)


def _matmul(x, w, tm, tn, out_dtype):
    m, k = x.shape
    n = w.shape[1]
    return pl.pallas_call(
        _mm_kernel,
        grid=(n // tn, m // tm),
        in_specs=[pl.BlockSpec((tm, k), lambda j, i: (i, 0)),
                  pl.BlockSpec((k, tn), lambda j, i: (0, j))],
        out_specs=pl.BlockSpec((tm, tn), lambda j, i: (i, j)),
        out_shape=jax.ShapeDtypeStruct((m, n), out_dtype),
        compiler_params=_cparams(("arbitrary", "arbitrary")),
        name="in_proj",
    )(x, w)


def _gla_kernel(q_ref, k_ref, v_ref, og_ref, lr_ref, wg_ref, bg_ref, ng_ref, o_ref,
                state_ref, la_ref, cum_ref, kf_ref, vf_ref, oacc_ref, intra_ref, *, n_chunks, fast_max_decay):
    c_len = GLA_CHUNK
    hk = GLA_HEADS * GLA_DK
    hv = GLA_HEADS * GLA_DV

    @pl.when(pl.program_id(1) == 0)
    def _():
        state_ref[...] = jnp.zeros_like(state_ref)

    zg = _dot(lr_ref[...], wg_ref[...]) + bg_ref[...]
    la = (jnp.minimum(zg, 0.0) - jnp.log1p(jnp.exp(-jnp.abs(zg)))) / GLA_TAU
    la_ref[...] = la
    chunk_tot = jnp.sum(la.reshape(n_chunks, c_len, hk), axis=1)
    fast = jnp.min(chunk_tot) > -fast_max_decay

    row = lax.broadcasted_iota(jnp.int32, (c_len, c_len), 0)
    col = lax.broadcasted_iota(jnp.int32, (c_len, c_len), 1)
    tril = (row >= col).astype(BF16)
    sr = lax.broadcasted_iota(jnp.int32, (hv, hk), 0)
    sc = lax.broadcasted_iota(jnp.int32, (hv, hk), 1)
    state_mask = ((sr // GLA_DV) == (sc // GLA_DK)).astype(F32)
    lane_k = lax.broadcasted_iota(jnp.int32, (1, hk), 1) // GLA_DK
    lane_v = lax.broadcasted_iota(jnp.int32, (1, hv), 1) // GLA_DV
    r4 = lax.broadcasted_iota(jnp.int32, (GLA_HEADS * c_len, c_len), 0) % c_len
    c4 = lax.broadcasted_iota(jnp.int32, (GLA_HEADS * c_len, c_len), 1)
    causal4 = r4 >= c4
    er = lax.broadcasted_iota(jnp.int32, (hk, hv), 0) // GLA_DK
    ec = lax.broadcasted_iota(jnp.int32, (hk, hv), 1) // GLA_DV
    head_expand = (er == ec).astype(BF16)
    trow = lax.broadcasted_iota(jnp.int32, (c_len, hv), 0)

    def chunk(c, carry):
        r0 = pl.multiple_of(c * c_len, c_len)
        rows = pl.ds(r0, c_len)
        q = q_ref[rows, :].astype(F32) * (GLA_DK ** -0.5)
        k = k_ref[rows, :].astype(F32)
        v = v_ref[rows, :]
        la_c = la_ref[rows, :]
        la_hi, la_lo = _split_hi_lo(la_c)
        cum = _dot(tril, la_hi) + _dot(tril, la_lo)
        last = cum[c_len - 1:c_len, :]

        @pl.when(fast)
        def _():
            mid = cum[c_len // 2:c_len // 2 + 1, :]
            qa = q * jnp.exp(cum - mid)
            kb = (k * jnp.exp(mid - cum)).astype(BF16)
            qa4 = jnp.concatenate(
                [jnp.where(lane_k == h, qa, 0.0) for h in range(GLA_HEADS)], axis=0).astype(BF16)
            s4 = jnp.where(causal4, _dot_nt(qa4, kb), 0.0)
            pv = _dot(s4.astype(BF16), v)
            acc = jnp.zeros((c_len, hv), F32)
            for h in range(GLA_HEADS):
                acc = acc + jnp.where(lane_v == h, pv[h * c_len:(h + 1) * c_len, :], 0.0)
            intra_ref[...] = acc

        @pl.when(jnp.logical_not(fast))
        def _():
            cum_ref[...] = cum
            kf_ref[...] = k
            vf_ref[...] = v.astype(F32)

            def one_key(s, acc):
                ks = kf_ref[pl.ds(s, 1), :]
                vs = vf_ref[pl.ds(s, 1), :]
                cs = cum_ref[pl.ds(s, 1), :]
                w = q * ks * jnp.exp(jnp.minimum(cum - cs, 0.0))
                wsum = _dot(w.astype(BF16), head_expand)
                return acc + jnp.where(trow >= s, wsum, 0.0) * vs

            intra_ref[...] = lax.fori_loop(0, c_len, one_key, jnp.zeros((c_len, hv), F32))

        st = state_ref[...]
        qg = (q * jnp.exp(cum)).astype(BF16)
        inter = _dot_nt(qg, st.astype(BF16))
        oacc_ref[rows, :] = intra_ref[...] + inter
        kg = (k * jnp.exp(last - cum)).astype(BF16)
        upd = _dot_tn(v, kg)
        state_ref[...] = (st * jnp.exp(last) + upd) * state_mask
        return carry

    lax.fori_loop(0, n_chunks, chunk, 0)

    o = oacc_ref[...]
    vr = lax.broadcasted_iota(jnp.int32, (hv, hv), 0) // GLA_DV
    vc = lax.broadcasted_iota(jnp.int32, (hv, hv), 1) // GLA_DV
    seg = (vr == vc).astype(BF16)
    ms = _seg_sum(o * o, seg) * (1.0 / GLA_DV)
    y = o * lax.rsqrt(ms + HEAD_NORM_EPS) * ng_ref[...]
    g = og_ref[...].astype(F32)
    o_ref[...] = (y * (g * jax.nn.sigmoid(g))).astype(o_ref.dtype)


def _gla(z, w_gate, b_gate, norm_g, batch, seq, tg=512, fast_max_decay=GLA_FAST_MAX_DECAY):
    tg = min(tg, seq)
    nt = seq // tg
    hk = GLA_HEADS * GLA_DK
    hv = GLA_HEADS * GLA_DV

    def rowmap(cb):
        return lambda b, t: (b * nt + t, cb)

    const = lambda b, t: (0, 0)
    return pl.pallas_call(
        functools.partial(_gla_kernel, n_chunks=tg // GLA_CHUNK, fast_max_decay=fast_max_decay),
        grid=(batch, nt),
        in_specs=[pl.BlockSpec((tg, hk), rowmap(_GQ // hk)),
                  pl.BlockSpec((tg, hk), rowmap(_GK // hk)),
                  pl.BlockSpec((tg, hv), rowmap(_GV // hv)),
                  pl.BlockSpec((tg, hv), rowmap(_GO // hv)),
                  pl.BlockSpec((tg, LANE), rowmap(_LR // LANE)),
                  pl.BlockSpec((LANE, hk), const),
                  pl.BlockSpec((1, hk), const),
                  pl.BlockSpec((1, hv), const)],
        out_specs=pl.BlockSpec((tg, hv), lambda b, t: (b * nt + t, 0)),
        out_shape=jax.ShapeDtypeStruct((batch * seq, hv), BF16),
        scratch_shapes=[pltpu.VMEM((hv, hk), F32),
                        pltpu.VMEM((tg, hk), F32),
                        pltpu.VMEM((GLA_CHUNK, hk), F32),
                        pltpu.VMEM((GLA_CHUNK, hk), F32),
                        pltpu.VMEM((GLA_CHUNK, hv), F32),
                        pltpu.VMEM((tg, hv), F32),
                        pltpu.VMEM((GLA_CHUNK, hv), F32)],
        compiler_params=_cparams(("arbitrary", "arbitrary")),
        name="gla",
    )(z, z, z, z, z, w_gate, b_gate, norm_g)


def _diff_kernel(far_ref, lam_ref, q_ref, k_ref, v_ref, bias_ref, ng_ref, o_ref,
                 m_ref, l_ref, acc_ref, *, blk, lam_init):
    qi = pl.program_id(2)
    d = DIFF_D
    q = q_ref[...] * (DIFF_D ** -0.5)
    qm = (q[:, :d], q[:, d:])
    m_ref[...] = jnp.full_like(m_ref, -jnp.inf)
    l_ref[...] = jnp.zeros_like(l_ref)
    acc_ref[...] = jnp.zeros_like(acc_ref)

    def kv_step(j, bias, shift):
        rows = pl.ds(pl.multiple_of(j * blk, blk), blk)
        kj = k_ref[rows, :]
        vj = v_ref[rows, :]
        for m in range(2):
            s = _dot_nt(qm[m], kj[:, m * d:(m + 1) * d])
            if bias is not None:
                s = s + bias
            m_cur = jnp.max(s, axis=-1, keepdims=True)
            if shift is not None:
                m_cur = m_cur + shift
            m_old = m_ref[m]
            m_new = jnp.maximum(m_old, m_cur)
            alpha = jnp.exp(m_old - m_new)
            sub = m_new if shift is None else m_new - shift
            p = jnp.exp(s - jnp.concatenate([sub] * (blk // LANE), axis=1))
            l_ref[m] = alpha * l_ref[m] + jnp.sum(p, axis=-1, keepdims=True)
            acc_ref[m] = alpha * acc_ref[m] + _dot(p.astype(BF16), vj)
            m_ref[m] = m_new

    far = far_ref[pl.program_id(1)]

    def far_step(j, carry):
        kv_step(j, None, far)
        return carry

    lax.fori_loop(0, jnp.maximum(qi - 1, 0), far_step, 0)

    @pl.when(qi >= 1)
    def _():
        kv_step(qi - 1, bias_ref[0, 1], None)

    kv_step(qi, bias_ref[0, 0], None)

    lf = lam_ref[...]
    lam = (jnp.exp(jnp.sum(lf[0:1] * lf[1:2], axis=-1, keepdims=True))
           - jnp.exp(jnp.sum(lf[2:3] * lf[3:4], axis=-1, keepdims=True)) + lam_init)
    o = acc_ref[0] / l_ref[0] - lam * (acc_ref[1] / l_ref[1])
    ms = jnp.mean(o * o, axis=-1, keepdims=True)
    y = o * lax.rsqrt(ms + HEAD_NORM_EPS) * ng_ref[...] * (1.0 - lam_init)
    o_ref[...] = y.astype(o_ref.dtype)


def _t5_bucket(rel):
    n = jnp.maximum(-rel, 0)
    max_exact = T5_BUCKETS // 2
    nf = jnp.maximum(n, 1).astype(F32)
    large = max_exact + (jnp.log(nf / max_exact) / math.log(T5_MAX_DIST / max_exact)
                         * (T5_BUCKETS - max_exact)).astype(jnp.int32)
    large = jnp.minimum(large, T5_BUCKETS - 1)
    return jnp.where(n < max_exact, n, large)


def _bias_tiles(rel_bias, blk):
    assert blk + 1 >= T5_MAX_DIST
    i = jnp.arange(blk)[:, None]
    j = jnp.arange(blk)[None, :]
    tiles = []
    for delta in range(3):
        rel = j - (i + delta * blk)
        t = rel_bias[_t5_bucket(rel)].astype(F32)
        if delta == 0:
            t = jnp.where((rel <= 0)[:, :, None], t, -jnp.inf)
        tiles.append(jnp.transpose(t, (2, 0, 1)))
    return jnp.stack(tiles[:2], axis=1), tiles[2][:, 0, 0]


def _diff(z, diff_lambda, norm_g, bias, batch, seq, lam_init, blk):
    blk = min(blk, seq)
    nq = seq // blk
    w = 2 * DIFF_D
    bias_tiles, bias_far = bias
    return pl.pallas_call(
        functools.partial(_diff_kernel, blk=blk, lam_init=lam_init),
        grid=(batch, DIFF_HEADS, nq),
        in_specs=[pl.BlockSpec(memory_space=pltpu.SMEM),
                  pl.BlockSpec((4, DIFF_D), lambda b, h, i: (0, 0)),
                  pl.BlockSpec((blk, w), lambda b, h, i: (b * nq + i, _DQ // w + h)),
                  pl.BlockSpec((seq, w), lambda b, h, i: (b, _DK // w + h)),
                  pl.BlockSpec((seq, w), lambda b, h, i: (b, _DV // w + h)),
                  pl.BlockSpec((1, 2, blk, blk), lambda b, h, i: (h, 0, 0, 0)),
                  pl.BlockSpec((1, w), lambda b, h, i: (0, 0))],
        out_specs=pl.BlockSpec((blk, w), lambda b, h, i: (b * nq + i, h)),
        out_shape=jax.ShapeDtypeStruct((batch * seq, DIFF_HEADS * w), BF16),
        scratch_shapes=[pltpu.VMEM((2, blk, LANE), F32),
                        pltpu.VMEM((2, blk, LANE), F32),
                        pltpu.VMEM((2, blk, w), F32)],
        compiler_params=_cparams(("arbitrary", "arbitrary", "arbitrary")),
        name="diff_attn",
    )(bias_far, diff_lambda, z, z, z, bias_tiles, norm_g)


def _ret_kernel(q_ref, k_ref, v_ref, g_ref, sin_ref, cos_ref, inner_ref, cross_ref, sdec_ref, cdec_ref,
                o_ref, state_ref):
    c_len = RET_CHUNK
    hk = RET_HEADS * RET_DK
    hv = RET_HEADS * RET_DV

    @pl.when(pl.program_id(1) == 0)
    def _():
        state_ref[...] = jnp.zeros_like(state_ref)

    sin = sin_ref[...]
    cos = cos_ref[...]
    even = (lax.broadcasted_iota(jnp.int32, (1, hk), 1) % 2) == 0

    def rotary(t):
        nxt = pltpu.roll(t, hk - 1, 1)
        prv = pltpu.roll(t, 1, 1)
        return t * cos + jnp.where(even, -nxt, prv) * sin

    qf = rotary(q_ref[...].astype(F32))
    kf = rotary(k_ref[...].astype(F32)) * (RET_DK ** -0.5)
    v = v_ref[...]

    lane_k = lax.broadcasted_iota(jnp.int32, (1, hk), 1) // RET_DK
    lane_v = lax.broadcasted_iota(jnp.int32, (1, hv), 1) // RET_DV
    q4 = jnp.concatenate([jnp.where(lane_k == h, qf, 0.0) for h in range(RET_HEADS)], axis=0).astype(BF16)
    s4 = _dot_nt(q4, kf.astype(BF16)) * inner_ref[...]
    pv = _dot(s4.astype(BF16), v)
    acc = jnp.zeros((c_len, hv), F32)
    for h in range(RET_HEADS):
        acc = acc + jnp.where(lane_v == h, pv[h * c_len:(h + 1) * c_len, :], 0.0)

    st = state_ref[...]
    o = acc + _dot_nt(qf.astype(BF16), st.astype(BF16)) * cross_ref[...]
    ks = (kf * sdec_ref[...]).astype(BF16)
    sr = lax.broadcasted_iota(jnp.int32, (hv, hk), 0) // RET_DV
    sc = lax.broadcasted_iota(jnp.int32, (hv, hk), 1) // RET_DK
    state_ref[...] = jnp.where(sr == sc, st * cdec_ref[...] + _dot_tn(v, ks), 0.0)

    vr = lax.broadcasted_iota(jnp.int32, (hv, hv), 0) // RET_DV
    vc = lax.broadcasted_iota(jnp.int32, (hv, hv), 1) // RET_DV
    seg = (vr == vc).astype(BF16)
    ms = _seg_sum(o * o, seg) * (1.0 / RET_DV)
    g = g_ref[...].astype(F32)
    o_ref[...] = (o * lax.rsqrt(ms + HEAD_NORM_EPS) * (g * jax.nn.sigmoid(g))).astype(o_ref.dtype)


def _ret_tables(seq):
    c_len = RET_CHUNK
    pos = jnp.arange(seq, dtype=F32)
    angle = 1.0 / (10000.0 ** jnp.linspace(0.0, 1.0, RET_DK // 2, dtype=F32))
    angle = jnp.repeat(angle, 2)
    sin = jnp.tile(jnp.sin(pos[:, None] * angle), (1, RET_HEADS))
    cos = jnp.tile(jnp.cos(pos[:, None] * angle), (1, RET_HEADS))
    log_g = jnp.log1p(-jnp.exp2(-5.0 - jnp.arange(RET_HEADS, dtype=F32)))
    idx = jnp.arange(c_len, dtype=F32)
    rel = idx[:, None] - idx[None, :]
    inner = jnp.where(rel[None] >= 0, jnp.exp(jnp.maximum(rel, 0.0)[None] * log_g[:, None, None]), 0.0)
    inner = inner.reshape(RET_HEADS * c_len, c_len)
    cross = jnp.exp((idx + 1.0)[None] * log_g[:, None])
    cross = jnp.repeat(cross.T, RET_DV, axis=1)
    sdec = jnp.exp((c_len - 1.0 - idx)[None] * log_g[:, None])
    sdec = jnp.repeat(sdec.T, RET_DK, axis=1)
    cdec = jnp.repeat(jnp.exp(c_len * log_g), RET_DK)[None, :]
    return sin, cos, inner, cross, sdec, cdec


def _ret(z, tables, batch, seq):
    c_len = RET_CHUNK
    nt = seq // c_len
    hk = RET_HEADS * RET_DK
    hv = RET_HEADS * RET_DV
    sin, cos, inner, cross, sdec, cdec = tables

    def rowmap(cb):
        return lambda b, t: (b * nt + t, cb)

    const = lambda b, t: (0, 0)
    return pl.pallas_call(
        _ret_kernel,
        grid=(batch, nt),
        in_specs=[pl.BlockSpec((c_len, hk), rowmap(_RQ // hk)),
                  pl.BlockSpec((c_len, hk), rowmap(_RK // hk)),
                  pl.BlockSpec((c_len, hv), rowmap(_RV // hv)),
                  pl.BlockSpec((c_len, hv), rowmap(_RG // hv)),
                  pl.BlockSpec((c_len, hk), lambda b, t: (t, 0)),
                  pl.BlockSpec((c_len, hk), lambda b, t: (t, 0)),
                  pl.BlockSpec((RET_HEADS * c_len, c_len), const),
                  pl.BlockSpec((c_len, hv), const),
                  pl.BlockSpec((c_len, hk), const),
                  pl.BlockSpec((1, hk), const)],
        out_specs=pl.BlockSpec((c_len, hv), lambda b, t: (b * nt + t, 0)),
        out_shape=jax.ShapeDtypeStruct((batch * seq, hv), BF16),
        scratch_shapes=[pltpu.VMEM((hv, hk), F32)],
        compiler_params=_cparams(("arbitrary", "arbitrary")),
        name="retention",
    )(z, z, z, z, sin, cos, inner, cross, sdec, cdec)


def _layer_norm(t, g, b):
    mu = jnp.mean(t, axis=-1, keepdims=True)
    var = jnp.mean(jnp.square(t - mu), axis=-1, keepdims=True)
    return (t - mu) * lax.rsqrt(var + LN_EPS) * g + b


def _out_ln_kernel(gla_ref, dif_ref, ret_ref, x_ref, wo_ref, g_ref, b_ref, wr_hi_ref, wr_lo_ref, br_ref,
                   x1_ref, x1b_ref, gate_ref, idx_ref, cnt_ref, *, alpha):
    w0 = gla_ref.shape[1]
    w1 = w0 + dif_ref.shape[1]
    a = (_dot(gla_ref[...], wo_ref[0:w0, :]) + _dot(dif_ref[...], wo_ref[w0:w1, :])
         + _dot(ret_ref[...], wo_ref[w1:, :]))
    x1 = _layer_norm(alpha * x_ref[...] + a, g_ref[...], b_ref[...])
    x1_ref[...] = x1
    x1b_ref[...] = x1.astype(BF16)

    hi, lo = _split_hi_lo(x1)
    logits = (_dot(hi, wr_hi_ref[...]) + _dot(lo, wr_hi_ref[...]) + _dot(hi, wr_lo_ref[...])) + br_ref[...]
    lane = lax.broadcasted_iota(jnp.int32, logits.shape, 1)
    work = jnp.where(lane < N_EXPERTS, logits, -jnp.inf)
    vals = jnp.zeros(logits.shape, F32)
    idxs = jnp.zeros(logits.shape, jnp.int32)
    chosen = jnp.zeros(logits.shape, F32)
    picks = []
    top0 = None
    for r in range(TOP_K):
        mx = jnp.max(work, axis=-1, keepdims=True)
        am = jnp.min(jnp.where(work == mx, lane, LANE), axis=-1, keepdims=True)
        if r == 0:
            top0 = mx
        vals = jnp.where(lane == r, jnp.exp(mx - top0), vals)
        idxs = jnp.where(lane == r, am, idxs)
        chosen = jnp.where(lane == am, 1.0, chosen)
        work = jnp.where(lane == am, -jnp.inf, work)
        picks.append(am)
    gate_ref[...] = vals / jnp.sum(vals, axis=-1, keepdims=True)

    @pl.when(pl.program_id(0) == 0)
    def _():
        cnt_ref[...] = jnp.zeros_like(cnt_ref)

    tm = logits.shape[0]
    earlier = (lax.broadcasted_iota(jnp.int32, (tm, tm), 0)
               > lax.broadcasted_iota(jnp.int32, (tm, tm), 1)).astype(BF16)
    before = _dot(earlier, chosen.astype(BF16)) + cnt_ref[...]
    for r in range(TOP_K):
        rank = jnp.sum(jnp.where(lane == picks[r], before, 0.0), axis=-1, keepdims=True)
        idxs = jnp.where(lane == TOP_K + r, rank.astype(jnp.int32), idxs)
    idx_ref[...] = idxs
    cnt_ref[...] = cnt_ref[...] + jnp.sum(chosen, axis=0, keepdims=True)


def _out_ln(gla_o, dif_o, ret_o, x, w_out, ln_g, ln_b, wr_hi, wr_lo, b_router, alpha, tm=512):
    n, d = x.shape
    tm = min(tm, n)
    row = lambda i: (i, 0)
    const = lambda i: (0, 0)
    return pl.pallas_call(
        functools.partial(_out_ln_kernel, alpha=alpha),
        grid=(n // tm,),
        in_specs=[pl.BlockSpec((tm, gla_o.shape[1]), row),
                  pl.BlockSpec((tm, dif_o.shape[1]), row),
                  pl.BlockSpec((tm, ret_o.shape[1]), row),
                  pl.BlockSpec((tm, d), row),
                  pl.BlockSpec(w_out.shape, const),
                  pl.BlockSpec((1, d), const),
                  pl.BlockSpec((1, d), const),
                  pl.BlockSpec((d, LANE), const),
                  pl.BlockSpec((d, LANE), const),
                  pl.BlockSpec((1, LANE), const)],
        out_specs=[pl.BlockSpec((tm, d), row),
                   pl.BlockSpec((tm, d), row),
                   pl.BlockSpec((tm, LANE), row),
                   pl.BlockSpec((tm, LANE), row),
                   pl.BlockSpec((1, LANE), const)],
        out_shape=[jax.ShapeDtypeStruct((n, d), F32),
                   jax.ShapeDtypeStruct((n, d), BF16),
                   jax.ShapeDtypeStruct((n, LANE), F32),
                   jax.ShapeDtypeStruct((n, LANE), jnp.int32),
                   jax.ShapeDtypeStruct((1, LANE), F32)],
        compiler_params=_cparams(("arbitrary",)),
        name="out_ln_router",
    )(gla_o, dif_o, ret_o, x, w_out, ln_g, ln_b, wr_hi, wr_lo, b_router)


GU_GROUP = 2 * LANE


def _deinterleave_kernel(w_ref, perm_ref, o_ref):
    w = w_ref[...].astype(BF16)
    for g in range(w.shape[1] // GU_GROUP):
        cols = slice(g * GU_GROUP, (g + 1) * GU_GROUP)
        o_ref[:, cols] = _dot(w[:, cols], perm_ref[...]).astype(o_ref.dtype)


def _deinterleave_perm():
    perm = np.zeros((GU_GROUP, GU_GROUP), np.float32)
    half = np.arange(LANE)
    perm[2 * half, half] = 1.0
    perm[2 * half + 1, LANE + half] = 1.0
    return jnp.asarray(perm, BF16)


def _deinterleave_gate_up(w, tm=512):
    rows = math.prod(w.shape[:-1])
    cols = w.shape[-1]
    out = pl.pallas_call(
        _deinterleave_kernel,
        grid=(rows // tm,),
        in_specs=[pl.BlockSpec((tm, cols), lambda i: (i, 0)),
                  pl.BlockSpec((GU_GROUP, GU_GROUP), lambda i: (0, 0))],
        out_specs=pl.BlockSpec((tm, cols), lambda i: (i, 0)),
        out_shape=jax.ShapeDtypeStruct((rows, cols), BF16),
        compiler_params=_cparams(("arbitrary",)),
        name="deinterleave_gate_up",
    )(w.reshape(rows, cols), _deinterleave_perm())
    return out.reshape(w.shape)


def _moe_kernel(blk_e_ref, n_used_ref, x_ref, wgu_ref, bgu_ref, wdn_ref, bdn_ref, o_ref):
    i = pl.program_id(0)

    @pl.when(i < n_used_ref[0])
    def _():
        gu = _dot(x_ref[...], wgu_ref[...]) + bgu_ref[...]
        acts = []
        for grp in range(gu.shape[1] // GU_GROUP):
            g = jnp.minimum(gu[:, grp * GU_GROUP:grp * GU_GROUP + LANE], SWIGLU_LIMIT)
            u = jnp.clip(gu[:, grp * GU_GROUP + LANE:(grp + 1) * GU_GROUP], -SWIGLU_LIMIT, SWIGLU_LIMIT)
            acts.append(((u + 1.0) * (g * jax.nn.sigmoid(g * SWIGLU_ALPHA))).astype(BF16))
        act = jnp.concatenate(acts, axis=1)
        o_ref[...] = (_dot(act, wdn_ref[...]) + bdn_ref[...]).astype(o_ref.dtype)

    @pl.when(i >= n_used_ref[0])
    def _():
        o_ref[...] = jnp.zeros_like(o_ref)


def _moe(xs, blk_e, n_used, w_gu, b_gu, w_dn, b_dn, blk):
    rows, d = xs.shape
    f2 = w_gu.shape[2]
    f = w_dn.shape[1]
    grid_spec = pltpu.PrefetchScalarGridSpec(
        num_scalar_prefetch=2,
        grid=(rows // blk,),
        in_specs=[pl.BlockSpec((blk, d), lambda i, be, nu: (i, 0)),
                  pl.BlockSpec((None, d, f2), lambda i, be, nu: (be[i], 0, 0)),
                  pl.BlockSpec((None, 1, f2), lambda i, be, nu: (be[i], 0, 0)),
                  pl.BlockSpec((None, f, d), lambda i, be, nu: (be[i], 0, 0)),
                  pl.BlockSpec((None, 1, d), lambda i, be, nu: (be[i], 0, 0))],
        out_specs=pl.BlockSpec((blk, d), lambda i, be, nu: (i, 0)),
    )
    return pl.pallas_call(
        _moe_kernel,
        grid_spec=grid_spec,
        out_shape=jax.ShapeDtypeStruct((rows, d), BF16),
        compiler_params=_cparams(("arbitrary",)),
        name="moe_experts",
    )(blk_e, n_used, xs, w_gu, b_gu, w_dn, b_dn)


def _combine_kernel(yg_ref, gate_ref, x1_ref, x1b_ref, p_ref, wg_ref, bg_ref, wp_ref, g_ref, b_ref,
                    x2_ref, x2b_ref, *, alpha):
    d = x1_ref.shape[1]
    gates = gate_ref[...]
    m = jnp.zeros(x1_ref.shape, F32)
    for r in range(TOP_K):
        m = m + yg_ref[r].astype(F32) * gates[:, r:r + 1]
    e = (jax.nn.sigmoid(_dot(x1b_ref[...], wg_ref[...]) + bg_ref[...])
         * _dot(p_ref[...].astype(BF16), wp_ref[...]))
    x2 = _layer_norm(alpha * x1_ref[...] + m + e, g_ref[...], b_ref[...])
    x2_ref[...] = x2
    x2b_ref[...] = x2.astype(BF16)


def _combine(yg, gates, x1, x1b, p, w_pg, b_pg, w_pp, ln_g, ln_b, alpha, tm=512):
    n, d = x1.shape
    tm = min(tm, n)
    row = lambda i: (i, 0)
    const = lambda i: (0, 0)
    return pl.pallas_call(
        functools.partial(_combine_kernel, alpha=alpha),
        grid=(n // tm,),
        in_specs=[pl.BlockSpec((TOP_K, tm, d), lambda i: (0, i, 0)),
                  pl.BlockSpec((tm, LANE), row),
                  pl.BlockSpec((tm, d), row),
                  pl.BlockSpec((tm, d), row),
                  pl.BlockSpec((tm, p.shape[1]), row),
                  pl.BlockSpec(w_pg.shape, const),
                  pl.BlockSpec((1, d), const),
                  pl.BlockSpec(w_pp.shape, const),
                  pl.BlockSpec((1, d), const),
                  pl.BlockSpec((1, d), const)],
        out_specs=[pl.BlockSpec((tm, d), row), pl.BlockSpec((tm, d), row)],
        out_shape=[jax.ShapeDtypeStruct((n, d), F32), jax.ShapeDtypeStruct((n, d), BF16)],
        compiler_params=_cparams(("arbitrary",)),
        name="combine_ln",
    )(yg, gates, x1, x1b, p, w_pg, b_pg, w_pp, ln_g, ln_b)


def _route(experts, ranks, counts, blk):
    n_tok = experts.shape[0]
    n_assign = n_tok * TOP_K
    sizes = counts.astype(jnp.int32)
    padded = ((sizes + blk - 1) // blk) * blk
    pends = jnp.cumsum(padded)
    pstarts = pends - padded
    starts = jnp.cumsum(sizes) - sizes
    onehot = experts[:, :, None] == jnp.arange(N_EXPERTS, dtype=jnp.int32)
    slot = jnp.sum(jnp.where(onehot, pstarts, 0), axis=-1) + ranks
    keys = experts * n_assign + jnp.arange(n_assign, dtype=jnp.int32).reshape(n_tok, TOP_K)
    sorted_tok = (jnp.sort(keys.reshape(-1)) % n_assign) // TOP_K
    sorted_tok = jnp.concatenate([sorted_tok, jnp.zeros((blk,), jnp.int32)])
    buf_len = ((n_assign + N_EXPERTS * (blk - 1) + blk - 1) // blk) * blk
    n_blk = buf_len // blk
    blk_e = jnp.clip(jnp.searchsorted(pends, jnp.arange(n_blk, dtype=jnp.int32) * blk, side='right'),
                     0, N_EXPERTS - 1).astype(jnp.int32)
    src0 = starts[blk_e] + jnp.arange(n_blk, dtype=jnp.int32) * blk - pstarts[blk_e]
    src0 = jnp.clip(src0, 0, n_assign)
    buf_tok = jax.vmap(lambda s0: lax.dynamic_slice(sorted_tok, (s0,), (blk,)))(src0).reshape(-1)
    n_used = (pends[-1] // blk).astype(jnp.int32).reshape(1)
    return buf_tok, slot, blk_e, n_used


def _regroup_w_in(w_in):
    gq, gk, gv, glr, gout, rest = jnp.split(w_in, [128, 256, 512, 528, 784], axis=-1)
    pad = jnp.zeros(w_in.shape[:-1] + (Z_WIDTH - w_in.shape[-1],), w_in.dtype)
    return jnp.concatenate([gq, gk, gv, gout, rest, glr, pad], axis=-1)


def kernel(x, p, w_in, w_gla_gate, b_gla_gate, gla_norm_g, diff_lambda, diff_norm_g, w_out, rel_bias, ln1_g, ln1_b, w_router, b_router, w_gate_up, b_gate_up, w_down, b_down, w_ple_gate, b_ple_gate, w_ple_proj, ln2_g, ln2_b):
    batch, seq, d = x.shape
    depth = w_in.shape[0]
    n = batch * seq
    alpha = (2 * depth) ** 0.25
    moe_blk = 256
    diff_blk = min(512, seq)

    w_in_b = _regroup_w_in(w_in).astype(BF16)
    wg = jnp.zeros((depth, LANE, GLA_HEADS * GLA_DK), F32).at[:, :GLA_GATE_RANK, :].set(w_gla_gate).astype(BF16)
    gla_g = jnp.tile(gla_norm_g, (1, GLA_HEADS))[:, None, :]
    w_out_b = w_out.astype(BF16)
    wr = jnp.zeros((depth, d, LANE), F32).at[:, :, :N_EXPERTS].set(w_router)
    wr_hi = wr.astype(BF16)
    wr_lo = (wr - wr_hi.astype(F32)).astype(BF16)
    br = jnp.zeros((depth, 1, LANE), F32).at[:, 0, :N_EXPERTS].set(b_router)
    w_gu_b = _deinterleave_gate_up(w_gate_up)
    b_gu = b_gate_up.reshape(depth, N_EXPERTS, -1, LANE, 2).swapaxes(-1, -2).reshape(depth, N_EXPERTS, 1, -1)
    w_dn_b = w_down.astype(BF16)
    b_dn = b_down[:, :, None, :]
    w_pg_b = w_ple_gate.astype(BF16)
    w_pp_b = w_ple_proj.astype(BF16)
    bias_tiles = _bias_tiles(rel_bias, diff_blk)
    ret_tables = _ret_tables(seq)

    xf = x.reshape(n, d)
    xb = xf.astype(BF16)
    pf = p.reshape(depth, n, p.shape[-1])
    for i in range(depth):
        lam_init = 0.8 - 0.6 * math.exp(-0.3 * i)
        z = _matmul(xb, w_in_b[i], min(1024, n), Z_WIDTH // 3, BF16)
        gla_o = _gla(z, wg[i], b_gla_gate[i][None, :], gla_g[i], batch, seq)
        dif_o = _diff(z, diff_lambda[i], diff_norm_g[i][None, :], bias_tiles, batch, seq, lam_init, diff_blk)
        ret_o = _ret(z, ret_tables, batch, seq)
        x1, x1b, gates, idx, cnt = _out_ln(gla_o, dif_o, ret_o, xf, w_out_b[i], ln1_g[i][None, :],
                                           ln1_b[i][None, :], wr_hi[i], wr_lo[i], br[i], alpha)
        buf_tok, slot, blk_e, n_used = _route(idx[:, :TOP_K], idx[:, TOP_K:2 * TOP_K], cnt[0, :N_EXPERTS], moe_blk)
        xs = x1b[buf_tok]
        yb = _moe(xs, blk_e, n_used, w_gu_b[i], b_gu[i], w_dn_b[i], b_dn[i], moe_blk)
        yg = yb[slot.T]
        xf, xb = _combine(yg, gates, x1, x1b, pf[i], w_pg_b[i], b_ple_gate[i][None, :], w_pp_b[i],
                          ln2_g[i][None, :], ln2_b[i][None, :], alpha)
    return xf.reshape(batch, seq, d)
```

```python
import functools
import math

import numpy as np
import jax
import jax.numpy as jnp
from jax import lax
from jax.experimental import pallas as pl
from jax.experimental.pallas import tpu as pltpu

F32 = jnp.float32
BF16 = jnp.bfloat16

GLA_HEADS = 4
GLA_DK = 32
GLA_DV = 64
GLA_GATE_RANK = 16
GLA_TAU = 16.0
GLA_CHUNK = 64
DIFF_HEADS = 4
DIFF_D = 64
RET_HEADS = 4
RET_DK = 64
RET_DV = 64
RET_CHUNK = 128
T5_BUCKETS = 32
T5_MAX_DIST = 128
N_EXPERTS = 32
TOP_K = 4
SWIGLU_LIMIT = 7.0
SWIGLU_ALPHA = 1.702
LN_EPS = 1e-5
HEAD_NORM_EPS = 1e-5

LANE = 128
VMEM_LIMIT = 56 * 1024 * 1024

_GQ, _GK, _GV, _GO = 0, 128, 256, 512
_DQ, _DK, _DV = 768, 1280, 1792
_RQ, _RK, _RV, _RG = 2304, 2560, 2816, 3072
_LR = 3328
Z_WIDTH = 3456

GLA_FAST_MAX_DECAY = 60.0


def _cparams(sem):
    return pltpu.CompilerParams(dimension_semantics=sem, vmem_limit_bytes=VMEM_LIMIT)


def _split_hi_lo(a):
    hi = a.astype(BF16)
    lo = (a - hi.astype(F32)).astype(BF16)
    return hi, lo


def _dot(a, b):
    return jnp.dot(a, b, preferred_element_type=F32)


def _dot_nt(a, b):
    return lax.dot_general(a, b, (((1,), (1,)), ((), ())), preferred_element_type=F32)


def _dot_tn(a, b):
    return lax.dot_general(a, b, (((0,), (0,)), ((), ())), preferred_element_type=F32)


def _seg_sum(a, ind):
    hi, lo = _split_hi_lo(a)
    return _dot(hi, ind) + _dot(lo, ind)


def _mm_kernel(x_ref, w_ref, o_ref):
    o_ref[...] = _dot(x_ref[...], w_ref[...]).astype(o_ref.dtype)


def _matmul(x, w, tm, tn, out_dtype):
    m, k = x.shape
    n = w.shape[1]
    return pl.pallas_call(
        _mm_kernel,
        grid=(n // tn, m // tm),
        in_specs=[pl.BlockSpec((tm, k), lambda j, i: (i, 0)),
                  pl.BlockSpec((k, tn), lambda j, i: (0, j))],
        out_specs=pl.BlockSpec((tm, tn), lambda j, i: (i, j)),
        out_shape=jax.ShapeDtypeStruct((m, n), out_dtype),
        compiler_params=_cparams(("arbitrary", "arbitrary")),
        name="in_proj",
    )(x, w)


def _gla_kernel(q_ref, k_ref, v_ref, og_ref, lr_ref, wg_ref, bg_ref, ng_ref, o_ref,
                state_ref, la_ref, cum_ref, kf_ref, vf_ref, oacc_ref, intra_ref, *, n_chunks, fast_max_decay):
    c_len = GLA_CHUNK
    hk = GLA_HEADS * GLA_DK
    hv = GLA_HEADS * GLA_DV

    @pl.when(pl.program_id(1) == 0)
    def _():
        state_ref[...] = jnp.zeros_like(state_ref)

    zg = _dot(lr_ref[...], wg_ref[...]) + bg_ref[...]
    la = (jnp.minimum(zg, 0.0) - jnp.log1p(jnp.exp(-jnp.abs(zg)))) / GLA_TAU
    la_ref[...] = la
    chunk_tot = jnp.sum(la.reshape(n_chunks, c_len, hk), axis=1)
    fast = jnp.min(chunk_tot) > -fast_max_decay

    row = lax.broadcasted_iota(jnp.int32, (c_len, c_len), 0)
    col = lax.broadcasted_iota(jnp.int32, (c_len, c_len), 1)
    tril = (row >= col).astype(BF16)
    sr = lax.broadcasted_iota(jnp.int32, (hv, hk), 0)
    sc = lax.broadcasted_iota(jnp.int32, (hv, hk), 1)
    state_mask = ((sr // GLA_DV) == (sc // GLA_DK)).astype(F32)
    lane_k = lax.broadcasted_iota(jnp.int32, (1, hk), 1) // GLA_DK
    lane_v = lax.broadcasted_iota(jnp.int32, (1, hv), 1) // GLA_DV
    r4 = lax.broadcasted_iota(jnp.int32, (GLA_HEADS * c_len, c_len), 0) % c_len
    c4 = lax.broadcasted_iota(jnp.int32, (GLA_HEADS * c_len, c_len), 1)
    causal4 = r4 >= c4
    er = lax.broadcasted_iota(jnp.int32, (hk, hv), 0) // GLA_DK
    ec = lax.broadcasted_iota(jnp.int32, (hk, hv), 1) // GLA_DV
    head_expand = (er == ec).astype(BF16)
    trow = lax.broadcasted_iota(jnp.int32, (c_len, hv), 0)

    def chunk(c, carry):
        r0 = pl.multiple_of(c * c_len, c_len)
        rows = pl.ds(r0, c_len)
        q = q_ref[rows, :].astype(F32) * (GLA_DK ** -0.5)
        k = k_ref[rows, :].astype(F32)
        v = v_ref[rows, :]
        la_c = la_ref[rows, :]
        la_hi, la_lo = _split_hi_lo(la_c)
        cum = _dot(tril, la_hi) + _dot(tril, la_lo)
        last = cum[c_len - 1:c_len, :]

        @pl.when(fast)
        def _():
            mid = cum[c_len // 2:c_len // 2 + 1, :]
            qa = q * jnp.exp(cum - mid)
            kb = (k * jnp.exp(mid - cum)).astype(BF16)
            qa4 = jnp.concatenate(
                [jnp.where(lane_k == h, qa, 0.0) for h in range(GLA_HEADS)], axis=0).astype(BF16)
            s4 = jnp.where(causal4, _dot_nt(qa4, kb), 0.0)
            pv = _dot(s4.astype(BF16), v)
            acc = jnp.zeros((c_len, hv), F32)
            for h in range(GLA_HEADS):
                acc = acc + jnp.where(lane_v == h, pv[h * c_len:(h + 1) * c_len, :], 0.0)
            intra_ref[...] = acc

        @pl.when(jnp.logical_not(fast))
        def _():
            cum_ref[...] = cum
            kf_ref[...] = k
            vf_ref[...] = v.astype(F32)

            def one_key(s, acc):
                ks = kf_ref[pl.ds(s, 1), :]
                vs = vf_ref[pl.ds(s, 1), :]
                cs = cum_ref[pl.ds(s, 1), :]
                w = q * ks * jnp.exp(jnp.minimum(cum - cs, 0.0))
                wsum = _dot(w.astype(BF16), head_expand)
                return acc + jnp.where(trow >= s, wsum, 0.0) * vs

            intra_ref[...] = lax.fori_loop(0, c_len, one_key, jnp.zeros((c_len, hv), F32))

        st = state_ref[...]
        qg = (q * jnp.exp(cum)).astype(BF16)
        inter = _dot_nt(qg, st.astype(BF16))
        oacc_ref[rows, :] = intra_ref[...] + inter
        kg = (k * jnp.exp(last - cum)).astype(BF16)
        upd = _dot_tn(v, kg)
        state_ref[...] = (st * jnp.exp(last) + upd) * state_mask
        return carry

    lax.fori_loop(0, n_chunks, chunk, 0)

    o = oacc_ref[...]
    vr = lax.broadcasted_iota(jnp.int32, (hv, hv), 0) // GLA_DV
    vc = lax.broadcasted_iota(jnp.int32, (hv, hv), 1) // GLA_DV
    seg = (vr == vc).astype(BF16)
    ms = _seg_sum(o * o, seg) * (1.0 / GLA_DV)
    y = o * lax.rsqrt(ms + HEAD_NORM_EPS) * ng_ref[...]
    g = og_ref[...].astype(F32)
    o_ref[...] = (y * (g * jax.nn.sigmoid(g))).astype(o_ref.dtype)


def _gla(z, w_gate, b_gate, norm_g, batch, seq, tg=512, fast_max_decay=GLA_FAST_MAX_DECAY):
    tg = min(tg, seq)
    nt = seq // tg
    hk = GLA_HEADS * GLA_DK
    hv = GLA_HEADS * GLA_DV

    def rowmap(cb):
        return lambda b, t: (b * nt + t, cb)

    const = lambda b, t: (0, 0)
    return pl.pallas_call(
        functools.partial(_gla_kernel, n_chunks=tg // GLA_CHUNK, fast_max_decay=fast_max_decay),
        grid=(batch, nt),
        in_specs=[pl.BlockSpec((tg, hk), rowmap(_GQ // hk)),
                  pl.BlockSpec((tg, hk), rowmap(_GK // hk)),
                  pl.BlockSpec((tg, hv), rowmap(_GV // hv)),
                  pl.BlockSpec((tg, hv), rowmap(_GO // hv)),
                  pl.BlockSpec((tg, LANE), rowmap(_LR // LANE)),
                  pl.BlockSpec((LANE, hk), const),
                  pl.BlockSpec((1, hk), const),
                  pl.BlockSpec((1, hv), const)],
        out_specs=pl.BlockSpec((tg, hv), lambda b, t: (b * nt + t, 0)),
        out_shape=jax.ShapeDtypeStruct((batch * seq, hv), BF16),
        scratch_shapes=[pltpu.VMEM((hv, hk), F32),
                        pltpu.VMEM((tg, hk), F32),
                        pltpu.VMEM((GLA_CHUNK, hk), F32),
                        pltpu.VMEM((GLA_CHUNK, hk), F32),
                        pltpu.VMEM((GLA_CHUNK, hv), F32),
                        pltpu.VMEM((tg, hv), F32),
                        pltpu.VMEM((GLA_CHUNK, hv), F32)],
        compiler_params=_cparams(("arbitrary", "arbitrary")),
        name="gla",
    )(z, z, z, z, z, w_gate, b_gate, norm_g)


LOG2E = math.log2(math.e)


def _diff_kernel(far_ref, lam_ref, q_ref, k_ref, v_ref, bias_ref, ng_ref, o_ref,
                 m_ref, acc_ref, *, blk, lam_init):
    qi = pl.program_id(2)
    d = DIFF_D
    w = 2 * DIFF_D
    q = (q_ref[...].astype(F32) * (DIFF_D ** -0.5 * LOG2E)).astype(BF16)
    lane_q = lax.broadcasted_iota(jnp.int32, q.shape, 1)
    q2 = jnp.concatenate([jnp.where(lane_q < d, q, jnp.zeros_like(q)),
                          jnp.where(lane_q >= d, q, jnp.zeros_like(q))], axis=0)
    m_ref[...] = jnp.full_like(m_ref, -jnp.inf)
    acc_ref[...] = jnp.zeros_like(acc_ref)

    def kv_step(j, bias, shift):
        rows = pl.ds(pl.multiple_of(j * blk, blk), blk)
        vj = v_ref[rows, :]
        s_both = _dot_nt(q2, k_ref[rows, :])
        for m in range(2):
            s = s_both[m * blk:(m + 1) * blk, :]
            if bias is not None:
                s = s + bias
            m_cur = jnp.max(s, axis=-1, keepdims=True)
            if shift is not None:
                m_cur = m_cur + shift
            m_old = m_ref[m]
            m_new = jnp.maximum(m_old, m_cur)
            alpha = jnp.exp2(m_old - m_new)
            sub = m_new if shift is None else m_new - shift
            p = jnp.exp2(s - jnp.concatenate([sub] * (blk // LANE), axis=1))
            row_sum = jnp.broadcast_to(jnp.sum(p, axis=-1, keepdims=True), (blk, LANE))
            acc_ref[m] = (jnp.concatenate([alpha, alpha], axis=1) * acc_ref[m]
                          + jnp.concatenate([_dot(p.astype(BF16), vj), row_sum], axis=1))
            m_ref[m] = m_new

    far = far_ref[pl.program_id(1)]

    def far_step(j, carry):
        kv_step(j, None, far)
        return carry

    lax.fori_loop(0, jnp.maximum(qi - 1, 0), far_step, 0)

    @pl.when(qi >= 1)
    def _():
        kv_step(qi - 1, bias_ref[0, 1], None)

    kv_step(qi, bias_ref[0, 0], None)

    lf = lam_ref[...]
    lam = (jnp.exp(jnp.sum(lf[0:1] * lf[1:2], axis=-1, keepdims=True))
           - jnp.exp(jnp.sum(lf[2:3] * lf[3:4], axis=-1, keepdims=True)) + lam_init)
    a0 = acc_ref[0]
    a1 = acc_ref[1]
    o = a0[:, :w] / a0[:, w:] - lam * (a1[:, :w] / a1[:, w:])
    ms = jnp.mean(o * o, axis=-1, keepdims=True)
    y = o * lax.rsqrt(ms + HEAD_NORM_EPS) * ng_ref[...] * (1.0 - lam_init)
    o_ref[...] = y.astype(o_ref.dtype)


def _t5_bucket(rel):
    n = jnp.maximum(-rel, 0)
    max_exact = T5_BUCKETS // 2
    nf = jnp.maximum(n, 1).astype(F32)
    large = max_exact + (jnp.log(nf / max_exact) / math.log(T5_MAX_DIST / max_exact)
                         * (T5_BUCKETS - max_exact)).astype(jnp.int32)
    large = jnp.minimum(large, T5_BUCKETS - 1)
    return jnp.where(n < max_exact, n, large)


def _bias_tiles(rel_bias, blk):
    assert blk + 1 >= T5_MAX_DIST
    m = jnp.arange(2 * blk)
    col_minus_row = jnp.where(m < blk, m, m - 2 * blk)
    tiles = []
    for delta in range(2):
        rel = col_minus_row - delta * blk
        w = rel_bias[_t5_bucket(rel)].astype(F32).T
        if delta == 0:
            w = jnp.where(rel <= 0, w, -jnp.inf)
        flat = jnp.tile(w, (1, blk))[:, :blk * (2 * blk - 1)]
        tiles.append(flat.reshape(-1, blk, 2 * blk - 1)[:, :, :blk])
    far = rel_bias[_t5_bucket(jnp.full((1,), -(blk + 1)))[0]].astype(F32)
    return jnp.stack(tiles, axis=1) * LOG2E, far * LOG2E


def _diff(z, diff_lambda, norm_g, bias, batch, seq, lam_init, blk):
    blk = min(blk, seq)
    nq = seq // blk
    w = 2 * DIFF_D
    bias_tiles, bias_far = bias
    return pl.pallas_call(
        functools.partial(_diff_kernel, blk=blk, lam_init=lam_init),
        grid=(batch, DIFF_HEADS, nq),
        in_specs=[pl.BlockSpec(memory_space=pltpu.SMEM),
                  pl.BlockSpec((4, DIFF_D), lambda b, h, i: (0, 0)),
                  pl.BlockSpec((blk, w), lambda b, h, i: (b * nq + i, _DQ // w + h)),
                  pl.BlockSpec((seq, w), lambda b, h, i: (b, _DK // w + h)),
                  pl.BlockSpec((seq, w), lambda b, h, i: (b, _DV // w + h)),
                  pl.BlockSpec((1, 2, blk, blk), lambda b, h, i: (h, 0, 0, 0)),
                  pl.BlockSpec((1, w), lambda b, h, i: (0, 0))],
        out_specs=pl.BlockSpec((blk, w), lambda b, h, i: (b * nq + i, h)),
        out_shape=jax.ShapeDtypeStruct((batch * seq, DIFF_HEADS * w), BF16),
        scratch_shapes=[pltpu.VMEM((2, blk, LANE), F32),
                        pltpu.VMEM((2, blk, w + LANE), F32)],
        compiler_params=_cparams(("arbitrary", "arbitrary", "arbitrary")),
        name="diff_attn",
    )(bias_far, diff_lambda, z, z, z, bias_tiles, norm_g)


def _ret_kernel(q_ref, k_ref, v_ref, g_ref, sin_ref, cos_ref, inner_ref, cross_ref, sdec_ref, cdec_ref,
                o_ref, state_ref):
    c_len = RET_CHUNK
    hk = RET_HEADS * RET_DK
    hv = RET_HEADS * RET_DV

    @pl.when(pl.program_id(1) == 0)
    def _():
        state_ref[...] = jnp.zeros_like(state_ref)

    sin = sin_ref[...]
    cos = cos_ref[...]
    even = (lax.broadcasted_iota(jnp.int32, (1, hk), 1) % 2) == 0

    def rotary(t):
        nxt = pltpu.roll(t, hk - 1, 1)
        prv = pltpu.roll(t, 1, 1)
        return t * cos + jnp.where(even, -nxt, prv) * sin

    qf = rotary(q_ref[...].astype(F32))
    kf = rotary(k_ref[...].astype(F32)) * (RET_DK ** -0.5)
    v = v_ref[...]

    lane_k = lax.broadcasted_iota(jnp.int32, (1, hk), 1) // RET_DK
    lane_v = lax.broadcasted_iota(jnp.int32, (1, hv), 1) // RET_DV
    q4 = jnp.concatenate([jnp.where(lane_k == h, qf, 0.0) for h in range(RET_HEADS)], axis=0).astype(BF16)
    s4 = _dot_nt(q4, kf.astype(BF16)) * inner_ref[...]
    pv = _dot(s4.astype(BF16), v)
    acc = jnp.zeros((c_len, hv), F32)
    for h in range(RET_HEADS):
        acc = acc + jnp.where(lane_v == h, pv[h * c_len:(h + 1) * c_len, :], 0.0)

    st = state_ref[...]
    o = acc + _dot_nt(qf.astype(BF16), st.astype(BF16)) * cross_ref[...]
    ks = (kf * sdec_ref[...]).astype(BF16)
    sr = lax.broadcasted_iota(jnp.int32, (hv, hk), 0) // RET_DV
    sc = lax.broadcasted_iota(jnp.int32, (hv, hk), 1) // RET_DK
    state_ref[...] = jnp.where(sr == sc, st * cdec_ref[...] + _dot_tn(v, ks), 0.0)

    vr = lax.broadcasted_iota(jnp.int32, (hv, hv), 0) // RET_DV
    vc = lax.broadcasted_iota(jnp.int32, (hv, hv), 1) // RET_DV
    seg = (vr == vc).astype(BF16)
    ms = _seg_sum(o * o, seg) * (1.0 / RET_DV)
    g = g_ref[...].astype(F32)
    o_ref[...] = (o * lax.rsqrt(ms + HEAD_NORM_EPS) * (g * jax.nn.sigmoid(g))).astype(o_ref.dtype)


def _ret_tables(seq):
    c_len = RET_CHUNK
    pos = jnp.arange(seq, dtype=F32)
    angle = 1.0 / (10000.0 ** jnp.linspace(0.0, 1.0, RET_DK // 2, dtype=F32))
    angle = jnp.repeat(angle, 2)
    sin = jnp.tile(jnp.sin(pos[:, None] * angle), (1, RET_HEADS))
    cos = jnp.tile(jnp.cos(pos[:, None] * angle), (1, RET_HEADS))
    log_g = jnp.log1p(-jnp.exp2(-5.0 - jnp.arange(RET_HEADS, dtype=F32)))
    idx = jnp.arange(c_len, dtype=F32)
    rel = idx[:, None] - idx[None, :]
    inner = jnp.where(rel[None] >= 0, jnp.exp(jnp.maximum(rel, 0.0)[None] * log_g[:, None, None]), 0.0)
    inner = inner.reshape(RET_HEADS * c_len, c_len)
    cross = jnp.exp((idx + 1.0)[None] * log_g[:, None])
    cross = jnp.repeat(cross.T, RET_DV, axis=1)
    sdec = jnp.exp((c_len - 1.0 - idx)[None] * log_g[:, None])
    sdec = jnp.repeat(sdec.T, RET_DK, axis=1)
    cdec = jnp.repeat(jnp.exp(c_len * log_g), RET_DK)[None, :]
    return sin, cos, inner, cross, sdec, cdec


def _ret(z, tables, batch, seq):
    c_len = RET_CHUNK
    nt = seq // c_len
    hk = RET_HEADS * RET_DK
    hv = RET_HEADS * RET_DV
    sin, cos, inner, cross, sdec, cdec = tables

    def rowmap(cb):
        return lambda b, t: (b * nt + t, cb)

    const = lambda b, t: (0, 0)
    return pl.pallas_call(
        _ret_kernel,
        grid=(batch, nt),
        in_specs=[pl.BlockSpec((c_len, hk), rowmap(_RQ // hk)),
                  pl.BlockSpec((c_len, hk), rowmap(_RK // hk)),
                  pl.BlockSpec((c_len, hv), rowmap(_RV // hv)),
                  pl.BlockSpec((c_len, hv), rowmap(_RG // hv)),
                  pl.BlockSpec((c_len, hk), lambda b, t: (t, 0)),
                  pl.BlockSpec((c_len, hk), lambda b, t: (t, 0)),
                  pl.BlockSpec((RET_HEADS * c_len, c_len), const),
                  pl.BlockSpec((c_len, hv), const),
                  pl.BlockSpec((c_len, hk), const),
                  pl.BlockSpec((1, hk), const)],
        out_specs=pl.BlockSpec((c_len, hv), lambda b, t: (b * nt + t, 0)),
        out_shape=jax.ShapeDtypeStruct((batch * seq, hv), BF16),
        scratch_shapes=[pltpu.VMEM((hv, hk), F32)],
        compiler_params=_cparams(("arbitrary", "arbitrary")),
        name="retention",
    )(z, z, z, z, sin, cos, inner, cross, sdec, cdec)


def _layer_norm(t, g, b):
    mu = jnp.mean(t, axis=-1, keepdims=True)
    var = jnp.mean(jnp.square(t - mu), axis=-1, keepdims=True)
    return (t - mu) * lax.rsqrt(var + LN_EPS) * g + b


def _out_ln_kernel(gla_ref, dif_ref, ret_ref, x_ref, wo_ref, g_ref, b_ref, wr_hi_ref, wr_lo_ref, br_ref,
                   x1_ref, x1b_ref, gate_ref, idx_ref, cnt_ref, *, alpha):
    w0 = gla_ref.shape[1]
    w1 = w0 + dif_ref.shape[1]
    a = (_dot(gla_ref[...], wo_ref[0:w0, :]) + _dot(dif_ref[...], wo_ref[w0:w1, :])
         + _dot(ret_ref[...], wo_ref[w1:, :]))
    x1 = _layer_norm(alpha * x_ref[...] + a, g_ref[...], b_ref[...])
    x1_ref[...] = x1
    x1b_ref[...] = x1.astype(BF16)

    hi, lo = _split_hi_lo(x1)
    logits = (_dot(hi, wr_hi_ref[...]) + _dot(lo, wr_hi_ref[...]) + _dot(hi, wr_lo_ref[...])) + br_ref[...]
    lane = lax.broadcasted_iota(jnp.int32, logits.shape, 1)
    work = jnp.where(lane < N_EXPERTS, logits, -jnp.inf)
    vals = jnp.zeros(logits.shape, F32)
    idxs = jnp.zeros(logits.shape, jnp.int32)
    chosen = jnp.zeros(logits.shape, F32)
    picks = []
    top0 = None
    for r in range(TOP_K):
        mx = jnp.max(work, axis=-1, keepdims=True)
        am = jnp.min(jnp.where(work == mx, lane, LANE), axis=-1, keepdims=True)
        if r == 0:
            top0 = mx
        vals = jnp.where(lane == r, jnp.exp(mx - top0), vals)
        idxs = jnp.where(lane == r, am, idxs)
        chosen = jnp.where(lane == am, 1.0, chosen)
        work = jnp.where(lane == am, -jnp.inf, work)
        picks.append(am)
    gate_ref[...] = vals / jnp.sum(vals, axis=-1, keepdims=True)

    @pl.when(pl.program_id(0) == 0)
    def _():
        cnt_ref[...] = jnp.zeros_like(cnt_ref)

    tm = logits.shape[0]
    earlier = (lax.broadcasted_iota(jnp.int32, (tm, tm), 0)
               > lax.broadcasted_iota(jnp.int32, (tm, tm), 1)).astype(BF16)
    before = _dot(earlier, chosen.astype(BF16)) + cnt_ref[...]
    for r in range(TOP_K):
        rank = jnp.sum(jnp.where(lane == picks[r], before, 0.0), axis=-1, keepdims=True)
        idxs = jnp.where(lane == TOP_K + r, rank.astype(jnp.int32), idxs)
    idx_ref[...] = idxs
    cnt_ref[...] = cnt_ref[...] + jnp.sum(chosen, axis=0, keepdims=True)


def _out_ln(gla_o, dif_o, ret_o, x, w_out, ln_g, ln_b, wr_hi, wr_lo, b_router, alpha, tm=512):
    n, d = x.shape
    tm = min(tm, n)
    row = lambda i: (i, 0)
    const = lambda i: (0, 0)
    return pl.pallas_call(
        functools.partial(_out_ln_kernel, alpha=alpha),
        grid=(n // tm,),
        in_specs=[pl.BlockSpec((tm, gla_o.shape[1]), row),
                  pl.BlockSpec((tm, dif_o.shape[1]), row),
                  pl.BlockSpec((tm, ret_o.shape[1]), row),
                  pl.BlockSpec((tm, d), row),
                  pl.BlockSpec(w_out.shape, const),
                  pl.BlockSpec((1, d), const),
                  pl.BlockSpec((1, d), const),
                  pl.BlockSpec((d, LANE), const),
                  pl.BlockSpec((d, LANE), const),
                  pl.BlockSpec((1, LANE), const)],
        out_specs=[pl.BlockSpec((tm, d), row),
                   pl.BlockSpec((tm, d), row),
                   pl.BlockSpec((tm, LANE), row),
                   pl.BlockSpec((tm, LANE), row),
                   pl.BlockSpec((1, LANE), const)],
        out_shape=[jax.ShapeDtypeStruct((n, d), F32),
                   jax.ShapeDtypeStruct((n, d), BF16),
                   jax.ShapeDtypeStruct((n, LANE), F32),
                   jax.ShapeDtypeStruct((n, LANE), jnp.int32),
                   jax.ShapeDtypeStruct((1, LANE), F32)],
        compiler_params=_cparams(("arbitrary",)),
        name="out_ln_router",
    )(gla_o, dif_o, ret_o, x, w_out, ln_g, ln_b, wr_hi, wr_lo, b_router)


GU_GROUP = 2 * LANE


def _deinterleave_kernel(w_ref, perm_ref, o_ref):
    w = w_ref[...].astype(BF16)
    for g in range(w.shape[1] // GU_GROUP):
        cols = slice(g * GU_GROUP, (g + 1) * GU_GROUP)
        o_ref[:, cols] = _dot(w[:, cols], perm_ref[...]).astype(o_ref.dtype)


def _deinterleave_perm():
    perm = np.zeros((GU_GROUP, GU_GROUP), np.float32)
    half = np.arange(LANE)
    perm[2 * half, half] = 1.0
    perm[2 * half + 1, LANE + half] = 1.0
    return jnp.asarray(perm, BF16)


def _deinterleave_gate_up(w, tm=512):
    rows = math.prod(w.shape[:-1])
    cols = w.shape[-1]
    out = pl.pallas_call(
        _deinterleave_kernel,
        grid=(rows // tm,),
        in_specs=[pl.BlockSpec((tm, cols), lambda i: (i, 0)),
                  pl.BlockSpec((GU_GROUP, GU_GROUP), lambda i: (0, 0))],
        out_specs=pl.BlockSpec((tm, cols), lambda i: (i, 0)),
        out_shape=jax.ShapeDtypeStruct((rows, cols), BF16),
        compiler_params=_cparams(("arbitrary",)),
        name="deinterleave_gate_up",
    )(w.reshape(rows, cols), _deinterleave_perm())
    return out.reshape(w.shape)


def _moe_kernel(blk_e_ref, n_used_ref, x_ref, wgu_ref, bgu_ref, wdn_ref, bdn_ref, o_ref):
    i = pl.program_id(0)

    @pl.when(i < n_used_ref[0])
    def _():
        gu = _dot(x_ref[...], wgu_ref[...]) + bgu_ref[...]
        acts = []
        for grp in range(gu.shape[1] // GU_GROUP):
            g = jnp.minimum(gu[:, grp * GU_GROUP:grp * GU_GROUP + LANE], SWIGLU_LIMIT)
            u = jnp.clip(gu[:, grp * GU_GROUP + LANE:(grp + 1) * GU_GROUP], -SWIGLU_LIMIT, SWIGLU_LIMIT)
            acts.append(((u + 1.0) * (g * jax.nn.sigmoid(g * SWIGLU_ALPHA))).astype(BF16))
        act = jnp.concatenate(acts, axis=1)
        o_ref[...] = (_dot(act, wdn_ref[...]) + bdn_ref[...]).astype(o_ref.dtype)

    @pl.when(i >= n_used_ref[0])
    def _():
        o_ref[...] = jnp.zeros_like(o_ref)


def _moe(xs, blk_e, n_used, w_gu, b_gu, w_dn, b_dn, blk):
    rows, d = xs.shape
    f2 = w_gu.shape[2]
    f = w_dn.shape[1]
    grid_spec = pltpu.PrefetchScalarGridSpec(
        num_scalar_prefetch=2,
        grid=(rows // blk,),
        in_specs=[pl.BlockSpec((blk, d), lambda i, be, nu: (i, 0)),
                  pl.BlockSpec((None, d, f2), lambda i, be, nu: (be[i], 0, 0)),
                  pl.BlockSpec((None, 1, f2), lambda i, be, nu: (be[i], 0, 0)),
                  pl.BlockSpec((None, f, d), lambda i, be, nu: (be[i], 0, 0)),
                  pl.BlockSpec((None, 1, d), lambda i, be, nu: (be[i], 0, 0))],
        out_specs=pl.BlockSpec((blk, d), lambda i, be, nu: (i, 0)),
    )
    return pl.pallas_call(
        _moe_kernel,
        grid_spec=grid_spec,
        out_shape=jax.ShapeDtypeStruct((rows, d), BF16),
        compiler_params=_cparams(("arbitrary",)),
        name="moe_experts",
    )(blk_e, n_used, xs, w_gu, b_gu, w_dn, b_dn)


def _combine_kernel(yg_ref, gate_ref, x1_ref, x1b_ref, p_ref, wg_ref, bg_ref, wp_ref, g_ref, b_ref,
                    x2_ref, x2b_ref, *, alpha):
    d = x1_ref.shape[1]
    gates = gate_ref[...]
    m = jnp.zeros(x1_ref.shape, F32)
    for r in range(TOP_K):
        m = m + yg_ref[r].astype(F32) * gates[:, r:r + 1]
    e = (jax.nn.sigmoid(_dot(x1b_ref[...], wg_ref[...]) + bg_ref[...])
         * _dot(p_ref[...].astype(BF16), wp_ref[...]))
    x2 = _layer_norm(alpha * x1_ref[...] + m + e, g_ref[...], b_ref[...])
    x2_ref[...] = x2
    x2b_ref[...] = x2.astype(BF16)


def _combine(yg, gates, x1, x1b, p, w_pg, b_pg, w_pp, ln_g, ln_b, alpha, tm=512):
    n, d = x1.shape
    tm = min(tm, n)
    row = lambda i: (i, 0)
    const = lambda i: (0, 0)
    return pl.pallas_call(
        functools.partial(_combine_kernel, alpha=alpha),
        grid=(n // tm,),
        in_specs=[pl.BlockSpec((TOP_K, tm, d), lambda i: (0, i, 0)),
                  pl.BlockSpec((tm, LANE), row),
                  pl.BlockSpec((tm, d), row),
                  pl.BlockSpec((tm, d), row),
                  pl.BlockSpec((tm, p.shape[1]), row),
                  pl.BlockSpec(w_pg.shape, const),
                  pl.BlockSpec((1, d), const),
                  pl.BlockSpec(w_pp.shape, const),
                  pl.BlockSpec((1, d), const),
                  pl.BlockSpec((1, d), const)],
        out_specs=[pl.BlockSpec((tm, d), row), pl.BlockSpec((tm, d), row)],
        out_shape=[jax.ShapeDtypeStruct((n, d), F32), jax.ShapeDtypeStruct((n, d), BF16)],
        compiler_params=_cparams(("arbitrary",)),
        name="combine_ln",
    )(yg, gates, x1, x1b, p, w_pg, b_pg, w_pp, ln_g, ln_b)


def _route(experts, ranks, counts, blk):
    n_tok = experts.shape[0]
    n_assign = n_tok * TOP_K
    sizes = counts.astype(jnp.int32)
    padded = ((sizes + blk - 1) // blk) * blk
    pends = jnp.cumsum(padded)
    pstarts = pends - padded
    onehot = experts[:, :, None] == jnp.arange(N_EXPERTS, dtype=jnp.int32)
    slot = jnp.sum(jnp.where(onehot, pstarts, 0), axis=-1) + ranks
    span = 2 * n_assign
    real = experts * span + jnp.arange(n_assign, dtype=jnp.int32).reshape(n_tok, TOP_K)
    j = jnp.arange(blk - 1, dtype=jnp.int32)
    filler = jnp.where(j[None, :] < (padded - sizes)[:, None],
                       jnp.arange(N_EXPERTS, dtype=jnp.int32)[:, None] * span + n_assign + j[None, :],
                       jnp.iinfo(jnp.int32).max)
    buf_len = ((n_assign + N_EXPERTS * (blk - 1) + blk - 1) // blk) * blk
    n_blk = buf_len // blk
    tail = jnp.full((buf_len - n_assign - N_EXPERTS * (blk - 1),), jnp.iinfo(jnp.int32).max, jnp.int32)
    assign = jnp.sort(jnp.concatenate([real.reshape(-1), filler.reshape(-1), tail])) % span
    buf_tok = jnp.where(assign < n_assign, assign // TOP_K, 0)
    blk_e = jnp.clip(jnp.searchsorted(pends, jnp.arange(n_blk, dtype=jnp.int32) * blk, side='right'),
                     0, N_EXPERTS - 1).astype(jnp.int32)
    n_used = (pends[-1] // blk).astype(jnp.int32).reshape(1)
    return buf_tok, slot, blk_e, n_used


def _regroup_w_in(w_in):
    gq, gk, gv, glr, gout, rest = jnp.split(w_in, [128, 256, 512, 528, 784], axis=-1)
    pad = jnp.zeros(w_in.shape[:-1] + (Z_WIDTH - w_in.shape[-1],), w_in.dtype)
    return jnp.concatenate([gq, gk, gv, gout, rest, glr, pad], axis=-1)


def kernel(x, p, w_in, w_gla_gate, b_gla_gate, gla_norm_g, diff_lambda, diff_norm_g, w_out, rel_bias, ln1_g, ln1_b, w_router, b_router, w_gate_up, b_gate_up, w_down, b_down, w_ple_gate, b_ple_gate, w_ple_proj, ln2_g, ln2_b):
    batch, seq, d = x.shape
    depth = w_in.shape[0]
    n = batch * seq
    alpha = (2 * depth) ** 0.25
    moe_blk = 256
    diff_blk = min(512, seq)

    w_in_b = _regroup_w_in(w_in).astype(BF16)
    wg = jnp.zeros((depth, LANE, GLA_HEADS * GLA_DK), F32).at[:, :GLA_GATE_RANK, :].set(w_gla_gate).astype(BF16)
    gla_g = jnp.tile(gla_norm_g, (1, GLA_HEADS))[:, None, :]
    w_out_b = w_out.astype(BF16)
    wr = jnp.zeros((depth, d, LANE), F32).at[:, :, :N_EXPERTS].set(w_router)
    wr_hi = wr.astype(BF16)
    wr_lo = (wr - wr_hi.astype(F32)).astype(BF16)
    br = jnp.zeros((depth, 1, LANE), F32).at[:, 0, :N_EXPERTS].set(b_router)
    w_gu_b = _deinterleave_gate_up(w_gate_up)
    b_gu = b_gate_up.reshape(depth, N_EXPERTS, -1, LANE, 2).swapaxes(-1, -2).reshape(depth, N_EXPERTS, 1, -1)
    w_dn_b = w_down.astype(BF16)
    b_dn = b_down[:, :, None, :]
    w_pg_b = w_ple_gate.astype(BF16)
    w_pp_b = w_ple_proj.astype(BF16)
    bias_tiles = _bias_tiles(rel_bias, diff_blk)
    ret_tables = _ret_tables(seq)

    xf = x.reshape(n, d)
    xb = xf.astype(BF16)
    pf = p.reshape(depth, n, p.shape[-1])
    for i in range(depth):
        lam_init = 0.8 - 0.6 * math.exp(-0.3 * i)
        z = _matmul(xb, w_in_b[i], min(1024, n), Z_WIDTH // 3, BF16)
        gla_o = _gla(z, wg[i], b_gla_gate[i][None, :], gla_g[i], batch, seq)
        dif_o = _diff(z, diff_lambda[i], diff_norm_g[i][None, :], bias_tiles, batch, seq, lam_init, diff_blk)
        ret_o = _ret(z, ret_tables, batch, seq)
        x1, x1b, gates, idx, cnt = _out_ln(gla_o, dif_o, ret_o, xf, w_out_b[i], ln1_g[i][None, :],
                                           ln1_b[i][None, :], wr_hi[i], wr_lo[i], br[i], alpha)
        buf_tok, slot, blk_e, n_used = _route(idx[:, :TOP_K], idx[:, TOP_K:2 * TOP_K], cnt[0, :N_EXPERTS], moe_blk)
        xs = x1b[buf_tok]
        yb = _moe(xs, blk_e, n_used, w_gu_b[i], b_gu[i], w_dn_b[i], b_dn[i], moe_blk)
        yg = yb[slot.T]
        xf, xb = _combine(yg, gates, x1, x1b, pf[i], w_pg_b[i], b_ple_gate[i][None, :], w_pp_b[i],
                          ln2_g[i][None, :], ln2_b[i][None, :], alpha)
    return xf.reshape(batch, seq, d)
```

```python
import functools
import math

import numpy as np
import jax
import jax.numpy as jnp
from jax import lax
from jax.experimental import pallas as pl
from jax.experimental.pallas import tpu as pltpu

F32 = jnp.float32
BF16 = jnp.bfloat16

GLA_HEADS = 4
GLA_DK = 32
GLA_DV = 64
GLA_GATE_RANK = 16
GLA_TAU = 16.0
GLA_CHUNK = 64
DIFF_HEADS = 4
DIFF_D = 64
RET_HEADS = 4
RET_DK = 64
RET_DV = 64
RET_CHUNK = 128
T5_BUCKETS = 32
T5_MAX_DIST = 128
N_EXPERTS = 32
TOP_K = 4
SWIGLU_LIMIT = 7.0
SWIGLU_ALPHA = 1.702
LN_EPS = 1e-5
HEAD_NORM_EPS = 1e-5

LANE = 128
VMEM_LIMIT = 56 * 1024 * 1024

_GQ, _GK, _GV, _GO = 0, 128, 256, 512
_DQ, _DK, _DV = 768, 1280, 1792
_RQ, _RK, _RV, _RG = 2304, 2560, 2816, 3072
_LR = 3328
Z_WIDTH = 3456

GLA_FAST_MAX_DECAY = 60.0


def _cparams(sem):
    return pltpu.CompilerParams(dimension_semantics=sem, vmem_limit_bytes=VMEM_LIMIT)


def _split_hi_lo(a):
    hi = a.astype(BF16)
    lo = (a - hi.astype(F32)).astype(BF16)
    return hi, lo


def _dot(a, b):
    return jnp.dot(a, b, preferred_element_type=F32)


def _dot_nt(a, b):
    return lax.dot_general(a, b, (((1,), (1,)), ((), ())), preferred_element_type=F32)


def _dot_tn(a, b):
    return lax.dot_general(a, b, (((0,), (0,)), ((), ())), preferred_element_type=F32)


def _seg_sum(a, ind):
    hi, lo = _split_hi_lo(a)
    return _dot(hi, ind) + _dot(lo, ind)


def _mm_kernel(x_ref, w_ref, o_ref):
    o_ref[...] = _dot(x_ref[...], w_ref[...]).astype(o_ref.dtype)


def _matmul(x, w, tm, tn, out_dtype, name):
    m, k = x.shape
    n = w.shape[1]
    return pl.pallas_call(
        _mm_kernel,
        grid=(n // tn, m // tm),
        in_specs=[pl.BlockSpec((tm, k), lambda j, i: (i, 0)),
                  pl.BlockSpec((k, tn), lambda j, i: (0, j))],
        out_specs=pl.BlockSpec((tm, tn), lambda j, i: (i, j)),
        out_shape=jax.ShapeDtypeStruct((m, n), out_dtype),
        compiler_params=_cparams(("arbitrary", "arbitrary")),
        name=name,
    )(x, w)


def _gla_kernel(q_ref, k_ref, v_ref, og_ref, lr_ref, wg_ref, bg_ref, ng_ref, o_ref,
                state_ref, la_ref, cum_ref, kf_ref, vf_ref, oacc_ref, intra_ref, *, n_chunks, fast_max_decay):
    c_len = GLA_CHUNK
    hk = GLA_HEADS * GLA_DK
    hv = GLA_HEADS * GLA_DV

    @pl.when(pl.program_id(1) == 0)
    def _():
        state_ref[...] = jnp.zeros_like(state_ref)

    zg = _dot(lr_ref[...], wg_ref[...]) + bg_ref[...]
    la = (jnp.minimum(zg, 0.0) - jnp.log1p(jnp.exp(-jnp.abs(zg)))) / GLA_TAU
    la_ref[...] = la
    chunk_tot = jnp.sum(la.reshape(n_chunks, c_len, hk), axis=1)
    fast = jnp.min(chunk_tot) > -fast_max_decay

    row = lax.broadcasted_iota(jnp.int32, (c_len, c_len), 0)
    col = lax.broadcasted_iota(jnp.int32, (c_len, c_len), 1)
    tril = (row >= col).astype(BF16)
    sr = lax.broadcasted_iota(jnp.int32, (hv, hk), 0)
    sc = lax.broadcasted_iota(jnp.int32, (hv, hk), 1)
    state_mask = ((sr // GLA_DV) == (sc // GLA_DK)).astype(F32)
    lane_k = lax.broadcasted_iota(jnp.int32, (1, hk), 1) // GLA_DK
    lane_v = lax.broadcasted_iota(jnp.int32, (1, hv), 1) // GLA_DV
    r4 = lax.broadcasted_iota(jnp.int32, (GLA_HEADS * c_len, c_len), 0) % c_len
    c4 = lax.broadcasted_iota(jnp.int32, (GLA_HEADS * c_len, c_len), 1)
    causal4 = r4 >= c4
    er = lax.broadcasted_iota(jnp.int32, (hk, hv), 0) // GLA_DK
    ec = lax.broadcasted_iota(jnp.int32, (hk, hv), 1) // GLA_DV
    head_expand = (er == ec).astype(BF16)
    trow = lax.broadcasted_iota(jnp.int32, (c_len, hv), 0)

    def chunk(c, carry):
        r0 = pl.multiple_of(c * c_len, c_len)
        rows = pl.ds(r0, c_len)
        q = q_ref[rows, :].astype(F32) * (GLA_DK ** -0.5)
        k = k_ref[rows, :].astype(F32)
        v = v_ref[rows, :]
        la_c = la_ref[rows, :]
        la_hi, la_lo = _split_hi_lo(la_c)
        cum = _dot(tril, la_hi) + _dot(tril, la_lo)
        last = cum[c_len - 1:c_len, :]

        @pl.when(fast)
        def _():
            mid = cum[c_len // 2:c_len // 2 + 1, :]
            qa = q * jnp.exp(cum - mid)
            kb = (k * jnp.exp(mid - cum)).astype(BF16)
            qa4 = jnp.concatenate(
                [jnp.where(lane_k == h, qa, 0.0) for h in range(GLA_HEADS)], axis=0).astype(BF16)
            s4 = jnp.where(causal4, _dot_nt(qa4, kb), 0.0)
            pv = _dot(s4.astype(BF16), v)
            acc = jnp.zeros((c_len, hv), F32)
            for h in range(GLA_HEADS):
                acc = acc + jnp.where(lane_v == h, pv[h * c_len:(h + 1) * c_len, :], 0.0)
            intra_ref[...] = acc

        @pl.when(jnp.logical_not(fast))
        def _():
            cum_ref[...] = cum
            kf_ref[...] = k
            vf_ref[...] = v.astype(F32)

            def one_key(s, acc):
                ks = kf_ref[pl.ds(s, 1), :]
                vs = vf_ref[pl.ds(s, 1), :]
                cs = cum_ref[pl.ds(s, 1), :]
                w = q * ks * jnp.exp(jnp.minimum(cum - cs, 0.0))
                wsum = _dot(w.astype(BF16), head_expand)
                return acc + jnp.where(trow >= s, wsum, 0.0) * vs

            intra_ref[...] = lax.fori_loop(0, c_len, one_key, jnp.zeros((c_len, hv), F32))

        st = state_ref[...]
        qg = (q * jnp.exp(cum)).astype(BF16)
        inter = _dot_nt(qg, st.astype(BF16))
        oacc_ref[rows, :] = intra_ref[...] + inter
        kg = (k * jnp.exp(last - cum)).astype(BF16)
        upd = _dot_tn(v, kg)
        state_ref[...] = (st * jnp.exp(last) + upd) * state_mask
        return carry

    lax.fori_loop(0, n_chunks, chunk, 0)

    o = oacc_ref[...]
    vr = lax.broadcasted_iota(jnp.int32, (hv, hv), 0) // GLA_DV
    vc = lax.broadcasted_iota(jnp.int32, (hv, hv), 1) // GLA_DV
    seg = (vr == vc).astype(BF16)
    ms = _seg_sum(o * o, seg) * (1.0 / GLA_DV)
    y = o * lax.rsqrt(ms + HEAD_NORM_EPS) * ng_ref[...]
    g = og_ref[...].astype(F32)
    o_ref[...] = (y * (g * jax.nn.sigmoid(g))).astype(o_ref.dtype)


def _gla(z, w_gate, b_gate, norm_g, batch, seq, tg=512, fast_max_decay=GLA_FAST_MAX_DECAY):
    tg = min(tg, seq)
    nt = seq // tg
    hk = GLA_HEADS * GLA_DK
    hv = GLA_HEADS * GLA_DV

    def rowmap(cb):
        return lambda b, t: (b * nt + t, cb)

    const = lambda b, t: (0, 0)
    return pl.pallas_call(
        functools.partial(_gla_kernel, n_chunks=tg // GLA_CHUNK, fast_max_decay=fast_max_decay),
        grid=(batch, nt),
        in_specs=[pl.BlockSpec((tg, hk), rowmap(_GQ // hk)),
                  pl.BlockSpec((tg, hk), rowmap(_GK // hk)),
                  pl.BlockSpec((tg, hv), rowmap(_GV // hv)),
                  pl.BlockSpec((tg, hv), rowmap(_GO // hv)),
                  pl.BlockSpec((tg, LANE), rowmap(_LR // LANE)),
                  pl.BlockSpec((LANE, hk), const),
                  pl.BlockSpec((1, hk), const),
                  pl.BlockSpec((1, hv), const)],
        out_specs=pl.BlockSpec((tg, hv), lambda b, t: (b * nt + t, 0)),
        out_shape=jax.ShapeDtypeStruct((batch * seq, hv), BF16),
        scratch_shapes=[pltpu.VMEM((hv, hk), F32),
                        pltpu.VMEM((tg, hk), F32),
                        pltpu.VMEM((GLA_CHUNK, hk), F32),
                        pltpu.VMEM((GLA_CHUNK, hk), F32),
                        pltpu.VMEM((GLA_CHUNK, hv), F32),
                        pltpu.VMEM((tg, hv), F32),
                        pltpu.VMEM((GLA_CHUNK, hv), F32)],
        compiler_params=_cparams(("arbitrary", "arbitrary")),
        name="gla",
    )(z, z, z, z, z, w_gate, b_gate, norm_g)


LOG2E = math.log2(math.e)


def _diff_kernel(far_ref, lam_ref, q_ref, k_ref, v_ref, bias_ref, ng_ref, o_ref,
                 m_ref, acc_ref, *, blk, lam_init):
    qi = pl.program_id(2)
    d = DIFF_D
    w = 2 * DIFF_D
    q = (q_ref[...].astype(F32) * (DIFF_D ** -0.5 * LOG2E)).astype(BF16)
    lane_q = lax.broadcasted_iota(jnp.int32, q.shape, 1)
    q2 = jnp.concatenate([jnp.where(lane_q < d, q, jnp.zeros_like(q)),
                          jnp.where(lane_q >= d, q, jnp.zeros_like(q))], axis=0)
    m_ref[...] = jnp.full_like(m_ref, -jnp.inf)
    acc_ref[...] = jnp.zeros_like(acc_ref)

    def kv_step(j, bias, shift):
        rows = pl.ds(pl.multiple_of(j * blk, blk), blk)
        vj = v_ref[rows, :]
        s_both = _dot_nt(q2, k_ref[rows, :])
        for m in range(2):
            s = s_both[m * blk:(m + 1) * blk, :]
            if bias is not None:
                s = s + bias
            m_cur = jnp.max(s, axis=-1, keepdims=True)
            if shift is not None:
                m_cur = m_cur + shift
            m_old = m_ref[m]
            m_new = jnp.maximum(m_old, m_cur)
            alpha = jnp.exp2(m_old - m_new)
            sub = m_new if shift is None else m_new - shift
            p = jnp.exp2(s - jnp.concatenate([sub] * (blk // LANE), axis=1))
            row_sum = jnp.broadcast_to(jnp.sum(p, axis=-1, keepdims=True), (blk, LANE))
            acc_ref[m] = (jnp.concatenate([alpha, alpha], axis=1) * acc_ref[m]
                          + jnp.concatenate([_dot(p.astype(BF16), vj), row_sum], axis=1))
            m_ref[m] = m_new

    far = far_ref[pl.program_id(1)]

    def far_step(j, carry):
        kv_step(j, None, far)
        return carry

    lax.fori_loop(0, jnp.maximum(qi - 1, 0), far_step, 0)

    @pl.when(qi >= 1)
    def _():
        kv_step(qi - 1, bias_ref[0, 1], None)

    kv_step(qi, bias_ref[0, 0], None)

    lf = lam_ref[...]
    lam = (jnp.exp(jnp.sum(lf[0:1] * lf[1:2], axis=-1, keepdims=True))
           - jnp.exp(jnp.sum(lf[2:3] * lf[3:4], axis=-1, keepdims=True)) + lam_init)
    a0 = acc_ref[0]
    a1 = acc_ref[1]
    o = a0[:, :w] / a0[:, w:] - lam * (a1[:, :w] / a1[:, w:])
    ms = jnp.mean(o * o, axis=-1, keepdims=True)
    y = o * lax.rsqrt(ms + HEAD_NORM_EPS) * ng_ref[...] * (1.0 - lam_init)
    o_ref[...] = y.astype(o_ref.dtype)


def _t5_bucket(rel):
    n = jnp.maximum(-rel, 0)
    max_exact = T5_BUCKETS // 2
    nf = jnp.maximum(n, 1).astype(F32)
    large = max_exact + (jnp.log(nf / max_exact) / math.log(T5_MAX_DIST / max_exact)
                         * (T5_BUCKETS - max_exact)).astype(jnp.int32)
    large = jnp.minimum(large, T5_BUCKETS - 1)
    return jnp.where(n < max_exact, n, large)


def _bias_tiles(rel_bias, blk):
    assert blk + 1 >= T5_MAX_DIST
    m = jnp.arange(2 * blk)
    col_minus_row = jnp.where(m < blk, m, m - 2 * blk)
    tiles = []
    for delta in range(2):
        rel = col_minus_row - delta * blk
        w = rel_bias[_t5_bucket(rel)].astype(F32).T
        if delta == 0:
            w = jnp.where(rel <= 0, w, -jnp.inf)
        flat = jnp.tile(w, (1, blk))[:, :blk * (2 * blk - 1)]
        tiles.append(flat.reshape(-1, blk, 2 * blk - 1)[:, :, :blk])
    far = rel_bias[_t5_bucket(jnp.full((1,), -(blk + 1)))[0]].astype(F32)
    return jnp.stack(tiles, axis=1) * LOG2E, far * LOG2E


def _diff(z, diff_lambda, norm_g, bias, batch, seq, lam_init, blk):
    blk = min(blk, seq)
    nq = seq // blk
    w = 2 * DIFF_D
    bias_tiles, bias_far = bias
    return pl.pallas_call(
        functools.partial(_diff_kernel, blk=blk, lam_init=lam_init),
        grid=(batch, DIFF_HEADS, nq),
        in_specs=[pl.BlockSpec(memory_space=pltpu.SMEM),
                  pl.BlockSpec((4, DIFF_D), lambda b, h, i: (0, 0)),
                  pl.BlockSpec((blk, w), lambda b, h, i: (b * nq + i, _DQ // w + h)),
                  pl.BlockSpec((seq, w), lambda b, h, i: (b, _DK // w + h)),
                  pl.BlockSpec((seq, w), lambda b, h, i: (b, _DV // w + h)),
                  pl.BlockSpec((1, 2, blk, blk), lambda b, h, i: (h, 0, 0, 0)),
                  pl.BlockSpec((1, w), lambda b, h, i: (0, 0))],
        out_specs=pl.BlockSpec((blk, w), lambda b, h, i: (b * nq + i, h)),
        out_shape=jax.ShapeDtypeStruct((batch * seq, DIFF_HEADS * w), BF16),
        scratch_shapes=[pltpu.VMEM((2, blk, LANE), F32),
                        pltpu.VMEM((2, blk, w + LANE), F32)],
        compiler_params=_cparams(("arbitrary", "arbitrary", "arbitrary")),
        name="diff_attn",
    )(bias_far, diff_lambda, z, z, z, bias_tiles, norm_g)


def _ret_kernel(q_ref, k_ref, v_ref, g_ref, sin_ref, cos_ref, inner_ref, cross_ref, sdec_ref, cdec_ref,
                o_ref, state_ref):
    c_len = RET_CHUNK
    hk = RET_HEADS * RET_DK
    hv = RET_HEADS * RET_DV

    @pl.when(pl.program_id(1) == 0)
    def _():
        state_ref[...] = jnp.zeros_like(state_ref)

    sin = sin_ref[...]
    cos = cos_ref[...]
    even = (lax.broadcasted_iota(jnp.int32, (1, hk), 1) % 2) == 0

    def rotary(t):
        nxt = pltpu.roll(t, hk - 1, 1)
        prv = pltpu.roll(t, 1, 1)
        return t * cos + jnp.where(even, -nxt, prv) * sin

    qf = rotary(q_ref[...].astype(F32))
    kf = rotary(k_ref[...].astype(F32)) * (RET_DK ** -0.5)
    v = v_ref[...]

    lane_k = lax.broadcasted_iota(jnp.int32, (1, hk), 1) // RET_DK
    lane_v = lax.broadcasted_iota(jnp.int32, (1, hv), 1) // RET_DV
    q4 = jnp.concatenate([jnp.where(lane_k == h, qf, 0.0) for h in range(RET_HEADS)], axis=0).astype(BF16)
    s4 = _dot_nt(q4, kf.astype(BF16)) * inner_ref[...]
    pv = _dot(s4.astype(BF16), v)
    acc = jnp.zeros((c_len, hv), F32)
    for h in range(RET_HEADS):
        acc = acc + jnp.where(lane_v == h, pv[h * c_len:(h + 1) * c_len, :], 0.0)

    st = state_ref[...]
    o = acc + _dot_nt(qf.astype(BF16), st.astype(BF16)) * cross_ref[...]
    ks = (kf * sdec_ref[...]).astype(BF16)
    sr = lax.broadcasted_iota(jnp.int32, (hv, hk), 0) // RET_DV
    sc = lax.broadcasted_iota(jnp.int32, (hv, hk), 1) // RET_DK
    state_ref[...] = jnp.where(sr == sc, st * cdec_ref[...] + _dot_tn(v, ks), 0.0)

    vr = lax.broadcasted_iota(jnp.int32, (hv, hv), 0) // RET_DV
    vc = lax.broadcasted_iota(jnp.int32, (hv, hv), 1) // RET_DV
    seg = (vr == vc).astype(BF16)
    ms = _seg_sum(o * o, seg) * (1.0 / RET_DV)
    g = g_ref[...].astype(F32)
    o_ref[...] = (o * lax.rsqrt(ms + HEAD_NORM_EPS) * (g * jax.nn.sigmoid(g))).astype(o_ref.dtype)


def _ret_tables(seq):
    c_len = RET_CHUNK
    pos = jnp.arange(seq, dtype=F32)
    angle = 1.0 / (10000.0 ** jnp.linspace(0.0, 1.0, RET_DK // 2, dtype=F32))
    angle = jnp.repeat(angle, 2)
    sin = jnp.tile(jnp.sin(pos[:, None] * angle), (1, RET_HEADS))
    cos = jnp.tile(jnp.cos(pos[:, None] * angle), (1, RET_HEADS))
    log_g = jnp.log1p(-jnp.exp2(-5.0 - jnp.arange(RET_HEADS, dtype=F32)))
    idx = jnp.arange(c_len, dtype=F32)
    rel = idx[:, None] - idx[None, :]
    inner = jnp.where(rel[None] >= 0, jnp.exp(jnp.maximum(rel, 0.0)[None] * log_g[:, None, None]), 0.0)
    inner = inner.reshape(RET_HEADS * c_len, c_len)
    cross = jnp.exp((idx + 1.0)[None] * log_g[:, None])
    cross = jnp.repeat(cross.T, RET_DV, axis=1)
    sdec = jnp.exp((c_len - 1.0 - idx)[None] * log_g[:, None])
    sdec = jnp.repeat(sdec.T, RET_DK, axis=1)
    cdec = jnp.repeat(jnp.exp(c_len * log_g), RET_DK)[None, :]
    return sin, cos, inner, cross, sdec, cdec


def _ret(z, tables, batch, seq):
    c_len = RET_CHUNK
    nt = seq // c_len
    hk = RET_HEADS * RET_DK
    hv = RET_HEADS * RET_DV
    sin, cos, inner, cross, sdec, cdec = tables

    def rowmap(cb):
        return lambda b, t: (b * nt + t, cb)

    const = lambda b, t: (0, 0)
    return pl.pallas_call(
        _ret_kernel,
        grid=(batch, nt),
        in_specs=[pl.BlockSpec((c_len, hk), rowmap(_RQ // hk)),
                  pl.BlockSpec((c_len, hk), rowmap(_RK // hk)),
                  pl.BlockSpec((c_len, hv), rowmap(_RV // hv)),
                  pl.BlockSpec((c_len, hv), rowmap(_RG // hv)),
                  pl.BlockSpec((c_len, hk), lambda b, t: (t, 0)),
                  pl.BlockSpec((c_len, hk), lambda b, t: (t, 0)),
                  pl.BlockSpec((RET_HEADS * c_len, c_len), const),
                  pl.BlockSpec((c_len, hv), const),
                  pl.BlockSpec((c_len, hk), const),
                  pl.BlockSpec((1, hk), const)],
        out_specs=pl.BlockSpec((c_len, hv), lambda b, t: (b * nt + t, 0)),
        out_shape=jax.ShapeDtypeStruct((batch * seq, hv), BF16),
        scratch_shapes=[pltpu.VMEM((hv, hk), F32)],
        compiler_params=_cparams(("arbitrary", "arbitrary")),
        name="retention",
    )(z, z, z, z, sin, cos, inner, cross, sdec, cdec)


def _layer_norm(t, g, b):
    mu = jnp.mean(t, axis=-1, keepdims=True)
    var = jnp.mean(jnp.square(t - mu), axis=-1, keepdims=True)
    return (t - mu) * lax.rsqrt(var + LN_EPS) * g + b


def _out_ln_kernel(gla_ref, dif_ref, ret_ref, x_ref, wo_ref, g_ref, b_ref, wr_hi_ref, wr_lo_ref, br_ref,
                   x1_ref, x1b_ref, gate_ref, idx_ref, cnt_ref, *, alpha):
    w0 = gla_ref.shape[1]
    w1 = w0 + dif_ref.shape[1]
    a = (_dot(gla_ref[...], wo_ref[0:w0, :]) + _dot(dif_ref[...], wo_ref[w0:w1, :])
         + _dot(ret_ref[...], wo_ref[w1:, :]))
    x1 = _layer_norm(alpha * x_ref[...] + a, g_ref[...], b_ref[...])
    x1_ref[...] = x1
    x1b_ref[...] = x1.astype(BF16)

    hi, lo = _split_hi_lo(x1)
    logits = (_dot(hi, wr_hi_ref[...]) + _dot(lo, wr_hi_ref[...]) + _dot(hi, wr_lo_ref[...])) + br_ref[...]
    lane = lax.broadcasted_iota(jnp.int32, logits.shape, 1)
    work = jnp.where(lane < N_EXPERTS, logits, -jnp.inf)
    vals = jnp.zeros(logits.shape, F32)
    idxs = jnp.zeros(logits.shape, jnp.int32)
    chosen = jnp.zeros(logits.shape, F32)
    picks = []
    top0 = None
    for r in range(TOP_K):
        mx = jnp.max(work, axis=-1, keepdims=True)
        am = jnp.min(jnp.where(work == mx, lane, LANE), axis=-1, keepdims=True)
        if r == 0:
            top0 = mx
        vals = jnp.where(lane == r, jnp.exp(mx - top0), vals)
        idxs = jnp.where(lane == r, am, idxs)
        chosen = jnp.where(lane == am, 1.0, chosen)
        work = jnp.where(lane == am, -jnp.inf, work)
        picks.append(am)
    gate_ref[...] = vals / jnp.sum(vals, axis=-1, keepdims=True)

    @pl.when(pl.program_id(0) == 0)
    def _():
        cnt_ref[...] = jnp.zeros_like(cnt_ref)

    tm = logits.shape[0]
    earlier = (lax.broadcasted_iota(jnp.int32, (tm, tm), 0)
               > lax.broadcasted_iota(jnp.int32, (tm, tm), 1)).astype(BF16)
    before = _dot(earlier, chosen.astype(BF16)) + cnt_ref[...]
    for r in range(TOP_K):
        rank = jnp.sum(jnp.where(lane == picks[r], before, 0.0), axis=-1, keepdims=True)
        idxs = jnp.where(lane == TOP_K + r, rank.astype(jnp.int32), idxs)
    idx_ref[...] = idxs
    cnt_ref[...] = cnt_ref[...] + jnp.sum(chosen, axis=0, keepdims=True)


def _out_ln(gla_o, dif_o, ret_o, x, w_out, ln_g, ln_b, wr_hi, wr_lo, b_router, alpha, tm=512):
    n, d = x.shape
    tm = min(tm, n)
    row = lambda i: (i, 0)
    const = lambda i: (0, 0)
    return pl.pallas_call(
        functools.partial(_out_ln_kernel, alpha=alpha),
        grid=(n // tm,),
        in_specs=[pl.BlockSpec((tm, gla_o.shape[1]), row),
                  pl.BlockSpec((tm, dif_o.shape[1]), row),
                  pl.BlockSpec((tm, ret_o.shape[1]), row),
                  pl.BlockSpec((tm, d), row),
                  pl.BlockSpec(w_out.shape, const),
                  pl.BlockSpec((1, d), const),
                  pl.BlockSpec((1, d), const),
                  pl.BlockSpec((d, LANE), const),
                  pl.BlockSpec((d, LANE), const),
                  pl.BlockSpec((1, LANE), const)],
        out_specs=[pl.BlockSpec((tm, d), row),
                   pl.BlockSpec((tm, d), row),
                   pl.BlockSpec((tm, LANE), row),
                   pl.BlockSpec((tm, LANE), row),
                   pl.BlockSpec((1, LANE), const)],
        out_shape=[jax.ShapeDtypeStruct((n, d), F32),
                   jax.ShapeDtypeStruct((n, d), BF16),
                   jax.ShapeDtypeStruct((n, LANE), F32),
                   jax.ShapeDtypeStruct((n, LANE), jnp.int32),
                   jax.ShapeDtypeStruct((1, LANE), F32)],
        compiler_params=_cparams(("arbitrary",)),
        name="out_ln_router",
    )(gla_o, dif_o, ret_o, x, w_out, ln_g, ln_b, wr_hi, wr_lo, b_router)


GU_GROUP = 2 * LANE


def _deinterleave_perm():
    perm = np.zeros((GU_GROUP, GU_GROUP), np.float32)
    half = np.arange(LANE)
    perm[2 * half, half] = 1.0
    perm[2 * half + 1, LANE + half] = 1.0
    return jnp.asarray(perm, BF16)


def _moe_kernel(blk_e_ref, n_used_ref, x_ref, wgu_ref, bgu_ref, wdn_ref, bdn_ref, perm_ref, o_ref,
                wgu_b_ref, wdn_b_ref, *, rows_per_cast):
    i = pl.program_id(0)
    used = i < n_used_ref[0]
    new_expert = jnp.logical_or(i == 0, blk_e_ref[i] != blk_e_ref[jnp.maximum(i - 1, 0)])

    @pl.when(jnp.logical_and(used, new_expert))
    def _():
        d = wgu_ref.shape[0]
        for r in range(d // rows_per_cast):
            rows = slice(r * rows_per_cast, (r + 1) * rows_per_cast)
            wdn_b_ref[rows, :] = wdn_ref[rows, :].astype(BF16)
            for g in range(wgu_ref.shape[1] // GU_GROUP):
                cols = slice(g * GU_GROUP, (g + 1) * GU_GROUP)
                wgu_b_ref[rows, cols] = _dot(wgu_ref[rows, cols].astype(BF16), perm_ref[...]).astype(BF16)

    @pl.when(used)
    def _():
        gu = _dot(x_ref[...], wgu_b_ref[...]) + bgu_ref[...]
        acts = []
        for grp in range(gu.shape[1] // GU_GROUP):
            g = jnp.minimum(gu[:, grp * GU_GROUP:grp * GU_GROUP + LANE], SWIGLU_LIMIT)
            u = jnp.clip(gu[:, grp * GU_GROUP + LANE:(grp + 1) * GU_GROUP], -SWIGLU_LIMIT, SWIGLU_LIMIT)
            acts.append(((u + 1.0) * (g * jax.nn.sigmoid(g * SWIGLU_ALPHA))).astype(BF16))
        act = jnp.concatenate(acts, axis=1)
        o_ref[...] = (_dot(act, wdn_b_ref[...]) + bdn_ref[...]).astype(o_ref.dtype)

    @pl.when(jnp.logical_not(used))
    def _():
        o_ref[...] = jnp.zeros_like(o_ref)


def _moe(xs, blk_e, n_used, w_gu, b_gu, w_dn, b_dn, blk):
    rows, d = xs.shape
    f2 = w_gu.shape[2]
    f = w_dn.shape[1]
    assert f == d
    grid_spec = pltpu.PrefetchScalarGridSpec(
        num_scalar_prefetch=2,
        grid=(rows // blk,),
        in_specs=[pl.BlockSpec((blk, d), lambda i, be, nu: (i, 0)),
                  pl.BlockSpec((None, d, f2), lambda i, be, nu: (be[i], 0, 0)),
                  pl.BlockSpec((None, 1, f2), lambda i, be, nu: (be[i], 0, 0)),
                  pl.BlockSpec((None, f, d), lambda i, be, nu: (be[i], 0, 0)),
                  pl.BlockSpec((None, 1, d), lambda i, be, nu: (be[i], 0, 0)),
                  pl.BlockSpec((GU_GROUP, GU_GROUP), lambda i, be, nu: (0, 0))],
        out_specs=pl.BlockSpec((blk, d), lambda i, be, nu: (i, 0)),
        scratch_shapes=[pltpu.VMEM((d, f2), BF16), pltpu.VMEM((f, d), BF16)],
    )
    return pl.pallas_call(
        functools.partial(_moe_kernel, rows_per_cast=256),
        grid_spec=grid_spec,
        out_shape=jax.ShapeDtypeStruct((rows, d), BF16),
        compiler_params=_cparams(("arbitrary",)),
        name="moe_experts",
    )(blk_e, n_used, xs, w_gu, b_gu, w_dn, b_dn, _deinterleave_perm())


def _combine_kernel(yg_ref, gate_ref, x1_ref, x1b_ref, p_ref, wg_ref, bg_ref, wp_ref, g_ref, b_ref,
                    x2_ref, x2b_ref, *, alpha):
    d = x1_ref.shape[1]
    gates = gate_ref[...]
    m = jnp.zeros(x1_ref.shape, F32)
    for r in range(TOP_K):
        m = m + yg_ref[r].astype(F32) * gates[:, r:r + 1]
    e = (jax.nn.sigmoid(_dot(x1b_ref[...], wg_ref[...]) + bg_ref[...])
         * _dot(p_ref[...].astype(BF16), wp_ref[...]))
    x2 = _layer_norm(alpha * x1_ref[...] + m + e, g_ref[...], b_ref[...])
    x2_ref[...] = x2
    x2b_ref[...] = x2.astype(BF16)


def _combine(yg, gates, x1, x1b, p, w_pg, b_pg, w_pp, ln_g, ln_b, alpha, tm=512):
    n, d = x1.shape
    tm = min(tm, n)
    row = lambda i: (i, 0)
    const = lambda i: (0, 0)
    return pl.pallas_call(
        functools.partial(_combine_kernel, alpha=alpha),
        grid=(n // tm,),
        in_specs=[pl.BlockSpec((TOP_K, tm, d), lambda i: (0, i, 0)),
                  pl.BlockSpec((tm, LANE), row),
                  pl.BlockSpec((tm, d), row),
                  pl.BlockSpec((tm, d), row),
                  pl.BlockSpec((tm, p.shape[1]), row),
                  pl.BlockSpec(w_pg.shape, const),
                  pl.BlockSpec((1, d), const),
                  pl.BlockSpec(w_pp.shape, const),
                  pl.BlockSpec((1, d), const),
                  pl.BlockSpec((1, d), const)],
        out_specs=[pl.BlockSpec((tm, d), row), pl.BlockSpec((tm, d), row)],
        out_shape=[jax.ShapeDtypeStruct((n, d), F32), jax.ShapeDtypeStruct((n, d), BF16)],
        compiler_params=_cparams(("arbitrary",)),
        name="combine_ln",
    )(yg, gates, x1, x1b, p, w_pg, b_pg, w_pp, ln_g, ln_b)


def _route(experts, ranks, counts, blk):
    n_tok = experts.shape[0]
    n_assign = n_tok * TOP_K
    sizes = counts.astype(jnp.int32)
    padded = ((sizes + blk - 1) // blk) * blk
    pends = jnp.cumsum(padded)
    pstarts = pends - padded
    onehot = experts[:, :, None] == jnp.arange(N_EXPERTS, dtype=jnp.int32)
    slot = jnp.sum(jnp.where(onehot, pstarts, 0), axis=-1) + ranks
    span = 2 * n_assign
    real = experts * span + jnp.arange(n_assign, dtype=jnp.int32).reshape(n_tok, TOP_K)
    j = jnp.arange(blk - 1, dtype=jnp.int32)
    filler = jnp.where(j[None, :] < (padded - sizes)[:, None],
                       jnp.arange(N_EXPERTS, dtype=jnp.int32)[:, None] * span + n_assign + j[None, :],
                       jnp.iinfo(jnp.int32).max)
    buf_len = ((n_assign + N_EXPERTS * (blk - 1) + blk - 1) // blk) * blk
    n_blk = buf_len // blk
    tail = jnp.full((buf_len - n_assign - N_EXPERTS * (blk - 1),), jnp.iinfo(jnp.int32).max, jnp.int32)
    assign = jnp.sort(jnp.concatenate([real.reshape(-1), filler.reshape(-1), tail])) % span
    buf_tok = jnp.where(assign < n_assign, assign // TOP_K, 0)
    blk_e = jnp.clip(jnp.searchsorted(pends, jnp.arange(n_blk, dtype=jnp.int32) * blk, side='right'),
                     0, N_EXPERTS - 1).astype(jnp.int32)
    n_used = (pends[-1] // blk).astype(jnp.int32).reshape(1)
    return buf_tok, slot, blk_e, n_used


def _regroup_w_in(w_in):
    depth, d, width = w_in.shape
    lr0 = _GV + GLA_HEADS * GLA_DV
    src = np.full((Z_WIDTH,), -1, np.int32)
    src[:lr0] = np.arange(lr0)
    src[lr0:width - GLA_GATE_RANK] = np.arange(lr0 + GLA_GATE_RANK, width)
    src[_LR:_LR + GLA_GATE_RANK] = np.arange(lr0, lr0 + GLA_GATE_RANK)
    select = (jnp.arange(width, dtype=jnp.int32)[:, None] == jnp.asarray(src)[None, :]).astype(BF16)
    out = _matmul(w_in.reshape(depth * d, width).astype(BF16), select, d, Z_WIDTH // 3, BF16, "regroup_w_in")
    return out.reshape(depth, d, Z_WIDTH)


def kernel(x, p, w_in, w_gla_gate, b_gla_gate, gla_norm_g, diff_lambda, diff_norm_g, w_out, rel_bias, ln1_g, ln1_b, w_router, b_router, w_gate_up, b_gate_up, w_down, b_down, w_ple_gate, b_ple_gate, w_ple_proj, ln2_g, ln2_b):
    batch, seq, d = x.shape
    depth = w_in.shape[0]
    n = batch * seq
    alpha = (2 * depth) ** 0.25
    moe_blk = 256
    diff_blk = min(512, seq)

    w_in_b = _regroup_w_in(w_in)
    wg = jnp.zeros((depth, LANE, GLA_HEADS * GLA_DK), F32).at[:, :GLA_GATE_RANK, :].set(w_gla_gate).astype(BF16)
    gla_g = jnp.tile(gla_norm_g, (1, GLA_HEADS))[:, None, :]
    w_out_b = w_out.astype(BF16)
    wr = jnp.zeros((depth, d, LANE), F32).at[:, :, :N_EXPERTS].set(w_router)
    wr_hi = wr.astype(BF16)
    wr_lo = (wr - wr_hi.astype(F32)).astype(BF16)
    br = jnp.zeros((depth, 1, LANE), F32).at[:, 0, :N_EXPERTS].set(b_router)
    w_gu = w_gate_up.reshape((depth * N_EXPERTS,) + w_gate_up.shape[2:])
    b_gu = b_gate_up.reshape(depth * N_EXPERTS, -1, LANE, 2).swapaxes(-1, -2).reshape(depth * N_EXPERTS, 1, -1)
    w_dn = w_down.reshape((depth * N_EXPERTS,) + w_down.shape[2:])
    b_dn = b_down.reshape(depth * N_EXPERTS, 1, -1)
    w_pg_b = w_ple_gate.astype(BF16)
    w_pp_b = w_ple_proj.astype(BF16)
    bias_tiles = _bias_tiles(rel_bias, diff_blk)
    ret_tables = _ret_tables(seq)

    xf = x.reshape(n, d)
    xb = xf.astype(BF16)
    pf = p.reshape(depth, n, p.shape[-1])
    for i in range(depth):
        lam_init = 0.8 - 0.6 * math.exp(-0.3 * i)
        z = _matmul(xb, w_in_b[i], min(1024, n), Z_WIDTH // 3, BF16, "in_proj")
        gla_o = _gla(z, wg[i], b_gla_gate[i][None, :], gla_g[i], batch, seq)
        dif_o = _diff(z, diff_lambda[i], diff_norm_g[i][None, :], bias_tiles, batch, seq, lam_init, diff_blk)
        ret_o = _ret(z, ret_tables, batch, seq)
        x1, x1b, gates, idx, cnt = _out_ln(gla_o, dif_o, ret_o, xf, w_out_b[i], ln1_g[i][None, :],
                                           ln1_b[i][None, :], wr_hi[i], wr_lo[i], br[i], alpha)
        buf_tok, slot, blk_e, n_used = _route(idx[:, :TOP_K], idx[:, TOP_K:2 * TOP_K], cnt[0, :N_EXPERTS], moe_blk)
        xs = x1b[buf_tok]
        yb = _moe(xs, blk_e + i * N_EXPERTS, n_used, w_gu, b_gu, w_dn, b_dn, moe_blk)
        yg = yb[slot.T]
        xf, xb = _combine(yg, gates, x1, x1b, pf[i], w_pg_b[i], b_ple_gate[i][None, :], w_pp_b[i],
                          ln2_g[i][None, :], ln2_b[i][None, :], alpha)
    return xf.reshape(batch, seq, d)
```

```python
import functools
import math

import numpy as np
import jax
import jax.numpy as jnp
from jax import lax
from jax.experimental import pallas as pl
from jax.experimental.pallas import tpu as pltpu

F32 = jnp.float32
BF16 = jnp.bfloat16

GLA_HEADS = 4
GLA_DK = 32
GLA_DV = 64
GLA_GATE_RANK = 16
GLA_TAU = 16.0
GLA_CHUNK = 64
DIFF_HEADS = 4
DIFF_D = 64
RET_HEADS = 4
RET_DK = 64
RET_DV = 64
RET_CHUNK = 128
T5_BUCKETS = 32
T5_MAX_DIST = 128
N_EXPERTS = 32
TOP_K = 4
SWIGLU_LIMIT = 7.0
SWIGLU_ALPHA = 1.702
LN_EPS = 1e-5
HEAD_NORM_EPS = 1e-5

LANE = 128
VMEM_LIMIT = 56 * 1024 * 1024

_GQ, _GK, _GV, _GO = 0, 128, 256, 512
_DQ, _DK, _DV = 768, 1280, 1792
_RQ, _RK, _RV, _RG = 2304, 2560, 2816, 3072
_LR = 3328
Z_WIDTH = 3456

GLA_FAST_MAX_DECAY = 60.0


def _cparams(sem):
    return pltpu.CompilerParams(dimension_semantics=sem, vmem_limit_bytes=VMEM_LIMIT)


def _split_hi_lo(a):
    hi = a.astype(BF16)
    lo = (a - hi.astype(F32)).astype(BF16)
    return hi, lo


def _dot(a, b):
    return jnp.dot(a, b, preferred_element_type=F32)


def _dot_nt(a, b):
    return lax.dot_general(a, b, (((1,), (1,)), ((), ())), preferred_element_type=F32)


def _dot_tn(a, b):
    return lax.dot_general(a, b, (((0,), (0,)), ((), ())), preferred_element_type=F32)


def _seg_sum(a, ind):
    hi, lo = _split_hi_lo(a)
    return _dot(hi, ind) + _dot(lo, ind)


def _mm_kernel(x_ref, w_ref, o_ref):
    o_ref[...] = _dot(x_ref[...], w_ref[...]).astype(o_ref.dtype)


def _matmul(x, w, tm, tn, out_dtype, name):
    m, k = x.shape
    n = w.shape[1]
    return pl.pallas_call(
        _mm_kernel,
        grid=(n // tn, m // tm),
        in_specs=[pl.BlockSpec((tm, k), lambda j, i: (i, 0)),
                  pl.BlockSpec((k, tn), lambda j, i: (0, j))],
        out_specs=pl.BlockSpec((tm, tn), lambda j, i: (i, j)),
        out_shape=jax.ShapeDtypeStruct((m, n), out_dtype),
        compiler_params=_cparams(("arbitrary", "arbitrary")),
        name=name,
    )(x, w)


def _gla_kernel(q_ref, k_ref, v_ref, og_ref, lr_ref, wg_ref, bg_ref, ng_ref, o_ref,
                state_ref, la_ref, cum_ref, kf_ref, vf_ref, oacc_ref, intra_ref, *, n_chunks, fast_max_decay):
    c_len = GLA_CHUNK
    hk = GLA_HEADS * GLA_DK
    hv = GLA_HEADS * GLA_DV

    @pl.when(pl.program_id(1) == 0)
    def _():
        state_ref[...] = jnp.zeros_like(state_ref)

    zg = _dot(lr_ref[...], wg_ref[...]) + bg_ref[...]
    la = (jnp.minimum(zg, 0.0) - jnp.log1p(jnp.exp(-jnp.abs(zg)))) / GLA_TAU
    la_ref[...] = la
    chunk_tot = jnp.sum(la.reshape(n_chunks, c_len, hk), axis=1)
    fast = jnp.min(chunk_tot) > -fast_max_decay

    row = lax.broadcasted_iota(jnp.int32, (c_len, c_len), 0)
    col = lax.broadcasted_iota(jnp.int32, (c_len, c_len), 1)
    tril = (row >= col).astype(BF16)
    sr = lax.broadcasted_iota(jnp.int32, (hv, hk), 0)
    sc = lax.broadcasted_iota(jnp.int32, (hv, hk), 1)
    state_mask = ((sr // GLA_DV) == (sc // GLA_DK)).astype(F32)
    lane_k = lax.broadcasted_iota(jnp.int32, (1, hk), 1) // GLA_DK
    lane_v = lax.broadcasted_iota(jnp.int32, (1, hv), 1) // GLA_DV
    r4 = lax.broadcasted_iota(jnp.int32, (GLA_HEADS * c_len, c_len), 0) % c_len
    c4 = lax.broadcasted_iota(jnp.int32, (GLA_HEADS * c_len, c_len), 1)
    causal4 = r4 >= c4
    er = lax.broadcasted_iota(jnp.int32, (hk, hv), 0) // GLA_DK
    ec = lax.broadcasted_iota(jnp.int32, (hk, hv), 1) // GLA_DV
    head_expand = (er == ec).astype(BF16)
    trow = lax.broadcasted_iota(jnp.int32, (c_len, hv), 0)

    def chunk(c, carry):
        r0 = pl.multiple_of(c * c_len, c_len)
        rows = pl.ds(r0, c_len)
        q = q_ref[rows, :].astype(F32) * (GLA_DK ** -0.5)
        k = k_ref[rows, :].astype(F32)
        v = v_ref[rows, :]
        la_c = la_ref[rows, :]
        la_hi, la_lo = _split_hi_lo(la_c)
        cum = _dot(tril, la_hi) + _dot(tril, la_lo)
        last = cum[c_len - 1:c_len, :]

        @pl.when(fast)
        def _():
            mid = cum[c_len // 2:c_len // 2 + 1, :]
            qa = q * jnp.exp(cum - mid)
            kb = (k * jnp.exp(mid - cum)).astype(BF16)
            qa4 = jnp.concatenate(
                [jnp.where(lane_k == h, qa, 0.0) for h in range(GLA_HEADS)], axis=0).astype(BF16)
            s4 = jnp.where(causal4, _dot_nt(qa4, kb), 0.0)
            pv = _dot(s4.astype(BF16), v)
            acc = jnp.zeros((c_len, hv), F32)
            for h in range(GLA_HEADS):
                acc = acc + jnp.where(lane_v == h, pv[h * c_len:(h + 1) * c_len, :], 0.0)
            intra_ref[...] = acc

        @pl.when(jnp.logical_not(fast))
        def _():
            cum_ref[...] = cum
            kf_ref[...] = k
            vf_ref[...] = v.astype(F32)

            def one_key(s, acc):
                ks = kf_ref[pl.ds(s, 1), :]
                vs = vf_ref[pl.ds(s, 1), :]
                cs = cum_ref[pl.ds(s, 1), :]
                w = q * ks * jnp.exp(jnp.minimum(cum - cs, 0.0))
                wsum = _dot(w.astype(BF16), head_expand)
                return acc + jnp.where(trow >= s, wsum, 0.0) * vs

            intra_ref[...] = lax.fori_loop(0, c_len, one_key, jnp.zeros((c_len, hv), F32))

        st = state_ref[...]
        qg = (q * jnp.exp(cum)).astype(BF16)
        inter = _dot_nt(qg, st.astype(BF16))
        oacc_ref[rows, :] = intra_ref[...] + inter
        kg = (k * jnp.exp(last - cum)).astype(BF16)
        upd = _dot_tn(v, kg)
        state_ref[...] = (st * jnp.exp(last) + upd) * state_mask
        return carry

    lax.fori_loop(0, n_chunks, chunk, 0)

    o = oacc_ref[...]
    vr = lax.broadcasted_iota(jnp.int32, (hv, hv), 0) // GLA_DV
    vc = lax.broadcasted_iota(jnp.int32, (hv, hv), 1) // GLA_DV
    seg = (vr == vc).astype(BF16)
    ms = _seg_sum(o * o, seg) * (1.0 / GLA_DV)
    y = o * lax.rsqrt(ms + HEAD_NORM_EPS) * ng_ref[...]
    g = og_ref[...].astype(F32)
    o_ref[...] = (y * (g * jax.nn.sigmoid(g))).astype(o_ref.dtype)


def _gla(z, w_gate, b_gate, norm_g, batch, seq, tg=512, fast_max_decay=GLA_FAST_MAX_DECAY):
    tg = min(tg, seq)
    nt = seq // tg
    hk = GLA_HEADS * GLA_DK
    hv = GLA_HEADS * GLA_DV

    def rowmap(cb):
        return lambda b, t: (b * nt + t, cb)

    const = lambda b, t: (0, 0)
    return pl.pallas_call(
        functools.partial(_gla_kernel, n_chunks=tg // GLA_CHUNK, fast_max_decay=fast_max_decay),
        grid=(batch, nt),
        in_specs=[pl.BlockSpec((tg, hk), rowmap(_GQ // hk)),
                  pl.BlockSpec((tg, hk), rowmap(_GK // hk)),
                  pl.BlockSpec((tg, hv), rowmap(_GV // hv)),
                  pl.BlockSpec((tg, hv), rowmap(_GO // hv)),
                  pl.BlockSpec((tg, LANE), rowmap(_LR // LANE)),
                  pl.BlockSpec((LANE, hk), const),
                  pl.BlockSpec((1, hk), const),
                  pl.BlockSpec((1, hv), const)],
        out_specs=pl.BlockSpec((tg, hv), lambda b, t: (b * nt + t, 0)),
        out_shape=jax.ShapeDtypeStruct((batch * seq, hv), BF16),
        scratch_shapes=[pltpu.VMEM((hv, hk), F32),
                        pltpu.VMEM((tg, hk), F32),
                        pltpu.VMEM((GLA_CHUNK, hk), F32),
                        pltpu.VMEM((GLA_CHUNK, hk), F32),
                        pltpu.VMEM((GLA_CHUNK, hv), F32),
                        pltpu.VMEM((tg, hv), F32),
                        pltpu.VMEM((GLA_CHUNK, hv), F32)],
        compiler_params=_cparams(("arbitrary", "arbitrary")),
        name="gla",
    )(z, z, z, z, z, w_gate, b_gate, norm_g)


LOG2E = math.log2(math.e)


def _diff_kernel(far_ref, lam_ref, q_ref, k_ref, v_ref, bias_ref, ng_ref, o_ref,
                 m_ref, acc_ref, *, blk, lam_init):
    qi = pl.program_id(2)
    d = DIFF_D
    w = 2 * DIFF_D
    q = (q_ref[...].astype(F32) * (DIFF_D ** -0.5 * LOG2E)).astype(BF16)
    lane_q = lax.broadcasted_iota(jnp.int32, q.shape, 1)
    q2 = jnp.concatenate([jnp.where(lane_q < d, q, jnp.zeros_like(q)),
                          jnp.where(lane_q >= d, q, jnp.zeros_like(q))], axis=0)
    m_ref[...] = jnp.full_like(m_ref, -jnp.inf)
    acc_ref[...] = jnp.zeros_like(acc_ref)

    def kv_step(j, bias, shift):
        rows = pl.ds(pl.multiple_of(j * blk, blk), blk)
        vj = v_ref[rows, :]
        s_both = _dot_nt(q2, k_ref[rows, :])
        for m in range(2):
            s = s_both[m * blk:(m + 1) * blk, :]
            if bias is not None:
                s = s + bias
            m_cur = jnp.max(s, axis=-1, keepdims=True)
            if shift is not None:
                m_cur = m_cur + shift
            m_old = m_ref[m]
            m_new = jnp.maximum(m_old, m_cur)
            alpha = jnp.exp2(m_old - m_new)
            sub = m_new if shift is None else m_new - shift
            p = jnp.exp2(s - jnp.concatenate([sub] * (blk // LANE), axis=1))
            row_sum = jnp.broadcast_to(jnp.sum(p, axis=-1, keepdims=True), (blk, LANE))
            acc_ref[m] = (jnp.concatenate([alpha, alpha], axis=1) * acc_ref[m]
                          + jnp.concatenate([_dot(p.astype(BF16), vj), row_sum], axis=1))
            m_ref[m] = m_new

    far = far_ref[pl.program_id(1)]

    def far_step(j, carry):
        kv_step(j, None, far)
        return carry

    lax.fori_loop(0, jnp.maximum(qi - 1, 0), far_step, 0)

    @pl.when(qi >= 1)
    def _():
        kv_step(qi - 1, bias_ref[0, 1], None)

    kv_step(qi, bias_ref[0, 0], None)

    lf = lam_ref[...]
    lam = (jnp.exp(jnp.sum(lf[0:1] * lf[1:2], axis=-1, keepdims=True))
           - jnp.exp(jnp.sum(lf[2:3] * lf[3:4], axis=-1, keepdims=True)) + lam_init)
    a0 = acc_ref[0]
    a1 = acc_ref[1]
    o = a0[:, :w] / a0[:, w:] - lam * (a1[:, :w] / a1[:, w:])
    ms = jnp.mean(o * o, axis=-1, keepdims=True)
    y = o * lax.rsqrt(ms + HEAD_NORM_EPS) * ng_ref[...] * (1.0 - lam_init)
    o_ref[...] = y.astype(o_ref.dtype)


def _t5_bucket(rel):
    n = jnp.maximum(-rel, 0)
    max_exact = T5_BUCKETS // 2
    nf = jnp.maximum(n, 1).astype(F32)
    large = max_exact + (jnp.log(nf / max_exact) / math.log(T5_MAX_DIST / max_exact)
                         * (T5_BUCKETS - max_exact)).astype(jnp.int32)
    large = jnp.minimum(large, T5_BUCKETS - 1)
    return jnp.where(n < max_exact, n, large)


def _bias_tiles(rel_bias, blk):
    assert blk + 1 >= T5_MAX_DIST
    m = jnp.arange(2 * blk)
    col_minus_row = jnp.where(m < blk, m, m - 2 * blk)
    tiles = []
    for delta in range(2):
        rel = col_minus_row - delta * blk
        w = rel_bias[_t5_bucket(rel)].astype(F32).T
        if delta == 0:
            w = jnp.where(rel <= 0, w, -jnp.inf)
        flat = jnp.tile(w, (1, blk))[:, :blk * (2 * blk - 1)]
        tiles.append(flat.reshape(-1, blk, 2 * blk - 1)[:, :, :blk])
    far = rel_bias[_t5_bucket(jnp.full((1,), -(blk + 1)))[0]].astype(F32)
    return jnp.stack(tiles, axis=1) * LOG2E, far * LOG2E


def _diff(z, diff_lambda, norm_g, bias, batch, seq, lam_init, blk):
    blk = min(blk, seq)
    nq = seq // blk
    w = 2 * DIFF_D
    bias_tiles, bias_far = bias
    return pl.pallas_call(
        functools.partial(_diff_kernel, blk=blk, lam_init=lam_init),
        grid=(batch, DIFF_HEADS, nq),
        in_specs=[pl.BlockSpec(memory_space=pltpu.SMEM),
                  pl.BlockSpec((4, DIFF_D), lambda b, h, i: (0, 0)),
                  pl.BlockSpec((blk, w), lambda b, h, i: (b * nq + i, _DQ // w + h)),
                  pl.BlockSpec((seq, w), lambda b, h, i: (b, _DK // w + h)),
                  pl.BlockSpec((seq, w), lambda b, h, i: (b, _DV // w + h)),
                  pl.BlockSpec((1, 2, blk, blk), lambda b, h, i: (h, 0, 0, 0)),
                  pl.BlockSpec((1, w), lambda b, h, i: (0, 0))],
        out_specs=pl.BlockSpec((blk, w), lambda b, h, i: (b * nq + i, h)),
        out_shape=jax.ShapeDtypeStruct((batch * seq, DIFF_HEADS * w), BF16),
        scratch_shapes=[pltpu.VMEM((2, blk, LANE), F32),
                        pltpu.VMEM((2, blk, w + LANE), F32)],
        compiler_params=_cparams(("arbitrary", "arbitrary", "arbitrary")),
        name="diff_attn",
    )(bias_far, diff_lambda, z, z, z, bias_tiles, norm_g)


def _ret_kernel(q_ref, k_ref, v_ref, g_ref, sin_ref, cos_ref, inner_ref, cross_ref, sdec_ref, cdec_ref,
                o_ref, state_ref):
    c_len = RET_CHUNK
    hk = RET_HEADS * RET_DK
    hv = RET_HEADS * RET_DV

    @pl.when(pl.program_id(1) == 0)
    def _():
        state_ref[...] = jnp.zeros_like(state_ref)

    sin = sin_ref[...]
    cos = cos_ref[...]
    even = (lax.broadcasted_iota(jnp.int32, (1, hk), 1) % 2) == 0

    def rotary(t):
        nxt = pltpu.roll(t, hk - 1, 1)
        prv = pltpu.roll(t, 1, 1)
        return t * cos + jnp.where(even, -nxt, prv) * sin

    qf = rotary(q_ref[...].astype(F32))
    kf = rotary(k_ref[...].astype(F32)) * (RET_DK ** -0.5)
    v = v_ref[...]

    lane_k = lax.broadcasted_iota(jnp.int32, (1, hk), 1) // RET_DK
    lane_v = lax.broadcasted_iota(jnp.int32, (1, hv), 1) // RET_DV
    q4 = jnp.concatenate([jnp.where(lane_k == h, qf, 0.0) for h in range(RET_HEADS)], axis=0).astype(BF16)
    s4 = _dot_nt(q4, kf.astype(BF16)) * inner_ref[...]
    pv = _dot(s4.astype(BF16), v)
    acc = jnp.zeros((c_len, hv), F32)
    for h in range(RET_HEADS):
        acc = acc + jnp.where(lane_v == h, pv[h * c_len:(h + 1) * c_len, :], 0.0)

    st = state_ref[...]
    o = acc + _dot_nt(qf.astype(BF16), st.astype(BF16)) * cross_ref[...]
    ks = (kf * sdec_ref[...]).astype(BF16)
    sr = lax.broadcasted_iota(jnp.int32, (hv, hk), 0) // RET_DV
    sc = lax.broadcasted_iota(jnp.int32, (hv, hk), 1) // RET_DK
    state_ref[...] = jnp.where(sr == sc, st * cdec_ref[...] + _dot_tn(v, ks), 0.0)

    vr = lax.broadcasted_iota(jnp.int32, (hv, hv), 0) // RET_DV
    vc = lax.broadcasted_iota(jnp.int32, (hv, hv), 1) // RET_DV
    seg = (vr == vc).astype(BF16)
    ms = _seg_sum(o * o, seg) * (1.0 / RET_DV)
    g = g_ref[...].astype(F32)
    o_ref[...] = (o * lax.rsqrt(ms + HEAD_NORM_EPS) * (g * jax.nn.sigmoid(g))).astype(o_ref.dtype)


def _ret_tables(seq):
    c_len = RET_CHUNK
    pos = jnp.arange(seq, dtype=F32)
    angle = 1.0 / (10000.0 ** jnp.linspace(0.0, 1.0, RET_DK // 2, dtype=F32))
    angle = jnp.repeat(angle, 2)
    sin = jnp.tile(jnp.sin(pos[:, None] * angle), (1, RET_HEADS))
    cos = jnp.tile(jnp.cos(pos[:, None] * angle), (1, RET_HEADS))
    log_g = jnp.log1p(-jnp.exp2(-5.0 - jnp.arange(RET_HEADS, dtype=F32)))
    idx = jnp.arange(c_len, dtype=F32)
    rel = idx[:, None] - idx[None, :]
    inner = jnp.where(rel[None] >= 0, jnp.exp(jnp.maximum(rel, 0.0)[None] * log_g[:, None, None]), 0.0)
    inner = inner.reshape(RET_HEADS * c_len, c_len)
    cross = jnp.exp((idx + 1.0)[None] * log_g[:, None])
    cross = jnp.repeat(cross.T, RET_DV, axis=1)
    sdec = jnp.exp((c_len - 1.0 - idx)[None] * log_g[:, None])
    sdec = jnp.repeat(sdec.T, RET_DK, axis=1)
    cdec = jnp.repeat(jnp.exp(c_len * log_g), RET_DK)[None, :]
    return sin, cos, inner, cross, sdec, cdec


def _ret(z, tables, batch, seq):
    c_len = RET_CHUNK
    nt = seq // c_len
    hk = RET_HEADS * RET_DK
    hv = RET_HEADS * RET_DV
    sin, cos, inner, cross, sdec, cdec = tables

    def rowmap(cb):
        return lambda b, t: (b * nt + t, cb)

    const = lambda b, t: (0, 0)
    return pl.pallas_call(
        _ret_kernel,
        grid=(batch, nt),
        in_specs=[pl.BlockSpec((c_len, hk), rowmap(_RQ // hk)),
                  pl.BlockSpec((c_len, hk), rowmap(_RK // hk)),
                  pl.BlockSpec((c_len, hv), rowmap(_RV // hv)),
                  pl.BlockSpec((c_len, hv), rowmap(_RG // hv)),
                  pl.BlockSpec((c_len, hk), lambda b, t: (t, 0)),
                  pl.BlockSpec((c_len, hk), lambda b, t: (t, 0)),
                  pl.BlockSpec((RET_HEADS * c_len, c_len), const),
                  pl.BlockSpec((c_len, hv), const),
                  pl.BlockSpec((c_len, hk), const),
                  pl.BlockSpec((1, hk), const)],
        out_specs=pl.BlockSpec((c_len, hv), lambda b, t: (b * nt + t, 0)),
        out_shape=jax.ShapeDtypeStruct((batch * seq, hv), BF16),
        scratch_shapes=[pltpu.VMEM((hv, hk), F32)],
        compiler_params=_cparams(("arbitrary", "arbitrary")),
        name="retention",
    )(z, z, z, z, sin, cos, inner, cross, sdec, cdec)


def _layer_norm(t, g, b):
    mu = jnp.mean(t, axis=-1, keepdims=True)
    var = jnp.mean(jnp.square(t - mu), axis=-1, keepdims=True)
    return (t - mu) * lax.rsqrt(var + LN_EPS) * g + b


def _out_ln_kernel(gla_ref, dif_ref, ret_ref, x_ref, wo_ref, g_ref, b_ref, wr_hi_ref, wr_lo_ref, br_ref,
                   x1_ref, x1b_ref, gate_ref, idx_ref, cnt_ref, *, alpha):
    w0 = gla_ref.shape[1]
    w1 = w0 + dif_ref.shape[1]
    a = (_dot(gla_ref[...], wo_ref[0:w0, :]) + _dot(dif_ref[...], wo_ref[w0:w1, :])
         + _dot(ret_ref[...], wo_ref[w1:, :]))
    x1 = _layer_norm(alpha * x_ref[...] + a, g_ref[...], b_ref[...])
    x1_ref[...] = x1
    x1b_ref[...] = x1.astype(BF16)

    hi, lo = _split_hi_lo(x1)
    logits = (_dot(hi, wr_hi_ref[...]) + _dot(lo, wr_hi_ref[...]) + _dot(hi, wr_lo_ref[...])) + br_ref[...]
    lane = lax.broadcasted_iota(jnp.int32, logits.shape, 1)
    work = jnp.where(lane < N_EXPERTS, logits, -jnp.inf)
    vals = jnp.zeros(logits.shape, F32)
    idxs = jnp.zeros(logits.shape, jnp.int32)
    chosen = jnp.zeros(logits.shape, F32)
    picks = []
    top0 = None
    for r in range(TOP_K):
        mx = jnp.max(work, axis=-1, keepdims=True)
        am = jnp.min(jnp.where(work == mx, lane, LANE), axis=-1, keepdims=True)
        if r == 0:
            top0 = mx
        vals = jnp.where(lane == r, jnp.exp(mx - top0), vals)
        idxs = jnp.where(lane == r, am, idxs)
        chosen = jnp.where(lane == am, 1.0, chosen)
        work = jnp.where(lane == am, -jnp.inf, work)
        picks.append(am)
    gate_ref[...] = vals / jnp.sum(vals, axis=-1, keepdims=True)

    @pl.when(pl.program_id(0) == 0)
    def _():
        cnt_ref[...] = jnp.zeros_like(cnt_ref)

    tm = logits.shape[0]
    earlier = (lax.broadcasted_iota(jnp.int32, (tm, tm), 0)
               > lax.broadcasted_iota(jnp.int32, (tm, tm), 1)).astype(BF16)
    before = _dot(earlier, chosen.astype(BF16)) + cnt_ref[...]
    for r in range(TOP_K):
        rank = jnp.sum(jnp.where(lane == picks[r], before, 0.0), axis=-1, keepdims=True)
        idxs = jnp.where(lane == TOP_K + r, rank.astype(jnp.int32), idxs)
    idx_ref[...] = idxs
    cnt_ref[...] = cnt_ref[...] + jnp.sum(chosen, axis=0, keepdims=True)


def _out_ln(gla_o, dif_o, ret_o, x, w_out, ln_g, ln_b, wr_hi, wr_lo, b_router, alpha, tm=512):
    n, d = x.shape
    tm = min(tm, n)
    row = lambda i: (i, 0)
    const = lambda i: (0, 0)
    return pl.pallas_call(
        functools.partial(_out_ln_kernel, alpha=alpha),
        grid=(n // tm,),
        in_specs=[pl.BlockSpec((tm, gla_o.shape[1]), row),
                  pl.BlockSpec((tm, dif_o.shape[1]), row),
                  pl.BlockSpec((tm, ret_o.shape[1]), row),
                  pl.BlockSpec((tm, d), row),
                  pl.BlockSpec(w_out.shape, const),
                  pl.BlockSpec((1, d), const),
                  pl.BlockSpec((1, d), const),
                  pl.BlockSpec((d, LANE), const),
                  pl.BlockSpec((d, LANE), const),
                  pl.BlockSpec((1, LANE), const)],
        out_specs=[pl.BlockSpec((tm, d), row),
                   pl.BlockSpec((tm, d), row),
                   pl.BlockSpec((tm, LANE), row),
                   pl.BlockSpec((tm, LANE), row),
                   pl.BlockSpec((1, LANE), const)],
        out_shape=[jax.ShapeDtypeStruct((n, d), F32),
                   jax.ShapeDtypeStruct((n, d), BF16),
                   jax.ShapeDtypeStruct((n, LANE), F32),
                   jax.ShapeDtypeStruct((n, LANE), jnp.int32),
                   jax.ShapeDtypeStruct((1, LANE), F32)],
        compiler_params=_cparams(("arbitrary",)),
        name="out_ln_router",
    )(gla_o, dif_o, ret_o, x, w_out, ln_g, ln_b, wr_hi, wr_lo, b_router)


GU_GROUP = 2 * LANE


def _deinterleave_perm():
    perm = np.zeros((GU_GROUP, GU_GROUP), np.float32)
    half = np.arange(LANE)
    perm[2 * half, half] = 1.0
    perm[2 * half + 1, LANE + half] = 1.0
    return jnp.asarray(perm, BF16)


def _moe_kernel(blk_e_ref, n_used_ref, x_ref, wgu_ref, bgu_ref, wdn_ref, bdn_ref, perm_ref, o_ref,
                wgu_b_ref, wdn_b_ref, *, rows_per_cast):
    i = pl.program_id(0)
    used = i < n_used_ref[0]
    new_expert = jnp.logical_or(i == 0, blk_e_ref[i] != blk_e_ref[jnp.maximum(i - 1, 0)])

    @pl.when(jnp.logical_and(used, new_expert))
    def _():
        d = wgu_ref.shape[0]
        for r in range(d // rows_per_cast):
            rows = slice(r * rows_per_cast, (r + 1) * rows_per_cast)
            wdn_b_ref[rows, :] = wdn_ref[rows, :].astype(BF16)
            for g in range(wgu_ref.shape[1] // GU_GROUP):
                cols = slice(g * GU_GROUP, (g + 1) * GU_GROUP)
                wgu_b_ref[rows, cols] = _dot(wgu_ref[rows, cols].astype(BF16), perm_ref[...]).astype(BF16)

    @pl.when(used)
    def _():
        gu = _dot(x_ref[...], wgu_b_ref[...]) + bgu_ref[...]
        acts = []
        for grp in range(gu.shape[1] // GU_GROUP):
            g = jnp.minimum(gu[:, grp * GU_GROUP:grp * GU_GROUP + LANE], SWIGLU_LIMIT)
            u = jnp.clip(gu[:, grp * GU_GROUP + LANE:(grp + 1) * GU_GROUP], -SWIGLU_LIMIT, SWIGLU_LIMIT)
            acts.append(((u + 1.0) * (g * jax.nn.sigmoid(g * SWIGLU_ALPHA))).astype(BF16))
        act = jnp.concatenate(acts, axis=1)
        o_ref[...] = (_dot(act, wdn_b_ref[...]) + bdn_ref[...]).astype(o_ref.dtype)

    @pl.when(jnp.logical_not(used))
    def _():
        o_ref[...] = jnp.zeros_like(o_ref)


def _moe(xs, blk_e, n_used, w_gu, b_gu, w_dn, b_dn, blk):
    rows, d = xs.shape
    f2 = w_gu.shape[2]
    f = w_dn.shape[1]
    assert f == d
    grid_spec = pltpu.PrefetchScalarGridSpec(
        num_scalar_prefetch=2,
        grid=(rows // blk,),
        in_specs=[pl.BlockSpec((blk, d), lambda i, be, nu: (i, 0)),
                  pl.BlockSpec((None, d, f2), lambda i, be, nu: (be[i], 0, 0)),
                  pl.BlockSpec((None, 1, f2), lambda i, be, nu: (be[i], 0, 0)),
                  pl.BlockSpec((None, f, d), lambda i, be, nu: (be[i], 0, 0)),
                  pl.BlockSpec((None, 1, d), lambda i, be, nu: (be[i], 0, 0)),
                  pl.BlockSpec((GU_GROUP, GU_GROUP), lambda i, be, nu: (0, 0))],
        out_specs=pl.BlockSpec((blk, d), lambda i, be, nu: (i, 0)),
        scratch_shapes=[pltpu.VMEM((d, f2), BF16), pltpu.VMEM((f, d), BF16)],
    )
    return pl.pallas_call(
        functools.partial(_moe_kernel, rows_per_cast=256),
        grid_spec=grid_spec,
        out_shape=jax.ShapeDtypeStruct((rows, d), BF16),
        compiler_params=_cparams(("arbitrary",)),
        name="moe_experts",
    )(blk_e, n_used, xs, w_gu, b_gu, w_dn, b_dn, _deinterleave_perm())


def _combine_kernel(yg_ref, gate_ref, x1_ref, x1b_ref, p_ref, wg_ref, bg_ref, wp_ref, g_ref, b_ref,
                    x2_ref, x2b_ref, *, alpha):
    d = x1_ref.shape[1]
    gates = gate_ref[...]
    m = jnp.zeros(x1_ref.shape, F32)
    for r in range(TOP_K):
        m = m + yg_ref[r].astype(F32) * gates[:, r:r + 1]
    e = (jax.nn.sigmoid(_dot(x1b_ref[...], wg_ref[...]) + bg_ref[...])
         * _dot(p_ref[...].astype(BF16), wp_ref[...]))
    x2 = _layer_norm(alpha * x1_ref[...] + m + e, g_ref[...], b_ref[...])
    x2_ref[...] = x2
    x2b_ref[...] = x2.astype(BF16)


def _combine(yg, gates, x1, x1b, p, w_pg, b_pg, w_pp, ln_g, ln_b, alpha, tm=512):
    n, d = x1.shape
    tm = min(tm, n)
    row = lambda i: (i, 0)
    const = lambda i: (0, 0)
    return pl.pallas_call(
        functools.partial(_combine_kernel, alpha=alpha),
        grid=(n // tm,),
        in_specs=[pl.BlockSpec((TOP_K, tm, d), lambda i: (0, i, 0)),
                  pl.BlockSpec((tm, LANE), row),
                  pl.BlockSpec((tm, d), row),
                  pl.BlockSpec((tm, d), row),
                  pl.BlockSpec((tm, p.shape[1]), row),
                  pl.BlockSpec(w_pg.shape, const),
                  pl.BlockSpec((1, d), const),
                  pl.BlockSpec(w_pp.shape, const),
                  pl.BlockSpec((1, d), const),
                  pl.BlockSpec((1, d), const)],
        out_specs=[pl.BlockSpec((tm, d), row), pl.BlockSpec((tm, d), row)],
        out_shape=[jax.ShapeDtypeStruct((n, d), F32), jax.ShapeDtypeStruct((n, d), BF16)],
        compiler_params=_cparams(("arbitrary",)),
        name="combine_ln",
    )(yg, gates, x1, x1b, p, w_pg, b_pg, w_pp, ln_g, ln_b)


def _route(experts, ranks, counts, blk):
    n_tok = experts.shape[0]
    n_assign = n_tok * TOP_K
    sizes = counts.astype(jnp.int32)
    padded = ((sizes + blk - 1) // blk) * blk
    pends = jnp.cumsum(padded)
    pstarts = pends - padded
    onehot = experts[:, :, None] == jnp.arange(N_EXPERTS, dtype=jnp.int32)
    slot = jnp.sum(jnp.where(onehot, pstarts, 0), axis=-1) + ranks
    span = 2 * n_assign
    real = experts * span + jnp.arange(n_assign, dtype=jnp.int32).reshape(n_tok, TOP_K)
    j = jnp.arange(blk - 1, dtype=jnp.int32)
    filler = jnp.where(j[None, :] < (padded - sizes)[:, None],
                       jnp.arange(N_EXPERTS, dtype=jnp.int32)[:, None] * span + n_assign + j[None, :],
                       jnp.iinfo(jnp.int32).max)
    buf_len = ((n_assign + N_EXPERTS * (blk - 1) + blk - 1) // blk) * blk
    n_blk = buf_len // blk
    tail = jnp.full((buf_len - n_assign - N_EXPERTS * (blk - 1),), jnp.iinfo(jnp.int32).max, jnp.int32)
    assign = jnp.sort(jnp.concatenate([real.reshape(-1), filler.reshape(-1), tail])) % span
    buf_tok = jnp.where(assign < n_assign, assign // TOP_K, 0)
    blk_e = jnp.clip(jnp.searchsorted(pends, jnp.arange(n_blk, dtype=jnp.int32) * blk, side='right'),
                     0, N_EXPERTS - 1).astype(jnp.int32)
    n_used = (pends[-1] // blk).astype(jnp.int32).reshape(1)
    return buf_tok, slot, blk_e, n_used


def _regroup_w_in(w_in):
    depth, d, width = w_in.shape
    lr0 = _GV + GLA_HEADS * GLA_DV
    src = np.full((Z_WIDTH,), -1, np.int32)
    src[:lr0] = np.arange(lr0)
    src[lr0:width - GLA_GATE_RANK] = np.arange(lr0 + GLA_GATE_RANK, width)
    src[_LR:_LR + GLA_GATE_RANK] = np.arange(lr0, lr0 + GLA_GATE_RANK)
    select = (jnp.arange(width, dtype=jnp.int32)[:, None] == jnp.asarray(src)[None, :]).astype(BF16)
    out = _matmul(w_in.reshape(depth * d, width).astype(BF16), select, d, Z_WIDTH // 3, BF16, "regroup_w_in")
    return out.reshape(depth, d, Z_WIDTH)


def kernel(x, p, w_in, w_gla_gate, b_gla_gate, gla_norm_g, diff_lambda, diff_norm_g, w_out, rel_bias, ln1_g, ln1_b, w_router, b_router, w_gate_up, b_gate_up, w_down, b_down, w_ple_gate, b_ple_gate, w_ple_proj, ln2_g, ln2_b):
    batch, seq, d = x.shape
    depth = w_in.shape[0]
    n = batch * seq
    alpha = (2 * depth) ** 0.25
    moe_blk = 256
    diff_blk = min(512, seq)

    w_in_b = _regroup_w_in(w_in)
    wg = jnp.zeros((depth, LANE, GLA_HEADS * GLA_DK), F32).at[:, :GLA_GATE_RANK, :].set(w_gla_gate).astype(BF16)
    gla_g = jnp.tile(gla_norm_g, (1, GLA_HEADS))[:, None, :]
    w_out_b = w_out.astype(BF16)
    wr = jnp.zeros((depth, d, LANE), F32).at[:, :, :N_EXPERTS].set(w_router)
    wr_hi = wr.astype(BF16)
    wr_lo = (wr - wr_hi.astype(F32)).astype(BF16)
    br = jnp.zeros((depth, 1, LANE), F32).at[:, 0, :N_EXPERTS].set(b_router)
    w_gu = w_gate_up.reshape((depth * N_EXPERTS,) + w_gate_up.shape[2:])
    b_gu = b_gate_up.reshape(depth * N_EXPERTS, -1, LANE, 2).swapaxes(-1, -2).reshape(depth * N_EXPERTS, 1, -1)
    w_dn = w_down.reshape((depth * N_EXPERTS,) + w_down.shape[2:])
    b_dn = b_down.reshape(depth * N_EXPERTS, 1, -1)
    w_pg_b = w_ple_gate.astype(BF16)
    w_pp_b = w_ple_proj.astype(BF16)
    bias_tiles = _bias_tiles(rel_bias, diff_blk)
    ret_tables = _ret_tables(seq)

    n_streams = 2 if batch % 2 == 0 else 1
    sb = batch // n_streams
    sn = sb * seq
    xfs = [x[s * sb:(s + 1) * sb].reshape(sn, d) for s in range(n_streams)]
    xbs = [xf.astype(BF16) for xf in xfs]
    pfs = [p[:, s * sb:(s + 1) * sb].reshape(depth, sn, p.shape[-1]) for s in range(n_streams)]
    for i in range(depth):
        lam_init = 0.8 - 0.6 * math.exp(-0.3 * i)
        for s in range(n_streams):
            z = _matmul(xbs[s], w_in_b[i], min(1024, sn), Z_WIDTH // 3, BF16, "in_proj")
            gla_o = _gla(z, wg[i], b_gla_gate[i][None, :], gla_g[i], sb, seq)
            dif_o = _diff(z, diff_lambda[i], diff_norm_g[i][None, :], bias_tiles, sb, seq, lam_init, diff_blk)
            ret_o = _ret(z, ret_tables, sb, seq)
            x1, x1b, gates, idx, cnt = _out_ln(gla_o, dif_o, ret_o, xfs[s], w_out_b[i], ln1_g[i][None, :],
                                               ln1_b[i][None, :], wr_hi[i], wr_lo[i], br[i], alpha)
            buf_tok, slot, blk_e, n_used = _route(idx[:, :TOP_K], idx[:, TOP_K:2 * TOP_K], cnt[0, :N_EXPERTS],
                                                  moe_blk)
            xs = x1b[buf_tok]
            yb = _moe(xs, blk_e + i * N_EXPERTS, n_used, w_gu, b_gu, w_dn, b_dn, moe_blk)
            yg = yb[slot.T]
            xfs[s], xbs[s] = _combine(yg, gates, x1, x1b, pfs[s][i], w_pg_b[i], b_ple_gate[i][None, :], w_pp_b[i],
                                      ln2_g[i][None, :], ln2_b[i][None, :], alpha)
    return jnp.concatenate(xfs, axis=0).reshape(batch, seq, d)
```

```python
import functools
import math

import numpy as np
import jax
import jax.numpy as jnp
from jax import lax
from jax.experimental import pallas as pl
from jax.experimental.pallas import tpu as pltpu
from jax.experimental.pallas import tpu_sc as plsc

F32 = jnp.float32
BF16 = jnp.bfloat16

GLA_HEADS = 4
GLA_DK = 32
GLA_DV = 64
GLA_GATE_RANK = 16
GLA_TAU = 16.0
GLA_CHUNK = 64
DIFF_HEADS = 4
DIFF_D = 64
RET_HEADS = 4
RET_DK = 64
RET_DV = 64
RET_CHUNK = 128
T5_BUCKETS = 32
T5_MAX_DIST = 128
N_EXPERTS = 32
TOP_K = 4
SWIGLU_LIMIT = 7.0
SWIGLU_ALPHA = 1.702
LN_EPS = 1e-5
HEAD_NORM_EPS = 1e-5

LANE = 128
VMEM_LIMIT = 56 * 1024 * 1024

_GQ, _GK, _GV, _GO = 0, 128, 256, 512
_DQ, _DK, _DV = 768, 1280, 1792
_RQ, _RK, _RV, _RG = 2304, 2560, 2816, 3072
_LR = 3328
Z_WIDTH = 3456

GLA_FAST_MAX_DECAY = 60.0


def _cparams(sem):
    return pltpu.CompilerParams(dimension_semantics=sem, vmem_limit_bytes=VMEM_LIMIT)


def _split_hi_lo(a):
    hi = a.astype(BF16)
    lo = (a - hi.astype(F32)).astype(BF16)
    return hi, lo


def _dot(a, b):
    return jnp.dot(a, b, preferred_element_type=F32)


def _dot_nt(a, b):
    return lax.dot_general(a, b, (((1,), (1,)), ((), ())), preferred_element_type=F32)


def _dot_tn(a, b):
    return lax.dot_general(a, b, (((0,), (0,)), ((), ())), preferred_element_type=F32)


def _seg_sum(a, ind):
    hi, lo = _split_hi_lo(a)
    return _dot(hi, ind) + _dot(lo, ind)


def _pack_bf16_pairs(t):
    h = t.shape[1] // 2
    bits = lax.bitcast_convert_type(t.astype(BF16).astype(F32), jnp.uint32)
    return (bits[:, :h] >> 16) | (bits[:, h:] & jnp.uint32(0xFFFF0000))


def _unpack_bf16_pairs(packed):
    lo = lax.bitcast_convert_type(packed << 16, F32)
    hi = lax.bitcast_convert_type(packed & jnp.uint32(0xFFFF0000), F32)
    return jnp.concatenate([lo, hi], axis=1)


SC_WINDOW = 128
SC_ROW_WORDS = 256


def _sc_gather(table, indices):
    m = indices.shape[0]
    words = table.shape[1]
    assert words == SC_ROW_WORDS and m % SC_WINDOW == 0
    mesh = plsc.VectorSubcoreMesh(core_axis_name="core", subcore_axis_name="subcore")

    def body(x_hbm, i_hbm, o_hbm):
        def step(i_vmem, o_vmem):
            pltpu.sync_copy(x_hbm.at[i_vmem.at[0]], o_vmem)

        pltpu.emit_pipeline(
            step,
            grid=(m // SC_WINDOW,),
            in_specs=[pl.BlockSpec((1, SC_WINDOW), index_map=lambda i: (0, i))],
            out_specs=[pl.BlockSpec((SC_WINDOW, words), index_map=lambda i: (i, 0))],
            core_axis_name=("core", "subcore"),
            dimension_semantics=(pltpu.PARALLEL,),
        )(i_hbm, o_hbm)

    return pl.kernel(body, out_type=jax.ShapeDtypeStruct((m, words), table.dtype), mesh=mesh,
                     scratch_types=[])(table, indices.reshape(1, m))


def _gather_rows(table, rows):
    r, w = table.shape
    parts = w // SC_ROW_WORDS
    idx = (rows[:, None] * parts + jnp.arange(parts, dtype=jnp.int32)[None, :]).reshape(-1)
    return _sc_gather(table.reshape(r * parts, SC_ROW_WORDS), idx).reshape(rows.shape[0], w)


def _mm_kernel(x_ref, w_ref, o_ref):
    o_ref[...] = _dot(x_ref[...], w_ref[...]).astype(o_ref.dtype)


def _matmul(x, w, tm, tn, out_dtype, name):
    m, k = x.shape
    n = w.shape[1]
    return pl.pallas_call(
        _mm_kernel,
        grid=(n // tn, m // tm),
        in_specs=[pl.BlockSpec((tm, k), lambda j, i: (i, 0)),
                  pl.BlockSpec((k, tn), lambda j, i: (0, j))],
        out_specs=pl.BlockSpec((tm, tn), lambda j, i: (i, j)),
        out_shape=jax.ShapeDtypeStruct((m, n), out_dtype),
        compiler_params=_cparams(("arbitrary", "arbitrary")),
        name=name,
    )(x, w)


def _gla_kernel(q_ref, k_ref, v_ref, og_ref, lr_ref, wg_ref, bg_ref, ng_ref, o_ref,
                state_ref, la_ref, cum_ref, kf_ref, vf_ref, oacc_ref, intra_ref, *, n_chunks, fast_max_decay):
    c_len = GLA_CHUNK
    hk = GLA_HEADS * GLA_DK
    hv = GLA_HEADS * GLA_DV

    @pl.when(pl.program_id(1) == 0)
    def _():
        state_ref[...] = jnp.zeros_like(state_ref)

    zg = _dot(lr_ref[...], wg_ref[...]) + bg_ref[...]
    la = (jnp.minimum(zg, 0.0) - jnp.log1p(jnp.exp(-jnp.abs(zg)))) / GLA_TAU
    la_ref[...] = la
    chunk_tot = jnp.sum(la.reshape(n_chunks, c_len, hk), axis=1)
    fast = jnp.min(chunk_tot) > -fast_max_decay

    row = lax.broadcasted_iota(jnp.int32, (c_len, c_len), 0)
    col = lax.broadcasted_iota(jnp.int32, (c_len, c_len), 1)
    tril = (row >= col).astype(BF16)
    sr = lax.broadcasted_iota(jnp.int32, (hv, hk), 0)
    sc = lax.broadcasted_iota(jnp.int32, (hv, hk), 1)
    state_mask = ((sr // GLA_DV) == (sc // GLA_DK)).astype(F32)
    lane_k = lax.broadcasted_iota(jnp.int32, (1, hk), 1) // GLA_DK
    lane_v = lax.broadcasted_iota(jnp.int32, (1, hv), 1) // GLA_DV
    r4 = lax.broadcasted_iota(jnp.int32, (GLA_HEADS * c_len, c_len), 0) % c_len
    c4 = lax.broadcasted_iota(jnp.int32, (GLA_HEADS * c_len, c_len), 1)
    causal4 = r4 >= c4
    er = lax.broadcasted_iota(jnp.int32, (hk, hv), 0) // GLA_DK
    ec = lax.broadcasted_iota(jnp.int32, (hk, hv), 1) // GLA_DV
    head_expand = (er == ec).astype(BF16)
    trow = lax.broadcasted_iota(jnp.int32, (c_len, hv), 0)

    def chunk(c, carry):
        r0 = pl.multiple_of(c * c_len, c_len)
        rows = pl.ds(r0, c_len)
        q = q_ref[rows, :].astype(F32) * (GLA_DK ** -0.5)
        k = k_ref[rows, :].astype(F32)
        v = v_ref[rows, :]
        la_c = la_ref[rows, :]
        la_hi, la_lo = _split_hi_lo(la_c)
        cum = _dot(tril, la_hi) + _dot(tril, la_lo)
        last = cum[c_len - 1:c_len, :]

        @pl.when(fast)
        def _():
            mid = cum[c_len // 2:c_len // 2 + 1, :]
            qa = q * jnp.exp(cum - mid)
            kb = (k * jnp.exp(mid - cum)).astype(BF16)
            qa4 = jnp.concatenate(
                [jnp.where(lane_k == h, qa, 0.0) for h in range(GLA_HEADS)], axis=0).astype(BF16)
            s4 = jnp.where(causal4, _dot_nt(qa4, kb), 0.0)
            pv = _dot(s4.astype(BF16), v)
            acc = jnp.zeros((c_len, hv), F32)
            for h in range(GLA_HEADS):
                acc = acc + jnp.where(lane_v == h, pv[h * c_len:(h + 1) * c_len, :], 0.0)
            intra_ref[...] = acc

        @pl.when(jnp.logical_not(fast))
        def _():
            cum_ref[...] = cum
            kf_ref[...] = k
            vf_ref[...] = v.astype(F32)

            def one_key(s, acc):
                ks = kf_ref[pl.ds(s, 1), :]
                vs = vf_ref[pl.ds(s, 1), :]
                cs = cum_ref[pl.ds(s, 1), :]
                w = q * ks * jnp.exp(jnp.minimum(cum - cs, 0.0))
                wsum = _dot(w.astype(BF16), head_expand)
                return acc + jnp.where(trow >= s, wsum, 0.0) * vs

            intra_ref[...] = lax.fori_loop(0, c_len, one_key, jnp.zeros((c_len, hv), F32))

        st = state_ref[...]
        qg = (q * jnp.exp(cum)).astype(BF16)
        inter = _dot_nt(qg, st.astype(BF16))
        oacc_ref[rows, :] = intra_ref[...] + inter
        kg = (k * jnp.exp(last - cum)).astype(BF16)
        upd = _dot_tn(v, kg)
        state_ref[...] = (st * jnp.exp(last) + upd) * state_mask
        return carry

    lax.fori_loop(0, n_chunks, chunk, 0)

    o = oacc_ref[...]
    vr = lax.broadcasted_iota(jnp.int32, (hv, hv), 0) // GLA_DV
    vc = lax.broadcasted_iota(jnp.int32, (hv, hv), 1) // GLA_DV
    seg = (vr == vc).astype(BF16)
    ms = _seg_sum(o * o, seg) * (1.0 / GLA_DV)
    y = o * lax.rsqrt(ms + HEAD_NORM_EPS) * ng_ref[...]
    g = og_ref[...].astype(F32)
    o_ref[...] = (y * (g * jax.nn.sigmoid(g))).astype(o_ref.dtype)


def _gla(z, w_gate, b_gate, norm_g, batch, seq, tg=512, fast_max_decay=GLA_FAST_MAX_DECAY):
    tg = min(tg, seq)
    nt = seq // tg
    hk = GLA_HEADS * GLA_DK
    hv = GLA_HEADS * GLA_DV

    def rowmap(cb):
        return lambda b, t: (b * nt + t, cb)

    const = lambda b, t: (0, 0)
    return pl.pallas_call(
        functools.partial(_gla_kernel, n_chunks=tg // GLA_CHUNK, fast_max_decay=fast_max_decay),
        grid=(batch, nt),
        in_specs=[pl.BlockSpec((tg, hk), rowmap(_GQ // hk)),
                  pl.BlockSpec((tg, hk), rowmap(_GK // hk)),
                  pl.BlockSpec((tg, hv), rowmap(_GV // hv)),
                  pl.BlockSpec((tg, hv), rowmap(_GO // hv)),
                  pl.BlockSpec((tg, LANE), rowmap(_LR // LANE)),
                  pl.BlockSpec((LANE, hk), const),
                  pl.BlockSpec((1, hk), const),
                  pl.BlockSpec((1, hv), const)],
        out_specs=pl.BlockSpec((tg, hv), lambda b, t: (b * nt + t, 0)),
        out_shape=jax.ShapeDtypeStruct((batch * seq, hv), BF16),
        scratch_shapes=[pltpu.VMEM((hv, hk), F32),
                        pltpu.VMEM((tg, hk), F32),
                        pltpu.VMEM((GLA_CHUNK, hk), F32),
                        pltpu.VMEM((GLA_CHUNK, hk), F32),
                        pltpu.VMEM((GLA_CHUNK, hv), F32),
                        pltpu.VMEM((tg, hv), F32),
                        pltpu.VMEM((GLA_CHUNK, hv), F32)],
        compiler_params=_cparams(("arbitrary", "arbitrary")),
        name="gla",
    )(z, z, z, z, z, w_gate, b_gate, norm_g)


LOG2E = math.log2(math.e)


def _diff_kernel(far_ref, lam_ref, q_ref, k_ref, v_ref, bias_ref, ng_ref, o_ref,
                 m_ref, acc_ref, *, blk, lam_init):
    qi = pl.program_id(2)
    d = DIFF_D
    w = 2 * DIFF_D
    q = (q_ref[...].astype(F32) * (DIFF_D ** -0.5 * LOG2E)).astype(BF16)
    lane_q = lax.broadcasted_iota(jnp.int32, q.shape, 1)
    q2 = jnp.concatenate([jnp.where(lane_q < d, q, jnp.zeros_like(q)),
                          jnp.where(lane_q >= d, q, jnp.zeros_like(q))], axis=0)
    m_ref[...] = jnp.full_like(m_ref, -jnp.inf)
    acc_ref[...] = jnp.zeros_like(acc_ref)

    def kv_step(j, bias, shift):
        rows = pl.ds(pl.multiple_of(j * blk, blk), blk)
        vj = v_ref[rows, :]
        s_both = _dot_nt(q2, k_ref[rows, :])
        for m in range(2):
            s = s_both[m * blk:(m + 1) * blk, :]
            if bias is not None:
                s = s + bias
            m_cur = jnp.max(s, axis=-1, keepdims=True)
            if shift is not None:
                m_cur = m_cur + shift
            m_old = m_ref[m]
            m_new = jnp.maximum(m_old, m_cur)
            alpha = jnp.exp2(m_old - m_new)
            sub = m_new if shift is None else m_new - shift
            p = jnp.exp2(s - jnp.concatenate([sub] * (blk // LANE), axis=1))
            row_sum = jnp.broadcast_to(jnp.sum(p, axis=-1, keepdims=True), (blk, LANE))
            acc_ref[m] = (jnp.concatenate([alpha, alpha], axis=1) * acc_ref[m]
                          + jnp.concatenate([_dot(p.astype(BF16), vj), row_sum], axis=1))
            m_ref[m] = m_new

    far = far_ref[pl.program_id(1)]

    def far_step(j, carry):
        kv_step(j, None, far)
        return carry

    lax.fori_loop(0, jnp.maximum(qi - 1, 0), far_step, 0)

    @pl.when(qi >= 1)
    def _():
        kv_step(qi - 1, bias_ref[0, 1], None)

    kv_step(qi, bias_ref[0, 0], None)

    lf = lam_ref[...]
    lam = (jnp.exp(jnp.sum(lf[0:1] * lf[1:2], axis=-1, keepdims=True))
           - jnp.exp(jnp.sum(lf[2:3] * lf[3:4], axis=-1, keepdims=True)) + lam_init)
    a0 = acc_ref[0]
    a1 = acc_ref[1]
    o = a0[:, :w] / a0[:, w:] - lam * (a1[:, :w] / a1[:, w:])
    ms = jnp.mean(o * o, axis=-1, keepdims=True)
    y = o * lax.rsqrt(ms + HEAD_NORM_EPS) * ng_ref[...] * (1.0 - lam_init)
    o_ref[...] = y.astype(o_ref.dtype)


def _t5_bucket(rel):
    n = jnp.maximum(-rel, 0)
    max_exact = T5_BUCKETS // 2
    nf = jnp.maximum(n, 1).astype(F32)
    large = max_exact + (jnp.log(nf / max_exact) / math.log(T5_MAX_DIST / max_exact)
                         * (T5_BUCKETS - max_exact)).astype(jnp.int32)
    large = jnp.minimum(large, T5_BUCKETS - 1)
    return jnp.where(n < max_exact, n, large)


def _bias_tiles(rel_bias, blk):
    assert blk + 1 >= T5_MAX_DIST
    m = jnp.arange(2 * blk)
    col_minus_row = jnp.where(m < blk, m, m - 2 * blk)
    tiles = []
    for delta in range(2):
        rel = col_minus_row - delta * blk
        w = rel_bias[_t5_bucket(rel)].astype(F32).T
        if delta == 0:
            w = jnp.where(rel <= 0, w, -jnp.inf)
        flat = jnp.tile(w, (1, blk))[:, :blk * (2 * blk - 1)]
        tiles.append(flat.reshape(-1, blk, 2 * blk - 1)[:, :, :blk])
    far = rel_bias[_t5_bucket(jnp.full((1,), -(blk + 1)))[0]].astype(F32)
    return jnp.stack(tiles, axis=1) * LOG2E, far * LOG2E


def _diff(z, diff_lambda, norm_g, bias, batch, seq, lam_init, blk):
    blk = min(blk, seq)
    nq = seq // blk
    w = 2 * DIFF_D
    bias_tiles, bias_far = bias
    return pl.pallas_call(
        functools.partial(_diff_kernel, blk=blk, lam_init=lam_init),
        grid=(batch, DIFF_HEADS, nq),
        in_specs=[pl.BlockSpec(memory_space=pltpu.SMEM),
                  pl.BlockSpec((4, DIFF_D), lambda b, h, i: (0, 0)),
                  pl.BlockSpec((blk, w), lambda b, h, i: (b * nq + i, _DQ // w + h)),
                  pl.BlockSpec((seq, w), lambda b, h, i: (b, _DK // w + h)),
                  pl.BlockSpec((seq, w), lambda b, h, i: (b, _DV // w + h)),
                  pl.BlockSpec((1, 2, blk, blk), lambda b, h, i: (h, 0, 0, 0)),
                  pl.BlockSpec((1, w), lambda b, h, i: (0, 0))],
        out_specs=pl.BlockSpec((blk, w), lambda b, h, i: (b * nq + i, h)),
        out_shape=jax.ShapeDtypeStruct((batch * seq, DIFF_HEADS * w), BF16),
        scratch_shapes=[pltpu.VMEM((2, blk, LANE), F32),
                        pltpu.VMEM((2, blk, w + LANE), F32)],
        compiler_params=_cparams(("arbitrary", "arbitrary", "arbitrary")),
        name="diff_attn",
    )(bias_far, diff_lambda, z, z, z, bias_tiles, norm_g)


def _ret_kernel(q_ref, k_ref, v_ref, g_ref, sin_ref, cos_ref, inner_ref, cross_ref, sdec_ref, cdec_ref,
                o_ref, state_ref):
    c_len = RET_CHUNK
    hk = RET_HEADS * RET_DK
    hv = RET_HEADS * RET_DV

    @pl.when(pl.program_id(1) == 0)
    def _():
        state_ref[...] = jnp.zeros_like(state_ref)

    sin = sin_ref[...]
    cos = cos_ref[...]
    even = (lax.broadcasted_iota(jnp.int32, (1, hk), 1) % 2) == 0

    def rotary(t):
        nxt = pltpu.roll(t, hk - 1, 1)
        prv = pltpu.roll(t, 1, 1)
        return t * cos + jnp.where(even, -nxt, prv) * sin

    qf = rotary(q_ref[...].astype(F32))
    kf = rotary(k_ref[...].astype(F32)) * (RET_DK ** -0.5)
    v = v_ref[...]

    lane_k = lax.broadcasted_iota(jnp.int32, (1, hk), 1) // RET_DK
    lane_v = lax.broadcasted_iota(jnp.int32, (1, hv), 1) // RET_DV
    q4 = jnp.concatenate([jnp.where(lane_k == h, qf, 0.0) for h in range(RET_HEADS)], axis=0).astype(BF16)
    s4 = _dot_nt(q4, kf.astype(BF16)) * inner_ref[...]
    pv = _dot(s4.astype(BF16), v)
    acc = jnp.zeros((c_len, hv), F32)
    for h in range(RET_HEADS):
        acc = acc + jnp.where(lane_v == h, pv[h * c_len:(h + 1) * c_len, :], 0.0)

    st = state_ref[...]
    o = acc + _dot_nt(qf.astype(BF16), st.astype(BF16)) * cross_ref[...]
    ks = (kf * sdec_ref[...]).astype(BF16)
    sr = lax.broadcasted_iota(jnp.int32, (hv, hk), 0) // RET_DV
    sc = lax.broadcasted_iota(jnp.int32, (hv, hk), 1) // RET_DK
    state_ref[...] = jnp.where(sr == sc, st * cdec_ref[...] + _dot_tn(v, ks), 0.0)

    vr = lax.broadcasted_iota(jnp.int32, (hv, hv), 0) // RET_DV
    vc = lax.broadcasted_iota(jnp.int32, (hv, hv), 1) // RET_DV
    seg = (vr == vc).astype(BF16)
    ms = _seg_sum(o * o, seg) * (1.0 / RET_DV)
    g = g_ref[...].astype(F32)
    o_ref[...] = (o * lax.rsqrt(ms + HEAD_NORM_EPS) * (g * jax.nn.sigmoid(g))).astype(o_ref.dtype)


def _ret_tables(seq):
    c_len = RET_CHUNK
    pos = jnp.arange(seq, dtype=F32)
    angle = 1.0 / (10000.0 ** jnp.linspace(0.0, 1.0, RET_DK // 2, dtype=F32))
    angle = jnp.repeat(angle, 2)
    sin = jnp.tile(jnp.sin(pos[:, None] * angle), (1, RET_HEADS))
    cos = jnp.tile(jnp.cos(pos[:, None] * angle), (1, RET_HEADS))
    log_g = jnp.log1p(-jnp.exp2(-5.0 - jnp.arange(RET_HEADS, dtype=F32)))
    idx = jnp.arange(c_len, dtype=F32)
    rel = idx[:, None] - idx[None, :]
    inner = jnp.where(rel[None] >= 0, jnp.exp(jnp.maximum(rel, 0.0)[None] * log_g[:, None, None]), 0.0)
    inner = inner.reshape(RET_HEADS * c_len, c_len)
    cross = jnp.exp((idx + 1.0)[None] * log_g[:, None])
    cross = jnp.repeat(cross.T, RET_DV, axis=1)
    sdec = jnp.exp((c_len - 1.0 - idx)[None] * log_g[:, None])
    sdec = jnp.repeat(sdec.T, RET_DK, axis=1)
    cdec = jnp.repeat(jnp.exp(c_len * log_g), RET_DK)[None, :]
    return sin, cos, inner, cross, sdec, cdec


def _ret(z, tables, batch, seq):
    c_len = RET_CHUNK
    nt = seq // c_len
    hk = RET_HEADS * RET_DK
    hv = RET_HEADS * RET_DV
    sin, cos, inner, cross, sdec, cdec = tables

    def rowmap(cb):
        return lambda b, t: (b * nt + t, cb)

    const = lambda b, t: (0, 0)
    return pl.pallas_call(
        _ret_kernel,
        grid=(batch, nt),
        in_specs=[pl.BlockSpec((c_len, hk), rowmap(_RQ // hk)),
                  pl.BlockSpec((c_len, hk), rowmap(_RK // hk)),
                  pl.BlockSpec((c_len, hv), rowmap(_RV // hv)),
                  pl.BlockSpec((c_len, hv), rowmap(_RG // hv)),
                  pl.BlockSpec((c_len, hk), lambda b, t: (t, 0)),
                  pl.BlockSpec((c_len, hk), lambda b, t: (t, 0)),
                  pl.BlockSpec((RET_HEADS * c_len, c_len), const),
                  pl.BlockSpec((c_len, hv), const),
                  pl.BlockSpec((c_len, hk), const),
                  pl.BlockSpec((1, hk), const)],
        out_specs=pl.BlockSpec((c_len, hv), lambda b, t: (b * nt + t, 0)),
        out_shape=jax.ShapeDtypeStruct((batch * seq, hv), BF16),
        scratch_shapes=[pltpu.VMEM((hv, hk), F32)],
        compiler_params=_cparams(("arbitrary", "arbitrary")),
        name="retention",
    )(z, z, z, z, sin, cos, inner, cross, sdec, cdec)


def _layer_norm(t, g, b):
    mu = jnp.mean(t, axis=-1, keepdims=True)
    var = jnp.mean(jnp.square(t - mu), axis=-1, keepdims=True)
    return (t - mu) * lax.rsqrt(var + LN_EPS) * g + b


def _out_ln_kernel(gla_ref, dif_ref, ret_ref, x_ref, wo_ref, g_ref, b_ref, wr_hi_ref, wr_lo_ref, br_ref,
                   x1_ref, x1b_ref, x1p_ref, gate_ref, idx_ref, cnt_ref, *, alpha):
    w0 = gla_ref.shape[1]
    w1 = w0 + dif_ref.shape[1]
    a = (_dot(gla_ref[...], wo_ref[0:w0, :]) + _dot(dif_ref[...], wo_ref[w0:w1, :])
         + _dot(ret_ref[...], wo_ref[w1:, :]))
    x1 = _layer_norm(alpha * x_ref[...] + a, g_ref[...], b_ref[...])
    x1_ref[...] = x1
    x1b_ref[...] = x1.astype(BF16)
    x1p_ref[...] = _pack_bf16_pairs(x1)

    hi, lo = _split_hi_lo(x1)
    logits = (_dot(hi, wr_hi_ref[...]) + _dot(lo, wr_hi_ref[...]) + _dot(hi, wr_lo_ref[...])) + br_ref[...]
    lane = lax.broadcasted_iota(jnp.int32, logits.shape, 1)
    work = jnp.where(lane < N_EXPERTS, logits, -jnp.inf)
    vals = jnp.zeros(logits.shape, F32)
    idxs = jnp.zeros(logits.shape, jnp.int32)
    chosen = jnp.zeros(logits.shape, F32)
    picks = []
    top0 = None
    for r in range(TOP_K):
        mx = jnp.max(work, axis=-1, keepdims=True)
        am = jnp.min(jnp.where(work == mx, lane, LANE), axis=-1, keepdims=True)
        if r == 0:
            top0 = mx
        vals = jnp.where(lane == r, jnp.exp(mx - top0), vals)
        idxs = jnp.where(lane == r, am, idxs)
        chosen = jnp.where(lane == am, 1.0, chosen)
        work = jnp.where(lane == am, -jnp.inf, work)
        picks.append(am)
    gate_ref[...] = vals / jnp.sum(vals, axis=-1, keepdims=True)

    @pl.when(pl.program_id(0) == 0)
    def _():
        cnt_ref[...] = jnp.zeros_like(cnt_ref)

    tm = logits.shape[0]
    earlier = (lax.broadcasted_iota(jnp.int32, (tm, tm), 0)
               > lax.broadcasted_iota(jnp.int32, (tm, tm), 1)).astype(BF16)
    before = _dot(earlier, chosen.astype(BF16)) + cnt_ref[...]
    for r in range(TOP_K):
        rank = jnp.sum(jnp.where(lane == picks[r], before, 0.0), axis=-1, keepdims=True)
        idxs = jnp.where(lane == TOP_K + r, rank.astype(jnp.int32), idxs)
    idx_ref[...] = idxs
    cnt_ref[...] = cnt_ref[...] + jnp.sum(chosen, axis=0, keepdims=True)


def _out_ln(gla_o, dif_o, ret_o, x, w_out, ln_g, ln_b, wr_hi, wr_lo, b_router, alpha, tm=512):
    n, d = x.shape
    tm = min(tm, n)
    row = lambda i: (i, 0)
    const = lambda i: (0, 0)
    return pl.pallas_call(
        functools.partial(_out_ln_kernel, alpha=alpha),
        grid=(n // tm,),
        in_specs=[pl.BlockSpec((tm, gla_o.shape[1]), row),
                  pl.BlockSpec((tm, dif_o.shape[1]), row),
                  pl.BlockSpec((tm, ret_o.shape[1]), row),
                  pl.BlockSpec((tm, d), row),
                  pl.BlockSpec(w_out.shape, const),
                  pl.BlockSpec((1, d), const),
                  pl.BlockSpec((1, d), const),
                  pl.BlockSpec((d, LANE), const),
                  pl.BlockSpec((d, LANE), const),
                  pl.BlockSpec((1, LANE), const)],
        out_specs=[pl.BlockSpec((tm, d), row),
                   pl.BlockSpec((tm, d), row),
                   pl.BlockSpec((tm, d // 2), row),
                   pl.BlockSpec((tm, LANE), row),
                   pl.BlockSpec((tm, LANE), row),
                   pl.BlockSpec((1, LANE), const)],
        out_shape=[jax.ShapeDtypeStruct((n, d), F32),
                   jax.ShapeDtypeStruct((n, d), BF16),
                   jax.ShapeDtypeStruct((n, d // 2), jnp.uint32),
                   jax.ShapeDtypeStruct((n, LANE), F32),
                   jax.ShapeDtypeStruct((n, LANE), jnp.int32),
                   jax.ShapeDtypeStruct((1, LANE), F32)],
        compiler_params=_cparams(("arbitrary",)),
        name="out_ln_router",
    )(gla_o, dif_o, ret_o, x, w_out, ln_g, ln_b, wr_hi, wr_lo, b_router)


GU_GROUP = 2 * LANE


def _deinterleave_perm():
    perm = np.zeros((GU_GROUP, GU_GROUP), np.float32)
    half = np.arange(LANE)
    perm[2 * half, half] = 1.0
    perm[2 * half + 1, LANE + half] = 1.0
    return jnp.asarray(perm, BF16)


def _moe_kernel(blk_e_ref, n_used_ref, x_ref, wgu_ref, bgu_ref, wdn_ref, bdn_ref, perm_ref, o_ref,
                wgu_b_ref, wdn_b_ref, *, rows_per_cast):
    i = pl.program_id(0)
    used = i < n_used_ref[0]
    new_expert = jnp.logical_or(i == 0, blk_e_ref[i] != blk_e_ref[jnp.maximum(i - 1, 0)])

    @pl.when(jnp.logical_and(used, new_expert))
    def _():
        d = wgu_ref.shape[0]
        for r in range(d // rows_per_cast):
            rows = slice(r * rows_per_cast, (r + 1) * rows_per_cast)
            wdn_b_ref[rows, :] = wdn_ref[rows, :].astype(BF16)
            for g in range(wgu_ref.shape[1] // GU_GROUP):
                cols = slice(g * GU_GROUP, (g + 1) * GU_GROUP)
                wgu_b_ref[rows, cols] = _dot(wgu_ref[rows, cols].astype(BF16), perm_ref[...]).astype(BF16)

    @pl.when(used)
    def _():
        gu = _dot(_unpack_bf16_pairs(x_ref[...]).astype(BF16), wgu_b_ref[...]) + bgu_ref[...]
        acts = []
        for grp in range(gu.shape[1] // GU_GROUP):
            g = jnp.minimum(gu[:, grp * GU_GROUP:grp * GU_GROUP + LANE], SWIGLU_LIMIT)
            u = jnp.clip(gu[:, grp * GU_GROUP + LANE:(grp + 1) * GU_GROUP], -SWIGLU_LIMIT, SWIGLU_LIMIT)
            acts.append(((u + 1.0) * (g * jax.nn.sigmoid(g * SWIGLU_ALPHA))).astype(BF16))
        act = jnp.concatenate(acts, axis=1)
        o_ref[...] = _pack_bf16_pairs(_dot(act, wdn_b_ref[...]) + bdn_ref[...])

    @pl.when(jnp.logical_not(used))
    def _():
        o_ref[...] = jnp.zeros_like(o_ref)


def _moe(xs, blk_e, n_used, w_gu, b_gu, w_dn, b_dn, blk):
    rows = xs.shape[0]
    d = w_gu.shape[1]
    f2 = w_gu.shape[2]
    f = w_dn.shape[1]
    assert f == d
    grid_spec = pltpu.PrefetchScalarGridSpec(
        num_scalar_prefetch=2,
        grid=(rows // blk,),
        in_specs=[pl.BlockSpec((blk, d // 2), lambda i, be, nu: (i, 0)),
                  pl.BlockSpec((None, d, f2), lambda i, be, nu: (be[i], 0, 0)),
                  pl.BlockSpec((None, 1, f2), lambda i, be, nu: (be[i], 0, 0)),
                  pl.BlockSpec((None, f, d), lambda i, be, nu: (be[i], 0, 0)),
                  pl.BlockSpec((None, 1, d), lambda i, be, nu: (be[i], 0, 0)),
                  pl.BlockSpec((GU_GROUP, GU_GROUP), lambda i, be, nu: (0, 0))],
        out_specs=pl.BlockSpec((blk, d // 2), lambda i, be, nu: (i, 0)),
        scratch_shapes=[pltpu.VMEM((d, f2), BF16), pltpu.VMEM((f, d), BF16)],
    )
    return pl.pallas_call(
        functools.partial(_moe_kernel, rows_per_cast=256),
        grid_spec=grid_spec,
        out_shape=jax.ShapeDtypeStruct((rows, d // 2), jnp.uint32),
        compiler_params=_cparams(("arbitrary",)),
        name="moe_experts",
    )(blk_e, n_used, xs, w_gu, b_gu, w_dn, b_dn, _deinterleave_perm())


def _combine_kernel(yg_ref, gate_ref, x1_ref, x1b_ref, p_ref, wg_ref, bg_ref, wp_ref, g_ref, b_ref,
                    x2_ref, x2b_ref, *, alpha):
    d = x1_ref.shape[1]
    gates = gate_ref[...]
    m = jnp.zeros(x1_ref.shape, F32)
    for r in range(TOP_K):
        m = m + _unpack_bf16_pairs(yg_ref[r]) * gates[:, r:r + 1]
    e = (jax.nn.sigmoid(_dot(x1b_ref[...], wg_ref[...]) + bg_ref[...])
         * _dot(p_ref[...].astype(BF16), wp_ref[...]))
    x2 = _layer_norm(alpha * x1_ref[...] + m + e, g_ref[...], b_ref[...])
    x2_ref[...] = x2
    x2b_ref[...] = x2.astype(BF16)


def _combine(yg, gates, x1, x1b, p, w_pg, b_pg, w_pp, ln_g, ln_b, alpha, tm=512):
    n, d = x1.shape
    tm = min(tm, n)
    row = lambda i: (i, 0)
    const = lambda i: (0, 0)
    return pl.pallas_call(
        functools.partial(_combine_kernel, alpha=alpha),
        grid=(n // tm,),
        in_specs=[pl.BlockSpec((TOP_K, tm, d // 2), lambda i: (0, i, 0)),
                  pl.BlockSpec((tm, LANE), row),
                  pl.BlockSpec((tm, d), row),
                  pl.BlockSpec((tm, d), row),
                  pl.BlockSpec((tm, p.shape[1]), row),
                  pl.BlockSpec(w_pg.shape, const),
                  pl.BlockSpec((1, d), const),
                  pl.BlockSpec(w_pp.shape, const),
                  pl.BlockSpec((1, d), const),
                  pl.BlockSpec((1, d), const)],
        out_specs=[pl.BlockSpec((tm, d), row), pl.BlockSpec((tm, d), row)],
        out_shape=[jax.ShapeDtypeStruct((n, d), F32), jax.ShapeDtypeStruct((n, d), BF16)],
        compiler_params=_cparams(("arbitrary",)),
        name="combine_ln",
    )(yg, gates, x1, x1b, p, w_pg, b_pg, w_pp, ln_g, ln_b)


def _route(experts, ranks, counts, blk):
    n_tok = experts.shape[0]
    n_assign = n_tok * TOP_K
    sizes = counts.astype(jnp.int32)
    padded = ((sizes + blk - 1) // blk) * blk
    pends = jnp.cumsum(padded)
    pstarts = pends - padded
    onehot = experts[:, :, None] == jnp.arange(N_EXPERTS, dtype=jnp.int32)
    slot = jnp.sum(jnp.where(onehot, pstarts, 0), axis=-1) + ranks
    span = 2 * n_assign
    real = experts * span + jnp.arange(n_assign, dtype=jnp.int32).reshape(n_tok, TOP_K)
    j = jnp.arange(blk - 1, dtype=jnp.int32)
    filler = jnp.where(j[None, :] < (padded - sizes)[:, None],
                       jnp.arange(N_EXPERTS, dtype=jnp.int32)[:, None] * span + n_assign + j[None, :],
                       jnp.iinfo(jnp.int32).max)
    buf_len = ((n_assign + N_EXPERTS * (blk - 1) + blk - 1) // blk) * blk
    n_blk = buf_len // blk
    tail = jnp.full((buf_len - n_assign - N_EXPERTS * (blk - 1),), jnp.iinfo(jnp.int32).max, jnp.int32)
    assign = jnp.sort(jnp.concatenate([real.reshape(-1), filler.reshape(-1), tail])) % span
    buf_tok = jnp.where(assign < n_assign, assign // TOP_K, 0)
    blk_e = jnp.clip(jnp.searchsorted(pends, jnp.arange(n_blk, dtype=jnp.int32) * blk, side='right'),
                     0, N_EXPERTS - 1).astype(jnp.int32)
    n_used = (pends[-1] // blk).astype(jnp.int32).reshape(1)
    return buf_tok, slot, blk_e, n_used


def _regroup_w_in(w_in):
    depth, d, width = w_in.shape
    lr0 = _GV + GLA_HEADS * GLA_DV
    src = np.full((Z_WIDTH,), -1, np.int32)
    src[:lr0] = np.arange(lr0)
    src[lr0:width - GLA_GATE_RANK] = np.arange(lr0 + GLA_GATE_RANK, width)
    src[_LR:_LR + GLA_GATE_RANK] = np.arange(lr0, lr0 + GLA_GATE_RANK)
    select = (jnp.arange(width, dtype=jnp.int32)[:, None] == jnp.asarray(src)[None, :]).astype(BF16)
    out = _matmul(w_in.reshape(depth * d, width).astype(BF16), select, d, Z_WIDTH // 3, BF16, "regroup_w_in")
    return out.reshape(depth, d, Z_WIDTH)


def kernel(x, p, w_in, w_gla_gate, b_gla_gate, gla_norm_g, diff_lambda, diff_norm_g, w_out, rel_bias, ln1_g, ln1_b, w_router, b_router, w_gate_up, b_gate_up, w_down, b_down, w_ple_gate, b_ple_gate, w_ple_proj, ln2_g, ln2_b):
    batch, seq, d = x.shape
    depth = w_in.shape[0]
    n = batch * seq
    alpha = (2 * depth) ** 0.25
    moe_blk = 256
    diff_blk = min(512, seq)

    w_in_b = _regroup_w_in(w_in)
    wg = jnp.zeros((depth, LANE, GLA_HEADS * GLA_DK), F32).at[:, :GLA_GATE_RANK, :].set(w_gla_gate).astype(BF16)
    gla_g = jnp.tile(gla_norm_g, (1, GLA_HEADS))[:, None, :]
    w_out_b = w_out.astype(BF16)
    wr = jnp.zeros((depth, d, LANE), F32).at[:, :, :N_EXPERTS].set(w_router)
    wr_hi = wr.astype(BF16)
    wr_lo = (wr - wr_hi.astype(F32)).astype(BF16)
    br = jnp.zeros((depth, 1, LANE), F32).at[:, 0, :N_EXPERTS].set(b_router)
    w_gu = w_gate_up.reshape((depth * N_EXPERTS,) + w_gate_up.shape[2:])
    b_gu = b_gate_up.reshape(depth * N_EXPERTS, -1, LANE, 2).swapaxes(-1, -2).reshape(depth * N_EXPERTS, 1, -1)
    w_dn = w_down.reshape((depth * N_EXPERTS,) + w_down.shape[2:])
    b_dn = b_down.reshape(depth * N_EXPERTS, 1, -1)
    w_pg_b = w_ple_gate.astype(BF16)
    w_pp_b = w_ple_proj.astype(BF16)
    bias_tiles = _bias_tiles(rel_bias, diff_blk)
    ret_tables = _ret_tables(seq)

    xf = x.reshape(n, d)
    xb = xf.astype(BF16)
    pf = p.reshape(depth, n, p.shape[-1])
    for i in range(depth):
        lam_init = 0.8 - 0.6 * math.exp(-0.3 * i)
        z = _matmul(xb, w_in_b[i], min(1024, n), Z_WIDTH // 3, BF16, "in_proj")
        gla_o = _gla(z, wg[i], b_gla_gate[i][None, :], gla_g[i], batch, seq)
        dif_o = _diff(z, diff_lambda[i], diff_norm_g[i][None, :], bias_tiles, batch, seq, lam_init, diff_blk)
        ret_o = _ret(z, ret_tables, batch, seq)
        x1, x1b, x1p, gates, idx, cnt = _out_ln(gla_o, dif_o, ret_o, xf, w_out_b[i], ln1_g[i][None, :],
                                           ln1_b[i][None, :], wr_hi[i], wr_lo[i], br[i], alpha)
        buf_tok, slot, blk_e, n_used = _route(idx[:, :TOP_K], idx[:, TOP_K:2 * TOP_K], cnt[0, :N_EXPERTS], moe_blk)
        xs = _gather_rows(x1p, buf_tok)
        yb = _moe(xs, blk_e + i * N_EXPERTS, n_used, w_gu, b_gu, w_dn, b_dn, moe_blk)
        yg = _gather_rows(yb, slot.T.reshape(-1)).reshape(TOP_K, n, d // 2)
        xf, xb = _combine(yg, gates, x1, x1b, pf[i], w_pg_b[i], b_ple_gate[i][None, :], w_pp_b[i],
                          ln2_g[i][None, :], ln2_b[i][None, :], alpha)
    return xf.reshape(batch, seq, d)
```

```python
import functools
import math

import numpy as np
import jax
import jax.numpy as jnp
from jax import lax
from jax.experimental import pallas as pl
from jax.experimental.pallas import tpu as pltpu
from jax.experimental.pallas import tpu_sc as plsc

F32 = jnp.float32
BF16 = jnp.bfloat16

GLA_HEADS = 4
GLA_DK = 32
GLA_DV = 64
GLA_GATE_RANK = 16
GLA_TAU = 16.0
GLA_CHUNK = 64
DIFF_HEADS = 4
DIFF_D = 64
RET_HEADS = 4
RET_DK = 64
RET_DV = 64
RET_CHUNK = 128
T5_BUCKETS = 32
T5_MAX_DIST = 128
N_EXPERTS = 32
TOP_K = 4
SWIGLU_LIMIT = 7.0
SWIGLU_ALPHA = 1.702
LN_EPS = 1e-5
HEAD_NORM_EPS = 1e-5

LANE = 128
VMEM_LIMIT = 56 * 1024 * 1024

_GQ, _GK, _GV, _GO = 0, 128, 256, 512
_DQ, _DK, _DV = 768, 1280, 1792
_RQ, _RK, _RV, _RG = 2304, 2560, 2816, 3072
_LR = 3328
Z_WIDTH = 3456

GLA_FAST_MAX_DECAY = 60.0


def _cparams(sem):
    return pltpu.CompilerParams(dimension_semantics=sem, vmem_limit_bytes=VMEM_LIMIT)


def _split_hi_lo(a):
    hi = a.astype(BF16)
    lo = (a - hi.astype(F32)).astype(BF16)
    return hi, lo


def _dot(a, b):
    return jnp.dot(a, b, preferred_element_type=F32)


def _dot_nt(a, b):
    return lax.dot_general(a, b, (((1,), (1,)), ((), ())), preferred_element_type=F32)


def _dot_tn(a, b):
    return lax.dot_general(a, b, (((0,), (0,)), ((), ())), preferred_element_type=F32)


def _seg_sum(a, ind):
    hi, lo = _split_hi_lo(a)
    return _dot(hi, ind) + _dot(lo, ind)


def _pack_bf16_pairs(t):
    h = t.shape[1] // 2
    bits = lax.bitcast_convert_type(t.astype(BF16).astype(F32), jnp.uint32)
    return (bits[:, :h] >> 16) | (bits[:, h:] & jnp.uint32(0xFFFF0000))


def _unpack_bf16_pairs(packed):
    lo = lax.bitcast_convert_type(packed << 16, F32)
    hi = lax.bitcast_convert_type(packed & jnp.uint32(0xFFFF0000), F32)
    return jnp.concatenate([lo, hi], axis=1)


SC_WINDOW = 128
SC_ROW_WORDS = 256


def _sc_gather(table, indices):
    m = indices.shape[0]
    words = table.shape[1]
    assert words == SC_ROW_WORDS and m % SC_WINDOW == 0
    mesh = plsc.VectorSubcoreMesh(core_axis_name="core", subcore_axis_name="subcore")

    def body(x_hbm, i_hbm, o_hbm):
        def step(i_vmem, o_vmem):
            pltpu.sync_copy(x_hbm.at[i_vmem.at[0]], o_vmem)

        pltpu.emit_pipeline(
            step,
            grid=(m // SC_WINDOW,),
            in_specs=[pl.BlockSpec((1, SC_WINDOW), index_map=lambda i: (0, i))],
            out_specs=[pl.BlockSpec((SC_WINDOW, words), index_map=lambda i: (i, 0))],
            core_axis_name=("core", "subcore"),
            dimension_semantics=(pltpu.PARALLEL,),
        )(i_hbm, o_hbm)

    return pl.kernel(body, out_type=jax.ShapeDtypeStruct((m, words), table.dtype), mesh=mesh,
                     scratch_types=[])(table, indices.reshape(1, m))


def _gather_rows(tables, rows):
    return [_sc_gather(t, rows) for t in tables]


def _mm_kernel(x_ref, w_ref, o_ref):
    o_ref[...] = _dot(x_ref[...], w_ref[...]).astype(o_ref.dtype)


def _matmul(x, w, tm, tn, out_dtype, name):
    m, k = x.shape
    n = w.shape[1]
    return pl.pallas_call(
        _mm_kernel,
        grid=(n // tn, m // tm),
        in_specs=[pl.BlockSpec((tm, k), lambda j, i: (i, 0)),
                  pl.BlockSpec((k, tn), lambda j, i: (0, j))],
        out_specs=pl.BlockSpec((tm, tn), lambda j, i: (i, j)),
        out_shape=jax.ShapeDtypeStruct((m, n), out_dtype),
        compiler_params=_cparams(("arbitrary", "arbitrary")),
        name=name,
    )(x, w)


def _gla_kernel(q_ref, k_ref, v_ref, og_ref, lr_ref, wg_ref, bg_ref, ng_ref, o_ref,
                state_ref, la_ref, cum_ref, kf_ref, vf_ref, oacc_ref, intra_ref, *, n_chunks, fast_max_decay):
    c_len = GLA_CHUNK
    hk = GLA_HEADS * GLA_DK
    hv = GLA_HEADS * GLA_DV

    @pl.when(pl.program_id(1) == 0)
    def _():
        state_ref[...] = jnp.zeros_like(state_ref)

    zg = _dot(lr_ref[...], wg_ref[...]) + bg_ref[...]
    la = (jnp.minimum(zg, 0.0) - jnp.log1p(jnp.exp(-jnp.abs(zg)))) / GLA_TAU
    la_ref[...] = la
    chunk_tot = jnp.sum(la.reshape(n_chunks, c_len, hk), axis=1)
    fast = jnp.min(chunk_tot) > -fast_max_decay

    row = lax.broadcasted_iota(jnp.int32, (c_len, c_len), 0)
    col = lax.broadcasted_iota(jnp.int32, (c_len, c_len), 1)
    tril = (row >= col).astype(BF16)
    sr = lax.broadcasted_iota(jnp.int32, (hv, hk), 0)
    sc = lax.broadcasted_iota(jnp.int32, (hv, hk), 1)
    state_mask = ((sr // GLA_DV) == (sc // GLA_DK)).astype(F32)
    lane_k = lax.broadcasted_iota(jnp.int32, (1, hk), 1) // GLA_DK
    lane_v = lax.broadcasted_iota(jnp.int32, (1, hv), 1) // GLA_DV
    r4 = lax.broadcasted_iota(jnp.int32, (GLA_HEADS * c_len, c_len), 0) % c_len
    c4 = lax.broadcasted_iota(jnp.int32, (GLA_HEADS * c_len, c_len), 1)
    causal4 = r4 >= c4
    er = lax.broadcasted_iota(jnp.int32, (hk, hv), 0) // GLA_DK
    ec = lax.broadcasted_iota(jnp.int32, (hk, hv), 1) // GLA_DV
    head_expand = (er == ec).astype(BF16)
    trow = lax.broadcasted_iota(jnp.int32, (c_len, hv), 0)

    def chunk(c, carry):
        r0 = pl.multiple_of(c * c_len, c_len)
        rows = pl.ds(r0, c_len)
        q = q_ref[rows, :].astype(F32) * (GLA_DK ** -0.5)
        k = k_ref[rows, :].astype(F32)
        v = v_ref[rows, :]
        la_c = la_ref[rows, :]
        la_hi, la_lo = _split_hi_lo(la_c)
        cum = _dot(tril, la_hi) + _dot(tril, la_lo)
        last = cum[c_len - 1:c_len, :]

        @pl.when(fast)
        def _():
            mid = cum[c_len // 2:c_len // 2 + 1, :]
            qa = q * jnp.exp(cum - mid)
            kb = (k * jnp.exp(mid - cum)).astype(BF16)
            qa4 = jnp.concatenate(
                [jnp.where(lane_k == h, qa, 0.0) for h in range(GLA_HEADS)], axis=0).astype(BF16)
            s4 = jnp.where(causal4, _dot_nt(qa4, kb), 0.0)
            pv = _dot(s4.astype(BF16), v)
            acc = jnp.zeros((c_len, hv), F32)
            for h in range(GLA_HEADS):
                acc = acc + jnp.where(lane_v == h, pv[h * c_len:(h + 1) * c_len, :], 0.0)
            intra_ref[...] = acc

        @pl.when(jnp.logical_not(fast))
        def _():
            cum_ref[...] = cum
            kf_ref[...] = k
            vf_ref[...] = v.astype(F32)

            def one_key(s, acc):
                ks = kf_ref[pl.ds(s, 1), :]
                vs = vf_ref[pl.ds(s, 1), :]
                cs = cum_ref[pl.ds(s, 1), :]
                w = q * ks * jnp.exp(jnp.minimum(cum - cs, 0.0))
                wsum = _dot(w.astype(BF16), head_expand)
                return acc + jnp.where(trow >= s, wsum, 0.0) * vs

            intra_ref[...] = lax.fori_loop(0, c_len, one_key, jnp.zeros((c_len, hv), F32))

        st = state_ref[...]
        qg = (q * jnp.exp(cum)).astype(BF16)
        inter = _dot_nt(qg, st.astype(BF16))
        oacc_ref[rows, :] = intra_ref[...] + inter
        kg = (k * jnp.exp(last - cum)).astype(BF16)
        upd = _dot_tn(v, kg)
        state_ref[...] = (st * jnp.exp(last) + upd) * state_mask
        return carry

    lax.fori_loop(0, n_chunks, chunk, 0)

    o = oacc_ref[...]
    vr = lax.broadcasted_iota(jnp.int32, (hv, hv), 0) // GLA_DV
    vc = lax.broadcasted_iota(jnp.int32, (hv, hv), 1) // GLA_DV
    seg = (vr == vc).astype(BF16)
    ms = _seg_sum(o * o, seg) * (1.0 / GLA_DV)
    y = o * lax.rsqrt(ms + HEAD_NORM_EPS) * ng_ref[...]
    g = og_ref[...].astype(F32)
    o_ref[...] = (y * (g * jax.nn.sigmoid(g))).astype(o_ref.dtype)


def _gla(z, w_gate, b_gate, norm_g, batch, seq, tg=512, fast_max_decay=GLA_FAST_MAX_DECAY):
    tg = min(tg, seq)
    nt = seq // tg
    hk = GLA_HEADS * GLA_DK
    hv = GLA_HEADS * GLA_DV

    def rowmap(cb):
        return lambda b, t: (b * nt + t, cb)

    const = lambda b, t: (0, 0)
    return pl.pallas_call(
        functools.partial(_gla_kernel, n_chunks=tg // GLA_CHUNK, fast_max_decay=fast_max_decay),
        grid=(batch, nt),
        in_specs=[pl.BlockSpec((tg, hk), rowmap(_GQ // hk)),
                  pl.BlockSpec((tg, hk), rowmap(_GK // hk)),
                  pl.BlockSpec((tg, hv), rowmap(_GV // hv)),
                  pl.BlockSpec((tg, hv), rowmap(_GO // hv)),
                  pl.BlockSpec((tg, LANE), rowmap(_LR // LANE)),
                  pl.BlockSpec((LANE, hk), const),
                  pl.BlockSpec((1, hk), const),
                  pl.BlockSpec((1, hv), const)],
        out_specs=pl.BlockSpec((tg, hv), lambda b, t: (b * nt + t, 0)),
        out_shape=jax.ShapeDtypeStruct((batch * seq, hv), BF16),
        scratch_shapes=[pltpu.VMEM((hv, hk), F32),
                        pltpu.VMEM((tg, hk), F32),
                        pltpu.VMEM((GLA_CHUNK, hk), F32),
                        pltpu.VMEM((GLA_CHUNK, hk), F32),
                        pltpu.VMEM((GLA_CHUNK, hv), F32),
                        pltpu.VMEM((tg, hv), F32),
                        pltpu.VMEM((GLA_CHUNK, hv), F32)],
        compiler_params=_cparams(("arbitrary", "arbitrary")),
        name="gla",
    )(z, z, z, z, z, w_gate, b_gate, norm_g)


LOG2E = math.log2(math.e)


def _diff_kernel(far_ref, lam_ref, q_ref, k_ref, v_ref, bias_ref, ng_ref, o_ref,
                 m_ref, acc_ref, *, blk, lam_init):
    qi = pl.program_id(2)
    d = DIFF_D
    w = 2 * DIFF_D
    q = (q_ref[...].astype(F32) * (DIFF_D ** -0.5 * LOG2E)).astype(BF16)
    lane_q = lax.broadcasted_iota(jnp.int32, q.shape, 1)
    q2 = jnp.concatenate([jnp.where(lane_q < d, q, jnp.zeros_like(q)),
                          jnp.where(lane_q >= d, q, jnp.zeros_like(q))], axis=0)
    m_ref[...] = jnp.full_like(m_ref, -jnp.inf)
    acc_ref[...] = jnp.zeros_like(acc_ref)

    def kv_step(j, bias, shift):
        rows = pl.ds(pl.multiple_of(j * blk, blk), blk)
        vj = v_ref[rows, :]
        s_both = _dot_nt(q2, k_ref[rows, :])
        for m in range(2):
            s = s_both[m * blk:(m + 1) * blk, :]
            if bias is not None:
                s = s + bias
            m_cur = jnp.max(s, axis=-1, keepdims=True)
            if shift is not None:
                m_cur = m_cur + shift
            m_old = m_ref[m]
            m_new = jnp.maximum(m_old, m_cur)
            alpha = jnp.exp2(m_old - m_new)
            sub = m_new if shift is None else m_new - shift
            p = jnp.exp2(s - jnp.concatenate([sub] * (blk // LANE), axis=1))
            row_sum = jnp.broadcast_to(jnp.sum(p, axis=-1, keepdims=True), (blk, LANE))
            acc_ref[m] = (jnp.concatenate([alpha, alpha], axis=1) * acc_ref[m]
                          + jnp.concatenate([_dot(p.astype(BF16), vj), row_sum], axis=1))
            m_ref[m] = m_new

    far = far_ref[pl.program_id(1)]

    def far_step(j, carry):
        kv_step(j, None, far)
        return carry

    lax.fori_loop(0, jnp.maximum(qi - 1, 0), far_step, 0)

    @pl.when(qi >= 1)
    def _():
        kv_step(qi - 1, bias_ref[0, 1], None)

    kv_step(qi, bias_ref[0, 0], None)

    lf = lam_ref[...]
    lam = (jnp.exp(jnp.sum(lf[0:1] * lf[1:2], axis=-1, keepdims=True))
           - jnp.exp(jnp.sum(lf[2:3] * lf[3:4], axis=-1, keepdims=True)) + lam_init)
    a0 = acc_ref[0]
    a1 = acc_ref[1]
    o = a0[:, :w] / a0[:, w:] - lam * (a1[:, :w] / a1[:, w:])
    ms = jnp.mean(o * o, axis=-1, keepdims=True)
    y = o * lax.rsqrt(ms + HEAD_NORM_EPS) * ng_ref[...] * (1.0 - lam_init)
    o_ref[...] = y.astype(o_ref.dtype)


def _t5_bucket(rel):
    n = jnp.maximum(-rel, 0)
    max_exact = T5_BUCKETS // 2
    nf = jnp.maximum(n, 1).astype(F32)
    large = max_exact + (jnp.log(nf / max_exact) / math.log(T5_MAX_DIST / max_exact)
                         * (T5_BUCKETS - max_exact)).astype(jnp.int32)
    large = jnp.minimum(large, T5_BUCKETS - 1)
    return jnp.where(n < max_exact, n, large)


def _bias_tiles(rel_bias, blk):
    assert blk + 1 >= T5_MAX_DIST
    m = jnp.arange(2 * blk)
    col_minus_row = jnp.where(m < blk, m, m - 2 * blk)
    tiles = []
    for delta in range(2):
        rel = col_minus_row - delta * blk
        w = rel_bias[_t5_bucket(rel)].astype(F32).T
        if delta == 0:
            w = jnp.where(rel <= 0, w, -jnp.inf)
        flat = jnp.tile(w, (1, blk))[:, :blk * (2 * blk - 1)]
        tiles.append(flat.reshape(-1, blk, 2 * blk - 1)[:, :, :blk])
    far = rel_bias[_t5_bucket(jnp.full((1,), -(blk + 1)))[0]].astype(F32)
    return jnp.stack(tiles, axis=1) * LOG2E, far * LOG2E


def _diff(z, diff_lambda, norm_g, bias, batch, seq, lam_init, blk):
    blk = min(blk, seq)
    nq = seq // blk
    w = 2 * DIFF_D
    bias_tiles, bias_far = bias
    return pl.pallas_call(
        functools.partial(_diff_kernel, blk=blk, lam_init=lam_init),
        grid=(batch, DIFF_HEADS, nq),
        in_specs=[pl.BlockSpec(memory_space=pltpu.SMEM),
                  pl.BlockSpec((4, DIFF_D), lambda b, h, i: (0, 0)),
                  pl.BlockSpec((blk, w), lambda b, h, i: (b * nq + i, _DQ // w + h)),
                  pl.BlockSpec((seq, w), lambda b, h, i: (b, _DK // w + h)),
                  pl.BlockSpec((seq, w), lambda b, h, i: (b, _DV // w + h)),
                  pl.BlockSpec((1, 2, blk, blk), lambda b, h, i: (h, 0, 0, 0)),
                  pl.BlockSpec((1, w), lambda b, h, i: (0, 0))],
        out_specs=pl.BlockSpec((blk, w), lambda b, h, i: (b * nq + i, h)),
        out_shape=jax.ShapeDtypeStruct((batch * seq, DIFF_HEADS * w), BF16),
        scratch_shapes=[pltpu.VMEM((2, blk, LANE), F32),
                        pltpu.VMEM((2, blk, w + LANE), F32)],
        compiler_params=_cparams(("arbitrary", "arbitrary", "arbitrary")),
        name="diff_attn",
    )(bias_far, diff_lambda, z, z, z, bias_tiles, norm_g)


def _ret_kernel(q_ref, k_ref, v_ref, g_ref, sin_ref, cos_ref, inner_ref, cross_ref, sdec_ref, cdec_ref,
                o_ref, state_ref):
    c_len = RET_CHUNK
    hk = RET_HEADS * RET_DK
    hv = RET_HEADS * RET_DV

    @pl.when(pl.program_id(1) == 0)
    def _():
        state_ref[...] = jnp.zeros_like(state_ref)

    sin = sin_ref[...]
    cos = cos_ref[...]
    even = (lax.broadcasted_iota(jnp.int32, (1, hk), 1) % 2) == 0

    def rotary(t):
        nxt = pltpu.roll(t, hk - 1, 1)
        prv = pltpu.roll(t, 1, 1)
        return t * cos + jnp.where(even, -nxt, prv) * sin

    qf = rotary(q_ref[...].astype(F32))
    kf = rotary(k_ref[...].astype(F32)) * (RET_DK ** -0.5)
    v = v_ref[...]

    lane_k = lax.broadcasted_iota(jnp.int32, (1, hk), 1) // RET_DK
    lane_v = lax.broadcasted_iota(jnp.int32, (1, hv), 1) // RET_DV
    q4 = jnp.concatenate([jnp.where(lane_k == h, qf, 0.0) for h in range(RET_HEADS)], axis=0).astype(BF16)
    s4 = _dot_nt(q4, kf.astype(BF16)) * inner_ref[...]
    pv = _dot(s4.astype(BF16), v)
    acc = jnp.zeros((c_len, hv), F32)
    for h in range(RET_HEADS):
        acc = acc + jnp.where(lane_v == h, pv[h * c_len:(h + 1) * c_len, :], 0.0)

    st = state_ref[...]
    o = acc + _dot_nt(qf.astype(BF16), st.astype(BF16)) * cross_ref[...]
    ks = (kf * sdec_ref[...]).astype(BF16)
    sr = lax.broadcasted_iota(jnp.int32, (hv, hk), 0) // RET_DV
    sc = lax.broadcasted_iota(jnp.int32, (hv, hk), 1) // RET_DK
    state_ref[...] = jnp.where(sr == sc, st * cdec_ref[...] + _dot_tn(v, ks), 0.0)

    vr = lax.broadcasted_iota(jnp.int32, (hv, hv), 0) // RET_DV
    vc = lax.broadcasted_iota(jnp.int32, (hv, hv), 1) // RET_DV
    seg = (vr == vc).astype(BF16)
    ms = _seg_sum(o * o, seg) * (1.0 / RET_DV)
    g = g_ref[...].astype(F32)
    o_ref[...] = (o * lax.rsqrt(ms + HEAD_NORM_EPS) * (g * jax.nn.sigmoid(g))).astype(o_ref.dtype)


def _ret_tables(seq):
    c_len = RET_CHUNK
    pos = jnp.arange(seq, dtype=F32)
    angle = 1.0 / (10000.0 ** jnp.linspace(0.0, 1.0, RET_DK // 2, dtype=F32))
    angle = jnp.repeat(angle, 2)
    sin = jnp.tile(jnp.sin(pos[:, None] * angle), (1, RET_HEADS))
    cos = jnp.tile(jnp.cos(pos[:, None] * angle), (1, RET_HEADS))
    log_g = jnp.log1p(-jnp.exp2(-5.0 - jnp.arange(RET_HEADS, dtype=F32)))
    idx = jnp.arange(c_len, dtype=F32)
    rel = idx[:, None] - idx[None, :]
    inner = jnp.where(rel[None] >= 0, jnp.exp(jnp.maximum(rel, 0.0)[None] * log_g[:, None, None]), 0.0)
    inner = inner.reshape(RET_HEADS * c_len, c_len)
    cross = jnp.exp((idx + 1.0)[None] * log_g[:, None])
    cross = jnp.repeat(cross.T, RET_DV, axis=1)
    sdec = jnp.exp((c_len - 1.0 - idx)[None] * log_g[:, None])
    sdec = jnp.repeat(sdec.T, RET_DK, axis=1)
    cdec = jnp.repeat(jnp.exp(c_len * log_g), RET_DK)[None, :]
    return sin, cos, inner, cross, sdec, cdec


def _ret(z, tables, batch, seq):
    c_len = RET_CHUNK
    nt = seq // c_len
    hk = RET_HEADS * RET_DK
    hv = RET_HEADS * RET_DV
    sin, cos, inner, cross, sdec, cdec = tables

    def rowmap(cb):
        return lambda b, t: (b * nt + t, cb)

    const = lambda b, t: (0, 0)
    return pl.pallas_call(
        _ret_kernel,
        grid=(batch, nt),
        in_specs=[pl.BlockSpec((c_len, hk), rowmap(_RQ // hk)),
                  pl.BlockSpec((c_len, hk), rowmap(_RK // hk)),
                  pl.BlockSpec((c_len, hv), rowmap(_RV // hv)),
                  pl.BlockSpec((c_len, hv), rowmap(_RG // hv)),
                  pl.BlockSpec((c_len, hk), lambda b, t: (t, 0)),
                  pl.BlockSpec((c_len, hk), lambda b, t: (t, 0)),
                  pl.BlockSpec((RET_HEADS * c_len, c_len), const),
                  pl.BlockSpec((c_len, hv), const),
                  pl.BlockSpec((c_len, hk), const),
                  pl.BlockSpec((1, hk), const)],
        out_specs=pl.BlockSpec((c_len, hv), lambda b, t: (b * nt + t, 0)),
        out_shape=jax.ShapeDtypeStruct((batch * seq, hv), BF16),
        scratch_shapes=[pltpu.VMEM((hv, hk), F32)],
        compiler_params=_cparams(("arbitrary", "arbitrary")),
        name="retention",
    )(z, z, z, z, sin, cos, inner, cross, sdec, cdec)


def _layer_norm(t, g, b):
    mu = jnp.mean(t, axis=-1, keepdims=True)
    var = jnp.mean(jnp.square(t - mu), axis=-1, keepdims=True)
    return (t - mu) * lax.rsqrt(var + LN_EPS) * g + b


def _out_ln_kernel(gla_ref, dif_ref, ret_ref, x_ref, wo_ref, g_ref, b_ref, wr_hi_ref, wr_lo_ref, br_ref,
                   x1_ref, x1b_ref, x1pa_ref, x1pb_ref, gate_ref, idx_ref, cnt_ref, *, alpha):
    w0 = gla_ref.shape[1]
    w1 = w0 + dif_ref.shape[1]
    a = (_dot(gla_ref[...], wo_ref[0:w0, :]) + _dot(dif_ref[...], wo_ref[w0:w1, :])
         + _dot(ret_ref[...], wo_ref[w1:, :]))
    x1 = _layer_norm(alpha * x_ref[...] + a, g_ref[...], b_ref[...])
    x1_ref[...] = x1
    x1b_ref[...] = x1.astype(BF16)
    x1p = _pack_bf16_pairs(x1)
    x1pa_ref[...] = x1p[:, :SC_ROW_WORDS]
    x1pb_ref[...] = x1p[:, SC_ROW_WORDS:]

    hi, lo = _split_hi_lo(x1)
    logits = (_dot(hi, wr_hi_ref[...]) + _dot(lo, wr_hi_ref[...]) + _dot(hi, wr_lo_ref[...])) + br_ref[...]
    lane = lax.broadcasted_iota(jnp.int32, logits.shape, 1)
    work = jnp.where(lane < N_EXPERTS, logits, -jnp.inf)
    vals = jnp.zeros(logits.shape, F32)
    idxs = jnp.zeros(logits.shape, jnp.int32)
    chosen = jnp.zeros(logits.shape, F32)
    picks = []
    top0 = None
    for r in range(TOP_K):
        mx = jnp.max(work, axis=-1, keepdims=True)
        am = jnp.min(jnp.where(work == mx, lane, LANE), axis=-1, keepdims=True)
        if r == 0:
            top0 = mx
        vals = jnp.where(lane == r, jnp.exp(mx - top0), vals)
        idxs = jnp.where(lane == r, am, idxs)
        chosen = jnp.where(lane == am, 1.0, chosen)
        work = jnp.where(lane == am, -jnp.inf, work)
        picks.append(am)
    gate_ref[...] = vals / jnp.sum(vals, axis=-1, keepdims=True)

    @pl.when(pl.program_id(0) == 0)
    def _():
        cnt_ref[...] = jnp.zeros_like(cnt_ref)

    tm = logits.shape[0]
    earlier = (lax.broadcasted_iota(jnp.int32, (tm, tm), 0)
               > lax.broadcasted_iota(jnp.int32, (tm, tm), 1)).astype(BF16)
    before = _dot(earlier, chosen.astype(BF16)) + cnt_ref[...]
    for r in range(TOP_K):
        rank = jnp.sum(jnp.where(lane == picks[r], before, 0.0), axis=-1, keepdims=True)
        idxs = jnp.where(lane == TOP_K + r, rank.astype(jnp.int32), idxs)
    idx_ref[...] = idxs
    cnt_ref[...] = cnt_ref[...] + jnp.sum(chosen, axis=0, keepdims=True)


def _out_ln(gla_o, dif_o, ret_o, x, w_out, ln_g, ln_b, wr_hi, wr_lo, b_router, alpha, tm=512):
    n, d = x.shape
    tm = min(tm, n)
    row = lambda i: (i, 0)
    const = lambda i: (0, 0)
    return pl.pallas_call(
        functools.partial(_out_ln_kernel, alpha=alpha),
        grid=(n // tm,),
        in_specs=[pl.BlockSpec((tm, gla_o.shape[1]), row),
                  pl.BlockSpec((tm, dif_o.shape[1]), row),
                  pl.BlockSpec((tm, ret_o.shape[1]), row),
                  pl.BlockSpec((tm, d), row),
                  pl.BlockSpec(w_out.shape, const),
                  pl.BlockSpec((1, d), const),
                  pl.BlockSpec((1, d), const),
                  pl.BlockSpec((d, LANE), const),
                  pl.BlockSpec((d, LANE), const),
                  pl.BlockSpec((1, LANE), const)],
        out_specs=[pl.BlockSpec((tm, d), row),
                   pl.BlockSpec((tm, d), row),
                   pl.BlockSpec((tm, SC_ROW_WORDS), row),
                   pl.BlockSpec((tm, SC_ROW_WORDS), row),
                   pl.BlockSpec((tm, LANE), row),
                   pl.BlockSpec((tm, LANE), row),
                   pl.BlockSpec((1, LANE), const)],
        out_shape=[jax.ShapeDtypeStruct((n, d), F32),
                   jax.ShapeDtypeStruct((n, d), BF16),
                   jax.ShapeDtypeStruct((n, SC_ROW_WORDS), jnp.uint32),
                   jax.ShapeDtypeStruct((n, SC_ROW_WORDS), jnp.uint32),
                   jax.ShapeDtypeStruct((n, LANE), F32),
                   jax.ShapeDtypeStruct((n, LANE), jnp.int32),
                   jax.ShapeDtypeStruct((1, LANE), F32)],
        compiler_params=_cparams(("arbitrary",)),
        name="out_ln_router",
    )(gla_o, dif_o, ret_o, x, w_out, ln_g, ln_b, wr_hi, wr_lo, b_router)


GU_GROUP = 2 * LANE


def _deinterleave_perm():
    perm = np.zeros((GU_GROUP, GU_GROUP), np.float32)
    half = np.arange(LANE)
    perm[2 * half, half] = 1.0
    perm[2 * half + 1, LANE + half] = 1.0
    return jnp.asarray(perm, BF16)


def _moe_kernel(blk_e_ref, n_used_ref, xa_ref, xb_ref, wgu_ref, bgu_ref, wdn_ref, bdn_ref, perm_ref,
                oa_ref, ob_ref,
                wgu_b_ref, wdn_b_ref, *, rows_per_cast):
    i = pl.program_id(0)
    used = i < n_used_ref[0]
    new_expert = jnp.logical_or(i == 0, blk_e_ref[i] != blk_e_ref[jnp.maximum(i - 1, 0)])

    @pl.when(jnp.logical_and(used, new_expert))
    def _():
        d = wgu_ref.shape[0]
        for r in range(d // rows_per_cast):
            rows = slice(r * rows_per_cast, (r + 1) * rows_per_cast)
            wdn_b_ref[rows, :] = wdn_ref[rows, :].astype(BF16)
            for g in range(wgu_ref.shape[1] // GU_GROUP):
                cols = slice(g * GU_GROUP, (g + 1) * GU_GROUP)
                wgu_b_ref[rows, cols] = _dot(wgu_ref[rows, cols].astype(BF16), perm_ref[...]).astype(BF16)

    @pl.when(used)
    def _():
        x = _unpack_bf16_pairs(jnp.concatenate([xa_ref[...], xb_ref[...]], axis=1)).astype(BF16)
        gu = _dot(x, wgu_b_ref[...]) + bgu_ref[...]
        acts = []
        for grp in range(gu.shape[1] // GU_GROUP):
            g = jnp.minimum(gu[:, grp * GU_GROUP:grp * GU_GROUP + LANE], SWIGLU_LIMIT)
            u = jnp.clip(gu[:, grp * GU_GROUP + LANE:(grp + 1) * GU_GROUP], -SWIGLU_LIMIT, SWIGLU_LIMIT)
            acts.append(((u + 1.0) * (g * jax.nn.sigmoid(g * SWIGLU_ALPHA))).astype(BF16))
        act = jnp.concatenate(acts, axis=1)
        y = _pack_bf16_pairs(_dot(act, wdn_b_ref[...]) + bdn_ref[...])
        oa_ref[...] = y[:, :SC_ROW_WORDS]
        ob_ref[...] = y[:, SC_ROW_WORDS:]

    @pl.when(jnp.logical_not(used))
    def _():
        oa_ref[...] = jnp.zeros_like(oa_ref)
        ob_ref[...] = jnp.zeros_like(ob_ref)


def _moe(xs, blk_e, n_used, w_gu, b_gu, w_dn, b_dn, blk):
    rows = xs[0].shape[0]
    assert w_gu.shape[1] == 4 * SC_ROW_WORDS
    d = w_gu.shape[1]
    f2 = w_gu.shape[2]
    f = w_dn.shape[1]
    assert f == d
    grid_spec = pltpu.PrefetchScalarGridSpec(
        num_scalar_prefetch=2,
        grid=(rows // blk,),
        in_specs=[pl.BlockSpec((blk, SC_ROW_WORDS), lambda i, be, nu: (i, 0)),
                  pl.BlockSpec((blk, SC_ROW_WORDS), lambda i, be, nu: (i, 0)),
                  pl.BlockSpec((None, d, f2), lambda i, be, nu: (be[i], 0, 0)),
                  pl.BlockSpec((None, 1, f2), lambda i, be, nu: (be[i], 0, 0)),
                  pl.BlockSpec((None, f, d), lambda i, be, nu: (be[i], 0, 0)),
                  pl.BlockSpec((None, 1, d), lambda i, be, nu: (be[i], 0, 0)),
                  pl.BlockSpec((GU_GROUP, GU_GROUP), lambda i, be, nu: (0, 0))],
        out_specs=[pl.BlockSpec((blk, SC_ROW_WORDS), lambda i, be, nu: (i, 0)),
                   pl.BlockSpec((blk, SC_ROW_WORDS), lambda i, be, nu: (i, 0))],
        scratch_shapes=[pltpu.VMEM((d, f2), BF16), pltpu.VMEM((f, d), BF16)],
    )
    return pl.pallas_call(
        functools.partial(_moe_kernel, rows_per_cast=256),
        grid_spec=grid_spec,
        out_shape=[jax.ShapeDtypeStruct((rows, SC_ROW_WORDS), jnp.uint32)] * 2,
        compiler_params=_cparams(("arbitrary",)),
        name="moe_experts",
    )(blk_e, n_used, xs[0], xs[1], w_gu, b_gu, w_dn, b_dn, _deinterleave_perm())


def _combine_kernel(yga_ref, ygb_ref, gate_ref, x1_ref, x1b_ref, p_ref, wg_ref, bg_ref, wp_ref, g_ref, b_ref,
                    x2_ref, x2b_ref, *, alpha):
    d = x1_ref.shape[1]
    gates = gate_ref[...]
    m = jnp.zeros(x1_ref.shape, F32)
    for r in range(TOP_K):
        m = m + _unpack_bf16_pairs(jnp.concatenate([yga_ref[r], ygb_ref[r]], axis=1)) * gates[:, r:r + 1]
    e = (jax.nn.sigmoid(_dot(x1b_ref[...], wg_ref[...]) + bg_ref[...])
         * _dot(p_ref[...].astype(BF16), wp_ref[...]))
    x2 = _layer_norm(alpha * x1_ref[...] + m + e, g_ref[...], b_ref[...])
    x2_ref[...] = x2
    x2b_ref[...] = x2.astype(BF16)


def _combine(yg, gates, x1, x1b, p, w_pg, b_pg, w_pp, ln_g, ln_b, alpha, tm=512):
    n, d = x1.shape
    tm = min(tm, n)
    row = lambda i: (i, 0)
    const = lambda i: (0, 0)
    return pl.pallas_call(
        functools.partial(_combine_kernel, alpha=alpha),
        grid=(n // tm,),
        in_specs=[pl.BlockSpec((TOP_K, tm, SC_ROW_WORDS), lambda i: (0, i, 0)),
                  pl.BlockSpec((TOP_K, tm, SC_ROW_WORDS), lambda i: (0, i, 0)),
                  pl.BlockSpec((tm, LANE), row),
                  pl.BlockSpec((tm, d), row),
                  pl.BlockSpec((tm, d), row),
                  pl.BlockSpec((tm, p.shape[1]), row),
                  pl.BlockSpec(w_pg.shape, const),
                  pl.BlockSpec((1, d), const),
                  pl.BlockSpec(w_pp.shape, const),
                  pl.BlockSpec((1, d), const),
                  pl.BlockSpec((1, d), const)],
        out_specs=[pl.BlockSpec((tm, d), row), pl.BlockSpec((tm, d), row)],
        out_shape=[jax.ShapeDtypeStruct((n, d), F32), jax.ShapeDtypeStruct((n, d), BF16)],
        compiler_params=_cparams(("arbitrary",)),
        name="combine_ln",
    )(yg[0], yg[1], gates, x1, x1b, p, w_pg, b_pg, w_pp, ln_g, ln_b)


def _route(experts, ranks, counts, blk):
    n_tok = experts.shape[0]
    n_assign = n_tok * TOP_K
    sizes = counts.astype(jnp.int32)
    padded = ((sizes + blk - 1) // blk) * blk
    pends = jnp.cumsum(padded)
    pstarts = pends - padded
    onehot = experts[:, :, None] == jnp.arange(N_EXPERTS, dtype=jnp.int32)
    slot = jnp.sum(jnp.where(onehot, pstarts, 0), axis=-1) + ranks
    span = 2 * n_assign
    real = experts * span + jnp.arange(n_assign, dtype=jnp.int32).reshape(n_tok, TOP_K)
    j = jnp.arange(blk - 1, dtype=jnp.int32)
    filler = jnp.where(j[None, :] < (padded - sizes)[:, None],
                       jnp.arange(N_EXPERTS, dtype=jnp.int32)[:, None] * span + n_assign + j[None, :],
                       jnp.iinfo(jnp.int32).max)
    buf_len = ((n_assign + N_EXPERTS * (blk - 1) + blk - 1) // blk) * blk
    n_blk = buf_len // blk
    tail = jnp.full((buf_len - n_assign - N_EXPERTS * (blk - 1),), jnp.iinfo(jnp.int32).max, jnp.int32)
    assign = jnp.sort(jnp.concatenate([real.reshape(-1), filler.reshape(-1), tail])) % span
    buf_tok = jnp.where(assign < n_assign, assign // TOP_K, 0)
    blk_e = jnp.clip(jnp.searchsorted(pends, jnp.arange(n_blk, dtype=jnp.int32) * blk, side='right'),
                     0, N_EXPERTS - 1).astype(jnp.int32)
    n_used = (pends[-1] // blk).astype(jnp.int32).reshape(1)
    return buf_tok, slot, blk_e, n_used


def _regroup_w_in(w_in):
    depth, d, width = w_in.shape
    lr0 = _GV + GLA_HEADS * GLA_DV
    src = np.full((Z_WIDTH,), -1, np.int32)
    src[:lr0] = np.arange(lr0)
    src[lr0:width - GLA_GATE_RANK] = np.arange(lr0 + GLA_GATE_RANK, width)
    src[_LR:_LR + GLA_GATE_RANK] = np.arange(lr0, lr0 + GLA_GATE_RANK)
    select = (jnp.arange(width, dtype=jnp.int32)[:, None] == jnp.asarray(src)[None, :]).astype(BF16)
    out = _matmul(w_in.reshape(depth * d, width).astype(BF16), select, d, Z_WIDTH // 3, BF16, "regroup_w_in")
    return out.reshape(depth, d, Z_WIDTH)


def kernel(x, p, w_in, w_gla_gate, b_gla_gate, gla_norm_g, diff_lambda, diff_norm_g, w_out, rel_bias, ln1_g, ln1_b, w_router, b_router, w_gate_up, b_gate_up, w_down, b_down, w_ple_gate, b_ple_gate, w_ple_proj, ln2_g, ln2_b):
    batch, seq, d = x.shape
    depth = w_in.shape[0]
    n = batch * seq
    alpha = (2 * depth) ** 0.25
    moe_blk = 256
    diff_blk = min(512, seq)

    w_in_b = _regroup_w_in(w_in)
    wg = jnp.zeros((depth, LANE, GLA_HEADS * GLA_DK), F32).at[:, :GLA_GATE_RANK, :].set(w_gla_gate).astype(BF16)
    gla_g = jnp.tile(gla_norm_g, (1, GLA_HEADS))[:, None, :]
    w_out_b = w_out.astype(BF16)
    wr = jnp.zeros((depth, d, LANE), F32).at[:, :, :N_EXPERTS].set(w_router)
    wr_hi = wr.astype(BF16)
    wr_lo = (wr - wr_hi.astype(F32)).astype(BF16)
    br = jnp.zeros((depth, 1, LANE), F32).at[:, 0, :N_EXPERTS].set(b_router)
    w_gu = w_gate_up.reshape((depth * N_EXPERTS,) + w_gate_up.shape[2:])
    b_gu = b_gate_up.reshape(depth * N_EXPERTS, -1, LANE, 2).swapaxes(-1, -2).reshape(depth * N_EXPERTS, 1, -1)
    w_dn = w_down.reshape((depth * N_EXPERTS,) + w_down.shape[2:])
    b_dn = b_down.reshape(depth * N_EXPERTS, 1, -1)
    w_pg_b = w_ple_gate.astype(BF16)
    w_pp_b = w_ple_proj.astype(BF16)
    bias_tiles = _bias_tiles(rel_bias, diff_blk)
    ret_tables = _ret_tables(seq)

    xf = x.reshape(n, d)
    xb = xf.astype(BF16)
    pf = p.reshape(depth, n, p.shape[-1])
    for i in range(depth):
        lam_init = 0.8 - 0.6 * math.exp(-0.3 * i)
        z = _matmul(xb, w_in_b[i], min(1024, n), Z_WIDTH // 3, BF16, "in_proj")
        gla_o = _gla(z, wg[i], b_gla_gate[i][None, :], gla_g[i], batch, seq)
        dif_o = _diff(z, diff_lambda[i], diff_norm_g[i][None, :], bias_tiles, batch, seq, lam_init, diff_blk)
        ret_o = _ret(z, ret_tables, batch, seq)
        x1, x1b, x1pa, x1pb, gates, idx, cnt = _out_ln(gla_o, dif_o, ret_o, xf, w_out_b[i], ln1_g[i][None, :],
                                           ln1_b[i][None, :], wr_hi[i], wr_lo[i], br[i], alpha)
        buf_tok, slot, blk_e, n_used = _route(idx[:, :TOP_K], idx[:, TOP_K:2 * TOP_K], cnt[0, :N_EXPERTS], moe_blk)
        xs = _gather_rows([x1pa, x1pb], buf_tok)
        yb = _moe(xs, blk_e + i * N_EXPERTS, n_used, w_gu, b_gu, w_dn, b_dn, moe_blk)
        yg = [t.reshape(TOP_K, n, SC_ROW_WORDS) for t in _gather_rows(yb, slot.T.reshape(-1))]
        xf, xb = _combine(yg, gates, x1, x1b, pf[i], w_pg_b[i], b_ple_gate[i][None, :], w_pp_b[i],
                          ln2_g[i][None, :], ln2_b[i][None, :], alpha)
    return xf.reshape(batch, seq, d)
```

```python
import functools
import math

import numpy as np
import jax
import jax.numpy as jnp
from jax import lax
from jax.experimental import pallas as pl
from jax.experimental.pallas import tpu as pltpu
from jax.experimental.pallas import tpu_sc as plsc

F32 = jnp.float32
BF16 = jnp.bfloat16

GLA_HEADS = 4
GLA_DK = 32
GLA_DV = 64
GLA_GATE_RANK = 16
GLA_TAU = 16.0
GLA_CHUNK = 64
DIFF_HEADS = 4
DIFF_D = 64
RET_HEADS = 4
RET_DK = 64
RET_DV = 64
RET_CHUNK = 128
T5_BUCKETS = 32
T5_MAX_DIST = 128
N_EXPERTS = 32
TOP_K = 4
SWIGLU_LIMIT = 7.0
SWIGLU_ALPHA = 1.702
LN_EPS = 1e-5
HEAD_NORM_EPS = 1e-5

LANE = 128
VMEM_LIMIT = 56 * 1024 * 1024

_GQ, _GK, _GV, _GO = 0, 128, 256, 512
_DQ, _DK, _DV = 768, 1280, 1792
_RQ, _RK, _RV, _RG = 2304, 2560, 2816, 3072
_LR = 3328
Z_WIDTH = 3456

GLA_FAST_MAX_DECAY = 60.0


def _cparams(sem):
    return pltpu.CompilerParams(dimension_semantics=sem, vmem_limit_bytes=VMEM_LIMIT)


def _split_hi_lo(a):
    hi = a.astype(BF16)
    lo = (a - hi.astype(F32)).astype(BF16)
    return hi, lo


def _dot(a, b):
    return jnp.dot(a, b, preferred_element_type=F32)


def _dot_nt(a, b):
    return lax.dot_general(a, b, (((1,), (1,)), ((), ())), preferred_element_type=F32)


def _dot_tn(a, b):
    return lax.dot_general(a, b, (((0,), (0,)), ((), ())), preferred_element_type=F32)


def _seg_sum(a, ind):
    hi, lo = _split_hi_lo(a)
    return _dot(hi, ind) + _dot(lo, ind)


def _pack_bf16_pairs(t):
    h = t.shape[1] // 2
    bits = lax.bitcast_convert_type(t.astype(BF16).astype(F32), jnp.uint32)
    return (bits[:, :h] >> 16) | (bits[:, h:] & jnp.uint32(0xFFFF0000))


def _unpack_bf16_pairs(packed):
    lo = lax.bitcast_convert_type(packed << 16, F32)
    hi = lax.bitcast_convert_type(packed & jnp.uint32(0xFFFF0000), F32)
    return jnp.concatenate([lo, hi], axis=1)


SC_WINDOW = 128
SC_ROW_WORDS = 256


def _sc_gather(table, indices):
    m = indices.shape[0]
    words = table.shape[1]
    assert words == SC_ROW_WORDS and m % SC_WINDOW == 0
    mesh = plsc.VectorSubcoreMesh(core_axis_name="core", subcore_axis_name="subcore")

    def body(x_hbm, i_hbm, o_hbm):
        def step(i_vmem, o_vmem):
            pltpu.sync_copy(x_hbm.at[i_vmem.at[0]], o_vmem)

        pltpu.emit_pipeline(
            step,
            grid=(m // SC_WINDOW,),
            in_specs=[pl.BlockSpec((1, SC_WINDOW), index_map=lambda i: (0, i))],
            out_specs=[pl.BlockSpec((SC_WINDOW, words), index_map=lambda i: (i, 0))],
            core_axis_name=("core", "subcore"),
            dimension_semantics=(pltpu.PARALLEL,),
        )(i_hbm, o_hbm)

    return pl.kernel(body, out_type=jax.ShapeDtypeStruct((m, words), table.dtype), mesh=mesh,
                     scratch_types=[])(table, indices.reshape(1, m))


def _sc_scatter(table, indices, out_rows):
    r, words = table.shape
    m = indices.shape[0]
    assert words == SC_ROW_WORDS and r % SC_WINDOW == 0 and m % r == 0
    mesh = plsc.VectorSubcoreMesh(core_axis_name="core", subcore_axis_name="subcore")
    steps_per_pass = r // SC_WINDOW

    def body(x_hbm, i_hbm, o_hbm):
        def step(x_vmem, i_vmem):
            pltpu.sync_copy(x_vmem, o_hbm.at[i_vmem.at[0]])

        pltpu.emit_pipeline(
            step,
            grid=(m // SC_WINDOW,),
            in_specs=[pl.BlockSpec((SC_WINDOW, words), index_map=lambda i: (i % steps_per_pass, 0)),
                      pl.BlockSpec((1, SC_WINDOW), index_map=lambda i: (0, i))],
            out_specs=[],
            core_axis_name=("core", "subcore"),
            dimension_semantics=(pltpu.PARALLEL,),
        )(x_hbm, i_hbm)

    return pl.kernel(body, out_type=jax.ShapeDtypeStruct((out_rows, words), table.dtype), mesh=mesh,
                     scratch_types=[])(table, indices.reshape(1, m))


def _gather_rows(tables, rows):
    return [_sc_gather(t, rows) for t in tables]


def _mm_kernel(x_ref, w_ref, o_ref):
    o_ref[...] = _dot(x_ref[...], w_ref[...]).astype(o_ref.dtype)


def _matmul(x, w, tm, tn, out_dtype, name):
    m, k = x.shape
    n = w.shape[1]
    return pl.pallas_call(
        _mm_kernel,
        grid=(n // tn, m // tm),
        in_specs=[pl.BlockSpec((tm, k), lambda j, i: (i, 0)),
                  pl.BlockSpec((k, tn), lambda j, i: (0, j))],
        out_specs=pl.BlockSpec((tm, tn), lambda j, i: (i, j)),
        out_shape=jax.ShapeDtypeStruct((m, n), out_dtype),
        compiler_params=_cparams(("arbitrary", "arbitrary")),
        name=name,
    )(x, w)


def _gla_kernel(q_ref, k_ref, v_ref, og_ref, lr_ref, wg_ref, bg_ref, ng_ref, o_ref,
                state_ref, la_ref, cum_ref, kf_ref, vf_ref, oacc_ref, intra_ref, *, n_chunks, fast_max_decay):
    c_len = GLA_CHUNK
    hk = GLA_HEADS * GLA_DK
    hv = GLA_HEADS * GLA_DV

    @pl.when(pl.program_id(1) == 0)
    def _():
        state_ref[...] = jnp.zeros_like(state_ref)

    zg = _dot(lr_ref[...], wg_ref[...]) + bg_ref[...]
    la = (jnp.minimum(zg, 0.0) - jnp.log1p(jnp.exp(-jnp.abs(zg)))) / GLA_TAU
    la_ref[...] = la
    chunk_tot = jnp.sum(la.reshape(n_chunks, c_len, hk), axis=1)
    fast = jnp.min(chunk_tot) > -fast_max_decay

    row = lax.broadcasted_iota(jnp.int32, (c_len, c_len), 0)
    col = lax.broadcasted_iota(jnp.int32, (c_len, c_len), 1)
    tril = (row >= col).astype(BF16)
    sr = lax.broadcasted_iota(jnp.int32, (hv, hk), 0)
    sc = lax.broadcasted_iota(jnp.int32, (hv, hk), 1)
    state_mask = ((sr // GLA_DV) == (sc // GLA_DK)).astype(F32)
    lane_k = lax.broadcasted_iota(jnp.int32, (1, hk), 1) // GLA_DK
    lane_v = lax.broadcasted_iota(jnp.int32, (1, hv), 1) // GLA_DV
    r4 = lax.broadcasted_iota(jnp.int32, (GLA_HEADS * c_len, c_len), 0) % c_len
    c4 = lax.broadcasted_iota(jnp.int32, (GLA_HEADS * c_len, c_len), 1)
    causal4 = r4 >= c4
    er = lax.broadcasted_iota(jnp.int32, (hk, hv), 0) // GLA_DK
    ec = lax.broadcasted_iota(jnp.int32, (hk, hv), 1) // GLA_DV
    head_expand = (er == ec).astype(BF16)
    trow = lax.broadcasted_iota(jnp.int32, (c_len, hv), 0)

    def chunk(c, carry):
        r0 = pl.multiple_of(c * c_len, c_len)
        rows = pl.ds(r0, c_len)
        q = q_ref[rows, :].astype(F32) * (GLA_DK ** -0.5)
        k = k_ref[rows, :].astype(F32)
        v = v_ref[rows, :]
        la_c = la_ref[rows, :]
        la_hi, la_lo = _split_hi_lo(la_c)
        cum = _dot(tril, la_hi) + _dot(tril, la_lo)
        last = cum[c_len - 1:c_len, :]

        @pl.when(fast)
        def _():
            mid = cum[c_len // 2:c_len // 2 + 1, :]
            qa = q * jnp.exp(cum - mid)
            kb = (k * jnp.exp(mid - cum)).astype(BF16)
            qa4 = jnp.concatenate(
                [jnp.where(lane_k == h, qa, 0.0) for h in range(GLA_HEADS)], axis=0).astype(BF16)
            s4 = jnp.where(causal4, _dot_nt(qa4, kb), 0.0)
            pv = _dot(s4.astype(BF16), v)
            acc = jnp.zeros((c_len, hv), F32)
            for h in range(GLA_HEADS):
                acc = acc + jnp.where(lane_v == h, pv[h * c_len:(h + 1) * c_len, :], 0.0)
            intra_ref[...] = acc

        @pl.when(jnp.logical_not(fast))
        def _():
            cum_ref[...] = cum
            kf_ref[...] = k
            vf_ref[...] = v.astype(F32)

            def one_key(s, acc):
                ks = kf_ref[pl.ds(s, 1), :]
                vs = vf_ref[pl.ds(s, 1), :]
                cs = cum_ref[pl.ds(s, 1), :]
                w = q * ks * jnp.exp(jnp.minimum(cum - cs, 0.0))
                wsum = _dot(w.astype(BF16), head_expand)
                return acc + jnp.where(trow >= s, wsum, 0.0) * vs

            intra_ref[...] = lax.fori_loop(0, c_len, one_key, jnp.zeros((c_len, hv), F32))

        st = state_ref[...]
        qg = (q * jnp.exp(cum)).astype(BF16)
        inter = _dot_nt(qg, st.astype(BF16))
        oacc_ref[rows, :] = intra_ref[...] + inter
        kg = (k * jnp.exp(last - cum)).astype(BF16)
        upd = _dot_tn(v, kg)
        state_ref[...] = (st * jnp.exp(last) + upd) * state_mask
        return carry

    lax.fori_loop(0, n_chunks, chunk, 0)

    o = oacc_ref[...]
    vr = lax.broadcasted_iota(jnp.int32, (hv, hv), 0) // GLA_DV
    vc = lax.broadcasted_iota(jnp.int32, (hv, hv), 1) // GLA_DV
    seg = (vr == vc).astype(BF16)
    ms = _seg_sum(o * o, seg) * (1.0 / GLA_DV)
    y = o * lax.rsqrt(ms + HEAD_NORM_EPS) * ng_ref[...]
    g = og_ref[...].astype(F32)
    o_ref[...] = (y * (g * jax.nn.sigmoid(g))).astype(o_ref.dtype)


def _gla(z, w_gate, b_gate, norm_g, batch, seq, tg=512, fast_max_decay=GLA_FAST_MAX_DECAY):
    tg = min(tg, seq)
    nt = seq // tg
    hk = GLA_HEADS * GLA_DK
    hv = GLA_HEADS * GLA_DV

    def rowmap(cb):
        return lambda b, t: (b * nt + t, cb)

    const = lambda b, t: (0, 0)
    return pl.pallas_call(
        functools.partial(_gla_kernel, n_chunks=tg // GLA_CHUNK, fast_max_decay=fast_max_decay),
        grid=(batch, nt),
        in_specs=[pl.BlockSpec((tg, hk), rowmap(_GQ // hk)),
                  pl.BlockSpec((tg, hk), rowmap(_GK // hk)),
                  pl.BlockSpec((tg, hv), rowmap(_GV // hv)),
                  pl.BlockSpec((tg, hv), rowmap(_GO // hv)),
                  pl.BlockSpec((tg, LANE), rowmap(_LR // LANE)),
                  pl.BlockSpec((LANE, hk), const),
                  pl.BlockSpec((1, hk), const),
                  pl.BlockSpec((1, hv), const)],
        out_specs=pl.BlockSpec((tg, hv), lambda b, t: (b * nt + t, 0)),
        out_shape=jax.ShapeDtypeStruct((batch * seq, hv), BF16),
        scratch_shapes=[pltpu.VMEM((hv, hk), F32),
                        pltpu.VMEM((tg, hk), F32),
                        pltpu.VMEM((GLA_CHUNK, hk), F32),
                        pltpu.VMEM((GLA_CHUNK, hk), F32),
                        pltpu.VMEM((GLA_CHUNK, hv), F32),
                        pltpu.VMEM((tg, hv), F32),
                        pltpu.VMEM((GLA_CHUNK, hv), F32)],
        compiler_params=_cparams(("arbitrary", "arbitrary")),
        name="gla",
    )(z, z, z, z, z, w_gate, b_gate, norm_g)


LOG2E = math.log2(math.e)


def _diff_kernel(far_ref, lam_ref, q_ref, k_ref, v_ref, bias_ref, ng_ref, o_ref,
                 m_ref, acc_ref, *, blk, lam_init):
    qi = pl.program_id(2)
    d = DIFF_D
    w = 2 * DIFF_D
    q = (q_ref[...].astype(F32) * (DIFF_D ** -0.5 * LOG2E)).astype(BF16)
    lane_q = lax.broadcasted_iota(jnp.int32, q.shape, 1)
    q2 = jnp.concatenate([jnp.where(lane_q < d, q, jnp.zeros_like(q)),
                          jnp.where(lane_q >= d, q, jnp.zeros_like(q))], axis=0)
    m_ref[...] = jnp.full_like(m_ref, -jnp.inf)
    acc_ref[...] = jnp.zeros_like(acc_ref)

    def kv_step(j, bias, shift):
        rows = pl.ds(pl.multiple_of(j * blk, blk), blk)
        vj = v_ref[rows, :]
        s_both = _dot_nt(q2, k_ref[rows, :])
        for m in range(2):
            s = s_both[m * blk:(m + 1) * blk, :]
            if bias is not None:
                s = s + bias
            m_cur = jnp.max(s, axis=-1, keepdims=True)
            if shift is not None:
                m_cur = m_cur + shift
            m_old = m_ref[m]
            m_new = jnp.maximum(m_old, m_cur)
            alpha = jnp.exp2(m_old - m_new)
            sub = m_new if shift is None else m_new - shift
            p = jnp.exp2(s - jnp.concatenate([sub] * (blk // LANE), axis=1))
            row_sum = jnp.broadcast_to(jnp.sum(p, axis=-1, keepdims=True), (blk, LANE))
            acc_ref[m] = (jnp.concatenate([alpha, alpha], axis=1) * acc_ref[m]
                          + jnp.concatenate([_dot(p.astype(BF16), vj), row_sum], axis=1))
            m_ref[m] = m_new

    far = far_ref[pl.program_id(1)]

    def far_step(j, carry):
        kv_step(j, None, far)
        return carry

    lax.fori_loop(0, jnp.maximum(qi - 1, 0), far_step, 0)

    @pl.when(qi >= 1)
    def _():
        kv_step(qi - 1, bias_ref[0, 1], None)

    kv_step(qi, bias_ref[0, 0], None)

    lf = lam_ref[...]
    lam = (jnp.exp(jnp.sum(lf[0:1] * lf[1:2], axis=-1, keepdims=True))
           - jnp.exp(jnp.sum(lf[2:3] * lf[3:4], axis=-1, keepdims=True)) + lam_init)
    a0 = acc_ref[0]
    a1 = acc_ref[1]
    o = a0[:, :w] / a0[:, w:] - lam * (a1[:, :w] / a1[:, w:])
    ms = jnp.mean(o * o, axis=-1, keepdims=True)
    y = o * lax.rsqrt(ms + HEAD_NORM_EPS) * ng_ref[...] * (1.0 - lam_init)
    o_ref[...] = y.astype(o_ref.dtype)


def _t5_bucket(rel):
    n = jnp.maximum(-rel, 0)
    max_exact = T5_BUCKETS // 2
    nf = jnp.maximum(n, 1).astype(F32)
    large = max_exact + (jnp.log(nf / max_exact) / math.log(T5_MAX_DIST / max_exact)
                         * (T5_BUCKETS - max_exact)).astype(jnp.int32)
    large = jnp.minimum(large, T5_BUCKETS - 1)
    return jnp.where(n < max_exact, n, large)


def _bias_tiles(rel_bias, blk):
    assert blk + 1 >= T5_MAX_DIST
    m = jnp.arange(2 * blk)
    col_minus_row = jnp.where(m < blk, m, m - 2 * blk)
    tiles = []
    for delta in range(2):
        rel = col_minus_row - delta * blk
        w = rel_bias[_t5_bucket(rel)].astype(F32).T
        if delta == 0:
            w = jnp.where(rel <= 0, w, -jnp.inf)
        flat = jnp.tile(w, (1, blk))[:, :blk * (2 * blk - 1)]
        tiles.append(flat.reshape(-1, blk, 2 * blk - 1)[:, :, :blk])
    far = rel_bias[_t5_bucket(jnp.full((1,), -(blk + 1)))[0]].astype(F32)
    return jnp.stack(tiles, axis=1) * LOG2E, far * LOG2E


def _diff(z, diff_lambda, norm_g, bias, batch, seq, lam_init, blk):
    blk = min(blk, seq)
    nq = seq // blk
    w = 2 * DIFF_D
    bias_tiles, bias_far = bias
    return pl.pallas_call(
        functools.partial(_diff_kernel, blk=blk, lam_init=lam_init),
        grid=(batch, DIFF_HEADS, nq),
        in_specs=[pl.BlockSpec(memory_space=pltpu.SMEM),
                  pl.BlockSpec((4, DIFF_D), lambda b, h, i: (0, 0)),
                  pl.BlockSpec((blk, w), lambda b, h, i: (b * nq + i, _DQ // w + h)),
                  pl.BlockSpec((seq, w), lambda b, h, i: (b, _DK // w + h)),
                  pl.BlockSpec((seq, w), lambda b, h, i: (b, _DV // w + h)),
                  pl.BlockSpec((1, 2, blk, blk), lambda b, h, i: (h, 0, 0, 0)),
                  pl.BlockSpec((1, w), lambda b, h, i: (0, 0))],
        out_specs=pl.BlockSpec((blk, w), lambda b, h, i: (b * nq + i, h)),
        out_shape=jax.ShapeDtypeStruct((batch * seq, DIFF_HEADS * w), BF16),
        scratch_shapes=[pltpu.VMEM((2, blk, LANE), F32),
                        pltpu.VMEM((2, blk, w + LANE), F32)],
        compiler_params=_cparams(("arbitrary", "arbitrary", "arbitrary")),
        name="diff_attn",
    )(bias_far, diff_lambda, z, z, z, bias_tiles, norm_g)


def _ret_kernel(q_ref, k_ref, v_ref, g_ref, sin_ref, cos_ref, inner_ref, cross_ref, sdec_ref, cdec_ref,
                o_ref, state_ref):
    c_len = RET_CHUNK
    hk = RET_HEADS * RET_DK
    hv = RET_HEADS * RET_DV

    @pl.when(pl.program_id(1) == 0)
    def _():
        state_ref[...] = jnp.zeros_like(state_ref)

    sin = sin_ref[...]
    cos = cos_ref[...]
    even = (lax.broadcasted_iota(jnp.int32, (1, hk), 1) % 2) == 0

    def rotary(t):
        nxt = pltpu.roll(t, hk - 1, 1)
        prv = pltpu.roll(t, 1, 1)
        return t * cos + jnp.where(even, -nxt, prv) * sin

    qf = rotary(q_ref[...].astype(F32))
    kf = rotary(k_ref[...].astype(F32)) * (RET_DK ** -0.5)
    v = v_ref[...]

    lane_k = lax.broadcasted_iota(jnp.int32, (1, hk), 1) // RET_DK
    lane_v = lax.broadcasted_iota(jnp.int32, (1, hv), 1) // RET_DV
    q4 = jnp.concatenate([jnp.where(lane_k == h, qf, 0.0) for h in range(RET_HEADS)], axis=0).astype(BF16)
    s4 = _dot_nt(q4, kf.astype(BF16)) * inner_ref[...]
    pv = _dot(s4.astype(BF16), v)
    acc = jnp.zeros((c_len, hv), F32)
    for h in range(RET_HEADS):
        acc = acc + jnp.where(lane_v == h, pv[h * c_len:(h + 1) * c_len, :], 0.0)

    st = state_ref[...]
    o = acc + _dot_nt(qf.astype(BF16), st.astype(BF16)) * cross_ref[...]
    ks = (kf * sdec_ref[...]).astype(BF16)
    sr = lax.broadcasted_iota(jnp.int32, (hv, hk), 0) // RET_DV
    sc = lax.broadcasted_iota(jnp.int32, (hv, hk), 1) // RET_DK
    state_ref[...] = jnp.where(sr == sc, st * cdec_ref[...] + _dot_tn(v, ks), 0.0)

    vr = lax.broadcasted_iota(jnp.int32, (hv, hv), 0) // RET_DV
    vc = lax.broadcasted_iota(jnp.int32, (hv, hv), 1) // RET_DV
    seg = (vr == vc).astype(BF16)
    ms = _seg_sum(o * o, seg) * (1.0 / RET_DV)
    g = g_ref[...].astype(F32)
    o_ref[...] = (o * lax.rsqrt(ms + HEAD_NORM_EPS) * (g * jax.nn.sigmoid(g))).astype(o_ref.dtype)


def _ret_tables(seq):
    c_len = RET_CHUNK
    pos = jnp.arange(seq, dtype=F32)
    angle = 1.0 / (10000.0 ** jnp.linspace(0.0, 1.0, RET_DK // 2, dtype=F32))
    angle = jnp.repeat(angle, 2)
    sin = jnp.tile(jnp.sin(pos[:, None] * angle), (1, RET_HEADS))
    cos = jnp.tile(jnp.cos(pos[:, None] * angle), (1, RET_HEADS))
    log_g = jnp.log1p(-jnp.exp2(-5.0 - jnp.arange(RET_HEADS, dtype=F32)))
    idx = jnp.arange(c_len, dtype=F32)
    rel = idx[:, None] - idx[None, :]
    inner = jnp.where(rel[None] >= 0, jnp.exp(jnp.maximum(rel, 0.0)[None] * log_g[:, None, None]), 0.0)
    inner = inner.reshape(RET_HEADS * c_len, c_len)
    cross = jnp.exp((idx + 1.0)[None] * log_g[:, None])
    cross = jnp.repeat(cross.T, RET_DV, axis=1)
    sdec = jnp.exp((c_len - 1.0 - idx)[None] * log_g[:, None])
    sdec = jnp.repeat(sdec.T, RET_DK, axis=1)
    cdec = jnp.repeat(jnp.exp(c_len * log_g), RET_DK)[None, :]
    return sin, cos, inner, cross, sdec, cdec


def _ret(z, tables, batch, seq):
    c_len = RET_CHUNK
    nt = seq // c_len
    hk = RET_HEADS * RET_DK
    hv = RET_HEADS * RET_DV
    sin, cos, inner, cross, sdec, cdec = tables

    def rowmap(cb):
        return lambda b, t: (b * nt + t, cb)

    const = lambda b, t: (0, 0)
    return pl.pallas_call(
        _ret_kernel,
        grid=(batch, nt),
        in_specs=[pl.BlockSpec((c_len, hk), rowmap(_RQ // hk)),
                  pl.BlockSpec((c_len, hk), rowmap(_RK // hk)),
                  pl.BlockSpec((c_len, hv), rowmap(_RV // hv)),
                  pl.BlockSpec((c_len, hv), rowmap(_RG // hv)),
                  pl.BlockSpec((c_len, hk), lambda b, t: (t, 0)),
                  pl.BlockSpec((c_len, hk), lambda b, t: (t, 0)),
                  pl.BlockSpec((RET_HEADS * c_len, c_len), const),
                  pl.BlockSpec((c_len, hv), const),
                  pl.BlockSpec((c_len, hk), const),
                  pl.BlockSpec((1, hk), const)],
        out_specs=pl.BlockSpec((c_len, hv), lambda b, t: (b * nt + t, 0)),
        out_shape=jax.ShapeDtypeStruct((batch * seq, hv), BF16),
        scratch_shapes=[pltpu.VMEM((hv, hk), F32)],
        compiler_params=_cparams(("arbitrary", "arbitrary")),
        name="retention",
    )(z, z, z, z, sin, cos, inner, cross, sdec, cdec)


def _layer_norm(t, g, b):
    mu = jnp.mean(t, axis=-1, keepdims=True)
    var = jnp.mean(jnp.square(t - mu), axis=-1, keepdims=True)
    return (t - mu) * lax.rsqrt(var + LN_EPS) * g + b


def _out_ln_kernel(gla_ref, dif_ref, ret_ref, x_ref, wo_ref, g_ref, b_ref, wr_hi_ref, wr_lo_ref, br_ref,
                   x1_ref, x1b_ref, x1pa_ref, x1pb_ref, gate_ref, idx_ref, cnt_ref, *, alpha):
    w0 = gla_ref.shape[1]
    w1 = w0 + dif_ref.shape[1]
    a = (_dot(gla_ref[...], wo_ref[0:w0, :]) + _dot(dif_ref[...], wo_ref[w0:w1, :])
         + _dot(ret_ref[...], wo_ref[w1:, :]))
    x1 = _layer_norm(alpha * x_ref[...] + a, g_ref[...], b_ref[...])
    x1_ref[...] = x1
    x1b_ref[...] = x1.astype(BF16)
    x1p = _pack_bf16_pairs(x1)
    x1pa_ref[...] = x1p[:, :SC_ROW_WORDS]
    x1pb_ref[...] = x1p[:, SC_ROW_WORDS:]

    hi, lo = _split_hi_lo(x1)
    logits = (_dot(hi, wr_hi_ref[...]) + _dot(lo, wr_hi_ref[...]) + _dot(hi, wr_lo_ref[...])) + br_ref[...]
    lane = lax.broadcasted_iota(jnp.int32, logits.shape, 1)
    work = jnp.where(lane < N_EXPERTS, logits, -jnp.inf)
    vals = jnp.zeros(logits.shape, F32)
    idxs = jnp.zeros(logits.shape, jnp.int32)
    chosen = jnp.zeros(logits.shape, F32)
    picks = []
    top0 = None
    for r in range(TOP_K):
        mx = jnp.max(work, axis=-1, keepdims=True)
        am = jnp.min(jnp.where(work == mx, lane, LANE), axis=-1, keepdims=True)
        if r == 0:
            top0 = mx
        vals = jnp.where(lane == r, jnp.exp(mx - top0), vals)
        idxs = jnp.where(lane == r, am, idxs)
        chosen = jnp.where(lane == am, 1.0, chosen)
        work = jnp.where(lane == am, -jnp.inf, work)
        picks.append(am)
    gate_ref[...] = vals / jnp.sum(vals, axis=-1, keepdims=True)

    @pl.when(pl.program_id(0) == 0)
    def _():
        cnt_ref[...] = jnp.zeros_like(cnt_ref)

    tm = logits.shape[0]
    earlier = (lax.broadcasted_iota(jnp.int32, (tm, tm), 0)
               > lax.broadcasted_iota(jnp.int32, (tm, tm), 1)).astype(BF16)
    before = _dot(earlier, chosen.astype(BF16)) + cnt_ref[...]
    for r in range(TOP_K):
        rank = jnp.sum(jnp.where(lane == picks[r], before, 0.0), axis=-1, keepdims=True)
        idxs = jnp.where(lane == TOP_K + r, rank.astype(jnp.int32), idxs)
    idx_ref[...] = idxs
    cnt_ref[...] = cnt_ref[...] + jnp.sum(chosen, axis=0, keepdims=True)


def _out_ln(gla_o, dif_o, ret_o, x, w_out, ln_g, ln_b, wr_hi, wr_lo, b_router, alpha, tm=512):
    n, d = x.shape
    tm = min(tm, n)
    row = lambda i: (i, 0)
    const = lambda i: (0, 0)
    return pl.pallas_call(
        functools.partial(_out_ln_kernel, alpha=alpha),
        grid=(n // tm,),
        in_specs=[pl.BlockSpec((tm, gla_o.shape[1]), row),
                  pl.BlockSpec((tm, dif_o.shape[1]), row),
                  pl.BlockSpec((tm, ret_o.shape[1]), row),
                  pl.BlockSpec((tm, d), row),
                  pl.BlockSpec(w_out.shape, const),
                  pl.BlockSpec((1, d), const),
                  pl.BlockSpec((1, d), const),
                  pl.BlockSpec((d, LANE), const),
                  pl.BlockSpec((d, LANE), const),
                  pl.BlockSpec((1, LANE), const)],
        out_specs=[pl.BlockSpec((tm, d), row),
                   pl.BlockSpec((tm, d), row),
                   pl.BlockSpec((tm, SC_ROW_WORDS), row),
                   pl.BlockSpec((tm, SC_ROW_WORDS), row),
                   pl.BlockSpec((tm, LANE), row),
                   pl.BlockSpec((tm, LANE), row),
                   pl.BlockSpec((1, LANE), const)],
        out_shape=[jax.ShapeDtypeStruct((n, d), F32),
                   jax.ShapeDtypeStruct((n, d), BF16),
                   jax.ShapeDtypeStruct((n, SC_ROW_WORDS), jnp.uint32),
                   jax.ShapeDtypeStruct((n, SC_ROW_WORDS), jnp.uint32),
                   jax.ShapeDtypeStruct((n, LANE), F32),
                   jax.ShapeDtypeStruct((n, LANE), jnp.int32),
                   jax.ShapeDtypeStruct((1, LANE), F32)],
        compiler_params=_cparams(("arbitrary",)),
        name="out_ln_router",
    )(gla_o, dif_o, ret_o, x, w_out, ln_g, ln_b, wr_hi, wr_lo, b_router)


GU_GROUP = 2 * LANE


def _deinterleave_perm():
    perm = np.zeros((GU_GROUP, GU_GROUP), np.float32)
    half = np.arange(LANE)
    perm[2 * half, half] = 1.0
    perm[2 * half + 1, LANE + half] = 1.0
    return jnp.asarray(perm, BF16)


def _moe_kernel(blk_e_ref, n_used_ref, xa_ref, xb_ref, wgu_ref, bgu_ref, wdn_ref, bdn_ref, perm_ref,
                oa_ref, ob_ref,
                wgu_b_ref, wdn_b_ref, *, rows_per_cast):
    i = pl.program_id(0)
    used = i < n_used_ref[0]
    new_expert = jnp.logical_or(i == 0, blk_e_ref[i] != blk_e_ref[jnp.maximum(i - 1, 0)])

    @pl.when(jnp.logical_and(used, new_expert))
    def _():
        d = wgu_ref.shape[0]
        for r in range(d // rows_per_cast):
            rows = slice(r * rows_per_cast, (r + 1) * rows_per_cast)
            wdn_b_ref[rows, :] = wdn_ref[rows, :].astype(BF16)
            for g in range(wgu_ref.shape[1] // GU_GROUP):
                cols = slice(g * GU_GROUP, (g + 1) * GU_GROUP)
                wgu_b_ref[rows, cols] = _dot(wgu_ref[rows, cols].astype(BF16), perm_ref[...]).astype(BF16)

    @pl.when(used)
    def _():
        x = _unpack_bf16_pairs(jnp.concatenate([xa_ref[...], xb_ref[...]], axis=1)).astype(BF16)
        gu = _dot(x, wgu_b_ref[...]) + bgu_ref[...]
        acts = []
        for grp in range(gu.shape[1] // GU_GROUP):
            g = jnp.minimum(gu[:, grp * GU_GROUP:grp * GU_GROUP + LANE], SWIGLU_LIMIT)
            u = jnp.clip(gu[:, grp * GU_GROUP + LANE:(grp + 1) * GU_GROUP], -SWIGLU_LIMIT, SWIGLU_LIMIT)
            acts.append(((u + 1.0) * (g * jax.nn.sigmoid(g * SWIGLU_ALPHA))).astype(BF16))
        act = jnp.concatenate(acts, axis=1)
        y = _pack_bf16_pairs(_dot(act, wdn_b_ref[...]) + bdn_ref[...])
        oa_ref[...] = y[:, :SC_ROW_WORDS]
        ob_ref[...] = y[:, SC_ROW_WORDS:]

    @pl.when(jnp.logical_not(used))
    def _():
        oa_ref[...] = jnp.zeros_like(oa_ref)
        ob_ref[...] = jnp.zeros_like(ob_ref)


def _moe(xs, blk_e, n_used, w_gu, b_gu, w_dn, b_dn, blk):
    rows = xs[0].shape[0]
    assert w_gu.shape[1] == 4 * SC_ROW_WORDS
    d = w_gu.shape[1]
    f2 = w_gu.shape[2]
    f = w_dn.shape[1]
    assert f == d
    grid_spec = pltpu.PrefetchScalarGridSpec(
        num_scalar_prefetch=2,
        grid=(rows // blk,),
        in_specs=[pl.BlockSpec((blk, SC_ROW_WORDS), lambda i, be, nu: (i, 0)),
                  pl.BlockSpec((blk, SC_ROW_WORDS), lambda i, be, nu: (i, 0)),
                  pl.BlockSpec((None, d, f2), lambda i, be, nu: (be[i], 0, 0)),
                  pl.BlockSpec((None, 1, f2), lambda i, be, nu: (be[i], 0, 0)),
                  pl.BlockSpec((None, f, d), lambda i, be, nu: (be[i], 0, 0)),
                  pl.BlockSpec((None, 1, d), lambda i, be, nu: (be[i], 0, 0)),
                  pl.BlockSpec((GU_GROUP, GU_GROUP), lambda i, be, nu: (0, 0))],
        out_specs=[pl.BlockSpec((blk, SC_ROW_WORDS), lambda i, be, nu: (i, 0)),
                   pl.BlockSpec((blk, SC_ROW_WORDS), lambda i, be, nu: (i, 0))],
        scratch_shapes=[pltpu.VMEM((d, f2), BF16), pltpu.VMEM((f, d), BF16)],
    )
    return pl.pallas_call(
        functools.partial(_moe_kernel, rows_per_cast=256),
        grid_spec=grid_spec,
        out_shape=[jax.ShapeDtypeStruct((rows, SC_ROW_WORDS), jnp.uint32)] * 2,
        compiler_params=_cparams(("arbitrary",)),
        name="moe_experts",
    )(blk_e, n_used, xs[0], xs[1], w_gu, b_gu, w_dn, b_dn, _deinterleave_perm())


def _combine_kernel(yga_ref, ygb_ref, gate_ref, x1_ref, x1b_ref, p_ref, wg_ref, bg_ref, wp_ref, g_ref, b_ref,
                    x2_ref, x2b_ref, *, alpha):
    gates = gate_ref[...]
    m = jnp.zeros(x1_ref.shape, F32)
    for r in range(TOP_K):
        m = m + _unpack_bf16_pairs(jnp.concatenate([yga_ref[r], ygb_ref[r]], axis=1)) * gates[:, r:r + 1]
    e = (jax.nn.sigmoid(_dot(x1b_ref[...], wg_ref[...]) + bg_ref[...])
         * _dot(p_ref[...].astype(BF16), wp_ref[...]))
    x2 = _layer_norm(alpha * x1_ref[...] + m + e, g_ref[...], b_ref[...])
    x2_ref[...] = x2
    x2b_ref[...] = x2.astype(BF16)


def _combine(yg, gates, x1, x1b, p, w_pg, b_pg, w_pp, ln_g, ln_b, alpha, tm=512):
    n, d = x1.shape
    tm = min(tm, n)
    row = lambda i: (i, 0)
    const = lambda i: (0, 0)
    return pl.pallas_call(
        functools.partial(_combine_kernel, alpha=alpha),
        grid=(n // tm,),
        in_specs=[pl.BlockSpec((TOP_K, tm, SC_ROW_WORDS), lambda i: (0, i, 0)),
                  pl.BlockSpec((TOP_K, tm, SC_ROW_WORDS), lambda i: (0, i, 0)),
                  pl.BlockSpec((tm, LANE), row),
                  pl.BlockSpec((tm, d), row),
                  pl.BlockSpec((tm, d), row),
                  pl.BlockSpec((tm, p.shape[1]), row),
                  pl.BlockSpec(w_pg.shape, const),
                  pl.BlockSpec((1, d), const),
                  pl.BlockSpec(w_pp.shape, const),
                  pl.BlockSpec((1, d), const),
                  pl.BlockSpec((1, d), const)],
        out_specs=[pl.BlockSpec((tm, d), row), pl.BlockSpec((tm, d), row)],
        out_shape=[jax.ShapeDtypeStruct((n, d), F32), jax.ShapeDtypeStruct((n, d), BF16)],
        compiler_params=_cparams(("arbitrary",)),
        name="combine_ln",
    )(yg[0], yg[1], gates, x1, x1b, p, w_pg, b_pg, w_pp, ln_g, ln_b)


def _route(experts, ranks, counts, blk):
    n_tok = experts.shape[0]
    n_assign = n_tok * TOP_K
    sizes = counts.astype(jnp.int32)
    padded = ((sizes + blk - 1) // blk) * blk
    pends = jnp.cumsum(padded)
    pstarts = pends - padded
    onehot = experts[:, :, None] == jnp.arange(N_EXPERTS, dtype=jnp.int32)
    slot = jnp.sum(jnp.where(onehot, pstarts, 0), axis=-1) + ranks
    buf_len = ((n_assign + N_EXPERTS * (blk - 1) + blk - 1) // blk) * blk
    n_blk = buf_len // blk
    blk_e = jnp.clip(jnp.searchsorted(pends, jnp.arange(n_blk, dtype=jnp.int32) * blk, side='right'),
                     0, N_EXPERTS - 1).astype(jnp.int32)
    n_used = (pends[-1] // blk).astype(jnp.int32).reshape(1)
    return slot, buf_len, blk_e, n_used


def _regroup_w_in(w_in):
    depth, d, width = w_in.shape
    lr0 = _GV + GLA_HEADS * GLA_DV
    src = np.full((Z_WIDTH,), -1, np.int32)
    src[:lr0] = np.arange(lr0)
    src[lr0:width - GLA_GATE_RANK] = np.arange(lr0 + GLA_GATE_RANK, width)
    src[_LR:_LR + GLA_GATE_RANK] = np.arange(lr0, lr0 + GLA_GATE_RANK)
    select = (jnp.arange(width, dtype=jnp.int32)[:, None] == jnp.asarray(src)[None, :]).astype(BF16)
    out = _matmul(w_in.reshape(depth * d, width).astype(BF16), select, d, Z_WIDTH // 3, BF16, "regroup_w_in")
    return out.reshape(depth, d, Z_WIDTH)


def kernel(x, p, w_in, w_gla_gate, b_gla_gate, gla_norm_g, diff_lambda, diff_norm_g, w_out, rel_bias, ln1_g, ln1_b, w_router, b_router, w_gate_up, b_gate_up, w_down, b_down, w_ple_gate, b_ple_gate, w_ple_proj, ln2_g, ln2_b):
    batch, seq, d = x.shape
    depth = w_in.shape[0]
    n = batch * seq
    alpha = (2 * depth) ** 0.25
    moe_blk = 256
    diff_blk = min(512, seq)

    w_in_b = _regroup_w_in(w_in)
    wg = jnp.zeros((depth, LANE, GLA_HEADS * GLA_DK), F32).at[:, :GLA_GATE_RANK, :].set(w_gla_gate).astype(BF16)
    gla_g = jnp.tile(gla_norm_g, (1, GLA_HEADS))[:, None, :]
    w_out_b = w_out.astype(BF16)
    wr = jnp.zeros((depth, d, LANE), F32).at[:, :, :N_EXPERTS].set(w_router)
    wr_hi = wr.astype(BF16)
    wr_lo = (wr - wr_hi.astype(F32)).astype(BF16)
    br = jnp.zeros((depth, 1, LANE), F32).at[:, 0, :N_EXPERTS].set(b_router)
    w_gu = w_gate_up.reshape((depth * N_EXPERTS,) + w_gate_up.shape[2:])
    b_gu = b_gate_up.reshape(depth * N_EXPERTS, -1, LANE, 2).swapaxes(-1, -2).reshape(depth * N_EXPERTS, 1, -1)
    w_dn = w_down.reshape((depth * N_EXPERTS,) + w_down.shape[2:])
    b_dn = b_down.reshape(depth * N_EXPERTS, 1, -1)
    w_pg_b = w_ple_gate.astype(BF16)
    w_pp_b = w_ple_proj.astype(BF16)
    bias_tiles = _bias_tiles(rel_bias, diff_blk)
    ret_tables = _ret_tables(seq)

    xf = x.reshape(n, d)
    xb = xf.astype(BF16)
    pf = p.reshape(depth, n, p.shape[-1])
    for i in range(depth):
        lam_init = 0.8 - 0.6 * math.exp(-0.3 * i)
        z = _matmul(xb, w_in_b[i], min(1024, n), Z_WIDTH // 3, BF16, "in_proj")
        gla_o = _gla(z, wg[i], b_gla_gate[i][None, :], gla_g[i], batch, seq)
        dif_o = _diff(z, diff_lambda[i], diff_norm_g[i][None, :], bias_tiles, batch, seq, lam_init, diff_blk)
        ret_o = _ret(z, ret_tables, batch, seq)
        x1, x1b, x1pa, x1pb, gates, idx, cnt = _out_ln(gla_o, dif_o, ret_o, xf, w_out_b[i], ln1_g[i][None, :],
                                                       ln1_b[i][None, :], wr_hi[i], wr_lo[i], br[i], alpha)
        slot, buf_len, blk_e, n_used = _route(idx[:, :TOP_K], idx[:, TOP_K:2 * TOP_K], cnt[0, :N_EXPERTS], moe_blk)
        slot_k = slot.T.reshape(-1)
        xs = [_sc_scatter(t, slot_k, buf_len) for t in (x1pa, x1pb)]
        yb = _moe(xs, blk_e + i * N_EXPERTS, n_used, w_gu, b_gu, w_dn, b_dn, moe_blk)
        yg = [t.reshape(TOP_K, n, SC_ROW_WORDS) for t in _gather_rows(yb, slot_k)]
        xf, xb = _combine(yg, gates, x1, x1b, pf[i], w_pg_b[i], b_ple_gate[i][None, :], w_pp_b[i],
                          ln2_g[i][None, :], ln2_b[i][None, :], alpha)
    return xf.reshape(batch, seq, d)
```

```python
import functools
import math

import numpy as np
import jax
import jax.numpy as jnp
from jax import lax
from jax.experimental import pallas as pl
from jax.experimental.pallas import tpu as pltpu
from jax.experimental.pallas import tpu_sc as plsc

F32 = jnp.float32
BF16 = jnp.bfloat16

GLA_HEADS = 4
GLA_DK = 32
GLA_DV = 64
GLA_GATE_RANK = 16
GLA_TAU = 16.0
GLA_CHUNK = 64
DIFF_HEADS = 4
DIFF_D = 64
RET_HEADS = 4
RET_DK = 64
RET_DV = 64
RET_CHUNK = 128
T5_BUCKETS = 32
T5_MAX_DIST = 128
N_EXPERTS = 32
TOP_K = 4
SWIGLU_LIMIT = 7.0
SWIGLU_ALPHA = 1.702
LN_EPS = 1e-5
HEAD_NORM_EPS = 1e-5

LANE = 128
VMEM_LIMIT = 56 * 1024 * 1024

_GQ, _GK, _GV, _GO = 0, 128, 256, 512
_DQ, _DK, _DV = 768, 1280, 1792
_RQ, _RK, _RV, _RG = 2304, 2560, 2816, 3072
_LR = 3328
Z_WIDTH = 3456

GLA_FAST_MAX_DECAY = 60.0


def _cparams(sem):
    return pltpu.CompilerParams(dimension_semantics=sem, vmem_limit_bytes=VMEM_LIMIT)


def _split_hi_lo(a):
    hi = a.astype(BF16)
    lo = (a - hi.astype(F32)).astype(BF16)
    return hi, lo


def _dot(a, b):
    return jnp.dot(a, b, preferred_element_type=F32)


def _dot_nt(a, b):
    return lax.dot_general(a, b, (((1,), (1,)), ((), ())), preferred_element_type=F32)


def _dot_tn(a, b):
    return lax.dot_general(a, b, (((0,), (0,)), ((), ())), preferred_element_type=F32)


def _seg_sum(a, ind):
    hi, lo = _split_hi_lo(a)
    return _dot(hi, ind) + _dot(lo, ind)


def _pack_bf16_pairs(t):
    h = t.shape[1] // 2
    bits = lax.bitcast_convert_type(t.astype(BF16).astype(F32), jnp.uint32)
    return (bits[:, :h] >> 16) | (bits[:, h:] & jnp.uint32(0xFFFF0000))


def _unpack_bf16_pairs(packed):
    lo = lax.bitcast_convert_type(packed << 16, F32)
    hi = lax.bitcast_convert_type(packed & jnp.uint32(0xFFFF0000), F32)
    return jnp.concatenate([lo, hi], axis=1)


SC_WINDOW = 128
SC_ROW_WORDS = 256


def _sc_gather(table, indices):
    m = indices.shape[0]
    words = table.shape[1]
    assert words == SC_ROW_WORDS and m % SC_WINDOW == 0
    mesh = plsc.VectorSubcoreMesh(core_axis_name="core", subcore_axis_name="subcore")

    def body(x_hbm, i_hbm, o_hbm):
        def step(i_vmem, o_vmem):
            pltpu.sync_copy(x_hbm.at[i_vmem.at[0]], o_vmem)

        pltpu.emit_pipeline(
            step,
            grid=(m // SC_WINDOW,),
            in_specs=[pl.BlockSpec((1, SC_WINDOW), index_map=lambda i: (0, i))],
            out_specs=[pl.BlockSpec((SC_WINDOW, words), index_map=lambda i: (i, 0))],
            core_axis_name=("core", "subcore"),
            dimension_semantics=(pltpu.PARALLEL,),
        )(i_hbm, o_hbm)

    return pl.kernel(body, out_type=jax.ShapeDtypeStruct((m, words), table.dtype), mesh=mesh,
                     scratch_types=[])(table, indices.reshape(1, m))


def _sc_scatter(table, indices, out_rows):
    r, words = table.shape
    m = indices.shape[0]
    assert words == SC_ROW_WORDS and r % SC_WINDOW == 0 and m % r == 0
    mesh = plsc.VectorSubcoreMesh(core_axis_name="core", subcore_axis_name="subcore")
    steps_per_pass = r // SC_WINDOW

    def body(x_hbm, i_hbm, o_hbm):
        def step(x_vmem, i_vmem):
            pltpu.sync_copy(x_vmem, o_hbm.at[i_vmem.at[0]])

        pltpu.emit_pipeline(
            step,
            grid=(m // SC_WINDOW,),
            in_specs=[pl.BlockSpec((SC_WINDOW, words), index_map=lambda i: (i % steps_per_pass, 0)),
                      pl.BlockSpec((1, SC_WINDOW), index_map=lambda i: (0, i))],
            out_specs=[],
            core_axis_name=("core", "subcore"),
            dimension_semantics=(pltpu.PARALLEL,),
        )(x_hbm, i_hbm)

    return pl.kernel(body, out_type=jax.ShapeDtypeStruct((out_rows, words), table.dtype), mesh=mesh,
                     scratch_types=[])(table, indices.reshape(1, m))


def _gather_rows(tables, rows):
    return [_sc_gather(t, rows) for t in tables]


def _mm_kernel(x_ref, w_ref, o_ref):
    o_ref[...] = _dot(x_ref[...], w_ref[...]).astype(o_ref.dtype)


def _matmul(x, w, tm, tn, out_dtype, name):
    m, k = x.shape
    n = w.shape[1]
    return pl.pallas_call(
        _mm_kernel,
        grid=(n // tn, m // tm),
        in_specs=[pl.BlockSpec((tm, k), lambda j, i: (i, 0)),
                  pl.BlockSpec((k, tn), lambda j, i: (0, j))],
        out_specs=pl.BlockSpec((tm, tn), lambda j, i: (i, j)),
        out_shape=jax.ShapeDtypeStruct((m, n), out_dtype),
        compiler_params=_cparams(("arbitrary", "arbitrary")),
        name=name,
    )(x, w)


def _gla_kernel(q_ref, k_ref, v_ref, og_ref, lr_ref, wg_ref, bg_ref, ng_ref, o_ref,
                state_ref, la_ref, cum_ref, kf_ref, vf_ref, oacc_ref, *, n_chunks, fast_max_decay):
    c_len = GLA_CHUNK
    hk = GLA_HEADS * GLA_DK
    hv = GLA_HEADS * GLA_DV

    @pl.when(pl.program_id(1) == 0)
    def _():
        state_ref[...] = jnp.zeros_like(state_ref)

    zg = _dot(lr_ref[...], wg_ref[...]) + bg_ref[...]
    la = (jnp.minimum(zg, 0.0) - jnp.log1p(jnp.exp(-jnp.abs(zg)))) / GLA_TAU
    la_ref[...] = la
    chunk_tot = jnp.sum(la.reshape(n_chunks, c_len, hk), axis=1)
    fast = jnp.min(chunk_tot) > -fast_max_decay

    row = lax.broadcasted_iota(jnp.int32, (c_len, c_len), 0)
    col = lax.broadcasted_iota(jnp.int32, (c_len, c_len), 1)
    tril = (row >= col).astype(BF16)
    sr = lax.broadcasted_iota(jnp.int32, (hv, hk), 0)
    sc = lax.broadcasted_iota(jnp.int32, (hv, hk), 1)
    state_mask = ((sr // GLA_DV) == (sc // GLA_DK)).astype(F32)
    lane_k = lax.broadcasted_iota(jnp.int32, (1, hk), 1) // GLA_DK
    lane_v = lax.broadcasted_iota(jnp.int32, (1, hv), 1) // GLA_DV
    r4 = lax.broadcasted_iota(jnp.int32, (GLA_HEADS * c_len, c_len), 0) % c_len
    c4 = lax.broadcasted_iota(jnp.int32, (GLA_HEADS * c_len, c_len), 1)
    causal4 = r4 >= c4
    er = lax.broadcasted_iota(jnp.int32, (hk, hv), 0) // GLA_DK
    ec = lax.broadcasted_iota(jnp.int32, (hk, hv), 1) // GLA_DV
    head_expand = (er == ec).astype(BF16)
    trow = lax.broadcasted_iota(jnp.int32, (c_len, hv), 0)

    def load_chunk(rows):
        q = q_ref[rows, :].astype(F32) * (GLA_DK ** -0.5)
        k = k_ref[rows, :].astype(F32)
        v = v_ref[rows, :]
        la_hi, la_lo = _split_hi_lo(la_ref[rows, :])
        cum = _dot(tril, la_hi) + _dot(tril, la_lo)
        return q, k, v, cum

    def fast_intra(q, k, v, cum):
        mid = cum[c_len // 2:c_len // 2 + 1, :]
        qa = q * jnp.exp(cum - mid)
        kb = (k * jnp.exp(mid - cum)).astype(BF16)
        qa4 = jnp.concatenate(
            [jnp.where(lane_k == h, qa, 0.0) for h in range(GLA_HEADS)], axis=0).astype(BF16)
        s4 = jnp.where(causal4, _dot_nt(qa4, kb), 0.0)
        pv = _dot(s4.astype(BF16), v)
        acc = jnp.zeros((c_len, hv), F32)
        for h in range(GLA_HEADS):
            acc = acc + jnp.where(lane_v == h, pv[h * c_len:(h + 1) * c_len, :], 0.0)
        return acc

    def slow_intra(q, k, v, cum):
        cum_ref[...] = cum
        kf_ref[...] = k
        vf_ref[...] = v.astype(F32)

        def one_key(s, acc):
            ks = kf_ref[pl.ds(s, 1), :]
            vs = vf_ref[pl.ds(s, 1), :]
            cs = cum_ref[pl.ds(s, 1), :]
            w = q * ks * jnp.exp(jnp.minimum(cum - cs, 0.0))
            wsum = _dot(w.astype(BF16), head_expand)
            return acc + jnp.where(trow >= s, wsum, 0.0) * vs

        return lax.fori_loop(0, c_len, one_key, jnp.zeros((c_len, hv), F32))

    def finish_chunk(rows, q, k, v, cum, intra, st):
        last = cum[c_len - 1:c_len, :]
        qg = (q * jnp.exp(cum)).astype(BF16)
        oacc_ref[rows, :] = intra + _dot_nt(qg, st.astype(BF16))
        kg = (k * jnp.exp(last - cum)).astype(BF16)
        upd = _dot_tn(v, kg)
        return (st * jnp.exp(last) + upd) * state_mask

    @pl.when(fast)
    def _():
        st = state_ref[...]
        for c in range(n_chunks):
            rows = pl.ds(c * c_len, c_len)
            q, k, v, cum = load_chunk(rows)
            st = finish_chunk(rows, q, k, v, cum, fast_intra(q, k, v, cum), st)
        state_ref[...] = st

    @pl.when(jnp.logical_not(fast))
    def _():
        def chunk(c, carry):
            rows = pl.ds(pl.multiple_of(c * c_len, c_len), c_len)
            q, k, v, cum = load_chunk(rows)
            state_ref[...] = finish_chunk(rows, q, k, v, cum, slow_intra(q, k, v, cum), state_ref[...])
            return carry

        lax.fori_loop(0, n_chunks, chunk, 0)

    o = oacc_ref[...]
    vr = lax.broadcasted_iota(jnp.int32, (hv, hv), 0) // GLA_DV
    vc = lax.broadcasted_iota(jnp.int32, (hv, hv), 1) // GLA_DV
    seg = (vr == vc).astype(BF16)
    ms = _seg_sum(o * o, seg) * (1.0 / GLA_DV)
    y = o * lax.rsqrt(ms + HEAD_NORM_EPS) * ng_ref[...]
    g = og_ref[...].astype(F32)
    o_ref[...] = (y * (g * jax.nn.sigmoid(g))).astype(o_ref.dtype)


def _gla(z, w_gate, b_gate, norm_g, batch, seq, tg=512, fast_max_decay=GLA_FAST_MAX_DECAY):
    tg = min(tg, seq)
    nt = seq // tg
    hk = GLA_HEADS * GLA_DK
    hv = GLA_HEADS * GLA_DV

    def rowmap(cb):
        return lambda b, t: (b * nt + t, cb)

    const = lambda b, t: (0, 0)
    return pl.pallas_call(
        functools.partial(_gla_kernel, n_chunks=tg // GLA_CHUNK, fast_max_decay=fast_max_decay),
        grid=(batch, nt),
        in_specs=[pl.BlockSpec((tg, hk), rowmap(_GQ // hk)),
                  pl.BlockSpec((tg, hk), rowmap(_GK // hk)),
                  pl.BlockSpec((tg, hv), rowmap(_GV // hv)),
                  pl.BlockSpec((tg, hv), rowmap(_GO // hv)),
                  pl.BlockSpec((tg, LANE), rowmap(_LR // LANE)),
                  pl.BlockSpec((LANE, hk), const),
                  pl.BlockSpec((1, hk), const),
                  pl.BlockSpec((1, hv), const)],
        out_specs=pl.BlockSpec((tg, hv), lambda b, t: (b * nt + t, 0)),
        out_shape=jax.ShapeDtypeStruct((batch * seq, hv), BF16),
        scratch_shapes=[pltpu.VMEM((hv, hk), F32),
                        pltpu.VMEM((tg, hk), F32),
                        pltpu.VMEM((GLA_CHUNK, hk), F32),
                        pltpu.VMEM((GLA_CHUNK, hk), F32),
                        pltpu.VMEM((GLA_CHUNK, hv), F32),
                        pltpu.VMEM((tg, hv), F32)],
        compiler_params=_cparams(("arbitrary", "arbitrary")),
        name="gla",
    )(z, z, z, z, z, w_gate, b_gate, norm_g)


LOG2E = math.log2(math.e)


def _diff_kernel(far_ref, lam_ref, q_ref, k_ref, v_ref, bias_ref, ng_ref, o_ref,
                 m_ref, acc_ref, *, blk, lam_init):
    qi = pl.program_id(2)
    d = DIFF_D
    w = 2 * DIFF_D
    q = (q_ref[...].astype(F32) * (DIFF_D ** -0.5 * LOG2E)).astype(BF16)
    lane_q = lax.broadcasted_iota(jnp.int32, q.shape, 1)
    q2 = jnp.concatenate([jnp.where(lane_q < d, q, jnp.zeros_like(q)),
                          jnp.where(lane_q >= d, q, jnp.zeros_like(q))], axis=0)
    m_ref[...] = jnp.full_like(m_ref, -jnp.inf)
    acc_ref[...] = jnp.zeros_like(acc_ref)

    def kv_step(j, bias, shift):
        rows = pl.ds(pl.multiple_of(j * blk, blk), blk)
        vj = v_ref[rows, :]
        s_both = _dot_nt(q2, k_ref[rows, :])
        for m in range(2):
            s = s_both[m * blk:(m + 1) * blk, :]
            if bias is not None:
                s = s + bias
            m_cur = jnp.max(s, axis=-1, keepdims=True)
            if shift is not None:
                m_cur = m_cur + shift
            m_old = m_ref[m]
            m_new = jnp.maximum(m_old, m_cur)
            alpha = jnp.exp2(m_old - m_new)
            sub = m_new if shift is None else m_new - shift
            p = jnp.exp2(s - jnp.concatenate([sub] * (blk // LANE), axis=1))
            row_sum = jnp.broadcast_to(jnp.sum(p, axis=-1, keepdims=True), (blk, LANE))
            acc_ref[m] = (jnp.concatenate([alpha, alpha], axis=1) * acc_ref[m]
                          + jnp.concatenate([_dot(p.astype(BF16), vj), row_sum], axis=1))
            m_ref[m] = m_new

    far = far_ref[pl.program_id(1)]

    def far_step(j, carry):
        kv_step(j, None, far)
        return carry

    lax.fori_loop(0, jnp.maximum(qi - 1, 0), far_step, 0)

    @pl.when(qi >= 1)
    def _():
        kv_step(qi - 1, bias_ref[0, 1], None)

    kv_step(qi, bias_ref[0, 0], None)

    lf = lam_ref[...]
    lam = (jnp.exp(jnp.sum(lf[0:1] * lf[1:2], axis=-1, keepdims=True))
           - jnp.exp(jnp.sum(lf[2:3] * lf[3:4], axis=-1, keepdims=True)) + lam_init)
    a0 = acc_ref[0]
    a1 = acc_ref[1]
    o = a0[:, :w] / a0[:, w:] - lam * (a1[:, :w] / a1[:, w:])
    ms = jnp.mean(o * o, axis=-1, keepdims=True)
    y = o * lax.rsqrt(ms + HEAD_NORM_EPS) * ng_ref[...] * (1.0 - lam_init)
    o_ref[...] = y.astype(o_ref.dtype)


def _t5_bucket(rel):
    n = jnp.maximum(-rel, 0)
    max_exact = T5_BUCKETS // 2
    nf = jnp.maximum(n, 1).astype(F32)
    large = max_exact + (jnp.log(nf / max_exact) / math.log(T5_MAX_DIST / max_exact)
                         * (T5_BUCKETS - max_exact)).astype(jnp.int32)
    large = jnp.minimum(large, T5_BUCKETS - 1)
    return jnp.where(n < max_exact, n, large)


def _bias_tiles(rel_bias, blk):
    assert blk + 1 >= T5_MAX_DIST
    m = jnp.arange(2 * blk)
    col_minus_row = jnp.where(m < blk, m, m - 2 * blk)
    tiles = []
    for delta in range(2):
        rel = col_minus_row - delta * blk
        w = rel_bias[_t5_bucket(rel)].astype(F32).T
        if delta == 0:
            w = jnp.where(rel <= 0, w, -jnp.inf)
        flat = jnp.tile(w, (1, blk))[:, :blk * (2 * blk - 1)]
        tiles.append(flat.reshape(-1, blk, 2 * blk - 1)[:, :, :blk])
    far = rel_bias[_t5_bucket(jnp.full((1,), -(blk + 1)))[0]].astype(F32)
    return jnp.stack(tiles, axis=1) * LOG2E, far * LOG2E


def _diff(z, diff_lambda, norm_g, bias, batch, seq, lam_init, blk):
    blk = min(blk, seq)
    nq = seq // blk
    w = 2 * DIFF_D
    bias_tiles, bias_far = bias
    return pl.pallas_call(
        functools.partial(_diff_kernel, blk=blk, lam_init=lam_init),
        grid=(batch, DIFF_HEADS, nq),
        in_specs=[pl.BlockSpec(memory_space=pltpu.SMEM),
                  pl.BlockSpec((4, DIFF_D), lambda b, h, i: (0, 0)),
                  pl.BlockSpec((blk, w), lambda b, h, i: (b * nq + i, _DQ // w + h)),
                  pl.BlockSpec((seq, w), lambda b, h, i: (b, _DK // w + h)),
                  pl.BlockSpec((seq, w), lambda b, h, i: (b, _DV // w + h)),
                  pl.BlockSpec((1, 2, blk, blk), lambda b, h, i: (h, 0, 0, 0)),
                  pl.BlockSpec((1, w), lambda b, h, i: (0, 0))],
        out_specs=pl.BlockSpec((blk, w), lambda b, h, i: (b * nq + i, h)),
        out_shape=jax.ShapeDtypeStruct((batch * seq, DIFF_HEADS * w), BF16),
        scratch_shapes=[pltpu.VMEM((2, blk, LANE), F32),
                        pltpu.VMEM((2, blk, w + LANE), F32)],
        compiler_params=_cparams(("arbitrary", "arbitrary", "arbitrary")),
        name="diff_attn",
    )(bias_far, diff_lambda, z, z, z, bias_tiles, norm_g)


def _ret_kernel(q_ref, k_ref, v_ref, g_ref, sin_ref, cos_ref, inner_ref, cross_ref, sdec_ref, cdec_ref,
                o_ref, state_ref):
    c_len = RET_CHUNK
    hk = RET_HEADS * RET_DK
    hv = RET_HEADS * RET_DV

    @pl.when(pl.program_id(1) == 0)
    def _():
        state_ref[...] = jnp.zeros_like(state_ref)

    sin = sin_ref[...]
    cos = cos_ref[...]
    even = (lax.broadcasted_iota(jnp.int32, (1, hk), 1) % 2) == 0

    def rotary(t):
        nxt = pltpu.roll(t, hk - 1, 1)
        prv = pltpu.roll(t, 1, 1)
        return t * cos + jnp.where(even, -nxt, prv) * sin

    qf = rotary(q_ref[...].astype(F32))
    kf = rotary(k_ref[...].astype(F32)) * (RET_DK ** -0.5)
    v = v_ref[...]

    lane_k = lax.broadcasted_iota(jnp.int32, (1, hk), 1) // RET_DK
    lane_v = lax.broadcasted_iota(jnp.int32, (1, hv), 1) // RET_DV
    q4 = jnp.concatenate([jnp.where(lane_k == h, qf, 0.0) for h in range(RET_HEADS)], axis=0).astype(BF16)
    s4 = _dot_nt(q4, kf.astype(BF16)) * inner_ref[...]
    pv = _dot(s4.astype(BF16), v)
    acc = jnp.zeros((c_len, hv), F32)
    for h in range(RET_HEADS):
        acc = acc + jnp.where(lane_v == h, pv[h * c_len:(h + 1) * c_len, :], 0.0)

    st = state_ref[...]
    o = acc + _dot_nt(qf.astype(BF16), st.astype(BF16)) * cross_ref[...]
    ks = (kf * sdec_ref[...]).astype(BF16)
    sr = lax.broadcasted_iota(jnp.int32, (hv, hk), 0) // RET_DV
    sc = lax.broadcasted_iota(jnp.int32, (hv, hk), 1) // RET_DK
    state_ref[...] = jnp.where(sr == sc, st * cdec_ref[...] + _dot_tn(v, ks), 0.0)

    vr = lax.broadcasted_iota(jnp.int32, (hv, hv), 0) // RET_DV
    vc = lax.broadcasted_iota(jnp.int32, (hv, hv), 1) // RET_DV
    seg = (vr == vc).astype(BF16)
    ms = _seg_sum(o * o, seg) * (1.0 / RET_DV)
    g = g_ref[...].astype(F32)
    o_ref[...] = (o * lax.rsqrt(ms + HEAD_NORM_EPS) * (g * jax.nn.sigmoid(g))).astype(o_ref.dtype)


def _ret_tables(seq):
    c_len = RET_CHUNK
    pos = jnp.arange(seq, dtype=F32)
    angle = 1.0 / (10000.0 ** jnp.linspace(0.0, 1.0, RET_DK // 2, dtype=F32))
    angle = jnp.repeat(angle, 2)
    sin = jnp.tile(jnp.sin(pos[:, None] * angle), (1, RET_HEADS))
    cos = jnp.tile(jnp.cos(pos[:, None] * angle), (1, RET_HEADS))
    log_g = jnp.log1p(-jnp.exp2(-5.0 - jnp.arange(RET_HEADS, dtype=F32)))
    idx = jnp.arange(c_len, dtype=F32)
    rel = idx[:, None] - idx[None, :]
    inner = jnp.where(rel[None] >= 0, jnp.exp(jnp.maximum(rel, 0.0)[None] * log_g[:, None, None]), 0.0)
    inner = inner.reshape(RET_HEADS * c_len, c_len)
    cross = jnp.exp((idx + 1.0)[None] * log_g[:, None])
    cross = jnp.repeat(cross.T, RET_DV, axis=1)
    sdec = jnp.exp((c_len - 1.0 - idx)[None] * log_g[:, None])
    sdec = jnp.repeat(sdec.T, RET_DK, axis=1)
    cdec = jnp.repeat(jnp.exp(c_len * log_g), RET_DK)[None, :]
    return sin, cos, inner, cross, sdec, cdec


def _ret(z, tables, batch, seq):
    c_len = RET_CHUNK
    nt = seq // c_len
    hk = RET_HEADS * RET_DK
    hv = RET_HEADS * RET_DV
    sin, cos, inner, cross, sdec, cdec = tables

    def rowmap(cb):
        return lambda b, t: (b * nt + t, cb)

    const = lambda b, t: (0, 0)
    return pl.pallas_call(
        _ret_kernel,
        grid=(batch, nt),
        in_specs=[pl.BlockSpec((c_len, hk), rowmap(_RQ // hk)),
                  pl.BlockSpec((c_len, hk), rowmap(_RK // hk)),
                  pl.BlockSpec((c_len, hv), rowmap(_RV // hv)),
                  pl.BlockSpec((c_len, hv), rowmap(_RG // hv)),
                  pl.BlockSpec((c_len, hk), lambda b, t: (t, 0)),
                  pl.BlockSpec((c_len, hk), lambda b, t: (t, 0)),
                  pl.BlockSpec((RET_HEADS * c_len, c_len), const),
                  pl.BlockSpec((c_len, hv), const),
                  pl.BlockSpec((c_len, hk), const),
                  pl.BlockSpec((1, hk), const)],
        out_specs=pl.BlockSpec((c_len, hv), lambda b, t: (b * nt + t, 0)),
        out_shape=jax.ShapeDtypeStruct((batch * seq, hv), BF16),
        scratch_shapes=[pltpu.VMEM((hv, hk), F32)],
        compiler_params=_cparams(("arbitrary", "arbitrary")),
        name="retention",
    )(z, z, z, z, sin, cos, inner, cross, sdec, cdec)


def _layer_norm(t, g, b):
    mu = jnp.mean(t, axis=-1, keepdims=True)
    var = jnp.mean(jnp.square(t - mu), axis=-1, keepdims=True)
    return (t - mu) * lax.rsqrt(var + LN_EPS) * g + b


def _out_ln_kernel(gla_ref, dif_ref, ret_ref, x_ref, wo_ref, g_ref, b_ref, wr_hi_ref, wr_lo_ref, br_ref,
                   x1_ref, x1b_ref, x1pa_ref, x1pb_ref, gate_ref, idx_ref, cnt_ref, *, alpha):
    w0 = gla_ref.shape[1]
    w1 = w0 + dif_ref.shape[1]
    a = (_dot(gla_ref[...], wo_ref[0:w0, :]) + _dot(dif_ref[...], wo_ref[w0:w1, :])
         + _dot(ret_ref[...], wo_ref[w1:, :]))
    x1 = _layer_norm(alpha * x_ref[...] + a, g_ref[...], b_ref[...])
    x1_ref[...] = x1
    x1b_ref[...] = x1.astype(BF16)
    x1p = _pack_bf16_pairs(x1)
    x1pa_ref[...] = x1p[:, :SC_ROW_WORDS]
    x1pb_ref[...] = x1p[:, SC_ROW_WORDS:]

    hi, lo = _split_hi_lo(x1)
    logits = (_dot(hi, wr_hi_ref[...]) + _dot(lo, wr_hi_ref[...]) + _dot(hi, wr_lo_ref[...])) + br_ref[...]
    lane = lax.broadcasted_iota(jnp.int32, logits.shape, 1)
    work = jnp.where(lane < N_EXPERTS, logits, -jnp.inf)
    vals = jnp.zeros(logits.shape, F32)
    idxs = jnp.zeros(logits.shape, jnp.int32)
    chosen = jnp.zeros(logits.shape, F32)
    picks = []
    top0 = None
    for r in range(TOP_K):
        mx = jnp.max(work, axis=-1, keepdims=True)
        am = jnp.min(jnp.where(work == mx, lane, LANE), axis=-1, keepdims=True)
        if r == 0:
            top0 = mx
        vals = jnp.where(lane == r, jnp.exp(mx - top0), vals)
        idxs = jnp.where(lane == r, am, idxs)
        chosen = jnp.where(lane == am, 1.0, chosen)
        work = jnp.where(lane == am, -jnp.inf, work)
        picks.append(am)
    gate_ref[...] = vals / jnp.sum(vals, axis=-1, keepdims=True)

    @pl.when(pl.program_id(0) == 0)
    def _():
        cnt_ref[...] = jnp.zeros_like(cnt_ref)

    tm = logits.shape[0]
    earlier = (lax.broadcasted_iota(jnp.int32, (tm, tm), 0)
               > lax.broadcasted_iota(jnp.int32, (tm, tm), 1)).astype(BF16)
    before = _dot(earlier, chosen.astype(BF16)) + cnt_ref[...]
    for r in range(TOP_K):
        rank = jnp.sum(jnp.where(lane == picks[r], before, 0.0), axis=-1, keepdims=True)
        idxs = jnp.where(lane == TOP_K + r, rank.astype(jnp.int32), idxs)
    idx_ref[...] = idxs
    cnt_ref[...] = cnt_ref[...] + jnp.sum(chosen, axis=0, keepdims=True)


def _out_ln(gla_o, dif_o, ret_o, x, w_out, ln_g, ln_b, wr_hi, wr_lo, b_router, alpha, tm=512):
    n, d = x.shape
    tm = min(tm, n)
    row = lambda i: (i, 0)
    const = lambda i: (0, 0)
    return pl.pallas_call(
        functools.partial(_out_ln_kernel, alpha=alpha),
        grid=(n // tm,),
        in_specs=[pl.BlockSpec((tm, gla_o.shape[1]), row),
                  pl.BlockSpec((tm, dif_o.shape[1]), row),
                  pl.BlockSpec((tm, ret_o.shape[1]), row),
                  pl.BlockSpec((tm, d), row),
                  pl.BlockSpec(w_out.shape, const),
                  pl.BlockSpec((1, d), const),
                  pl.BlockSpec((1, d), const),
                  pl.BlockSpec((d, LANE), const),
                  pl.BlockSpec((d, LANE), const),
                  pl.BlockSpec((1, LANE), const)],
        out_specs=[pl.BlockSpec((tm, d), row),
                   pl.BlockSpec((tm, d), row),
                   pl.BlockSpec((tm, SC_ROW_WORDS), row),
                   pl.BlockSpec((tm, SC_ROW_WORDS), row),
                   pl.BlockSpec((tm, LANE), row),
                   pl.BlockSpec((tm, LANE), row),
                   pl.BlockSpec((1, LANE), const)],
        out_shape=[jax.ShapeDtypeStruct((n, d), F32),
                   jax.ShapeDtypeStruct((n, d), BF16),
                   jax.ShapeDtypeStruct((n, SC_ROW_WORDS), jnp.uint32),
                   jax.ShapeDtypeStruct((n, SC_ROW_WORDS), jnp.uint32),
                   jax.ShapeDtypeStruct((n, LANE), F32),
                   jax.ShapeDtypeStruct((n, LANE), jnp.int32),
                   jax.ShapeDtypeStruct((1, LANE), F32)],
        compiler_params=_cparams(("arbitrary",)),
        name="out_ln_router",
    )(gla_o, dif_o, ret_o, x, w_out, ln_g, ln_b, wr_hi, wr_lo, b_router)


GU_GROUP = 2 * LANE


def _deinterleave_perm():
    perm = np.zeros((GU_GROUP, GU_GROUP), np.float32)
    half = np.arange(LANE)
    perm[2 * half, half] = 1.0
    perm[2 * half + 1, LANE + half] = 1.0
    return jnp.asarray(perm, BF16)


def _moe_kernel(blk_e_ref, n_used_ref, xa_ref, xb_ref, wgu_ref, bgu_ref, wdn_ref, bdn_ref, perm_ref,
                oa_ref, ob_ref,
                wgu_b_ref, wdn_b_ref, *, rows_per_cast):
    i = pl.program_id(0)
    used = i < n_used_ref[0]
    new_expert = jnp.logical_or(i == 0, blk_e_ref[i] != blk_e_ref[jnp.maximum(i - 1, 0)])

    @pl.when(jnp.logical_and(used, new_expert))
    def _():
        d = wgu_ref.shape[0]
        for r in range(d // rows_per_cast):
            rows = slice(r * rows_per_cast, (r + 1) * rows_per_cast)
            wdn_b_ref[rows, :] = wdn_ref[rows, :].astype(BF16)
            for g in range(wgu_ref.shape[1] // GU_GROUP):
                cols = slice(g * GU_GROUP, (g + 1) * GU_GROUP)
                wgu_b_ref[rows, cols] = _dot(wgu_ref[rows, cols].astype(BF16), perm_ref[...]).astype(BF16)

    @pl.when(used)
    def _():
        x = _unpack_bf16_pairs(jnp.concatenate([xa_ref[...], xb_ref[...]], axis=1)).astype(BF16)
        gu = _dot(x, wgu_b_ref[...]) + bgu_ref[...]
        acts = []
        for grp in range(gu.shape[1] // GU_GROUP):
            g = jnp.minimum(gu[:, grp * GU_GROUP:grp * GU_GROUP + LANE], SWIGLU_LIMIT)
            u = jnp.clip(gu[:, grp * GU_GROUP + LANE:(grp + 1) * GU_GROUP], -SWIGLU_LIMIT, SWIGLU_LIMIT)
            acts.append(((u + 1.0) * (g * jax.nn.sigmoid(g * SWIGLU_ALPHA))).astype(BF16))
        act = jnp.concatenate(acts, axis=1)
        y = _pack_bf16_pairs(_dot(act, wdn_b_ref[...]) + bdn_ref[...])
        oa_ref[...] = y[:, :SC_ROW_WORDS]
        ob_ref[...] = y[:, SC_ROW_WORDS:]

    @pl.when(jnp.logical_not(used))
    def _():
        oa_ref[...] = jnp.zeros_like(oa_ref)
        ob_ref[...] = jnp.zeros_like(ob_ref)


def _moe(xs, blk_e, n_used, w_gu, b_gu, w_dn, b_dn, blk):
    rows = xs[0].shape[0]
    assert w_gu.shape[1] == 4 * SC_ROW_WORDS
    d = w_gu.shape[1]
    f2 = w_gu.shape[2]
    f = w_dn.shape[1]
    assert f == d
    grid_spec = pltpu.PrefetchScalarGridSpec(
        num_scalar_prefetch=2,
        grid=(rows // blk,),
        in_specs=[pl.BlockSpec((blk, SC_ROW_WORDS), lambda i, be, nu: (i, 0)),
                  pl.BlockSpec((blk, SC_ROW_WORDS), lambda i, be, nu: (i, 0)),
                  pl.BlockSpec((None, d, f2), lambda i, be, nu: (be[i], 0, 0)),
                  pl.BlockSpec((None, 1, f2), lambda i, be, nu: (be[i], 0, 0)),
                  pl.BlockSpec((None, f, d), lambda i, be, nu: (be[i], 0, 0)),
                  pl.BlockSpec((None, 1, d), lambda i, be, nu: (be[i], 0, 0)),
                  pl.BlockSpec((GU_GROUP, GU_GROUP), lambda i, be, nu: (0, 0))],
        out_specs=[pl.BlockSpec((blk, SC_ROW_WORDS), lambda i, be, nu: (i, 0)),
                   pl.BlockSpec((blk, SC_ROW_WORDS), lambda i, be, nu: (i, 0))],
        scratch_shapes=[pltpu.VMEM((d, f2), BF16), pltpu.VMEM((f, d), BF16)],
    )
    return pl.pallas_call(
        functools.partial(_moe_kernel, rows_per_cast=256),
        grid_spec=grid_spec,
        out_shape=[jax.ShapeDtypeStruct((rows, SC_ROW_WORDS), jnp.uint32)] * 2,
        compiler_params=_cparams(("arbitrary",)),
        name="moe_experts",
    )(blk_e, n_used, xs[0], xs[1], w_gu, b_gu, w_dn, b_dn, _deinterleave_perm())


def _combine_kernel(yga_ref, ygb_ref, gate_ref, x1_ref, x1b_ref, p_ref, wg_ref, bg_ref, wp_ref, g_ref, b_ref,
                    x2_ref, x2b_ref, *, alpha):
    gates = gate_ref[...]
    m = jnp.zeros(x1_ref.shape, F32)
    for r in range(TOP_K):
        m = m + _unpack_bf16_pairs(jnp.concatenate([yga_ref[r], ygb_ref[r]], axis=1)) * gates[:, r:r + 1]
    e = (jax.nn.sigmoid(_dot(x1b_ref[...], wg_ref[...]) + bg_ref[...])
         * _dot(p_ref[...].astype(BF16), wp_ref[...]))
    x2 = _layer_norm(alpha * x1_ref[...] + m + e, g_ref[...], b_ref[...])
    x2_ref[...] = x2
    x2b_ref[...] = x2.astype(BF16)


def _combine(yg, gates, x1, x1b, p, w_pg, b_pg, w_pp, ln_g, ln_b, alpha, tm=512):
    n, d = x1.shape
    tm = min(tm, n)
    row = lambda i: (i, 0)
    const = lambda i: (0, 0)
    return pl.pallas_call(
        functools.partial(_combine_kernel, alpha=alpha),
        grid=(n // tm,),
        in_specs=[pl.BlockSpec((TOP_K, tm, SC_ROW_WORDS), lambda i: (0, i, 0)),
                  pl.BlockSpec((TOP_K, tm, SC_ROW_WORDS), lambda i: (0, i, 0)),
                  pl.BlockSpec((tm, LANE), row),
                  pl.BlockSpec((tm, d), row),
                  pl.BlockSpec((tm, d), row),
                  pl.BlockSpec((tm, p.shape[1]), row),
                  pl.BlockSpec(w_pg.shape, const),
                  pl.BlockSpec((1, d), const),
                  pl.BlockSpec(w_pp.shape, const),
                  pl.BlockSpec((1, d), const),
                  pl.BlockSpec((1, d), const)],
        out_specs=[pl.BlockSpec((tm, d), row), pl.BlockSpec((tm, d), row)],
        out_shape=[jax.ShapeDtypeStruct((n, d), F32), jax.ShapeDtypeStruct((n, d), BF16)],
        compiler_params=_cparams(("arbitrary",)),
        name="combine_ln",
    )(yg[0], yg[1], gates, x1, x1b, p, w_pg, b_pg, w_pp, ln_g, ln_b)


def _route(experts, ranks, counts, blk):
    n_tok = experts.shape[0]
    n_assign = n_tok * TOP_K
    sizes = counts.astype(jnp.int32)
    padded = ((sizes + blk - 1) // blk) * blk
    pends = jnp.cumsum(padded)
    pstarts = pends - padded
    onehot = experts[:, :, None] == jnp.arange(N_EXPERTS, dtype=jnp.int32)
    slot = jnp.sum(jnp.where(onehot, pstarts, 0), axis=-1) + ranks
    buf_len = ((n_assign + N_EXPERTS * (blk - 1) + blk - 1) // blk) * blk
    n_blk = buf_len // blk
    first_row = jnp.arange(n_blk, dtype=jnp.int32) * blk
    blk_e = jnp.minimum(jnp.sum(pends[None, :] <= first_row[:, None], axis=1), N_EXPERTS - 1).astype(jnp.int32)
    n_used = (pends[-1] // blk).astype(jnp.int32).reshape(1)
    return slot, buf_len, blk_e, n_used


def _regroup_w_in(w_in):
    depth, d, width = w_in.shape
    lr0 = _GV + GLA_HEADS * GLA_DV
    src = np.full((Z_WIDTH,), -1, np.int32)
    src[:lr0] = np.arange(lr0)
    src[lr0:width - GLA_GATE_RANK] = np.arange(lr0 + GLA_GATE_RANK, width)
    src[_LR:_LR + GLA_GATE_RANK] = np.arange(lr0, lr0 + GLA_GATE_RANK)
    select = (jnp.arange(width, dtype=jnp.int32)[:, None] == jnp.asarray(src)[None, :]).astype(BF16)
    out = _matmul(w_in.reshape(depth * d, width).astype(BF16), select, d, Z_WIDTH // 3, BF16, "regroup_w_in")
    return out.reshape(depth, d, Z_WIDTH)


def kernel(x, p, w_in, w_gla_gate, b_gla_gate, gla_norm_g, diff_lambda, diff_norm_g, w_out, rel_bias, ln1_g, ln1_b, w_router, b_router, w_gate_up, b_gate_up, w_down, b_down, w_ple_gate, b_ple_gate, w_ple_proj, ln2_g, ln2_b):
    batch, seq, d = x.shape
    depth = w_in.shape[0]
    n = batch * seq
    alpha = (2 * depth) ** 0.25
    moe_blk = 256
    diff_blk = min(512, seq)

    w_in_b = _regroup_w_in(w_in)
    wg = jnp.zeros((depth, LANE, GLA_HEADS * GLA_DK), F32).at[:, :GLA_GATE_RANK, :].set(w_gla_gate).astype(BF16)
    gla_g = jnp.tile(gla_norm_g, (1, GLA_HEADS))[:, None, :]
    w_out_b = w_out.astype(BF16)
    wr = jnp.zeros((depth, d, LANE), F32).at[:, :, :N_EXPERTS].set(w_router)
    wr_hi = wr.astype(BF16)
    wr_lo = (wr - wr_hi.astype(F32)).astype(BF16)
    br = jnp.zeros((depth, 1, LANE), F32).at[:, 0, :N_EXPERTS].set(b_router)
    w_gu = w_gate_up.reshape((depth * N_EXPERTS,) + w_gate_up.shape[2:])
    b_gu = b_gate_up.reshape(depth * N_EXPERTS, -1, LANE, 2).swapaxes(-1, -2).reshape(depth * N_EXPERTS, 1, -1)
    w_dn = w_down.reshape((depth * N_EXPERTS,) + w_down.shape[2:])
    b_dn = b_down.reshape(depth * N_EXPERTS, 1, -1)
    w_pg_b = w_ple_gate.astype(BF16)
    w_pp_b = w_ple_proj.astype(BF16)
    bias_tiles = _bias_tiles(rel_bias, diff_blk)
    ret_tables = _ret_tables(seq)

    xf = x.reshape(n, d)
    xb = xf.astype(BF16)
    pf = p.reshape(depth, n, p.shape[-1])
    for i in range(depth):
        lam_init = 0.8 - 0.6 * math.exp(-0.3 * i)
        z = _matmul(xb, w_in_b[i], min(1024, n), Z_WIDTH // 3, BF16, "in_proj")
        gla_o = _gla(z, wg[i], b_gla_gate[i][None, :], gla_g[i], batch, seq)
        dif_o = _diff(z, diff_lambda[i], diff_norm_g[i][None, :], bias_tiles, batch, seq, lam_init, diff_blk)
        ret_o = _ret(z, ret_tables, batch, seq)
        x1, x1b, x1pa, x1pb, gates, idx, cnt = _out_ln(gla_o, dif_o, ret_o, xf, w_out_b[i], ln1_g[i][None, :],
                                                       ln1_b[i][None, :], wr_hi[i], wr_lo[i], br[i], alpha)
        slot, buf_len, blk_e, n_used = _route(idx[:, :TOP_K], idx[:, TOP_K:2 * TOP_K], cnt[0, :N_EXPERTS], moe_blk)
        slot_k = slot.T.reshape(-1)
        xs = [_sc_scatter(t, slot_k, buf_len) for t in (x1pa, x1pb)]
        yb = _moe(xs, blk_e + i * N_EXPERTS, n_used, w_gu, b_gu, w_dn, b_dn, moe_blk)
        yg = [t.reshape(TOP_K, n, SC_ROW_WORDS) for t in _gather_rows(yb, slot_k)]
        xf, xb = _combine(yg, gates, x1, x1b, pf[i], w_pg_b[i], b_ple_gate[i][None, :], w_pp_b[i],
                          ln2_g[i][None, :], ln2_b[i][None, :], alpha)
    return xf.reshape(batch, seq, d)
```

```python
import functools
import math

import numpy as np
import jax
import jax.numpy as jnp
from jax import lax
from jax.experimental import pallas as pl
from jax.experimental.pallas import tpu as pltpu
from jax.experimental.pallas import tpu_sc as plsc

F32 = jnp.float32
BF16 = jnp.bfloat16

GLA_HEADS = 4
GLA_DK = 32
GLA_DV = 64
GLA_GATE_RANK = 16
GLA_TAU = 16.0
GLA_CHUNK = 64
DIFF_HEADS = 4
DIFF_D = 64
RET_HEADS = 4
RET_DK = 64
RET_DV = 64
RET_CHUNK = 128
T5_BUCKETS = 32
T5_MAX_DIST = 128
N_EXPERTS = 32
TOP_K = 4
SWIGLU_LIMIT = 7.0
SWIGLU_ALPHA = 1.702
LN_EPS = 1e-5
HEAD_NORM_EPS = 1e-5

LANE = 128
VMEM_LIMIT = 56 * 1024 * 1024

_GQ, _GK, _GV, _GO = 0, 128, 256, 512
_DQ, _DK, _DV = 768, 1280, 1792
_RQ, _RK, _RV, _RG = 2304, 2560, 2816, 3072
_LR = 3328
Z_WIDTH = 3456

GLA_FAST_MAX_DECAY = 60.0


def _cparams(sem):
    return pltpu.CompilerParams(dimension_semantics=sem, vmem_limit_bytes=VMEM_LIMIT)


def _split_hi_lo(a):
    hi = a.astype(BF16)
    lo = (a - hi.astype(F32)).astype(BF16)
    return hi, lo


def _dot(a, b):
    return jnp.dot(a, b, preferred_element_type=F32)


def _dot_nt(a, b):
    return lax.dot_general(a, b, (((1,), (1,)), ((), ())), preferred_element_type=F32)


def _dot_tn(a, b):
    return lax.dot_general(a, b, (((0,), (0,)), ((), ())), preferred_element_type=F32)


def _seg_sum(a, ind):
    hi, lo = _split_hi_lo(a)
    return _dot(hi, ind) + _dot(lo, ind)


def _pack_bf16_pairs(t):
    h = t.shape[1] // 2
    bits = lax.bitcast_convert_type(t.astype(BF16).astype(F32), jnp.uint32)
    return (bits[:, :h] >> 16) | (bits[:, h:] & jnp.uint32(0xFFFF0000))


def _unpack_bf16_pairs(packed):
    lo = lax.bitcast_convert_type(packed << 16, F32)
    hi = lax.bitcast_convert_type(packed & jnp.uint32(0xFFFF0000), F32)
    return jnp.concatenate([lo, hi], axis=1)


SC_WINDOW = 128
SC_ROW_WORDS = 256


def _sc_gather(table, indices):
    m = indices.shape[0]
    words = table.shape[1]
    assert words == SC_ROW_WORDS and m % SC_WINDOW == 0
    mesh = plsc.VectorSubcoreMesh(core_axis_name="core", subcore_axis_name="subcore")

    def body(x_hbm, i_hbm, o_hbm):
        def step(i_vmem, o_vmem):
            pltpu.sync_copy(x_hbm.at[i_vmem.at[0]], o_vmem)

        pltpu.emit_pipeline(
            step,
            grid=(m // SC_WINDOW,),
            in_specs=[pl.BlockSpec((1, SC_WINDOW), index_map=lambda i: (0, i))],
            out_specs=[pl.BlockSpec((SC_WINDOW, words), index_map=lambda i: (i, 0))],
            core_axis_name=("core", "subcore"),
            dimension_semantics=(pltpu.PARALLEL,),
        )(i_hbm, o_hbm)

    return pl.kernel(body, out_type=jax.ShapeDtypeStruct((m, words), table.dtype), mesh=mesh,
                     scratch_types=[])(table, indices.reshape(1, m))


def _sc_scatter(table, indices, out_rows):
    r, words = table.shape
    m = indices.shape[0]
    assert words == SC_ROW_WORDS and r % SC_WINDOW == 0 and m % r == 0
    mesh = plsc.VectorSubcoreMesh(core_axis_name="core", subcore_axis_name="subcore")
    steps_per_pass = r // SC_WINDOW

    def body(x_hbm, i_hbm, o_hbm):
        def step(x_vmem, i_vmem):
            pltpu.sync_copy(x_vmem, o_hbm.at[i_vmem.at[0]])

        pltpu.emit_pipeline(
            step,
            grid=(m // SC_WINDOW,),
            in_specs=[pl.BlockSpec((SC_WINDOW, words), index_map=lambda i: (i % steps_per_pass, 0)),
                      pl.BlockSpec((1, SC_WINDOW), index_map=lambda i: (0, i))],
            out_specs=[],
            core_axis_name=("core", "subcore"),
            dimension_semantics=(pltpu.PARALLEL,),
        )(x_hbm, i_hbm)

    return pl.kernel(body, out_type=jax.ShapeDtypeStruct((out_rows, words), table.dtype), mesh=mesh,
                     scratch_types=[])(table, indices.reshape(1, m))


def _gather_rows(tables, rows):
    return [_sc_gather(t, rows) for t in tables]


def _mm_kernel(x_ref, w_ref, o_ref):
    o_ref[...] = _dot(x_ref[...], w_ref[...]).astype(o_ref.dtype)


def _matmul(x, w, tm, tn, out_dtype, name):
    m, k = x.shape
    n = w.shape[1]
    return pl.pallas_call(
        _mm_kernel,
        grid=(n // tn, m // tm),
        in_specs=[pl.BlockSpec((tm, k), lambda j, i: (i, 0)),
                  pl.BlockSpec((k, tn), lambda j, i: (0, j))],
        out_specs=pl.BlockSpec((tm, tn), lambda j, i: (i, j)),
        out_shape=jax.ShapeDtypeStruct((m, n), out_dtype),
        compiler_params=_cparams(("arbitrary", "arbitrary")),
        name=name,
    )(x, w)


def _gla_kernel(q_ref, k_ref, v_ref, og_ref, lr_ref, wg_ref, bg_ref, ng_ref, o_ref,
                state_ref, la_ref, cum_ref, kf_ref, vf_ref, oacc_ref, *, n_chunks, fast_max_decay):
    c_len = GLA_CHUNK
    hk = GLA_HEADS * GLA_DK
    hv = GLA_HEADS * GLA_DV

    @pl.when(pl.program_id(1) == 0)
    def _():
        state_ref[...] = jnp.zeros_like(state_ref)

    zg = _dot(lr_ref[...], wg_ref[...]) + bg_ref[...]
    la = (jnp.minimum(zg, 0.0) - jnp.log1p(jnp.exp(-jnp.abs(zg)))) / GLA_TAU
    la_ref[...] = la
    chunk_tot = jnp.sum(la.reshape(n_chunks, c_len, hk), axis=1)
    fast = jnp.min(chunk_tot) > -fast_max_decay

    row = lax.broadcasted_iota(jnp.int32, (c_len, c_len), 0)
    col = lax.broadcasted_iota(jnp.int32, (c_len, c_len), 1)
    tril = (row >= col).astype(BF16)
    sr = lax.broadcasted_iota(jnp.int32, (hv, hk), 0)
    sc = lax.broadcasted_iota(jnp.int32, (hv, hk), 1)
    state_mask = ((sr // GLA_DV) == (sc // GLA_DK)).astype(F32)
    lane_k = lax.broadcasted_iota(jnp.int32, (1, hk), 1) // GLA_DK
    lane_v = lax.broadcasted_iota(jnp.int32, (1, hv), 1) // GLA_DV
    r4 = lax.broadcasted_iota(jnp.int32, (GLA_HEADS * c_len, c_len), 0) % c_len
    c4 = lax.broadcasted_iota(jnp.int32, (GLA_HEADS * c_len, c_len), 1)
    causal4 = r4 >= c4
    er = lax.broadcasted_iota(jnp.int32, (hk, hv), 0) // GLA_DK
    ec = lax.broadcasted_iota(jnp.int32, (hk, hv), 1) // GLA_DV
    head_expand = (er == ec).astype(BF16)
    trow = lax.broadcasted_iota(jnp.int32, (c_len, hv), 0)

    def load_chunk(rows):
        q = q_ref[rows, :].astype(F32) * (GLA_DK ** -0.5)
        k = k_ref[rows, :].astype(F32)
        v = v_ref[rows, :]
        la_hi, la_lo = _split_hi_lo(la_ref[rows, :])
        cum = _dot(tril, la_hi) + _dot(tril, la_lo)
        return q, k, v, cum

    def fast_intra(q, k, v, cum):
        mid = cum[c_len // 2:c_len // 2 + 1, :]
        qa = q * jnp.exp(cum - mid)
        kb = (k * jnp.exp(mid - cum)).astype(BF16)
        qa4 = jnp.concatenate(
            [jnp.where(lane_k == h, qa, 0.0) for h in range(GLA_HEADS)], axis=0).astype(BF16)
        s4 = jnp.where(causal4, _dot_nt(qa4, kb), 0.0)
        pv = _dot(s4.astype(BF16), v)
        acc = jnp.zeros((c_len, hv), F32)
        for h in range(GLA_HEADS):
            acc = acc + jnp.where(lane_v == h, pv[h * c_len:(h + 1) * c_len, :], 0.0)
        return acc

    def slow_intra(q, k, v, cum):
        cum_ref[...] = cum
        kf_ref[...] = k
        vf_ref[...] = v.astype(F32)

        def one_key(s, acc):
            ks = kf_ref[pl.ds(s, 1), :]
            vs = vf_ref[pl.ds(s, 1), :]
            cs = cum_ref[pl.ds(s, 1), :]
            w = q * ks * jnp.exp(jnp.minimum(cum - cs, 0.0))
            wsum = _dot(w.astype(BF16), head_expand)
            return acc + jnp.where(trow >= s, wsum, 0.0) * vs

        return lax.fori_loop(0, c_len, one_key, jnp.zeros((c_len, hv), F32))

    def finish_chunk(rows, q, k, v, cum, intra, st):
        last = cum[c_len - 1:c_len, :]
        qg = (q * jnp.exp(cum)).astype(BF16)
        oacc_ref[rows, :] = intra + _dot_nt(qg, st.astype(BF16))
        kg = (k * jnp.exp(last - cum)).astype(BF16)
        upd = _dot_tn(v, kg)
        return (st * jnp.exp(last) + upd) * state_mask

    @pl.when(fast)
    def _():
        st = state_ref[...]
        for c in range(n_chunks):
            rows = pl.ds(c * c_len, c_len)
            q, k, v, cum = load_chunk(rows)
            st = finish_chunk(rows, q, k, v, cum, fast_intra(q, k, v, cum), st)
        state_ref[...] = st

    @pl.when(jnp.logical_not(fast))
    def _():
        def chunk(c, carry):
            rows = pl.ds(pl.multiple_of(c * c_len, c_len), c_len)
            q, k, v, cum = load_chunk(rows)
            state_ref[...] = finish_chunk(rows, q, k, v, cum, slow_intra(q, k, v, cum), state_ref[...])
            return carry

        lax.fori_loop(0, n_chunks, chunk, 0)

    o = oacc_ref[...]
    vr = lax.broadcasted_iota(jnp.int32, (hv, hv), 0) // GLA_DV
    vc = lax.broadcasted_iota(jnp.int32, (hv, hv), 1) // GLA_DV
    seg = (vr == vc).astype(BF16)
    ms = _seg_sum(o * o, seg) * (1.0 / GLA_DV)
    y = o * lax.rsqrt(ms + HEAD_NORM_EPS) * ng_ref[...]
    g = og_ref[...].astype(F32)
    o_ref[...] = (y * (g * jax.nn.sigmoid(g))).astype(o_ref.dtype)


def _gla(z, w_gate, b_gate, norm_g, batch, seq, tg=512, fast_max_decay=GLA_FAST_MAX_DECAY):
    tg = min(tg, seq)
    nt = seq // tg
    hk = GLA_HEADS * GLA_DK
    hv = GLA_HEADS * GLA_DV

    def rowmap(cb):
        return lambda b, t: (b * nt + t, cb)

    const = lambda b, t: (0, 0)
    return pl.pallas_call(
        functools.partial(_gla_kernel, n_chunks=tg // GLA_CHUNK, fast_max_decay=fast_max_decay),
        grid=(batch, nt),
        in_specs=[pl.BlockSpec((tg, hk), rowmap(_GQ // hk)),
                  pl.BlockSpec((tg, hk), rowmap(_GK // hk)),
                  pl.BlockSpec((tg, hv), rowmap(_GV // hv)),
                  pl.BlockSpec((tg, hv), rowmap(_GO // hv)),
                  pl.BlockSpec((tg, LANE), rowmap(_LR // LANE)),
                  pl.BlockSpec((LANE, hk), const),
                  pl.BlockSpec((1, hk), const),
                  pl.BlockSpec((1, hv), const)],
        out_specs=pl.BlockSpec((tg, hv), lambda b, t: (b * nt + t, 0)),
        out_shape=jax.ShapeDtypeStruct((batch * seq, hv), BF16),
        scratch_shapes=[pltpu.VMEM((hv, hk), F32),
                        pltpu.VMEM((tg, hk), F32),
                        pltpu.VMEM((GLA_CHUNK, hk), F32),
                        pltpu.VMEM((GLA_CHUNK, hk), F32),
                        pltpu.VMEM((GLA_CHUNK, hv), F32),
                        pltpu.VMEM((tg, hv), F32)],
        compiler_params=_cparams(("arbitrary", "arbitrary")),
        name="gla",
    )(z, z, z, z, z, w_gate, b_gate, norm_g)


LOG2E = math.log2(math.e)


def _diff_kernel(far_ref, lam_ref, q_ref, k_ref, v_ref, bias_ref, ng_ref, o_ref,
                 m_ref, acc_ref, *, blk, lam_init):
    qi = pl.program_id(2)
    d = DIFF_D
    w = 2 * DIFF_D
    q = (q_ref[...].astype(F32) * (DIFF_D ** -0.5 * LOG2E)).astype(BF16)
    lane_q = lax.broadcasted_iota(jnp.int32, q.shape, 1)
    q2 = jnp.concatenate([jnp.where(lane_q < d, q, jnp.zeros_like(q)),
                          jnp.where(lane_q >= d, q, jnp.zeros_like(q))], axis=0)
    m_ref[...] = jnp.full_like(m_ref, -jnp.inf)
    acc_ref[...] = jnp.zeros_like(acc_ref)

    def kv_step(j, bias, shift):
        rows = pl.ds(pl.multiple_of(j * blk, blk), blk)
        vj = v_ref[rows, :]
        s_both = _dot_nt(q2, k_ref[rows, :])
        for m in range(2):
            s = s_both[m * blk:(m + 1) * blk, :]
            if bias is not None:
                s = s + bias
            m_cur = jnp.max(s, axis=-1, keepdims=True)
            if shift is not None:
                m_cur = m_cur + shift
            m_old = m_ref[m]
            m_new = jnp.maximum(m_old, m_cur)
            alpha = jnp.exp2(m_old - m_new)
            sub = m_new if shift is None else m_new - shift
            p = jnp.exp2(s - jnp.concatenate([sub] * (blk // LANE), axis=1))
            row_sum = jnp.broadcast_to(jnp.sum(p, axis=-1, keepdims=True), (blk, LANE))
            acc_ref[m] = (jnp.concatenate([alpha, alpha], axis=1) * acc_ref[m]
                          + jnp.concatenate([_dot(p.astype(BF16), vj), row_sum], axis=1))
            m_ref[m] = m_new

    far = far_ref[pl.program_id(1)]

    def far_step(j, carry):
        kv_step(j, None, far)
        return carry

    lax.fori_loop(0, jnp.maximum(qi - 1, 0), far_step, 0)

    @pl.when(qi >= 1)
    def _():
        kv_step(qi - 1, bias_ref[0, 1], None)

    kv_step(qi, bias_ref[0, 0], None)

    lf = lam_ref[...]
    lam = (jnp.exp(jnp.sum(lf[0:1] * lf[1:2], axis=-1, keepdims=True))
           - jnp.exp(jnp.sum(lf[2:3] * lf[3:4], axis=-1, keepdims=True)) + lam_init)
    a0 = acc_ref[0]
    a1 = acc_ref[1]
    o = a0[:, :w] / a0[:, w:] - lam * (a1[:, :w] / a1[:, w:])
    ms = jnp.mean(o * o, axis=-1, keepdims=True)
    y = o * lax.rsqrt(ms + HEAD_NORM_EPS) * ng_ref[...] * (1.0 - lam_init)
    o_ref[...] = y.astype(o_ref.dtype)


def _t5_bucket(rel):
    n = jnp.maximum(-rel, 0)
    max_exact = T5_BUCKETS // 2
    nf = jnp.maximum(n, 1).astype(F32)
    large = max_exact + (jnp.log(nf / max_exact) / math.log(T5_MAX_DIST / max_exact)
                         * (T5_BUCKETS - max_exact)).astype(jnp.int32)
    large = jnp.minimum(large, T5_BUCKETS - 1)
    return jnp.where(n < max_exact, n, large)


def _bias_tiles(rel_bias, blk):
    assert blk + 1 >= T5_MAX_DIST
    m = jnp.arange(2 * blk)
    col_minus_row = jnp.where(m < blk, m, m - 2 * blk)
    tiles = []
    for delta in range(2):
        rel = col_minus_row - delta * blk
        w = rel_bias[_t5_bucket(rel)].astype(F32).T
        if delta == 0:
            w = jnp.where(rel <= 0, w, -jnp.inf)
        flat = jnp.tile(w, (1, blk))[:, :blk * (2 * blk - 1)]
        tiles.append(flat.reshape(-1, blk, 2 * blk - 1)[:, :, :blk])
    far = rel_bias[_t5_bucket(jnp.full((1,), -(blk + 1)))[0]].astype(F32)
    return jnp.stack(tiles, axis=1) * LOG2E, far * LOG2E


def _diff(z, diff_lambda, norm_g, bias, batch, seq, lam_init, blk):
    blk = min(blk, seq)
    nq = seq // blk
    w = 2 * DIFF_D
    bias_tiles, bias_far = bias
    return pl.pallas_call(
        functools.partial(_diff_kernel, blk=blk, lam_init=lam_init),
        grid=(batch, DIFF_HEADS, nq),
        in_specs=[pl.BlockSpec(memory_space=pltpu.SMEM),
                  pl.BlockSpec((4, DIFF_D), lambda b, h, i: (0, 0)),
                  pl.BlockSpec((blk, w), lambda b, h, i: (b * nq + i, _DQ // w + h)),
                  pl.BlockSpec((seq, w), lambda b, h, i: (b, _DK // w + h)),
                  pl.BlockSpec((seq, w), lambda b, h, i: (b, _DV // w + h)),
                  pl.BlockSpec((1, 2, blk, blk), lambda b, h, i: (h, 0, 0, 0)),
                  pl.BlockSpec((1, w), lambda b, h, i: (0, 0))],
        out_specs=pl.BlockSpec((blk, w), lambda b, h, i: (b * nq + i, h)),
        out_shape=jax.ShapeDtypeStruct((batch * seq, DIFF_HEADS * w), BF16),
        scratch_shapes=[pltpu.VMEM((2, blk, LANE), F32),
                        pltpu.VMEM((2, blk, w + LANE), F32)],
        compiler_params=_cparams(("arbitrary", "arbitrary", "arbitrary")),
        name="diff_attn",
    )(bias_far, diff_lambda, z, z, z, bias_tiles, norm_g)


def _ret_kernel(q_ref, k_ref, v_ref, g_ref, sin_ref, cos_ref, inner_ref, cross_ref, sdec_ref, cdec_ref,
                o_ref, state_ref):
    c_len = RET_CHUNK
    hk = RET_HEADS * RET_DK
    hv = RET_HEADS * RET_DV

    @pl.when(pl.program_id(1) == 0)
    def _():
        state_ref[...] = jnp.zeros_like(state_ref)

    sin = sin_ref[...]
    cos = cos_ref[...]
    even = (lax.broadcasted_iota(jnp.int32, (1, hk), 1) % 2) == 0

    def rotary(t):
        nxt = pltpu.roll(t, hk - 1, 1)
        prv = pltpu.roll(t, 1, 1)
        return t * cos + jnp.where(even, -nxt, prv) * sin

    qf = rotary(q_ref[...].astype(F32))
    kf = rotary(k_ref[...].astype(F32)) * (RET_DK ** -0.5)
    v = v_ref[...]

    lane_k = lax.broadcasted_iota(jnp.int32, (1, hk), 1) // RET_DK
    lane_v = lax.broadcasted_iota(jnp.int32, (1, hv), 1) // RET_DV
    q4 = jnp.concatenate([jnp.where(lane_k == h, qf, 0.0) for h in range(RET_HEADS)], axis=0).astype(BF16)
    s4 = _dot_nt(q4, kf.astype(BF16)) * inner_ref[...]
    pv = _dot(s4.astype(BF16), v)
    acc = jnp.zeros((c_len, hv), F32)
    for h in range(RET_HEADS):
        acc = acc + jnp.where(lane_v == h, pv[h * c_len:(h + 1) * c_len, :], 0.0)

    st = state_ref[...]
    o = acc + _dot_nt(qf.astype(BF16), st.astype(BF16)) * cross_ref[...]
    ks = (kf * sdec_ref[...]).astype(BF16)
    sr = lax.broadcasted_iota(jnp.int32, (hv, hk), 0) // RET_DV
    sc = lax.broadcasted_iota(jnp.int32, (hv, hk), 1) // RET_DK
    state_ref[...] = jnp.where(sr == sc, st * cdec_ref[...] + _dot_tn(v, ks), 0.0)

    vr = lax.broadcasted_iota(jnp.int32, (hv, hv), 0) // RET_DV
    vc = lax.broadcasted_iota(jnp.int32, (hv, hv), 1) // RET_DV
    seg = (vr == vc).astype(BF16)
    ms = _seg_sum(o * o, seg) * (1.0 / RET_DV)
    g = g_ref[...].astype(F32)
    o_ref[...] = (o * lax.rsqrt(ms + HEAD_NORM_EPS) * (g * jax.nn.sigmoid(g))).astype(o_ref.dtype)


def _ret_tables(seq):
    c_len = RET_CHUNK
    pos = jnp.arange(seq, dtype=F32)
    angle = 1.0 / (10000.0 ** jnp.linspace(0.0, 1.0, RET_DK // 2, dtype=F32))
    angle = jnp.repeat(angle, 2)
    sin = jnp.tile(jnp.sin(pos[:, None] * angle), (1, RET_HEADS))
    cos = jnp.tile(jnp.cos(pos[:, None] * angle), (1, RET_HEADS))
    log_g = jnp.log1p(-jnp.exp2(-5.0 - jnp.arange(RET_HEADS, dtype=F32)))
    idx = jnp.arange(c_len, dtype=F32)
    rel = idx[:, None] - idx[None, :]
    inner = jnp.where(rel[None] >= 0, jnp.exp(jnp.maximum(rel, 0.0)[None] * log_g[:, None, None]), 0.0)
    inner = inner.reshape(RET_HEADS * c_len, c_len)
    cross = jnp.exp((idx + 1.0)[None] * log_g[:, None])
    cross = jnp.repeat(cross.T, RET_DV, axis=1)
    sdec = jnp.exp((c_len - 1.0 - idx)[None] * log_g[:, None])
    sdec = jnp.repeat(sdec.T, RET_DK, axis=1)
    cdec = jnp.repeat(jnp.exp(c_len * log_g), RET_DK)[None, :]
    return sin, cos, inner, cross, sdec, cdec


def _ret(z, tables, batch, seq):
    c_len = RET_CHUNK
    nt = seq // c_len
    hk = RET_HEADS * RET_DK
    hv = RET_HEADS * RET_DV
    sin, cos, inner, cross, sdec, cdec = tables

    def rowmap(cb):
        return lambda b, t: (b * nt + t, cb)

    const = lambda b, t: (0, 0)
    return pl.pallas_call(
        _ret_kernel,
        grid=(batch, nt),
        in_specs=[pl.BlockSpec((c_len, hk), rowmap(_RQ // hk)),
                  pl.BlockSpec((c_len, hk), rowmap(_RK // hk)),
                  pl.BlockSpec((c_len, hv), rowmap(_RV // hv)),
                  pl.BlockSpec((c_len, hv), rowmap(_RG // hv)),
                  pl.BlockSpec((c_len, hk), lambda b, t: (t, 0)),
                  pl.BlockSpec((c_len, hk), lambda b, t: (t, 0)),
                  pl.BlockSpec((RET_HEADS * c_len, c_len), const),
                  pl.BlockSpec((c_len, hv), const),
                  pl.BlockSpec((c_len, hk), const),
                  pl.BlockSpec((1, hk), const)],
        out_specs=pl.BlockSpec((c_len, hv), lambda b, t: (b * nt + t, 0)),
        out_shape=jax.ShapeDtypeStruct((batch * seq, hv), BF16),
        scratch_shapes=[pltpu.VMEM((hv, hk), F32)],
        compiler_params=_cparams(("arbitrary", "arbitrary")),
        name="retention",
    )(z, z, z, z, sin, cos, inner, cross, sdec, cdec)


def _layer_norm(t, g, b):
    mu = jnp.mean(t, axis=-1, keepdims=True)
    var = jnp.mean(jnp.square(t - mu), axis=-1, keepdims=True)
    return (t - mu) * lax.rsqrt(var + LN_EPS) * g + b


def _out_ln_kernel(gla_ref, dif_ref, ret_ref, x_ref, wo_ref, g_ref, b_ref, wr_ref, br_ref, earlier_ref,
                   x1_ref, x1b_ref, x1pa_ref, x1pb_ref, gate_ref, idx_ref, cnt_ref, *, alpha):
    w0 = gla_ref.shape[1]
    w1 = w0 + dif_ref.shape[1]
    a = (_dot(gla_ref[...], wo_ref[0:w0, :]) + _dot(dif_ref[...], wo_ref[w0:w1, :])
         + _dot(ret_ref[...], wo_ref[w1:, :]))
    x1 = _layer_norm(alpha * x_ref[...] + a, g_ref[...], b_ref[...])
    x1_ref[...] = x1
    x1b_ref[...] = x1.astype(BF16)
    x1p = _pack_bf16_pairs(x1)
    x1pa_ref[...] = x1p[:, :SC_ROW_WORDS]
    x1pb_ref[...] = x1p[:, SC_ROW_WORDS:]

    hi, lo = _split_hi_lo(x1)
    tm = x1.shape[0]
    parts = _dot(jnp.concatenate([hi, lo], axis=0), wr_ref[...])
    logits = (parts[:tm, :LANE] + parts[tm:, :LANE]) + (parts[:tm, LANE:] + parts[tm:, LANE:]) + br_ref[...]
    lane = lax.broadcasted_iota(jnp.int32, logits.shape, 1)
    work = jnp.where(lane < N_EXPERTS, logits, -jnp.inf)
    vals = jnp.zeros(logits.shape, F32)
    idxs = jnp.zeros(logits.shape, jnp.int32)
    chosen = jnp.zeros(logits.shape, F32)
    picks = []
    top0 = None
    for r in range(TOP_K):
        mx = jnp.max(work, axis=-1, keepdims=True)
        am = jnp.min(jnp.where(work == mx, lane, LANE), axis=-1, keepdims=True)
        if r == 0:
            top0 = mx
        vals = jnp.where(lane == r, jnp.exp(mx - top0), vals)
        idxs = jnp.where(lane == r, am, idxs)
        chosen = jnp.where(lane == am, 1.0, chosen)
        work = jnp.where(lane == am, -jnp.inf, work)
        picks.append(am)
    gate_ref[...] = vals / jnp.sum(vals, axis=-1, keepdims=True)

    @pl.when(pl.program_id(0) == 0)
    def _():
        cnt_ref[...] = jnp.zeros_like(cnt_ref)

    before = _dot(earlier_ref[...], chosen.astype(BF16)) + cnt_ref[...]
    for r in range(TOP_K):
        rank = jnp.sum(jnp.where(lane == picks[r], before, 0.0), axis=-1, keepdims=True)
        idxs = jnp.where(lane == TOP_K + r, rank.astype(jnp.int32), idxs)
    idx_ref[...] = idxs
    cnt_ref[...] = cnt_ref[...] + jnp.sum(chosen, axis=0, keepdims=True)


def _out_ln(gla_o, dif_o, ret_o, x, w_out, ln_g, ln_b, wr, b_router, alpha, tm=512):
    n, d = x.shape
    tm = min(tm, n)
    earlier = jnp.asarray(np.tril(np.ones((tm, tm), np.float32), -1), BF16)
    row = lambda i: (i, 0)
    const = lambda i: (0, 0)
    return pl.pallas_call(
        functools.partial(_out_ln_kernel, alpha=alpha),
        grid=(n // tm,),
        in_specs=[pl.BlockSpec((tm, gla_o.shape[1]), row),
                  pl.BlockSpec((tm, dif_o.shape[1]), row),
                  pl.BlockSpec((tm, ret_o.shape[1]), row),
                  pl.BlockSpec((tm, d), row),
                  pl.BlockSpec(w_out.shape, const),
                  pl.BlockSpec((1, d), const),
                  pl.BlockSpec((1, d), const),
                  pl.BlockSpec((d, 2 * LANE), const),
                  pl.BlockSpec((1, LANE), const),
                  pl.BlockSpec((tm, tm), const)],
        out_specs=[pl.BlockSpec((tm, d), row),
                   pl.BlockSpec((tm, d), row),
                   pl.BlockSpec((tm, SC_ROW_WORDS), row),
                   pl.BlockSpec((tm, SC_ROW_WORDS), row),
                   pl.BlockSpec((tm, LANE), row),
                   pl.BlockSpec((tm, LANE), row),
                   pl.BlockSpec((1, LANE), const)],
        out_shape=[jax.ShapeDtypeStruct((n, d), F32),
                   jax.ShapeDtypeStruct((n, d), BF16),
                   jax.ShapeDtypeStruct((n, SC_ROW_WORDS), jnp.uint32),
                   jax.ShapeDtypeStruct((n, SC_ROW_WORDS), jnp.uint32),
                   jax.ShapeDtypeStruct((n, LANE), F32),
                   jax.ShapeDtypeStruct((n, LANE), jnp.int32),
                   jax.ShapeDtypeStruct((1, LANE), F32)],
        compiler_params=_cparams(("arbitrary",)),
        name="out_ln_router",
    )(gla_o, dif_o, ret_o, x, w_out, ln_g, ln_b, wr, b_router, earlier)


GU_GROUP = 2 * LANE


def _deinterleave_perm():
    perm = np.zeros((GU_GROUP, GU_GROUP), np.float32)
    half = np.arange(LANE)
    perm[2 * half, half] = 1.0
    perm[2 * half + 1, LANE + half] = 1.0
    return jnp.asarray(perm, BF16)


def _moe_kernel(blk_e_ref, n_used_ref, n_valid_ref, xa_ref, xb_ref, wgu_ref, bgu_ref, wdn_ref, bdn_ref, perm_ref,
                oa_ref, ob_ref,
                wgu_b_ref, wdn_b_ref, *, rows_per_cast):
    i = pl.program_id(0)
    used = i < n_used_ref[0]
    new_expert = jnp.logical_or(i == 0, blk_e_ref[i] != blk_e_ref[jnp.maximum(i - 1, 0)])

    @pl.when(jnp.logical_and(used, new_expert))
    def _():
        d = wgu_ref.shape[0]
        for r in range(d // rows_per_cast):
            rows = slice(r * rows_per_cast, (r + 1) * rows_per_cast)
            wdn_b_ref[rows, :] = wdn_ref[rows, :].astype(BF16)
            for g in range(wgu_ref.shape[1] // GU_GROUP):
                cols = slice(g * GU_GROUP, (g + 1) * GU_GROUP)
                wgu_b_ref[rows, cols] = _dot(wgu_ref[rows, cols].astype(BF16), perm_ref[...]).astype(BF16)

    @pl.when(used)
    def _():
        x = _unpack_bf16_pairs(jnp.concatenate([xa_ref[...], xb_ref[...]], axis=1))
        row = lax.broadcasted_iota(jnp.int32, x.shape, 0)
        x = jnp.where(row < n_valid_ref[i], x, 0.0).astype(BF16)
        gu = _dot(x, wgu_b_ref[...]) + bgu_ref[...]
        acts = []
        for grp in range(gu.shape[1] // GU_GROUP):
            g = jnp.minimum(gu[:, grp * GU_GROUP:grp * GU_GROUP + LANE], SWIGLU_LIMIT)
            u = jnp.clip(gu[:, grp * GU_GROUP + LANE:(grp + 1) * GU_GROUP], -SWIGLU_LIMIT, SWIGLU_LIMIT)
            acts.append(((u + 1.0) * (g * jax.nn.sigmoid(g * SWIGLU_ALPHA))).astype(BF16))
        act = jnp.concatenate(acts, axis=1)
        y = _pack_bf16_pairs(_dot(act, wdn_b_ref[...]) + bdn_ref[...])
        oa_ref[...] = y[:, :SC_ROW_WORDS]
        ob_ref[...] = y[:, SC_ROW_WORDS:]

    @pl.when(jnp.logical_not(used))
    def _():
        oa_ref[...] = jnp.zeros_like(oa_ref)
        ob_ref[...] = jnp.zeros_like(ob_ref)


def _moe(xs, blk_e, n_used, n_valid, w_gu, b_gu, w_dn, b_dn, blk):
    rows = xs[0].shape[0]
    assert w_gu.shape[1] == 4 * SC_ROW_WORDS
    d = w_gu.shape[1]
    f2 = w_gu.shape[2]
    f = w_dn.shape[1]
    assert f == d
    grid_spec = pltpu.PrefetchScalarGridSpec(
        num_scalar_prefetch=3,
        grid=(rows // blk,),
        in_specs=[pl.BlockSpec((blk, SC_ROW_WORDS), lambda i, be, nu, nv: (i, 0)),
                  pl.BlockSpec((blk, SC_ROW_WORDS), lambda i, be, nu, nv: (i, 0)),
                  pl.BlockSpec((None, d, f2), lambda i, be, nu, nv: (be[i], 0, 0)),
                  pl.BlockSpec((None, 1, f2), lambda i, be, nu, nv: (be[i], 0, 0)),
                  pl.BlockSpec((None, f, d), lambda i, be, nu, nv: (be[i], 0, 0)),
                  pl.BlockSpec((None, 1, d), lambda i, be, nu, nv: (be[i], 0, 0)),
                  pl.BlockSpec((GU_GROUP, GU_GROUP), lambda i, be, nu, nv: (0, 0))],
        out_specs=[pl.BlockSpec((blk, SC_ROW_WORDS), lambda i, be, nu, nv: (i, 0)),
                   pl.BlockSpec((blk, SC_ROW_WORDS), lambda i, be, nu, nv: (i, 0))],
        scratch_shapes=[pltpu.VMEM((d, f2), BF16), pltpu.VMEM((f, d), BF16)],
    )
    return pl.pallas_call(
        functools.partial(_moe_kernel, rows_per_cast=256),
        grid_spec=grid_spec,
        out_shape=[jax.ShapeDtypeStruct((rows, SC_ROW_WORDS), jnp.uint32)] * 2,
        compiler_params=_cparams(("arbitrary",)),
        name="moe_experts",
    )(blk_e, n_used, n_valid, xs[0], xs[1], w_gu, b_gu, w_dn, b_dn, _deinterleave_perm())


def _combine_kernel(yga_ref, ygb_ref, gate_ref, x1_ref, x1b_ref, p_ref, wg_ref, bg_ref, wp_ref, g_ref, b_ref,
                    x2_ref, x2b_ref, *, alpha):
    gates = gate_ref[...]
    m = jnp.zeros(x1_ref.shape, F32)
    for r in range(TOP_K):
        m = m + _unpack_bf16_pairs(jnp.concatenate([yga_ref[r], ygb_ref[r]], axis=1)) * gates[:, r:r + 1]
    e = (jax.nn.sigmoid(_dot(x1b_ref[...], wg_ref[...]) + bg_ref[...])
         * _dot(p_ref[...].astype(BF16), wp_ref[...]))
    x2 = _layer_norm(alpha * x1_ref[...] + m + e, g_ref[...], b_ref[...])
    x2_ref[...] = x2
    x2b_ref[...] = x2.astype(BF16)


def _combine(yg, gates, x1, x1b, p, w_pg, b_pg, w_pp, ln_g, ln_b, alpha, tm=512):
    n, d = x1.shape
    tm = min(tm, n)
    row = lambda i: (i, 0)
    const = lambda i: (0, 0)
    return pl.pallas_call(
        functools.partial(_combine_kernel, alpha=alpha),
        grid=(n // tm,),
        in_specs=[pl.BlockSpec((TOP_K, tm, SC_ROW_WORDS), lambda i: (0, i, 0)),
                  pl.BlockSpec((TOP_K, tm, SC_ROW_WORDS), lambda i: (0, i, 0)),
                  pl.BlockSpec((tm, LANE), row),
                  pl.BlockSpec((tm, d), row),
                  pl.BlockSpec((tm, d), row),
                  pl.BlockSpec((tm, p.shape[1]), row),
                  pl.BlockSpec(w_pg.shape, const),
                  pl.BlockSpec((1, d), const),
                  pl.BlockSpec(w_pp.shape, const),
                  pl.BlockSpec((1, d), const),
                  pl.BlockSpec((1, d), const)],
        out_specs=[pl.BlockSpec((tm, d), row), pl.BlockSpec((tm, d), row)],
        out_shape=[jax.ShapeDtypeStruct((n, d), F32), jax.ShapeDtypeStruct((n, d), BF16)],
        compiler_params=_cparams(("arbitrary",)),
        name="combine_ln",
    )(yg[0], yg[1], gates, x1, x1b, p, w_pg, b_pg, w_pp, ln_g, ln_b)


def _route(experts, ranks, counts, blk):
    n_tok = experts.shape[0]
    n_assign = n_tok * TOP_K
    sizes = counts.astype(jnp.int32)
    padded = ((sizes + blk - 1) // blk) * blk
    pends = jnp.cumsum(padded)
    pstarts = pends - padded
    onehot = experts[:, :, None] == jnp.arange(N_EXPERTS, dtype=jnp.int32)
    slot = jnp.sum(jnp.where(onehot, pstarts, 0), axis=-1) + ranks
    buf_len = ((n_assign + N_EXPERTS * (blk - 1) + blk - 1) // blk) * blk
    n_blk = buf_len // blk
    first_row = jnp.arange(n_blk, dtype=jnp.int32) * blk
    blk_e = jnp.minimum(jnp.sum(pends[None, :] <= first_row[:, None], axis=1), N_EXPERTS - 1).astype(jnp.int32)
    n_used = (pends[-1] // blk).astype(jnp.int32).reshape(1)
    mine = blk_e[:, None] == jnp.arange(N_EXPERTS, dtype=jnp.int32)[None, :]
    group_rows_left = jnp.sum(jnp.where(mine, pstarts + sizes, 0), axis=1) - first_row
    n_valid = jnp.clip(group_rows_left, 0, blk).astype(jnp.int32)
    return slot, buf_len, blk_e, n_used, n_valid


def _regroup_w_in(w_in):
    depth, d, width = w_in.shape
    lr0 = _GV + GLA_HEADS * GLA_DV
    src = np.full((Z_WIDTH,), -1, np.int32)
    src[:lr0] = np.arange(lr0)
    src[lr0:width - GLA_GATE_RANK] = np.arange(lr0 + GLA_GATE_RANK, width)
    src[_LR:_LR + GLA_GATE_RANK] = np.arange(lr0, lr0 + GLA_GATE_RANK)
    select = (jnp.arange(width, dtype=jnp.int32)[:, None] == jnp.asarray(src)[None, :]).astype(BF16)
    out = _matmul(w_in.reshape(depth * d, width).astype(BF16), select, d, Z_WIDTH // 3, BF16, "regroup_w_in")
    return out.reshape(depth, d, Z_WIDTH)


def kernel(x, p, w_in, w_gla_gate, b_gla_gate, gla_norm_g, diff_lambda, diff_norm_g, w_out, rel_bias, ln1_g, ln1_b, w_router, b_router, w_gate_up, b_gate_up, w_down, b_down, w_ple_gate, b_ple_gate, w_ple_proj, ln2_g, ln2_b):
    batch, seq, d = x.shape
    depth = w_in.shape[0]
    n = batch * seq
    alpha = (2 * depth) ** 0.25
    moe_blk = 256
    diff_blk = min(512, seq)

    w_in_b = _regroup_w_in(w_in)
    wg = jnp.zeros((depth, LANE, GLA_HEADS * GLA_DK), F32).at[:, :GLA_GATE_RANK, :].set(w_gla_gate).astype(BF16)
    gla_g = jnp.tile(gla_norm_g, (1, GLA_HEADS))[:, None, :]
    w_out_b = w_out.astype(BF16)
    wr = jnp.zeros((depth, d, LANE), F32).at[:, :, :N_EXPERTS].set(w_router)
    wr_hi = wr.astype(BF16)
    wr_b = jnp.concatenate([wr_hi, (wr - wr_hi.astype(F32)).astype(BF16)], axis=-1)
    br = jnp.zeros((depth, 1, LANE), F32).at[:, 0, :N_EXPERTS].set(b_router)
    w_gu = w_gate_up.reshape((depth * N_EXPERTS,) + w_gate_up.shape[2:])
    b_gu = b_gate_up.reshape(depth * N_EXPERTS, -1, LANE, 2).swapaxes(-1, -2).reshape(depth * N_EXPERTS, 1, -1)
    w_dn = w_down.reshape((depth * N_EXPERTS,) + w_down.shape[2:])
    b_dn = b_down.reshape(depth * N_EXPERTS, 1, -1)
    w_pg_b = w_ple_gate.astype(BF16)
    w_pp_b = w_ple_proj.astype(BF16)
    bias_tiles = _bias_tiles(rel_bias, diff_blk)
    ret_tables = _ret_tables(seq)

    xf = x.reshape(n, d)
    xb = xf.astype(BF16)
    pf = p.reshape(depth, n, p.shape[-1])
    for i in range(depth):
        lam_init = 0.8 - 0.6 * math.exp(-0.3 * i)
        z = _matmul(xb, w_in_b[i], min(1024, n), Z_WIDTH // 3, BF16, "in_proj")
        gla_o = _gla(z, wg[i], b_gla_gate[i][None, :], gla_g[i], batch, seq)
        dif_o = _diff(z, diff_lambda[i], diff_norm_g[i][None, :], bias_tiles, batch, seq, lam_init, diff_blk)
        ret_o = _ret(z, ret_tables, batch, seq)
        x1, x1b, x1pa, x1pb, gates, idx, cnt = _out_ln(gla_o, dif_o, ret_o, xf, w_out_b[i], ln1_g[i][None, :],
                                                       ln1_b[i][None, :], wr_b[i], br[i], alpha)
        slot, buf_len, blk_e, n_used, n_valid = _route(idx[:, :TOP_K], idx[:, TOP_K:2 * TOP_K], cnt[0, :N_EXPERTS], moe_blk)
        slot_k = slot.T.reshape(-1)
        xs = [_sc_scatter(t, slot_k, buf_len) for t in (x1pa, x1pb)]
        yb = _moe(xs, blk_e + i * N_EXPERTS, n_used, n_valid, w_gu, b_gu, w_dn, b_dn, moe_blk)
        yg = [t.reshape(TOP_K, n, SC_ROW_WORDS) for t in _gather_rows(yb, slot_k)]
        xf, xb = _combine(yg, gates, x1, x1b, pf[i], w_pg_b[i], b_ple_gate[i][None, :], w_pp_b[i],
                          ln2_g[i][None, :], ln2_b[i][None, :], alpha)
    return xf.reshape(batch, seq, d)
```

```python
import functools
import math

import numpy as np
import jax
import jax.numpy as jnp
from jax import lax
from jax.experimental import pallas as pl
from jax.experimental.pallas import tpu as pltpu
from jax.experimental.pallas import tpu_sc as plsc

F32 = jnp.float32
BF16 = jnp.bfloat16

GLA_HEADS = 4
GLA_DK = 32
GLA_DV = 64
GLA_GATE_RANK = 16
GLA_TAU = 16.0
GLA_CHUNK = 64
DIFF_HEADS = 4
DIFF_D = 64
RET_HEADS = 4
RET_DK = 64
RET_DV = 64
RET_CHUNK = 128
T5_BUCKETS = 32
T5_MAX_DIST = 128
N_EXPERTS = 32
TOP_K = 4
SWIGLU_LIMIT = 7.0
SWIGLU_ALPHA = 1.702
LN_EPS = 1e-5
HEAD_NORM_EPS = 1e-5

LANE = 128
VMEM_LIMIT = 56 * 1024 * 1024

_GQ, _GK, _GV, _GO = 0, 128, 256, 512
_DQ, _DK, _DV = 768, 1280, 1792
_RQ, _RK, _RV, _RG = 2304, 2560, 2816, 3072
_LR = 3328
Z_WIDTH = 3456

GLA_FAST_MAX_DECAY = 60.0


def _cparams(sem):
    return pltpu.CompilerParams(dimension_semantics=sem, vmem_limit_bytes=VMEM_LIMIT)


def _split_hi_lo(a):
    hi = a.astype(BF16)
    lo = (a - hi.astype(F32)).astype(BF16)
    return hi, lo


def _dot(a, b):
    return jnp.dot(a, b, preferred_element_type=F32)


def _dot_nt(a, b):
    return lax.dot_general(a, b, (((1,), (1,)), ((), ())), preferred_element_type=F32)


def _dot_tn(a, b):
    return lax.dot_general(a, b, (((0,), (0,)), ((), ())), preferred_element_type=F32)


def _seg_sum(a, ind):
    hi, lo = _split_hi_lo(a)
    return _dot(hi, ind) + _dot(lo, ind)


def _pack_bf16_pairs(t):
    h = t.shape[1] // 2
    bits = lax.bitcast_convert_type(t.astype(BF16).astype(F32), jnp.uint32)
    return (bits[:, :h] >> 16) | (bits[:, h:] & jnp.uint32(0xFFFF0000))


def _unpack_bf16_pairs(packed):
    lo = lax.bitcast_convert_type(packed << 16, F32)
    hi = lax.bitcast_convert_type(packed & jnp.uint32(0xFFFF0000), F32)
    return jnp.concatenate([lo, hi], axis=1)


SC_WINDOW = 128
SC_ROW_WORDS = 256


def _sc_gather(table, indices):
    m = indices.shape[0]
    words = table.shape[1]
    assert words == SC_ROW_WORDS and m % SC_WINDOW == 0
    mesh = plsc.VectorSubcoreMesh(core_axis_name="core", subcore_axis_name="subcore")

    def body(x_hbm, i_hbm, o_hbm):
        def step(i_vmem, o_vmem):
            pltpu.sync_copy(x_hbm.at[i_vmem.at[0]], o_vmem)

        pltpu.emit_pipeline(
            step,
            grid=(m // SC_WINDOW,),
            in_specs=[pl.BlockSpec((1, SC_WINDOW), index_map=lambda i: (0, i))],
            out_specs=[pl.BlockSpec((SC_WINDOW, words), index_map=lambda i: (i, 0))],
            core_axis_name=("core", "subcore"),
            dimension_semantics=(pltpu.PARALLEL,),
        )(i_hbm, o_hbm)

    return pl.kernel(body, out_type=jax.ShapeDtypeStruct((m, words), table.dtype), mesh=mesh,
                     scratch_types=[])(table, indices.reshape(1, m))


def _sc_scatter(table, indices, out_rows):
    r, words = table.shape
    m = indices.shape[0]
    assert words == SC_ROW_WORDS and r % SC_WINDOW == 0 and m % r == 0
    mesh = plsc.VectorSubcoreMesh(core_axis_name="core", subcore_axis_name="subcore")
    steps_per_pass = r // SC_WINDOW

    def body(x_hbm, i_hbm, o_hbm):
        def step(x_vmem, i_vmem):
            pltpu.sync_copy(x_vmem, o_hbm.at[i_vmem.at[0]])

        pltpu.emit_pipeline(
            step,
            grid=(m // SC_WINDOW,),
            in_specs=[pl.BlockSpec((SC_WINDOW, words), index_map=lambda i: (i % steps_per_pass, 0)),
                      pl.BlockSpec((1, SC_WINDOW), index_map=lambda i: (0, i))],
            out_specs=[],
            core_axis_name=("core", "subcore"),
            dimension_semantics=(pltpu.PARALLEL,),
        )(x_hbm, i_hbm)

    return pl.kernel(body, out_type=jax.ShapeDtypeStruct((out_rows, words), table.dtype), mesh=mesh,
                     scratch_types=[])(table, indices.reshape(1, m))


def _gather_rows(tables, rows):
    return [_sc_gather(t, rows) for t in tables]


def _mm_kernel(x_ref, w_ref, o_ref):
    o_ref[...] = _dot(x_ref[...], w_ref[...]).astype(o_ref.dtype)


def _matmul(x, w, tm, tn, out_dtype, name):
    m, k = x.shape
    n = w.shape[1]
    return pl.pallas_call(
        _mm_kernel,
        grid=(n // tn, m // tm),
        in_specs=[pl.BlockSpec((tm, k), lambda j, i: (i, 0)),
                  pl.BlockSpec((k, tn), lambda j, i: (0, j))],
        out_specs=pl.BlockSpec((tm, tn), lambda j, i: (i, j)),
        out_shape=jax.ShapeDtypeStruct((m, n), out_dtype),
        compiler_params=_cparams(("arbitrary", "arbitrary")),
        name=name,
    )(x, w)


def _gla_kernel(q_ref, k_ref, v_ref, og_ref, lr_ref, wg_ref, bg_ref, ng_ref, o_ref,
                state_ref, la_ref, cum_ref, kf_ref, vf_ref, oacc_ref, *, n_chunks, fast_max_decay):
    c_len = GLA_CHUNK
    hk = GLA_HEADS * GLA_DK
    hv = GLA_HEADS * GLA_DV

    @pl.when(pl.program_id(1) == 0)
    def _():
        state_ref[...] = jnp.zeros_like(state_ref)

    zg = _dot(lr_ref[...], wg_ref[...]) + bg_ref[...]
    la = (jnp.minimum(zg, 0.0) - jnp.log1p(jnp.exp(-jnp.abs(zg)))) / GLA_TAU
    la_ref[...] = la
    chunk_tot = jnp.sum(la.reshape(n_chunks, c_len, hk), axis=1)
    fast = jnp.min(chunk_tot) > -fast_max_decay

    row = lax.broadcasted_iota(jnp.int32, (c_len, c_len), 0)
    col = lax.broadcasted_iota(jnp.int32, (c_len, c_len), 1)
    tril = (row >= col).astype(BF16)
    sr = lax.broadcasted_iota(jnp.int32, (hv, hk), 0)
    sc = lax.broadcasted_iota(jnp.int32, (hv, hk), 1)
    state_mask = ((sr // GLA_DV) == (sc // GLA_DK)).astype(F32)
    lane_k = lax.broadcasted_iota(jnp.int32, (1, hk), 1) // GLA_DK
    lane_v = lax.broadcasted_iota(jnp.int32, (1, hv), 1) // GLA_DV
    r4 = lax.broadcasted_iota(jnp.int32, (GLA_HEADS * c_len, c_len), 0) % c_len
    c4 = lax.broadcasted_iota(jnp.int32, (GLA_HEADS * c_len, c_len), 1)
    causal4 = r4 >= c4
    er = lax.broadcasted_iota(jnp.int32, (hk, hv), 0) // GLA_DK
    ec = lax.broadcasted_iota(jnp.int32, (hk, hv), 1) // GLA_DV
    head_expand = (er == ec).astype(BF16)
    trow = lax.broadcasted_iota(jnp.int32, (c_len, hv), 0)

    def load_chunk(rows):
        q = q_ref[rows, :].astype(F32) * (GLA_DK ** -0.5)
        k = k_ref[rows, :].astype(F32)
        v = v_ref[rows, :]
        la_hi, la_lo = _split_hi_lo(la_ref[rows, :])
        cum = _dot(tril, la_hi) + _dot(tril, la_lo)
        return q, k, v, cum

    def fast_intra(q, k, v, cum):
        mid = cum[c_len // 2:c_len // 2 + 1, :]
        qa = q * jnp.exp(cum - mid)
        kb = (k * jnp.exp(mid - cum)).astype(BF16)
        qa4 = jnp.concatenate(
            [jnp.where(lane_k == h, qa, 0.0) for h in range(GLA_HEADS)], axis=0).astype(BF16)
        s4 = jnp.where(causal4, _dot_nt(qa4, kb), 0.0)
        pv = _dot(s4.astype(BF16), v)
        acc = jnp.zeros((c_len, hv), F32)
        for h in range(GLA_HEADS):
            acc = acc + jnp.where(lane_v == h, pv[h * c_len:(h + 1) * c_len, :], 0.0)
        return acc

    def slow_intra(q, k, v, cum):
        cum_ref[...] = cum
        kf_ref[...] = k
        vf_ref[...] = v.astype(F32)

        def one_key(s, acc):
            ks = kf_ref[pl.ds(s, 1), :]
            vs = vf_ref[pl.ds(s, 1), :]
            cs = cum_ref[pl.ds(s, 1), :]
            w = q * ks * jnp.exp(jnp.minimum(cum - cs, 0.0))
            wsum = _dot(w.astype(BF16), head_expand)
            return acc + jnp.where(trow >= s, wsum, 0.0) * vs

        return lax.fori_loop(0, c_len, one_key, jnp.zeros((c_len, hv), F32))

    def finish_chunk(rows, q, k, v, cum, intra, st):
        last = cum[c_len - 1:c_len, :]
        qg = (q * jnp.exp(cum)).astype(BF16)
        oacc_ref[rows, :] = intra + _dot_nt(qg, st.astype(BF16))
        kg = (k * jnp.exp(last - cum)).astype(BF16)
        upd = _dot_tn(v, kg)
        return (st * jnp.exp(last) + upd) * state_mask

    @pl.when(fast)
    def _():
        st = state_ref[...]
        for c in range(n_chunks):
            rows = pl.ds(c * c_len, c_len)
            q, k, v, cum = load_chunk(rows)
            st = finish_chunk(rows, q, k, v, cum, fast_intra(q, k, v, cum), st)
        state_ref[...] = st

    @pl.when(jnp.logical_not(fast))
    def _():
        def chunk(c, carry):
            rows = pl.ds(pl.multiple_of(c * c_len, c_len), c_len)
            q, k, v, cum = load_chunk(rows)
            state_ref[...] = finish_chunk(rows, q, k, v, cum, slow_intra(q, k, v, cum), state_ref[...])
            return carry

        lax.fori_loop(0, n_chunks, chunk, 0)

    o = oacc_ref[...]
    vr = lax.broadcasted_iota(jnp.int32, (hv, hv), 0) // GLA_DV
    vc = lax.broadcasted_iota(jnp.int32, (hv, hv), 1) // GLA_DV
    seg = (vr == vc).astype(BF16)
    ms = _seg_sum(o * o, seg) * (1.0 / GLA_DV)
    y = o * lax.rsqrt(ms + HEAD_NORM_EPS) * ng_ref[...]
    g = og_ref[...].astype(F32)
    o_ref[...] = (y * (g * jax.nn.sigmoid(g))).astype(o_ref.dtype)


def _gla(z, w_gate, b_gate, norm_g, batch, seq, tg=512, fast_max_decay=GLA_FAST_MAX_DECAY):
    tg = min(tg, seq)
    nt = seq // tg
    hk = GLA_HEADS * GLA_DK
    hv = GLA_HEADS * GLA_DV

    def rowmap(cb):
        return lambda b, t: (b * nt + t, cb)

    const = lambda b, t: (0, 0)
    return pl.pallas_call(
        functools.partial(_gla_kernel, n_chunks=tg // GLA_CHUNK, fast_max_decay=fast_max_decay),
        grid=(batch, nt),
        in_specs=[pl.BlockSpec((tg, hk), rowmap(_GQ // hk)),
                  pl.BlockSpec((tg, hk), rowmap(_GK // hk)),
                  pl.BlockSpec((tg, hv), rowmap(_GV // hv)),
                  pl.BlockSpec((tg, hv), rowmap(_GO // hv)),
                  pl.BlockSpec((tg, LANE), rowmap(_LR // LANE)),
                  pl.BlockSpec((LANE, hk), const),
                  pl.BlockSpec((1, hk), const),
                  pl.BlockSpec((1, hv), const)],
        out_specs=pl.BlockSpec((tg, hv), lambda b, t: (b * nt + t, 0)),
        out_shape=jax.ShapeDtypeStruct((batch * seq, hv), BF16),
        scratch_shapes=[pltpu.VMEM((hv, hk), F32),
                        pltpu.VMEM((tg, hk), F32),
                        pltpu.VMEM((GLA_CHUNK, hk), F32),
                        pltpu.VMEM((GLA_CHUNK, hk), F32),
                        pltpu.VMEM((GLA_CHUNK, hv), F32),
                        pltpu.VMEM((tg, hv), F32)],
        compiler_params=_cparams(("arbitrary", "arbitrary")),
        name="gla",
    )(z, z, z, z, z, w_gate, b_gate, norm_g)


LOG2E = math.log2(math.e)


def _diff_kernel(far_ref, lam_ref, q_ref, k_ref, v_ref, bias_ref, ng_ref, o_ref,
                 m_ref, acc_ref, *, blk, lam_init):
    qi = pl.program_id(2)
    d = DIFF_D
    w = 2 * DIFF_D
    q = (q_ref[...].astype(F32) * (DIFF_D ** -0.5 * LOG2E)).astype(BF16)
    lane_q = lax.broadcasted_iota(jnp.int32, q.shape, 1)
    q2 = jnp.concatenate([jnp.where(lane_q < d, q, jnp.zeros_like(q)),
                          jnp.where(lane_q >= d, q, jnp.zeros_like(q))], axis=0)
    m_ref[...] = jnp.full_like(m_ref, -jnp.inf)
    acc_ref[...] = jnp.zeros_like(acc_ref)

    def kv_step(j, bias, shift):
        rows = pl.ds(pl.multiple_of(j * blk, blk), blk)
        vj = v_ref[rows, :]
        s_both = _dot_nt(q2, k_ref[rows, :])
        for m in range(2):
            s = s_both[m * blk:(m + 1) * blk, :]
            if bias is not None:
                s = s + bias
            m_cur = jnp.max(s, axis=-1, keepdims=True)
            if shift is not None:
                m_cur = m_cur + shift
            m_old = m_ref[m]
            m_new = jnp.maximum(m_old, m_cur)
            alpha = jnp.exp2(m_old - m_new)
            sub = m_new if shift is None else m_new - shift
            p = jnp.exp2(s - jnp.concatenate([sub] * (blk // LANE), axis=1))
            row_sum = jnp.broadcast_to(jnp.sum(p, axis=-1, keepdims=True), (blk, LANE))
            acc_ref[m] = (jnp.concatenate([alpha, alpha], axis=1) * acc_ref[m]
                          + jnp.concatenate([_dot(p.astype(BF16), vj), row_sum], axis=1))
            m_ref[m] = m_new

    far = far_ref[pl.program_id(1)]

    def far_step(j, carry):
        kv_step(j, None, far)
        return carry

    lax.fori_loop(0, jnp.maximum(qi - 1, 0), far_step, 0)

    @pl.when(qi >= 1)
    def _():
        kv_step(qi - 1, bias_ref[0, 1], None)

    kv_step(qi, bias_ref[0, 0], None)

    lf = lam_ref[...]
    lam = (jnp.exp(jnp.sum(lf[0:1] * lf[1:2], axis=-1, keepdims=True))
           - jnp.exp(jnp.sum(lf[2:3] * lf[3:4], axis=-1, keepdims=True)) + lam_init)
    a0 = acc_ref[0]
    a1 = acc_ref[1]
    o = a0[:, :w] / a0[:, w:] - lam * (a1[:, :w] / a1[:, w:])
    ms = jnp.mean(o * o, axis=-1, keepdims=True)
    y = o * lax.rsqrt(ms + HEAD_NORM_EPS) * ng_ref[...] * (1.0 - lam_init)
    o_ref[...] = y.astype(o_ref.dtype)


def _t5_bucket(rel):
    n = jnp.maximum(-rel, 0)
    max_exact = T5_BUCKETS // 2
    nf = jnp.maximum(n, 1).astype(F32)
    large = max_exact + (jnp.log(nf / max_exact) / math.log(T5_MAX_DIST / max_exact)
                         * (T5_BUCKETS - max_exact)).astype(jnp.int32)
    large = jnp.minimum(large, T5_BUCKETS - 1)
    return jnp.where(n < max_exact, n, large)


def _bias_tiles(rel_bias, blk):
    assert blk + 1 >= T5_MAX_DIST
    m = jnp.arange(2 * blk)
    col_minus_row = jnp.where(m < blk, m, m - 2 * blk)
    tiles = []
    for delta in range(2):
        rel = col_minus_row - delta * blk
        w = rel_bias[_t5_bucket(rel)].astype(F32).T
        if delta == 0:
            w = jnp.where(rel <= 0, w, -jnp.inf)
        flat = jnp.tile(w, (1, blk))[:, :blk * (2 * blk - 1)]
        tiles.append(flat.reshape(-1, blk, 2 * blk - 1)[:, :, :blk])
    far = rel_bias[_t5_bucket(jnp.full((1,), -(blk + 1)))[0]].astype(F32)
    return jnp.stack(tiles, axis=1) * LOG2E, far * LOG2E


def _diff(z, diff_lambda, norm_g, bias, batch, seq, lam_init, blk):
    blk = min(blk, seq)
    nq = seq // blk
    w = 2 * DIFF_D
    bias_tiles, bias_far = bias
    return pl.pallas_call(
        functools.partial(_diff_kernel, blk=blk, lam_init=lam_init),
        grid=(batch, DIFF_HEADS, nq),
        in_specs=[pl.BlockSpec(memory_space=pltpu.SMEM),
                  pl.BlockSpec((4, DIFF_D), lambda b, h, i: (0, 0)),
                  pl.BlockSpec((blk, w), lambda b, h, i: (b * nq + i, _DQ // w + h)),
                  pl.BlockSpec((seq, w), lambda b, h, i: (b, _DK // w + h)),
                  pl.BlockSpec((seq, w), lambda b, h, i: (b, _DV // w + h)),
                  pl.BlockSpec((1, 2, blk, blk), lambda b, h, i: (h, 0, 0, 0)),
                  pl.BlockSpec((1, w), lambda b, h, i: (0, 0))],
        out_specs=pl.BlockSpec((blk, w), lambda b, h, i: (b * nq + i, h)),
        out_shape=jax.ShapeDtypeStruct((batch * seq, DIFF_HEADS * w), BF16),
        scratch_shapes=[pltpu.VMEM((2, blk, LANE), F32),
                        pltpu.VMEM((2, blk, w + LANE), F32)],
        compiler_params=_cparams(("arbitrary", "arbitrary", "arbitrary")),
        name="diff_attn",
    )(bias_far, diff_lambda, z, z, z, bias_tiles, norm_g)


def _ret_kernel(q_ref, k_ref, v_ref, g_ref, sin_ref, cos_ref, inner_ref, cross_ref, sdec_ref, cdec_ref,
                o_ref, state_ref):
    c_len = RET_CHUNK
    hk = RET_HEADS * RET_DK
    hv = RET_HEADS * RET_DV

    @pl.when(pl.program_id(1) == 0)
    def _():
        state_ref[...] = jnp.zeros_like(state_ref)

    sin = sin_ref[...]
    cos = cos_ref[...]
    even = (lax.broadcasted_iota(jnp.int32, (1, hk), 1) % 2) == 0

    def rotary(t):
        nxt = pltpu.roll(t, hk - 1, 1)
        prv = pltpu.roll(t, 1, 1)
        return t * cos + jnp.where(even, -nxt, prv) * sin

    qf = rotary(q_ref[...].astype(F32))
    kf = rotary(k_ref[...].astype(F32)) * (RET_DK ** -0.5)
    v = v_ref[...]

    lane_k = lax.broadcasted_iota(jnp.int32, (1, hk), 1) // RET_DK
    lane_v = lax.broadcasted_iota(jnp.int32, (1, hv), 1) // RET_DV
    q4 = jnp.concatenate([jnp.where(lane_k == h, qf, 0.0) for h in range(RET_HEADS)], axis=0).astype(BF16)
    s4 = _dot_nt(q4, kf.astype(BF16)) * inner_ref[...]
    pv = _dot(s4.astype(BF16), v)
    acc = jnp.zeros((c_len, hv), F32)
    for h in range(RET_HEADS):
        acc = acc + jnp.where(lane_v == h, pv[h * c_len:(h + 1) * c_len, :], 0.0)

    st = state_ref[...]
    o = acc + _dot_nt(qf.astype(BF16), st.astype(BF16)) * cross_ref[...]
    ks = (kf * sdec_ref[...]).astype(BF16)
    sr = lax.broadcasted_iota(jnp.int32, (hv, hk), 0) // RET_DV
    sc = lax.broadcasted_iota(jnp.int32, (hv, hk), 1) // RET_DK
    state_ref[...] = jnp.where(sr == sc, st * cdec_ref[...] + _dot_tn(v, ks), 0.0)

    vr = lax.broadcasted_iota(jnp.int32, (hv, hv), 0) // RET_DV
    vc = lax.broadcasted_iota(jnp.int32, (hv, hv), 1) // RET_DV
    seg = (vr == vc).astype(BF16)
    ms = _seg_sum(o * o, seg) * (1.0 / RET_DV)
    g = g_ref[...].astype(F32)
    o_ref[...] = (o * lax.rsqrt(ms + HEAD_NORM_EPS) * (g * jax.nn.sigmoid(g))).astype(o_ref.dtype)


def _ret_tables(seq):
    c_len = RET_CHUNK
    pos = jnp.arange(seq, dtype=F32)
    angle = 1.0 / (10000.0 ** jnp.linspace(0.0, 1.0, RET_DK // 2, dtype=F32))
    angle = jnp.repeat(angle, 2)
    sin = jnp.tile(jnp.sin(pos[:, None] * angle), (1, RET_HEADS))
    cos = jnp.tile(jnp.cos(pos[:, None] * angle), (1, RET_HEADS))
    log_g = jnp.log1p(-jnp.exp2(-5.0 - jnp.arange(RET_HEADS, dtype=F32)))
    idx = jnp.arange(c_len, dtype=F32)
    rel = idx[:, None] - idx[None, :]
    inner = jnp.where(rel[None] >= 0, jnp.exp(jnp.maximum(rel, 0.0)[None] * log_g[:, None, None]), 0.0)
    inner = inner.reshape(RET_HEADS * c_len, c_len)
    cross = jnp.exp((idx + 1.0)[None] * log_g[:, None])
    cross = jnp.repeat(cross.T, RET_DV, axis=1)
    sdec = jnp.exp((c_len - 1.0 - idx)[None] * log_g[:, None])
    sdec = jnp.repeat(sdec.T, RET_DK, axis=1)
    cdec = jnp.repeat(jnp.exp(c_len * log_g), RET_DK)[None, :]
    return sin, cos, inner, cross, sdec, cdec


def _ret(z, tables, batch, seq):
    c_len = RET_CHUNK
    nt = seq // c_len
    hk = RET_HEADS * RET_DK
    hv = RET_HEADS * RET_DV
    sin, cos, inner, cross, sdec, cdec = tables

    def rowmap(cb):
        return lambda b, t: (b * nt + t, cb)

    const = lambda b, t: (0, 0)
    return pl.pallas_call(
        _ret_kernel,
        grid=(batch, nt),
        in_specs=[pl.BlockSpec((c_len, hk), rowmap(_RQ // hk)),
                  pl.BlockSpec((c_len, hk), rowmap(_RK // hk)),
                  pl.BlockSpec((c_len, hv), rowmap(_RV // hv)),
                  pl.BlockSpec((c_len, hv), rowmap(_RG // hv)),
                  pl.BlockSpec((c_len, hk), lambda b, t: (t, 0)),
                  pl.BlockSpec((c_len, hk), lambda b, t: (t, 0)),
                  pl.BlockSpec((RET_HEADS * c_len, c_len), const),
                  pl.BlockSpec((c_len, hv), const),
                  pl.BlockSpec((c_len, hk), const),
                  pl.BlockSpec((1, hk), const)],
        out_specs=pl.BlockSpec((c_len, hv), lambda b, t: (b * nt + t, 0)),
        out_shape=jax.ShapeDtypeStruct((batch * seq, hv), BF16),
        scratch_shapes=[pltpu.VMEM((hv, hk), F32)],
        compiler_params=_cparams(("arbitrary", "arbitrary")),
        name="retention",
    )(z, z, z, z, sin, cos, inner, cross, sdec, cdec)


def _layer_norm(t, g, b):
    mu = jnp.mean(t, axis=-1, keepdims=True)
    var = jnp.mean(jnp.square(t - mu), axis=-1, keepdims=True)
    return (t - mu) * lax.rsqrt(var + LN_EPS) * g + b


def _out_ln_kernel(gla_ref, dif_ref, ret_ref, x_ref, wo_ref, g_ref, b_ref, wr_ref, br_ref, earlier_ref,
                   x1_ref, x1b_ref, x1pa_ref, x1pb_ref, gate_ref, idx_ref, cnt_ref, *, alpha):
    w0 = gla_ref.shape[1]
    w1 = w0 + dif_ref.shape[1]
    a = (_dot(gla_ref[...], wo_ref[0:w0, :]) + _dot(dif_ref[...], wo_ref[w0:w1, :])
         + _dot(ret_ref[...], wo_ref[w1:, :]))
    x1 = _layer_norm(alpha * x_ref[...] + a, g_ref[...], b_ref[...])
    x1_ref[...] = x1
    x1b_ref[...] = x1.astype(BF16)
    x1p = _pack_bf16_pairs(x1)
    x1pa_ref[...] = x1p[:, :SC_ROW_WORDS]
    x1pb_ref[...] = x1p[:, SC_ROW_WORDS:]

    hi, lo = _split_hi_lo(x1)
    tm = x1.shape[0]
    parts = _dot(jnp.concatenate([hi, lo], axis=0), wr_ref[...])
    logits = (parts[:tm, :LANE] + parts[tm:, :LANE]) + (parts[:tm, LANE:] + parts[tm:, LANE:]) + br_ref[...]
    lane = lax.broadcasted_iota(jnp.int32, logits.shape, 1)
    work = jnp.where(lane < N_EXPERTS, logits, -jnp.inf)
    vals = jnp.zeros(logits.shape, F32)
    idxs = jnp.zeros(logits.shape, jnp.int32)
    chosen = jnp.zeros(logits.shape, F32)
    picks = []
    top0 = None
    for r in range(TOP_K):
        mx = jnp.max(work, axis=-1, keepdims=True)
        am = jnp.min(jnp.where(work == mx, lane, LANE), axis=-1, keepdims=True)
        if r == 0:
            top0 = mx
        vals = jnp.where(lane == r, jnp.exp(mx - top0), vals)
        idxs = jnp.where(lane == r, am, idxs)
        chosen = jnp.where(lane == am, 1.0, chosen)
        work = jnp.where(lane == am, -jnp.inf, work)
        picks.append(am)
    gate_ref[...] = vals / jnp.sum(vals, axis=-1, keepdims=True)

    @pl.when(pl.program_id(0) == 0)
    def _():
        cnt_ref[...] = jnp.zeros_like(cnt_ref)

    before = _dot(earlier_ref[...], chosen.astype(BF16)) + cnt_ref[...]
    for r in range(TOP_K):
        rank = jnp.sum(jnp.where(lane == picks[r], before, 0.0), axis=-1, keepdims=True)
        idxs = jnp.where(lane == TOP_K + r, rank.astype(jnp.int32), idxs)
    idx_ref[...] = idxs
    cnt_ref[...] = cnt_ref[...] + jnp.sum(chosen, axis=0, keepdims=True)


def _out_ln(gla_o, dif_o, ret_o, x, w_out, ln_g, ln_b, wr, b_router, alpha, tm=512):
    n, d = x.shape
    tm = min(tm, n)
    earlier = jnp.asarray(np.tril(np.ones((tm, tm), np.float32), -1), BF16)
    row = lambda i: (i, 0)
    const = lambda i: (0, 0)
    return pl.pallas_call(
        functools.partial(_out_ln_kernel, alpha=alpha),
        grid=(n // tm,),
        in_specs=[pl.BlockSpec((tm, gla_o.shape[1]), row),
                  pl.BlockSpec((tm, dif_o.shape[1]), row),
                  pl.BlockSpec((tm, ret_o.shape[1]), row),
                  pl.BlockSpec((tm, d), row),
                  pl.BlockSpec(w_out.shape, const),
                  pl.BlockSpec((1, d), const),
                  pl.BlockSpec((1, d), const),
                  pl.BlockSpec((d, 2 * LANE), const),
                  pl.BlockSpec((1, LANE), const),
                  pl.BlockSpec((tm, tm), const)],
        out_specs=[pl.BlockSpec((tm, d), row),
                   pl.BlockSpec((tm, d), row),
                   pl.BlockSpec((tm, SC_ROW_WORDS), row),
                   pl.BlockSpec((tm, SC_ROW_WORDS), row),
                   pl.BlockSpec((tm, LANE), row),
                   pl.BlockSpec((tm, LANE), row),
                   pl.BlockSpec((1, LANE), const)],
        out_shape=[jax.ShapeDtypeStruct((n, d), F32),
                   jax.ShapeDtypeStruct((n, d), BF16),
                   jax.ShapeDtypeStruct((n, SC_ROW_WORDS), jnp.uint32),
                   jax.ShapeDtypeStruct((n, SC_ROW_WORDS), jnp.uint32),
                   jax.ShapeDtypeStruct((n, LANE), F32),
                   jax.ShapeDtypeStruct((n, LANE), jnp.int32),
                   jax.ShapeDtypeStruct((1, LANE), F32)],
        compiler_params=_cparams(("arbitrary",)),
        name="out_ln_router",
    )(gla_o, dif_o, ret_o, x, w_out, ln_g, ln_b, wr, b_router, earlier)


GU_GROUP = 2 * LANE


def _deinterleave_perm():
    perm = np.zeros((GU_GROUP, GU_GROUP), np.float32)
    half = np.arange(LANE)
    perm[2 * half, half] = 1.0
    perm[2 * half + 1, LANE + half] = 1.0
    return jnp.asarray(perm, BF16)


def _moe_kernel(blk_e_ref, n_used_ref, n_valid_ref, xa_ref, xb_ref, wgu_ref, bgu_ref, wdn_ref, bdn_ref, perm_ref,
                oa_ref, ob_ref,
                wgu_b_ref, wdn_b_ref, *, rows_per_cast):
    i = pl.program_id(0)
    used = i < n_used_ref[0]
    new_expert = jnp.logical_or(i == 0, blk_e_ref[i] != blk_e_ref[jnp.maximum(i - 1, 0)])

    @pl.when(jnp.logical_and(used, new_expert))
    def _():
        d = wgu_ref.shape[0]
        for r in range(d // rows_per_cast):
            rows = slice(r * rows_per_cast, (r + 1) * rows_per_cast)
            wdn_b_ref[rows, :] = wdn_ref[rows, :].astype(BF16)
            for g in range(wgu_ref.shape[1] // GU_GROUP):
                cols = slice(g * GU_GROUP, (g + 1) * GU_GROUP)
                wgu_b_ref[rows, cols] = _dot(wgu_ref[rows, cols].astype(BF16), perm_ref[...]).astype(BF16)

    @pl.when(used)
    def _():
        x = _unpack_bf16_pairs(jnp.concatenate([xa_ref[...], xb_ref[...]], axis=1))
        row = lax.broadcasted_iota(jnp.int32, x.shape, 0)
        x = jnp.where(row < n_valid_ref[i], x, 0.0).astype(BF16)
        gu = _dot(x, wgu_b_ref[...]) + bgu_ref[...]
        acts = []
        for grp in range(gu.shape[1] // GU_GROUP):
            g = jnp.minimum(gu[:, grp * GU_GROUP:grp * GU_GROUP + LANE], SWIGLU_LIMIT)
            u = jnp.clip(gu[:, grp * GU_GROUP + LANE:(grp + 1) * GU_GROUP], -SWIGLU_LIMIT, SWIGLU_LIMIT)
            acts.append(((u + 1.0) * (g * jax.nn.sigmoid(g * SWIGLU_ALPHA))).astype(BF16))
        act = jnp.concatenate(acts, axis=1)
        y = _pack_bf16_pairs(_dot(act, wdn_b_ref[...]) + bdn_ref[...])
        oa_ref[...] = y[:, :SC_ROW_WORDS]
        ob_ref[...] = y[:, SC_ROW_WORDS:]

    @pl.when(jnp.logical_not(used))
    def _():
        oa_ref[...] = jnp.zeros_like(oa_ref)
        ob_ref[...] = jnp.zeros_like(ob_ref)


def _moe(xs, blk_e, n_used, n_valid, w_gu, b_gu, w_dn, b_dn, blk):
    rows = xs[0].shape[0]
    assert w_gu.shape[1] == 4 * SC_ROW_WORDS
    d = w_gu.shape[1]
    f2 = w_gu.shape[2]
    f = w_dn.shape[1]
    assert f == d
    grid_spec = pltpu.PrefetchScalarGridSpec(
        num_scalar_prefetch=3,
        grid=(rows // blk,),
        in_specs=[pl.BlockSpec((blk, SC_ROW_WORDS), lambda i, be, nu, nv: (i, 0)),
                  pl.BlockSpec((blk, SC_ROW_WORDS), lambda i, be, nu, nv: (i, 0)),
                  pl.BlockSpec((None, d, f2), lambda i, be, nu, nv: (be[i], 0, 0)),
                  pl.BlockSpec((None, 1, f2), lambda i, be, nu, nv: (be[i], 0, 0)),
                  pl.BlockSpec((None, f, d), lambda i, be, nu, nv: (be[i], 0, 0)),
                  pl.BlockSpec((None, 1, d), lambda i, be, nu, nv: (be[i], 0, 0)),
                  pl.BlockSpec((GU_GROUP, GU_GROUP), lambda i, be, nu, nv: (0, 0))],
        out_specs=[pl.BlockSpec((blk, SC_ROW_WORDS), lambda i, be, nu, nv: (i, 0)),
                   pl.BlockSpec((blk, SC_ROW_WORDS), lambda i, be, nu, nv: (i, 0))],
        scratch_shapes=[pltpu.VMEM((d, f2), BF16), pltpu.VMEM((f, d), BF16)],
    )
    return pl.pallas_call(
        functools.partial(_moe_kernel, rows_per_cast=256),
        grid_spec=grid_spec,
        out_shape=[jax.ShapeDtypeStruct((rows, SC_ROW_WORDS), jnp.uint32)] * 2,
        compiler_params=_cparams(("arbitrary",)),
        name="moe_experts",
    )(blk_e, n_used, n_valid, xs[0], xs[1], w_gu, b_gu, w_dn, b_dn, _deinterleave_perm())


def _combine_kernel(yga_ref, ygb_ref, gate_ref, x1_ref, x1b_ref, p_ref, wg_ref, bg_ref, wp_ref, g_ref, b_ref,
                    x2_ref, x2b_ref, *, alpha):
    gates = gate_ref[...]
    m = jnp.zeros(x1_ref.shape, F32)
    for r in range(TOP_K):
        m = m + _unpack_bf16_pairs(jnp.concatenate([yga_ref[r], ygb_ref[r]], axis=1)) * gates[:, r:r + 1]
    e = (jax.nn.sigmoid(_dot(x1b_ref[...], wg_ref[...]) + bg_ref[...])
         * _dot(p_ref[...].astype(BF16), wp_ref[...]))
    x2 = _layer_norm(alpha * x1_ref[...] + m + e, g_ref[...], b_ref[...])
    x2_ref[...] = x2
    x2b_ref[...] = x2.astype(BF16)


def _combine(yg, gates, x1, x1b, p, w_pg, b_pg, w_pp, ln_g, ln_b, alpha, tm=512):
    n, d = x1.shape
    tm = min(tm, n)
    row = lambda i: (i, 0)
    const = lambda i: (0, 0)
    return pl.pallas_call(
        functools.partial(_combine_kernel, alpha=alpha),
        grid=(n // tm,),
        in_specs=[pl.BlockSpec((TOP_K, tm, SC_ROW_WORDS), lambda i: (0, i, 0)),
                  pl.BlockSpec((TOP_K, tm, SC_ROW_WORDS), lambda i: (0, i, 0)),
                  pl.BlockSpec((tm, LANE), row),
                  pl.BlockSpec((tm, d), row),
                  pl.BlockSpec((tm, d), row),
                  pl.BlockSpec((tm, p.shape[1]), row),
                  pl.BlockSpec(w_pg.shape, const),
                  pl.BlockSpec((1, d), const),
                  pl.BlockSpec(w_pp.shape, const),
                  pl.BlockSpec((1, d), const),
                  pl.BlockSpec((1, d), const)],
        out_specs=[pl.BlockSpec((tm, d), row), pl.BlockSpec((tm, d), row)],
        out_shape=[jax.ShapeDtypeStruct((n, d), F32), jax.ShapeDtypeStruct((n, d), BF16)],
        compiler_params=_cparams(("arbitrary",)),
        name="combine_ln",
    )(yg[0], yg[1], gates, x1, x1b, p, w_pg, b_pg, w_pp, ln_g, ln_b)


def _route(experts, ranks, counts, blk):
    n_tok = experts.shape[0]
    n_assign = n_tok * TOP_K
    sizes = counts.astype(jnp.int32)
    padded = ((sizes + blk - 1) // blk) * blk
    pends = jnp.cumsum(padded)
    pstarts = pends - padded
    onehot = experts[:, :, None] == jnp.arange(N_EXPERTS, dtype=jnp.int32)
    slot = jnp.sum(jnp.where(onehot, pstarts, 0), axis=-1) + ranks
    buf_len = ((n_assign + N_EXPERTS * (blk - 1) + blk - 1) // blk) * blk
    n_blk = buf_len // blk
    first_row = jnp.arange(n_blk, dtype=jnp.int32) * blk
    blk_e = jnp.minimum(jnp.sum(pends[None, :] <= first_row[:, None], axis=1), N_EXPERTS - 1).astype(jnp.int32)
    n_used = (pends[-1] // blk).astype(jnp.int32).reshape(1)
    mine = blk_e[:, None] == jnp.arange(N_EXPERTS, dtype=jnp.int32)[None, :]
    group_rows_left = jnp.sum(jnp.where(mine, pstarts + sizes, 0), axis=1) - first_row
    n_valid = jnp.clip(group_rows_left, 0, blk).astype(jnp.int32)
    return slot, buf_len, blk_e, n_used, n_valid


def _regroup_w_in(w_in):
    depth, d, width = w_in.shape
    lr0 = _GV + GLA_HEADS * GLA_DV
    src = np.full((Z_WIDTH,), -1, np.int32)
    src[:lr0] = np.arange(lr0)
    src[lr0:width - GLA_GATE_RANK] = np.arange(lr0 + GLA_GATE_RANK, width)
    src[_LR:_LR + GLA_GATE_RANK] = np.arange(lr0, lr0 + GLA_GATE_RANK)
    select = (jnp.arange(width, dtype=jnp.int32)[:, None] == jnp.asarray(src)[None, :]).astype(BF16)
    out = _matmul(w_in.reshape(depth * d, width).astype(BF16), select, d, Z_WIDTH // 3, BF16, "regroup_w_in")
    return out.reshape(depth, d, Z_WIDTH)


def kernel(x, p, w_in, w_gla_gate, b_gla_gate, gla_norm_g, diff_lambda, diff_norm_g, w_out, rel_bias, ln1_g, ln1_b, w_router, b_router, w_gate_up, b_gate_up, w_down, b_down, w_ple_gate, b_ple_gate, w_ple_proj, ln2_g, ln2_b):
    batch, seq, d = x.shape
    depth = w_in.shape[0]
    n = batch * seq
    alpha = (2 * depth) ** 0.25
    moe_blk = 512
    diff_blk = min(512, seq)

    w_in_b = _regroup_w_in(w_in)
    wg = jnp.zeros((depth, LANE, GLA_HEADS * GLA_DK), F32).at[:, :GLA_GATE_RANK, :].set(w_gla_gate).astype(BF16)
    gla_g = jnp.tile(gla_norm_g, (1, GLA_HEADS))[:, None, :]
    w_out_b = w_out.astype(BF16)
    wr = jnp.zeros((depth, d, LANE), F32).at[:, :, :N_EXPERTS].set(w_router)
    wr_hi = wr.astype(BF16)
    wr_b = jnp.concatenate([wr_hi, (wr - wr_hi.astype(F32)).astype(BF16)], axis=-1)
    br = jnp.zeros((depth, 1, LANE), F32).at[:, 0, :N_EXPERTS].set(b_router)
    w_gu = w_gate_up.reshape((depth * N_EXPERTS,) + w_gate_up.shape[2:])
    b_gu = b_gate_up.reshape(depth * N_EXPERTS, -1, LANE, 2).swapaxes(-1, -2).reshape(depth * N_EXPERTS, 1, -1)
    w_dn = w_down.reshape((depth * N_EXPERTS,) + w_down.shape[2:])
    b_dn = b_down.reshape(depth * N_EXPERTS, 1, -1)
    w_pg_b = w_ple_gate.astype(BF16)
    w_pp_b = w_ple_proj.astype(BF16)
    bias_tiles = _bias_tiles(rel_bias, diff_blk)
    ret_tables = _ret_tables(seq)

    xf = x.reshape(n, d)
    xb = xf.astype(BF16)
    pf = p.reshape(depth, n, p.shape[-1])
    for i in range(depth):
        lam_init = 0.8 - 0.6 * math.exp(-0.3 * i)
        z = _matmul(xb, w_in_b[i], min(1024, n), Z_WIDTH // 3, BF16, "in_proj")
        gla_o = _gla(z, wg[i], b_gla_gate[i][None, :], gla_g[i], batch, seq)
        dif_o = _diff(z, diff_lambda[i], diff_norm_g[i][None, :], bias_tiles, batch, seq, lam_init, diff_blk)
        ret_o = _ret(z, ret_tables, batch, seq)
        x1, x1b, x1pa, x1pb, gates, idx, cnt = _out_ln(gla_o, dif_o, ret_o, xf, w_out_b[i], ln1_g[i][None, :],
                                                       ln1_b[i][None, :], wr_b[i], br[i], alpha)
        slot, buf_len, blk_e, n_used, n_valid = _route(idx[:, :TOP_K], idx[:, TOP_K:2 * TOP_K], cnt[0, :N_EXPERTS], moe_blk)
        slot_k = slot.T.reshape(-1)
        xs = [_sc_scatter(t, slot_k, buf_len) for t in (x1pa, x1pb)]
        yb = _moe(xs, blk_e + i * N_EXPERTS, n_used, n_valid, w_gu, b_gu, w_dn, b_dn, moe_blk)
        yg = [t.reshape(TOP_K, n, SC_ROW_WORDS) for t in _gather_rows(yb, slot_k)]
        xf, xb = _combine(yg, gates, x1, x1b, pf[i], w_pg_b[i], b_ple_gate[i][None, :], w_pp_b[i],
                          ln2_g[i][None, :], ln2_b[i][None, :], alpha)
    return xf.reshape(batch, seq, d)
```

```python
import functools
import math

import numpy as np
import jax
import jax.numpy as jnp
from jax import lax
from jax.experimental import pallas as pl
from jax.experimental.pallas import tpu as pltpu
from jax.experimental.pallas import tpu_sc as plsc

F32 = jnp.float32
BF16 = jnp.bfloat16

GLA_HEADS = 4
GLA_DK = 32
GLA_DV = 64
GLA_GATE_RANK = 16
GLA_TAU = 16.0
GLA_CHUNK = 64
DIFF_HEADS = 4
DIFF_D = 64
RET_HEADS = 4
RET_DK = 64
RET_DV = 64
RET_CHUNK = 128
T5_BUCKETS = 32
T5_MAX_DIST = 128
N_EXPERTS = 32
TOP_K = 4
SWIGLU_LIMIT = 7.0
SWIGLU_ALPHA = 1.702
LN_EPS = 1e-5
HEAD_NORM_EPS = 1e-5

LANE = 128
VMEM_LIMIT = 56 * 1024 * 1024

_GQ, _GK, _GV, _GO = 0, 128, 256, 512
_DQ, _DK, _DV = 768, 1280, 1792
_RQ, _RK, _RV, _RG = 2304, 2560, 2816, 3072
_LR = 3328
Z_WIDTH = 3456

GLA_FAST_MAX_DECAY = 60.0


def _cparams(sem):
    return pltpu.CompilerParams(dimension_semantics=sem, vmem_limit_bytes=VMEM_LIMIT)


def _split_hi_lo(a):
    hi = a.astype(BF16)
    lo = (a - hi.astype(F32)).astype(BF16)
    return hi, lo


def _dot(a, b):
    return jnp.dot(a, b, preferred_element_type=F32)


def _dot_nt(a, b):
    return lax.dot_general(a, b, (((1,), (1,)), ((), ())), preferred_element_type=F32)


def _dot_tn(a, b):
    return lax.dot_general(a, b, (((0,), (0,)), ((), ())), preferred_element_type=F32)


def _seg_sum(a, ind):
    hi, lo = _split_hi_lo(a)
    return _dot(hi, ind) + _dot(lo, ind)


def _pack_bf16_pairs(t):
    h = t.shape[1] // 2
    bits = lax.bitcast_convert_type(t.astype(BF16).astype(F32), jnp.uint32)
    return (bits[:, :h] >> 16) | (bits[:, h:] & jnp.uint32(0xFFFF0000))


def _unpack_bf16_pairs(packed):
    lo = lax.bitcast_convert_type(packed << 16, F32)
    hi = lax.bitcast_convert_type(packed & jnp.uint32(0xFFFF0000), F32)
    return jnp.concatenate([lo, hi], axis=1)


SC_WINDOW = 128
SC_ROW_WORDS = 256


def _sc_gather(table, indices):
    m = indices.shape[0]
    words = table.shape[1]
    assert words == SC_ROW_WORDS and m % SC_WINDOW == 0
    mesh = plsc.VectorSubcoreMesh(core_axis_name="core", subcore_axis_name="subcore")

    def body(x_hbm, i_hbm, o_hbm):
        def step(i_vmem, o_vmem):
            pltpu.sync_copy(x_hbm.at[i_vmem.at[0]], o_vmem)

        pltpu.emit_pipeline(
            step,
            grid=(m // SC_WINDOW,),
            in_specs=[pl.BlockSpec((1, SC_WINDOW), index_map=lambda i: (0, i))],
            out_specs=[pl.BlockSpec((SC_WINDOW, words), index_map=lambda i: (i, 0))],
            core_axis_name=("core", "subcore"),
            dimension_semantics=(pltpu.PARALLEL,),
        )(i_hbm, o_hbm)

    return pl.kernel(body, out_type=jax.ShapeDtypeStruct((m, words), table.dtype), mesh=mesh,
                     scratch_types=[])(table, indices.reshape(1, m))


def _sc_scatter(table, indices, out_rows):
    r, words = table.shape
    m = indices.shape[0]
    assert words == SC_ROW_WORDS and r % SC_WINDOW == 0 and m % r == 0
    mesh = plsc.VectorSubcoreMesh(core_axis_name="core", subcore_axis_name="subcore")
    steps_per_pass = r // SC_WINDOW

    def body(x_hbm, i_hbm, o_hbm):
        def step(x_vmem, i_vmem):
            pltpu.sync_copy(x_vmem, o_hbm.at[i_vmem.at[0]])

        pltpu.emit_pipeline(
            step,
            grid=(m // SC_WINDOW,),
            in_specs=[pl.BlockSpec((SC_WINDOW, words), index_map=lambda i: (i % steps_per_pass, 0)),
                      pl.BlockSpec((1, SC_WINDOW), index_map=lambda i: (0, i))],
            out_specs=[],
            core_axis_name=("core", "subcore"),
            dimension_semantics=(pltpu.PARALLEL,),
        )(x_hbm, i_hbm)

    return pl.kernel(body, out_type=jax.ShapeDtypeStruct((out_rows, words), table.dtype), mesh=mesh,
                     scratch_types=[])(table, indices.reshape(1, m))


def _gather_rows(tables, rows):
    return [_sc_gather(t, rows) for t in tables]


def _mm_kernel(x_ref, w_ref, o_ref):
    o_ref[...] = _dot(x_ref[...], w_ref[...]).astype(o_ref.dtype)


def _matmul(x, w, tm, tn, out_dtype, name):
    m, k = x.shape
    n = w.shape[1]
    return pl.pallas_call(
        _mm_kernel,
        grid=(n // tn, m // tm),
        in_specs=[pl.BlockSpec((tm, k), lambda j, i: (i, 0)),
                  pl.BlockSpec((k, tn), lambda j, i: (0, j))],
        out_specs=pl.BlockSpec((tm, tn), lambda j, i: (i, j)),
        out_shape=jax.ShapeDtypeStruct((m, n), out_dtype),
        compiler_params=_cparams(("arbitrary", "arbitrary")),
        name=name,
    )(x, w)


def _gla_kernel(q_ref, k_ref, v_ref, og_ref, lr_ref, wg_ref, bg_ref, ng_ref, o_ref,
                state_ref, la_ref, cum_ref, kf_ref, vf_ref, oacc_ref, *, n_chunks, fast_max_decay):
    c_len = GLA_CHUNK
    hk = GLA_HEADS * GLA_DK
    hv = GLA_HEADS * GLA_DV

    @pl.when(pl.program_id(1) == 0)
    def _():
        state_ref[...] = jnp.zeros_like(state_ref)

    zg = _dot(lr_ref[...], wg_ref[...]) + bg_ref[...]
    la = (jnp.minimum(zg, 0.0) - jnp.log1p(jnp.exp(-jnp.abs(zg)))) / GLA_TAU
    la_ref[...] = la
    chunk_tot = jnp.sum(la.reshape(n_chunks, c_len, hk), axis=1)
    fast = jnp.min(chunk_tot) > -fast_max_decay

    row = lax.broadcasted_iota(jnp.int32, (c_len, c_len), 0)
    col = lax.broadcasted_iota(jnp.int32, (c_len, c_len), 1)
    tril = (row >= col).astype(BF16)
    sr = lax.broadcasted_iota(jnp.int32, (hv, hk), 0)
    sc = lax.broadcasted_iota(jnp.int32, (hv, hk), 1)
    state_mask = ((sr // GLA_DV) == (sc // GLA_DK)).astype(F32)
    lane_k = lax.broadcasted_iota(jnp.int32, (1, hk), 1) // GLA_DK
    lane_v = lax.broadcasted_iota(jnp.int32, (1, hv), 1) // GLA_DV
    r4 = lax.broadcasted_iota(jnp.int32, (GLA_HEADS * c_len, c_len), 0) % c_len
    c4 = lax.broadcasted_iota(jnp.int32, (GLA_HEADS * c_len, c_len), 1)
    causal4 = r4 >= c4
    er = lax.broadcasted_iota(jnp.int32, (hk, hv), 0) // GLA_DK
    ec = lax.broadcasted_iota(jnp.int32, (hk, hv), 1) // GLA_DV
    head_expand = (er == ec).astype(BF16)
    trow = lax.broadcasted_iota(jnp.int32, (c_len, hv), 0)

    def load_chunk(rows):
        q = q_ref[rows, :].astype(F32) * (GLA_DK ** -0.5)
        k = k_ref[rows, :].astype(F32)
        v = v_ref[rows, :]
        la_hi, la_lo = _split_hi_lo(la_ref[rows, :])
        cum = _dot(tril, la_hi) + _dot(tril, la_lo)
        return q, k, v, cum

    def fast_intra(q, k, v, cum):
        mid = cum[c_len // 2:c_len // 2 + 1, :]
        qa = q * jnp.exp(cum - mid)
        kb = (k * jnp.exp(mid - cum)).astype(BF16)
        qa4 = jnp.concatenate(
            [jnp.where(lane_k == h, qa, 0.0) for h in range(GLA_HEADS)], axis=0).astype(BF16)
        s4 = jnp.where(causal4, _dot_nt(qa4, kb), 0.0)
        pv = _dot(s4.astype(BF16), v)
        acc = jnp.zeros((c_len, hv), F32)
        for h in range(GLA_HEADS):
            acc = acc + jnp.where(lane_v == h, pv[h * c_len:(h + 1) * c_len, :], 0.0)
        return acc

    def slow_intra(q, k, v, cum):
        cum_ref[...] = cum
        kf_ref[...] = k
        vf_ref[...] = v.astype(F32)

        def one_key(s, acc):
            ks = kf_ref[pl.ds(s, 1), :]
            vs = vf_ref[pl.ds(s, 1), :]
            cs = cum_ref[pl.ds(s, 1), :]
            w = q * ks * jnp.exp(jnp.minimum(cum - cs, 0.0))
            wsum = _dot(w.astype(BF16), head_expand)
            return acc + jnp.where(trow >= s, wsum, 0.0) * vs

        return lax.fori_loop(0, c_len, one_key, jnp.zeros((c_len, hv), F32))

    def finish_chunk(rows, q, k, v, cum, intra, st):
        last = cum[c_len - 1:c_len, :]
        qg = (q * jnp.exp(cum)).astype(BF16)
        oacc_ref[rows, :] = intra + _dot_nt(qg, st.astype(BF16))
        kg = (k * jnp.exp(last - cum)).astype(BF16)
        upd = _dot_tn(v, kg)
        return (st * jnp.exp(last) + upd) * state_mask

    @pl.when(fast)
    def _():
        st = state_ref[...]
        for c in range(n_chunks):
            rows = pl.ds(c * c_len, c_len)
            q, k, v, cum = load_chunk(rows)
            st = finish_chunk(rows, q, k, v, cum, fast_intra(q, k, v, cum), st)
        state_ref[...] = st

    @pl.when(jnp.logical_not(fast))
    def _():
        def chunk(c, carry):
            rows = pl.ds(pl.multiple_of(c * c_len, c_len), c_len)
            q, k, v, cum = load_chunk(rows)
            state_ref[...] = finish_chunk(rows, q, k, v, cum, slow_intra(q, k, v, cum), state_ref[...])
            return carry

        lax.fori_loop(0, n_chunks, chunk, 0)

    o = oacc_ref[...]
    vr = lax.broadcasted_iota(jnp.int32, (hv, hv), 0) // GLA_DV
    vc = lax.broadcasted_iota(jnp.int32, (hv, hv), 1) // GLA_DV
    seg = (vr == vc).astype(BF16)
    ms = _seg_sum(o * o, seg) * (1.0 / GLA_DV)
    y = o * lax.rsqrt(ms + HEAD_NORM_EPS) * ng_ref[...]
    g = og_ref[...].astype(F32)
    o_ref[...] = (y * (g * jax.nn.sigmoid(g))).astype(o_ref.dtype)


def _gla(z, w_gate, b_gate, norm_g, batch, seq, tg=512, fast_max_decay=GLA_FAST_MAX_DECAY):
    tg = min(tg, seq)
    nt = seq // tg
    hk = GLA_HEADS * GLA_DK
    hv = GLA_HEADS * GLA_DV

    def rowmap(cb):
        return lambda b, t: (b * nt + t, cb)

    const = lambda b, t: (0, 0)
    return pl.pallas_call(
        functools.partial(_gla_kernel, n_chunks=tg // GLA_CHUNK, fast_max_decay=fast_max_decay),
        grid=(batch, nt),
        in_specs=[pl.BlockSpec((tg, hk), rowmap(_GQ // hk)),
                  pl.BlockSpec((tg, hk), rowmap(_GK // hk)),
                  pl.BlockSpec((tg, hv), rowmap(_GV // hv)),
                  pl.BlockSpec((tg, hv), rowmap(_GO // hv)),
                  pl.BlockSpec((tg, LANE), rowmap(_LR // LANE)),
                  pl.BlockSpec((LANE, hk), const),
                  pl.BlockSpec((1, hk), const),
                  pl.BlockSpec((1, hv), const)],
        out_specs=pl.BlockSpec((tg, hv), lambda b, t: (b * nt + t, 0)),
        out_shape=jax.ShapeDtypeStruct((batch * seq, hv), BF16),
        scratch_shapes=[pltpu.VMEM((hv, hk), F32),
                        pltpu.VMEM((tg, hk), F32),
                        pltpu.VMEM((GLA_CHUNK, hk), F32),
                        pltpu.VMEM((GLA_CHUNK, hk), F32),
                        pltpu.VMEM((GLA_CHUNK, hv), F32),
                        pltpu.VMEM((tg, hv), F32)],
        compiler_params=_cparams(("arbitrary", "arbitrary")),
        name="gla",
    )(z, z, z, z, z, w_gate, b_gate, norm_g)


LOG2E = math.log2(math.e)


def _diff_kernel(far_ref, lam_ref, q_ref, k_ref, v_ref, bias_ref, ng_ref, o_ref,
                 m_ref, acc_ref, *, blk, lam_init):
    qi = pl.program_id(2)
    d = DIFF_D
    w = 2 * DIFF_D
    q = (q_ref[...].astype(F32) * (DIFF_D ** -0.5 * LOG2E)).astype(BF16)
    lane_q = lax.broadcasted_iota(jnp.int32, q.shape, 1)
    q2 = jnp.concatenate([jnp.where(lane_q < d, q, jnp.zeros_like(q)),
                          jnp.where(lane_q >= d, q, jnp.zeros_like(q))], axis=0)
    m_ref[...] = jnp.full_like(m_ref, -jnp.inf)
    acc_ref[...] = jnp.zeros_like(acc_ref)

    def kv_step(j, bias, shift):
        rows = pl.ds(pl.multiple_of(j * blk, blk), blk)
        vj = v_ref[rows, :]
        s_both = _dot_nt(q2, k_ref[rows, :])
        for m in range(2):
            s = s_both[m * blk:(m + 1) * blk, :]
            if bias is not None:
                s = s + bias
            m_cur = jnp.max(s, axis=-1, keepdims=True)
            if shift is not None:
                m_cur = m_cur + shift
            m_old = m_ref[m]
            m_new = jnp.maximum(m_old, m_cur)
            alpha = jnp.exp2(m_old - m_new)
            sub = m_new if shift is None else m_new - shift
            p = jnp.exp2(s - jnp.concatenate([sub] * (blk // LANE), axis=1))
            row_sum = jnp.broadcast_to(jnp.sum(p, axis=-1, keepdims=True), (blk, LANE))
            acc_ref[m] = (jnp.concatenate([alpha, alpha], axis=1) * acc_ref[m]
                          + jnp.concatenate([_dot(p.astype(BF16), vj), row_sum], axis=1))
            m_ref[m] = m_new

    far = far_ref[pl.program_id(1)]

    def far_step(j, carry):
        kv_step(j, None, far)
        return carry

    lax.fori_loop(0, jnp.maximum(qi - 1, 0), far_step, 0)

    @pl.when(qi >= 1)
    def _():
        kv_step(qi - 1, bias_ref[0, 1], None)

    kv_step(qi, bias_ref[0, 0], None)

    lf = lam_ref[...]
    lam = (jnp.exp(jnp.sum(lf[0:1] * lf[1:2], axis=-1, keepdims=True))
           - jnp.exp(jnp.sum(lf[2:3] * lf[3:4], axis=-1, keepdims=True)) + lam_init)
    a0 = acc_ref[0]
    a1 = acc_ref[1]
    o = a0[:, :w] / a0[:, w:] - lam * (a1[:, :w] / a1[:, w:])
    ms = jnp.mean(o * o, axis=-1, keepdims=True)
    y = o * lax.rsqrt(ms + HEAD_NORM_EPS) * ng_ref[...] * (1.0 - lam_init)
    o_ref[...] = y.astype(o_ref.dtype)


def _t5_bucket(rel):
    n = jnp.maximum(-rel, 0)
    max_exact = T5_BUCKETS // 2
    nf = jnp.maximum(n, 1).astype(F32)
    large = max_exact + (jnp.log(nf / max_exact) / math.log(T5_MAX_DIST / max_exact)
                         * (T5_BUCKETS - max_exact)).astype(jnp.int32)
    large = jnp.minimum(large, T5_BUCKETS - 1)
    return jnp.where(n < max_exact, n, large)


def _bias_tiles(rel_bias, blk):
    assert blk + 1 >= T5_MAX_DIST
    m = jnp.arange(2 * blk)
    col_minus_row = jnp.where(m < blk, m, m - 2 * blk)
    tiles = []
    for delta in range(2):
        rel = col_minus_row - delta * blk
        w = rel_bias[_t5_bucket(rel)].astype(F32).T
        if delta == 0:
            w = jnp.where(rel <= 0, w, -jnp.inf)
        flat = jnp.tile(w, (1, blk))[:, :blk * (2 * blk - 1)]
        tiles.append(flat.reshape(-1, blk, 2 * blk - 1)[:, :, :blk])
    far = rel_bias[_t5_bucket(jnp.full((1,), -(blk + 1)))[0]].astype(F32)
    return jnp.stack(tiles, axis=1) * LOG2E, far * LOG2E


def _diff(z, diff_lambda, norm_g, bias, batch, seq, lam_init, blk):
    blk = min(blk, seq)
    nq = seq // blk
    w = 2 * DIFF_D
    bias_tiles, bias_far = bias
    return pl.pallas_call(
        functools.partial(_diff_kernel, blk=blk, lam_init=lam_init),
        grid=(batch, DIFF_HEADS, nq),
        in_specs=[pl.BlockSpec(memory_space=pltpu.SMEM),
                  pl.BlockSpec((4, DIFF_D), lambda b, h, i: (0, 0)),
                  pl.BlockSpec((blk, w), lambda b, h, i: (b * nq + i, _DQ // w + h)),
                  pl.BlockSpec((seq, w), lambda b, h, i: (b, _DK // w + h)),
                  pl.BlockSpec((seq, w), lambda b, h, i: (b, _DV // w + h)),
                  pl.BlockSpec((1, 2, blk, blk), lambda b, h, i: (h, 0, 0, 0)),
                  pl.BlockSpec((1, w), lambda b, h, i: (0, 0))],
        out_specs=pl.BlockSpec((blk, w), lambda b, h, i: (b * nq + i, h)),
        out_shape=jax.ShapeDtypeStruct((batch * seq, DIFF_HEADS * w), BF16),
        scratch_shapes=[pltpu.VMEM((2, blk, LANE), F32),
                        pltpu.VMEM((2, blk, w + LANE), F32)],
        compiler_params=_cparams(("arbitrary", "arbitrary", "arbitrary")),
        name="diff_attn",
    )(bias_far, diff_lambda, z, z, z, bias_tiles, norm_g)


def _ret_kernel(q_ref, k_ref, v_ref, g_ref, sin_ref, cos_ref, inner_ref, cross_ref, sdec_ref, cdec_ref,
                o_ref, state_ref, *, n_chunks):
    c_len = RET_CHUNK
    hk = RET_HEADS * RET_DK
    hv = RET_HEADS * RET_DV

    @pl.when(pl.program_id(1) == 0)
    def _():
        state_ref[...] = jnp.zeros_like(state_ref)

    even = (lax.broadcasted_iota(jnp.int32, (1, hk), 1) % 2) == 0
    lane_k = lax.broadcasted_iota(jnp.int32, (1, hk), 1) // RET_DK
    lane_v = lax.broadcasted_iota(jnp.int32, (1, hv), 1) // RET_DV
    sr = lax.broadcasted_iota(jnp.int32, (hv, hk), 0) // RET_DV
    sc = lax.broadcasted_iota(jnp.int32, (hv, hk), 1) // RET_DK
    vr = lax.broadcasted_iota(jnp.int32, (hv, hv), 0) // RET_DV
    vc = lax.broadcasted_iota(jnp.int32, (hv, hv), 1) // RET_DV
    seg = (vr == vc).astype(BF16)

    st = state_ref[...]
    for c in range(n_chunks):
        rows = pl.ds(c * c_len, c_len)
        sin = sin_ref[rows, :]
        cos = cos_ref[rows, :]

        def rotary(t):
            nxt = pltpu.roll(t, hk - 1, 1)
            prv = pltpu.roll(t, 1, 1)
            return t * cos + jnp.where(even, -nxt, prv) * sin

        qf = rotary(q_ref[rows, :].astype(F32))
        kf = rotary(k_ref[rows, :].astype(F32)) * (RET_DK ** -0.5)
        v = v_ref[rows, :]

        q4 = jnp.concatenate([jnp.where(lane_k == h, qf, 0.0) for h in range(RET_HEADS)], axis=0).astype(BF16)
        s4 = _dot_nt(q4, kf.astype(BF16)) * inner_ref[...]
        pv = _dot(s4.astype(BF16), v)
        acc = jnp.zeros((c_len, hv), F32)
        for h in range(RET_HEADS):
            acc = acc + jnp.where(lane_v == h, pv[h * c_len:(h + 1) * c_len, :], 0.0)

        o = acc + _dot_nt(qf.astype(BF16), st.astype(BF16)) * cross_ref[...]
        ks = (kf * sdec_ref[...]).astype(BF16)
        st = jnp.where(sr == sc, st * cdec_ref[...] + _dot_tn(v, ks), 0.0)

        ms = _seg_sum(o * o, seg) * (1.0 / RET_DV)
        g = g_ref[rows, :].astype(F32)
        o_ref[rows, :] = (o * lax.rsqrt(ms + HEAD_NORM_EPS) * (g * jax.nn.sigmoid(g))).astype(o_ref.dtype)
    state_ref[...] = st


def _ret_tables(seq):
    c_len = RET_CHUNK
    pos = jnp.arange(seq, dtype=F32)
    angle = 1.0 / (10000.0 ** jnp.linspace(0.0, 1.0, RET_DK // 2, dtype=F32))
    angle = jnp.repeat(angle, 2)
    sin = jnp.tile(jnp.sin(pos[:, None] * angle), (1, RET_HEADS))
    cos = jnp.tile(jnp.cos(pos[:, None] * angle), (1, RET_HEADS))
    log_g = jnp.log1p(-jnp.exp2(-5.0 - jnp.arange(RET_HEADS, dtype=F32)))
    idx = jnp.arange(c_len, dtype=F32)
    rel = idx[:, None] - idx[None, :]
    inner = jnp.where(rel[None] >= 0, jnp.exp(jnp.maximum(rel, 0.0)[None] * log_g[:, None, None]), 0.0)
    inner = inner.reshape(RET_HEADS * c_len, c_len)
    cross = jnp.exp((idx + 1.0)[None] * log_g[:, None])
    cross = jnp.repeat(cross.T, RET_DV, axis=1)
    sdec = jnp.exp((c_len - 1.0 - idx)[None] * log_g[:, None])
    sdec = jnp.repeat(sdec.T, RET_DK, axis=1)
    cdec = jnp.repeat(jnp.exp(c_len * log_g), RET_DK)[None, :]
    return sin, cos, inner, cross, sdec, cdec


def _ret(z, tables, batch, seq, tr=512):
    c_len = RET_CHUNK
    tr = min(tr, seq)
    nt = seq // tr
    hk = RET_HEADS * RET_DK
    hv = RET_HEADS * RET_DV
    sin, cos, inner, cross, sdec, cdec = tables

    def rowmap(cb):
        return lambda b, t: (b * nt + t, cb)

    const = lambda b, t: (0, 0)
    return pl.pallas_call(
        functools.partial(_ret_kernel, n_chunks=tr // c_len),
        grid=(batch, nt),
        in_specs=[pl.BlockSpec((tr, hk), rowmap(_RQ // hk)),
                  pl.BlockSpec((tr, hk), rowmap(_RK // hk)),
                  pl.BlockSpec((tr, hv), rowmap(_RV // hv)),
                  pl.BlockSpec((tr, hv), rowmap(_RG // hv)),
                  pl.BlockSpec((tr, hk), lambda b, t: (t, 0)),
                  pl.BlockSpec((tr, hk), lambda b, t: (t, 0)),
                  pl.BlockSpec((RET_HEADS * c_len, c_len), const),
                  pl.BlockSpec((c_len, hv), const),
                  pl.BlockSpec((c_len, hk), const),
                  pl.BlockSpec((1, hk), const)],
        out_specs=pl.BlockSpec((tr, hv), lambda b, t: (b * nt + t, 0)),
        out_shape=jax.ShapeDtypeStruct((batch * seq, hv), BF16),
        scratch_shapes=[pltpu.VMEM((hv, hk), F32)],
        compiler_params=_cparams(("arbitrary", "arbitrary")),
        name="retention",
    )(z, z, z, z, sin, cos, inner, cross, sdec, cdec)


def _layer_norm(t, g, b):
    mu = jnp.mean(t, axis=-1, keepdims=True)
    var = jnp.mean(jnp.square(t - mu), axis=-1, keepdims=True)
    return (t - mu) * lax.rsqrt(var + LN_EPS) * g + b


def _out_ln_kernel(gla_ref, dif_ref, ret_ref, x_ref, wo_ref, g_ref, b_ref, wr_ref, br_ref, earlier_ref,
                   x1_ref, x1b_ref, x1pa_ref, x1pb_ref, gate_ref, idx_ref, cnt_ref, *, alpha):
    w0 = gla_ref.shape[1]
    w1 = w0 + dif_ref.shape[1]
    a = (_dot(gla_ref[...], wo_ref[0:w0, :]) + _dot(dif_ref[...], wo_ref[w0:w1, :])
         + _dot(ret_ref[...], wo_ref[w1:, :]))
    x1 = _layer_norm(alpha * x_ref[...] + a, g_ref[...], b_ref[...])
    x1_ref[...] = x1
    x1b_ref[...] = x1.astype(BF16)
    x1p = _pack_bf16_pairs(x1)
    x1pa_ref[...] = x1p[:, :SC_ROW_WORDS]
    x1pb_ref[...] = x1p[:, SC_ROW_WORDS:]

    hi, lo = _split_hi_lo(x1)
    tm = x1.shape[0]
    parts = _dot(jnp.concatenate([hi, lo], axis=0), wr_ref[...])
    logits = (parts[:tm, :LANE] + parts[tm:, :LANE]) + (parts[:tm, LANE:] + parts[tm:, LANE:]) + br_ref[...]
    lane = lax.broadcasted_iota(jnp.int32, logits.shape, 1)
    work = jnp.where(lane < N_EXPERTS, logits, -jnp.inf)
    vals = jnp.zeros(logits.shape, F32)
    idxs = jnp.zeros(logits.shape, jnp.int32)
    chosen = jnp.zeros(logits.shape, F32)
    picks = []
    top0 = None
    for r in range(TOP_K):
        mx = jnp.max(work, axis=-1, keepdims=True)
        am = jnp.min(jnp.where(work == mx, lane, LANE), axis=-1, keepdims=True)
        if r == 0:
            top0 = mx
        vals = jnp.where(lane == r, jnp.exp(mx - top0), vals)
        idxs = jnp.where(lane == r, am, idxs)
        chosen = jnp.where(lane == am, 1.0, chosen)
        work = jnp.where(lane == am, -jnp.inf, work)
        picks.append(am)
    gate_ref[...] = vals / jnp.sum(vals, axis=-1, keepdims=True)

    @pl.when(pl.program_id(0) == 0)
    def _():
        cnt_ref[...] = jnp.zeros_like(cnt_ref)

    before = _dot(earlier_ref[...], chosen.astype(BF16)) + cnt_ref[...]
    for r in range(TOP_K):
        rank = jnp.sum(jnp.where(lane == picks[r], before, 0.0), axis=-1, keepdims=True)
        idxs = jnp.where(lane == TOP_K + r, rank.astype(jnp.int32), idxs)
    idx_ref[...] = idxs
    cnt_ref[...] = cnt_ref[...] + jnp.sum(chosen, axis=0, keepdims=True)


def _out_ln(gla_o, dif_o, ret_o, x, w_out, ln_g, ln_b, wr, b_router, alpha, tm=512):
    n, d = x.shape
    tm = min(tm, n)
    earlier = jnp.asarray(np.tril(np.ones((tm, tm), np.float32), -1), BF16)
    row = lambda i: (i, 0)
    const = lambda i: (0, 0)
    return pl.pallas_call(
        functools.partial(_out_ln_kernel, alpha=alpha),
        grid=(n // tm,),
        in_specs=[pl.BlockSpec((tm, gla_o.shape[1]), row),
                  pl.BlockSpec((tm, dif_o.shape[1]), row),
                  pl.BlockSpec((tm, ret_o.shape[1]), row),
                  pl.BlockSpec((tm, d), row),
                  pl.BlockSpec(w_out.shape, const),
                  pl.BlockSpec((1, d), const),
                  pl.BlockSpec((1, d), const),
                  pl.BlockSpec((d, 2 * LANE), const),
                  pl.BlockSpec((1, LANE), const),
                  pl.BlockSpec((tm, tm), const)],
        out_specs=[pl.BlockSpec((tm, d), row),
                   pl.BlockSpec((tm, d), row),
                   pl.BlockSpec((tm, SC_ROW_WORDS), row),
                   pl.BlockSpec((tm, SC_ROW_WORDS), row),
                   pl.BlockSpec((tm, LANE), row),
                   pl.BlockSpec((tm, LANE), row),
                   pl.BlockSpec((1, LANE), const)],
        out_shape=[jax.ShapeDtypeStruct((n, d), F32),
                   jax.ShapeDtypeStruct((n, d), BF16),
                   jax.ShapeDtypeStruct((n, SC_ROW_WORDS), jnp.uint32),
                   jax.ShapeDtypeStruct((n, SC_ROW_WORDS), jnp.uint32),
                   jax.ShapeDtypeStruct((n, LANE), F32),
                   jax.ShapeDtypeStruct((n, LANE), jnp.int32),
                   jax.ShapeDtypeStruct((1, LANE), F32)],
        compiler_params=_cparams(("arbitrary",)),
        name="out_ln_router",
    )(gla_o, dif_o, ret_o, x, w_out, ln_g, ln_b, wr, b_router, earlier)


GU_GROUP = 2 * LANE


def _deinterleave_perm():
    perm = np.zeros((GU_GROUP, GU_GROUP), np.float32)
    half = np.arange(LANE)
    perm[2 * half, half] = 1.0
    perm[2 * half + 1, LANE + half] = 1.0
    return jnp.asarray(perm, BF16)


def _moe_kernel(blk_e_ref, n_used_ref, n_valid_ref, xa_ref, xb_ref, wgu_ref, bgu_ref, wdn_ref, bdn_ref, perm_ref,
                oa_ref, ob_ref,
                wgu_b_ref, wdn_b_ref, *, rows_per_cast):
    i = pl.program_id(0)
    used = i < n_used_ref[0]
    new_expert = jnp.logical_or(i == 0, blk_e_ref[i] != blk_e_ref[jnp.maximum(i - 1, 0)])

    @pl.when(jnp.logical_and(used, new_expert))
    def _():
        d = wgu_ref.shape[0]
        for r in range(d // rows_per_cast):
            rows = slice(r * rows_per_cast, (r + 1) * rows_per_cast)
            wdn_b_ref[rows, :] = wdn_ref[rows, :].astype(BF16)
            for g in range(wgu_ref.shape[1] // GU_GROUP):
                cols = slice(g * GU_GROUP, (g + 1) * GU_GROUP)
                wgu_b_ref[rows, cols] = _dot(wgu_ref[rows, cols].astype(BF16), perm_ref[...]).astype(BF16)

    @pl.when(used)
    def _():
        x = _unpack_bf16_pairs(jnp.concatenate([xa_ref[...], xb_ref[...]], axis=1))
        row = lax.broadcasted_iota(jnp.int32, x.shape, 0)
        x = jnp.where(row < n_valid_ref[i], x, 0.0).astype(BF16)
        gu = _dot(x, wgu_b_ref[...]) + bgu_ref[...]
        acts = []
        for grp in range(gu.shape[1] // GU_GROUP):
            g = jnp.minimum(gu[:, grp * GU_GROUP:grp * GU_GROUP + LANE], SWIGLU_LIMIT)
            u = jnp.clip(gu[:, grp * GU_GROUP + LANE:(grp + 1) * GU_GROUP], -SWIGLU_LIMIT, SWIGLU_LIMIT)
            acts.append(((u + 1.0) * (g * jax.nn.sigmoid(g * SWIGLU_ALPHA))).astype(BF16))
        act = jnp.concatenate(acts, axis=1)
        y = _pack_bf16_pairs(_dot(act, wdn_b_ref[...]) + bdn_ref[...])
        oa_ref[...] = y[:, :SC_ROW_WORDS]
        ob_ref[...] = y[:, SC_ROW_WORDS:]

    @pl.when(jnp.logical_not(used))
    def _():
        oa_ref[...] = jnp.zeros_like(oa_ref)
        ob_ref[...] = jnp.zeros_like(ob_ref)


def _moe(xs, blk_e, n_used, n_valid, w_gu, b_gu, w_dn, b_dn, blk):
    rows = xs[0].shape[0]
    assert w_gu.shape[1] == 4 * SC_ROW_WORDS
    d = w_gu.shape[1]
    f2 = w_gu.shape[2]
    f = w_dn.shape[1]
    assert f == d
    grid_spec = pltpu.PrefetchScalarGridSpec(
        num_scalar_prefetch=3,
        grid=(rows // blk,),
        in_specs=[pl.BlockSpec((blk, SC_ROW_WORDS), lambda i, be, nu, nv: (i, 0)),
                  pl.BlockSpec((blk, SC_ROW_WORDS), lambda i, be, nu, nv: (i, 0)),
                  pl.BlockSpec((None, d, f2), lambda i, be, nu, nv: (be[i], 0, 0)),
                  pl.BlockSpec((None, 1, f2), lambda i, be, nu, nv: (be[i], 0, 0)),
                  pl.BlockSpec((None, f, d), lambda i, be, nu, nv: (be[i], 0, 0)),
                  pl.BlockSpec((None, 1, d), lambda i, be, nu, nv: (be[i], 0, 0)),
                  pl.BlockSpec((GU_GROUP, GU_GROUP), lambda i, be, nu, nv: (0, 0))],
        out_specs=[pl.BlockSpec((blk, SC_ROW_WORDS), lambda i, be, nu, nv: (i, 0)),
                   pl.BlockSpec((blk, SC_ROW_WORDS), lambda i, be, nu, nv: (i, 0))],
        scratch_shapes=[pltpu.VMEM((d, f2), BF16), pltpu.VMEM((f, d), BF16)],
    )
    return pl.pallas_call(
        functools.partial(_moe_kernel, rows_per_cast=256),
        grid_spec=grid_spec,
        out_shape=[jax.ShapeDtypeStruct((rows, SC_ROW_WORDS), jnp.uint32)] * 2,
        compiler_params=_cparams(("arbitrary",)),
        name="moe_experts",
    )(blk_e, n_used, n_valid, xs[0], xs[1], w_gu, b_gu, w_dn, b_dn, _deinterleave_perm())


def _combine_kernel(yga_ref, ygb_ref, gate_ref, x1_ref, x1b_ref, p_ref, wg_ref, bg_ref, wp_ref, g_ref, b_ref,
                    x2_ref, x2b_ref, *, alpha):
    gates = gate_ref[...]
    m = jnp.zeros(x1_ref.shape, F32)
    for r in range(TOP_K):
        m = m + _unpack_bf16_pairs(jnp.concatenate([yga_ref[r], ygb_ref[r]], axis=1)) * gates[:, r:r + 1]
    e = (jax.nn.sigmoid(_dot(x1b_ref[...], wg_ref[...]) + bg_ref[...])
         * _dot(p_ref[...].astype(BF16), wp_ref[...]))
    x2 = _layer_norm(alpha * x1_ref[...] + m + e, g_ref[...], b_ref[...])
    x2_ref[...] = x2
    x2b_ref[...] = x2.astype(BF16)


def _combine(yg, gates, x1, x1b, p, w_pg, b_pg, w_pp, ln_g, ln_b, alpha, tm=512):
    n, d = x1.shape
    tm = min(tm, n)
    row = lambda i: (i, 0)
    const = lambda i: (0, 0)
    return pl.pallas_call(
        functools.partial(_combine_kernel, alpha=alpha),
        grid=(n // tm,),
        in_specs=[pl.BlockSpec((TOP_K, tm, SC_ROW_WORDS), lambda i: (0, i, 0)),
                  pl.BlockSpec((TOP_K, tm, SC_ROW_WORDS), lambda i: (0, i, 0)),
                  pl.BlockSpec((tm, LANE), row),
                  pl.BlockSpec((tm, d), row),
                  pl.BlockSpec((tm, d), row),
                  pl.BlockSpec((tm, p.shape[1]), row),
                  pl.BlockSpec(w_pg.shape, const),
                  pl.BlockSpec((1, d), const),
                  pl.BlockSpec(w_pp.shape, const),
                  pl.BlockSpec((1, d), const),
                  pl.BlockSpec((1, d), const)],
        out_specs=[pl.BlockSpec((tm, d), row), pl.BlockSpec((tm, d), row)],
        out_shape=[jax.ShapeDtypeStruct((n, d), F32), jax.ShapeDtypeStruct((n, d), BF16)],
        compiler_params=_cparams(("arbitrary",)),
        name="combine_ln",
    )(yg[0], yg[1], gates, x1, x1b, p, w_pg, b_pg, w_pp, ln_g, ln_b)


def _route(experts, ranks, counts, blk):
    n_tok = experts.shape[0]
    n_assign = n_tok * TOP_K
    sizes = counts.astype(jnp.int32)
    padded = ((sizes + blk - 1) // blk) * blk
    pends = jnp.cumsum(padded)
    pstarts = pends - padded
    onehot = experts[:, :, None] == jnp.arange(N_EXPERTS, dtype=jnp.int32)
    slot = jnp.sum(jnp.where(onehot, pstarts, 0), axis=-1) + ranks
    buf_len = ((n_assign + N_EXPERTS * (blk - 1) + blk - 1) // blk) * blk
    n_blk = buf_len // blk
    first_row = jnp.arange(n_blk, dtype=jnp.int32) * blk
    blk_e = jnp.minimum(jnp.sum(pends[None, :] <= first_row[:, None], axis=1), N_EXPERTS - 1).astype(jnp.int32)
    n_used = (pends[-1] // blk).astype(jnp.int32).reshape(1)
    mine = blk_e[:, None] == jnp.arange(N_EXPERTS, dtype=jnp.int32)[None, :]
    group_rows_left = jnp.sum(jnp.where(mine, pstarts + sizes, 0), axis=1) - first_row
    n_valid = jnp.clip(group_rows_left, 0, blk).astype(jnp.int32)
    return slot, buf_len, blk_e, n_used, n_valid


def _regroup_w_in(w_in):
    depth, d, width = w_in.shape
    lr0 = _GV + GLA_HEADS * GLA_DV
    src = np.full((Z_WIDTH,), -1, np.int32)
    src[:lr0] = np.arange(lr0)
    src[lr0:width - GLA_GATE_RANK] = np.arange(lr0 + GLA_GATE_RANK, width)
    src[_LR:_LR + GLA_GATE_RANK] = np.arange(lr0, lr0 + GLA_GATE_RANK)
    select = (jnp.arange(width, dtype=jnp.int32)[:, None] == jnp.asarray(src)[None, :]).astype(BF16)
    out = _matmul(w_in.reshape(depth * d, width).astype(BF16), select, d, Z_WIDTH // 3, BF16, "regroup_w_in")
    return out.reshape(depth, d, Z_WIDTH)


def kernel(x, p, w_in, w_gla_gate, b_gla_gate, gla_norm_g, diff_lambda, diff_norm_g, w_out, rel_bias, ln1_g, ln1_b, w_router, b_router, w_gate_up, b_gate_up, w_down, b_down, w_ple_gate, b_ple_gate, w_ple_proj, ln2_g, ln2_b):
    batch, seq, d = x.shape
    depth = w_in.shape[0]
    n = batch * seq
    alpha = (2 * depth) ** 0.25
    moe_blk = 512
    diff_blk = min(512, seq)

    w_in_b = _regroup_w_in(w_in)
    wg = jnp.zeros((depth, LANE, GLA_HEADS * GLA_DK), F32).at[:, :GLA_GATE_RANK, :].set(w_gla_gate).astype(BF16)
    gla_g = jnp.tile(gla_norm_g, (1, GLA_HEADS))[:, None, :]
    w_out_b = w_out.astype(BF16)
    wr = jnp.zeros((depth, d, LANE), F32).at[:, :, :N_EXPERTS].set(w_router)
    wr_hi = wr.astype(BF16)
    wr_b = jnp.concatenate([wr_hi, (wr - wr_hi.astype(F32)).astype(BF16)], axis=-1)
    br = jnp.zeros((depth, 1, LANE), F32).at[:, 0, :N_EXPERTS].set(b_router)
    w_gu = w_gate_up.reshape((depth * N_EXPERTS,) + w_gate_up.shape[2:])
    b_gu = b_gate_up.reshape(depth * N_EXPERTS, -1, LANE, 2).swapaxes(-1, -2).reshape(depth * N_EXPERTS, 1, -1)
    w_dn = w_down.reshape((depth * N_EXPERTS,) + w_down.shape[2:])
    b_dn = b_down.reshape(depth * N_EXPERTS, 1, -1)
    w_pg_b = w_ple_gate.astype(BF16)
    w_pp_b = w_ple_proj.astype(BF16)
    bias_tiles = _bias_tiles(rel_bias, diff_blk)
    ret_tables = _ret_tables(seq)

    xf = x.reshape(n, d)
    xb = xf.astype(BF16)
    pf = p.reshape(depth, n, p.shape[-1])
    for i in range(depth):
        lam_init = 0.8 - 0.6 * math.exp(-0.3 * i)
        z = _matmul(xb, w_in_b[i], min(1024, n), Z_WIDTH // 3, BF16, "in_proj")
        gla_o = _gla(z, wg[i], b_gla_gate[i][None, :], gla_g[i], batch, seq)
        dif_o = _diff(z, diff_lambda[i], diff_norm_g[i][None, :], bias_tiles, batch, seq, lam_init, diff_blk)
        ret_o = _ret(z, ret_tables, batch, seq)
        x1, x1b, x1pa, x1pb, gates, idx, cnt = _out_ln(gla_o, dif_o, ret_o, xf, w_out_b[i], ln1_g[i][None, :],
                                                       ln1_b[i][None, :], wr_b[i], br[i], alpha)
        slot, buf_len, blk_e, n_used, n_valid = _route(idx[:, :TOP_K], idx[:, TOP_K:2 * TOP_K], cnt[0, :N_EXPERTS], moe_blk)
        slot_k = slot.T.reshape(-1)
        xs = [_sc_scatter(t, slot_k, buf_len) for t in (x1pa, x1pb)]
        yb = _moe(xs, blk_e + i * N_EXPERTS, n_used, n_valid, w_gu, b_gu, w_dn, b_dn, moe_blk)
        yg = [t.reshape(TOP_K, n, SC_ROW_WORDS) for t in _gather_rows(yb, slot_k)]
        xf, xb = _combine(yg, gates, x1, x1b, pf[i], w_pg_b[i], b_ple_gate[i][None, :], w_pp_b[i],
                          ln2_g[i][None, :], ln2_b[i][None, :], alpha)
    return xf.reshape(batch, seq, d)
```

```python
import functools
import math

import numpy as np
import jax
import jax.numpy as jnp
from jax import lax
from jax.experimental import pallas as pl
from jax.experimental.pallas import tpu as pltpu
from jax.experimental.pallas import tpu_sc as plsc

F32 = jnp.float32
BF16 = jnp.bfloat16

GLA_HEADS = 4
GLA_DK = 32
GLA_DV = 64
GLA_GATE_RANK = 16
GLA_TAU = 16.0
GLA_CHUNK = 64
DIFF_HEADS = 4
DIFF_D = 64
RET_HEADS = 4
RET_DK = 64
RET_DV = 64
RET_CHUNK = 128
T5_BUCKETS = 32
T5_MAX_DIST = 128
N_EXPERTS = 32
TOP_K = 4
SWIGLU_LIMIT = 7.0
SWIGLU_ALPHA = 1.702
LN_EPS = 1e-5
HEAD_NORM_EPS = 1e-5

LANE = 128
VMEM_LIMIT = 56 * 1024 * 1024

_GQ, _GK, _GV, _GO = 0, 128, 256, 512
_DQ, _DK, _DV = 768, 1280, 1792
_RQ, _RK, _RV, _RG = 2304, 2560, 2816, 3072
_LR = 3328
Z_WIDTH = 3456

GLA_FAST_MAX_DECAY = 60.0


def _cparams(sem):
    return pltpu.CompilerParams(dimension_semantics=sem, vmem_limit_bytes=VMEM_LIMIT)


def _split_hi_lo(a):
    hi = a.astype(BF16)
    lo = (a - hi.astype(F32)).astype(BF16)
    return hi, lo


def _dot(a, b):
    return jnp.dot(a, b, preferred_element_type=F32)


def _dot_nt(a, b):
    return lax.dot_general(a, b, (((1,), (1,)), ((), ())), preferred_element_type=F32)


def _dot_tn(a, b):
    return lax.dot_general(a, b, (((0,), (0,)), ((), ())), preferred_element_type=F32)


def _seg_sum(a, ind):
    hi, lo = _split_hi_lo(a)
    return _dot(hi, ind) + _dot(lo, ind)


def _pack_bf16_pairs(t):
    h = t.shape[1] // 2
    bits = lax.bitcast_convert_type(t.astype(BF16).astype(F32), jnp.uint32)
    return (bits[:, :h] >> 16) | (bits[:, h:] & jnp.uint32(0xFFFF0000))


def _unpack_bf16_pairs(packed):
    lo = lax.bitcast_convert_type(packed << 16, F32)
    hi = lax.bitcast_convert_type(packed & jnp.uint32(0xFFFF0000), F32)
    return jnp.concatenate([lo, hi], axis=1)


SC_WINDOW = 128
SC_ROW_WORDS = 256


def _sc_gather(table, indices):
    m = indices.shape[0]
    words = table.shape[1]
    assert words == SC_ROW_WORDS and m % SC_WINDOW == 0
    mesh = plsc.VectorSubcoreMesh(core_axis_name="core", subcore_axis_name="subcore")

    def body(x_hbm, i_hbm, o_hbm):
        def step(i_vmem, o_vmem):
            pltpu.sync_copy(x_hbm.at[i_vmem.at[0]], o_vmem)

        pltpu.emit_pipeline(
            step,
            grid=(m // SC_WINDOW,),
            in_specs=[pl.BlockSpec((1, SC_WINDOW), index_map=lambda i: (0, i))],
            out_specs=[pl.BlockSpec((SC_WINDOW, words), index_map=lambda i: (i, 0))],
            core_axis_name=("core", "subcore"),
            dimension_semantics=(pltpu.PARALLEL,),
        )(i_hbm, o_hbm)

    return pl.kernel(body, out_type=jax.ShapeDtypeStruct((m, words), table.dtype), mesh=mesh,
                     scratch_types=[])(table, indices.reshape(1, m))


def _sc_scatter(table, indices, out_rows):
    r, words = table.shape
    m = indices.shape[0]
    assert words == SC_ROW_WORDS and r % SC_WINDOW == 0 and m % r == 0
    mesh = plsc.VectorSubcoreMesh(core_axis_name="core", subcore_axis_name="subcore")
    steps_per_pass = r // SC_WINDOW

    def body(x_hbm, i_hbm, o_hbm):
        def step(x_vmem, i_vmem):
            pltpu.sync_copy(x_vmem, o_hbm.at[i_vmem.at[0]])

        pltpu.emit_pipeline(
            step,
            grid=(m // SC_WINDOW,),
            in_specs=[pl.BlockSpec((SC_WINDOW, words), index_map=lambda i: (i % steps_per_pass, 0)),
                      pl.BlockSpec((1, SC_WINDOW), index_map=lambda i: (0, i))],
            out_specs=[],
            core_axis_name=("core", "subcore"),
            dimension_semantics=(pltpu.PARALLEL,),
        )(x_hbm, i_hbm)

    return pl.kernel(body, out_type=jax.ShapeDtypeStruct((out_rows, words), table.dtype), mesh=mesh,
                     scratch_types=[])(table, indices.reshape(1, m))


def _gather_rows(tables, rows):
    return [_sc_gather(t, rows) for t in tables]


def _mm_kernel(x_ref, w_ref, o_ref):
    o_ref[...] = _dot(x_ref[...], w_ref[...]).astype(o_ref.dtype)


def _matmul(x, w, tm, tn, out_dtype, name):
    m, k = x.shape
    n = w.shape[1]
    return pl.pallas_call(
        _mm_kernel,
        grid=(n // tn, m // tm),
        in_specs=[pl.BlockSpec((tm, k), lambda j, i: (i, 0)),
                  pl.BlockSpec((k, tn), lambda j, i: (0, j))],
        out_specs=pl.BlockSpec((tm, tn), lambda j, i: (i, j)),
        out_shape=jax.ShapeDtypeStruct((m, n), out_dtype),
        compiler_params=_cparams(("arbitrary", "arbitrary")),
        name=name,
    )(x, w)


def _gla_kernel(q_ref, k_ref, v_ref, og_ref, lr_ref, wg_ref, bg_ref, ng_ref, o_ref,
                state_ref, la_ref, cum_ref, kf_ref, vf_ref, oacc_ref, *, n_chunks, fast_max_decay):
    c_len = GLA_CHUNK
    hk = GLA_HEADS * GLA_DK
    hv = GLA_HEADS * GLA_DV

    @pl.when(pl.program_id(1) == 0)
    def _():
        state_ref[...] = jnp.zeros_like(state_ref)

    zg = _dot(lr_ref[...], wg_ref[...]) + bg_ref[...]
    la = (jnp.minimum(zg, 0.0) - jnp.log1p(jnp.exp(-jnp.abs(zg)))) / GLA_TAU
    la_ref[...] = la
    chunk_tot = jnp.sum(la.reshape(n_chunks, c_len, hk), axis=1)
    fast = jnp.min(chunk_tot) > -fast_max_decay

    row = lax.broadcasted_iota(jnp.int32, (c_len, c_len), 0)
    col = lax.broadcasted_iota(jnp.int32, (c_len, c_len), 1)
    tril = (row >= col).astype(BF16)
    sr = lax.broadcasted_iota(jnp.int32, (hv, hk), 0)
    sc = lax.broadcasted_iota(jnp.int32, (hv, hk), 1)
    state_mask = ((sr // GLA_DV) == (sc // GLA_DK)).astype(F32)
    lane_k = lax.broadcasted_iota(jnp.int32, (1, hk), 1) // GLA_DK
    lane_v = lax.broadcasted_iota(jnp.int32, (1, hv), 1) // GLA_DV
    r4 = lax.broadcasted_iota(jnp.int32, (GLA_HEADS * c_len, c_len), 0) % c_len
    c4 = lax.broadcasted_iota(jnp.int32, (GLA_HEADS * c_len, c_len), 1)
    causal4 = r4 >= c4
    er = lax.broadcasted_iota(jnp.int32, (hk, hv), 0) // GLA_DK
    ec = lax.broadcasted_iota(jnp.int32, (hk, hv), 1) // GLA_DV
    head_expand = (er == ec).astype(BF16)
    trow = lax.broadcasted_iota(jnp.int32, (c_len, hv), 0)

    def load_chunk(rows):
        q = q_ref[rows, :].astype(F32) * (GLA_DK ** -0.5)
        k = k_ref[rows, :].astype(F32)
        v = v_ref[rows, :]
        la_hi, la_lo = _split_hi_lo(la_ref[rows, :])
        cum = _dot(tril, la_hi) + _dot(tril, la_lo)
        return q, k, v, cum

    def fast_intra(q, k, v, cum):
        mid = cum[c_len // 2:c_len // 2 + 1, :]
        qa = q * jnp.exp(cum - mid)
        kb = (k * jnp.exp(mid - cum)).astype(BF16)
        qa4 = jnp.concatenate(
            [jnp.where(lane_k == h, qa, 0.0) for h in range(GLA_HEADS)], axis=0).astype(BF16)
        s4 = jnp.where(causal4, _dot_nt(qa4, kb), 0.0)
        pv = _dot(s4.astype(BF16), v)
        acc = jnp.zeros((c_len, hv), F32)
        for h in range(GLA_HEADS):
            acc = acc + jnp.where(lane_v == h, pv[h * c_len:(h + 1) * c_len, :], 0.0)
        return acc

    def slow_intra(q, k, v, cum):
        cum_ref[...] = cum
        kf_ref[...] = k
        vf_ref[...] = v.astype(F32)

        def one_key(s, acc):
            ks = kf_ref[pl.ds(s, 1), :]
            vs = vf_ref[pl.ds(s, 1), :]
            cs = cum_ref[pl.ds(s, 1), :]
            w = q * ks * jnp.exp(jnp.minimum(cum - cs, 0.0))
            wsum = _dot(w.astype(BF16), head_expand)
            return acc + jnp.where(trow >= s, wsum, 0.0) * vs

        return lax.fori_loop(0, c_len, one_key, jnp.zeros((c_len, hv), F32))

    def finish_chunk(rows, q, k, v, cum, intra, st):
        last = cum[c_len - 1:c_len, :]
        qg = (q * jnp.exp(cum)).astype(BF16)
        oacc_ref[rows, :] = intra + _dot_nt(qg, st.astype(BF16))
        kg = (k * jnp.exp(last - cum)).astype(BF16)
        upd = _dot_tn(v, kg)
        return (st * jnp.exp(last) + upd) * state_mask

    @pl.when(fast)
    def _():
        st = state_ref[...]
        for c in range(n_chunks):
            rows = pl.ds(c * c_len, c_len)
            q, k, v, cum = load_chunk(rows)
            st = finish_chunk(rows, q, k, v, cum, fast_intra(q, k, v, cum), st)
        state_ref[...] = st

    @pl.when(jnp.logical_not(fast))
    def _():
        def chunk(c, carry):
            rows = pl.ds(pl.multiple_of(c * c_len, c_len), c_len)
            q, k, v, cum = load_chunk(rows)
            state_ref[...] = finish_chunk(rows, q, k, v, cum, slow_intra(q, k, v, cum), state_ref[...])
            return carry

        lax.fori_loop(0, n_chunks, chunk, 0)

    o = oacc_ref[...]
    vr = lax.broadcasted_iota(jnp.int32, (hv, hv), 0) // GLA_DV
    vc = lax.broadcasted_iota(jnp.int32, (hv, hv), 1) // GLA_DV
    seg = (vr == vc).astype(BF16)
    ms = _seg_sum(o * o, seg) * (1.0 / GLA_DV)
    y = o * lax.rsqrt(ms + HEAD_NORM_EPS) * ng_ref[...]
    g = og_ref[...].astype(F32)
    o_ref[...] = (y * (g * jax.nn.sigmoid(g))).astype(o_ref.dtype)


def _gla(z, w_gate, b_gate, norm_g, batch, seq, tg=1024, fast_max_decay=GLA_FAST_MAX_DECAY):
    tg = min(tg, seq)
    nt = seq // tg
    hk = GLA_HEADS * GLA_DK
    hv = GLA_HEADS * GLA_DV

    def rowmap(cb):
        return lambda b, t: (b * nt + t, cb)

    const = lambda b, t: (0, 0)
    return pl.pallas_call(
        functools.partial(_gla_kernel, n_chunks=tg // GLA_CHUNK, fast_max_decay=fast_max_decay),
        grid=(batch, nt),
        in_specs=[pl.BlockSpec((tg, hk), rowmap(_GQ // hk)),
                  pl.BlockSpec((tg, hk), rowmap(_GK // hk)),
                  pl.BlockSpec((tg, hv), rowmap(_GV // hv)),
                  pl.BlockSpec((tg, hv), rowmap(_GO // hv)),
                  pl.BlockSpec((tg, LANE), rowmap(_LR // LANE)),
                  pl.BlockSpec((LANE, hk), const),
                  pl.BlockSpec((1, hk), const),
                  pl.BlockSpec((1, hv), const)],
        out_specs=pl.BlockSpec((tg, hv), lambda b, t: (b * nt + t, 0)),
        out_shape=jax.ShapeDtypeStruct((batch * seq, hv), BF16),
        scratch_shapes=[pltpu.VMEM((hv, hk), F32),
                        pltpu.VMEM((tg, hk), F32),
                        pltpu.VMEM((GLA_CHUNK, hk), F32),
                        pltpu.VMEM((GLA_CHUNK, hk), F32),
                        pltpu.VMEM((GLA_CHUNK, hv), F32),
                        pltpu.VMEM((tg, hv), F32)],
        compiler_params=_cparams(("arbitrary", "arbitrary")),
        name="gla",
    )(z, z, z, z, z, w_gate, b_gate, norm_g)


LOG2E = math.log2(math.e)


def _diff_kernel(far_ref, lam_ref, q_ref, k_ref, v_ref, bias_ref, ng_ref, o_ref,
                 m_ref, acc_ref, *, blk, lam_init):
    qi = pl.program_id(2)
    d = DIFF_D
    w = 2 * DIFF_D
    q = (q_ref[...].astype(F32) * (DIFF_D ** -0.5 * LOG2E)).astype(BF16)
    lane_q = lax.broadcasted_iota(jnp.int32, q.shape, 1)
    q2 = jnp.concatenate([jnp.where(lane_q < d, q, jnp.zeros_like(q)),
                          jnp.where(lane_q >= d, q, jnp.zeros_like(q))], axis=0)
    m_ref[...] = jnp.full_like(m_ref, -jnp.inf)
    acc_ref[...] = jnp.zeros_like(acc_ref)

    def kv_step(j, bias, shift):
        rows = pl.ds(pl.multiple_of(j * blk, blk), blk)
        vj = v_ref[rows, :]
        s_both = _dot_nt(q2, k_ref[rows, :])
        for m in range(2):
            s = s_both[m * blk:(m + 1) * blk, :]
            if bias is not None:
                s = s + bias
            m_cur = jnp.max(s, axis=-1, keepdims=True)
            if shift is not None:
                m_cur = m_cur + shift
            m_old = m_ref[m]
            m_new = jnp.maximum(m_old, m_cur)
            alpha = jnp.exp2(m_old - m_new)
            sub = m_new if shift is None else m_new - shift
            p = jnp.exp2(s - jnp.concatenate([sub] * (blk // LANE), axis=1))
            row_sum = jnp.broadcast_to(jnp.sum(p, axis=-1, keepdims=True), (blk, LANE))
            acc_ref[m] = (jnp.concatenate([alpha, alpha], axis=1) * acc_ref[m]
                          + jnp.concatenate([_dot(p.astype(BF16), vj), row_sum], axis=1))
            m_ref[m] = m_new

    far = far_ref[pl.program_id(1)]

    def far_step(j, carry):
        kv_step(j, None, far)
        return carry

    lax.fori_loop(0, jnp.maximum(qi - 1, 0), far_step, 0)

    @pl.when(qi >= 1)
    def _():
        kv_step(qi - 1, bias_ref[0, 1], None)

    kv_step(qi, bias_ref[0, 0], None)

    lf = lam_ref[...]
    lam = (jnp.exp(jnp.sum(lf[0:1] * lf[1:2], axis=-1, keepdims=True))
           - jnp.exp(jnp.sum(lf[2:3] * lf[3:4], axis=-1, keepdims=True)) + lam_init)
    a0 = acc_ref[0]
    a1 = acc_ref[1]
    o = a0[:, :w] / a0[:, w:] - lam * (a1[:, :w] / a1[:, w:])
    ms = jnp.mean(o * o, axis=-1, keepdims=True)
    y = o * lax.rsqrt(ms + HEAD_NORM_EPS) * ng_ref[...] * (1.0 - lam_init)
    o_ref[...] = y.astype(o_ref.dtype)


def _t5_bucket(rel):
    n = jnp.maximum(-rel, 0)
    max_exact = T5_BUCKETS // 2
    nf = jnp.maximum(n, 1).astype(F32)
    large = max_exact + (jnp.log(nf / max_exact) / math.log(T5_MAX_DIST / max_exact)
                         * (T5_BUCKETS - max_exact)).astype(jnp.int32)
    large = jnp.minimum(large, T5_BUCKETS - 1)
    return jnp.where(n < max_exact, n, large)


def _bias_tiles(rel_bias, blk):
    assert blk + 1 >= T5_MAX_DIST
    m = jnp.arange(2 * blk)
    col_minus_row = jnp.where(m < blk, m, m - 2 * blk)
    tiles = []
    for delta in range(2):
        rel = col_minus_row - delta * blk
        w = rel_bias[_t5_bucket(rel)].astype(F32).T
        if delta == 0:
            w = jnp.where(rel <= 0, w, -jnp.inf)
        flat = jnp.tile(w, (1, blk))[:, :blk * (2 * blk - 1)]
        tiles.append(flat.reshape(-1, blk, 2 * blk - 1)[:, :, :blk])
    far = rel_bias[_t5_bucket(jnp.full((1,), -(blk + 1)))[0]].astype(F32)
    return jnp.stack(tiles, axis=1) * LOG2E, far * LOG2E


def _diff(z, diff_lambda, norm_g, bias, batch, seq, lam_init, blk):
    blk = min(blk, seq)
    nq = seq // blk
    w = 2 * DIFF_D
    bias_tiles, bias_far = bias
    return pl.pallas_call(
        functools.partial(_diff_kernel, blk=blk, lam_init=lam_init),
        grid=(batch, DIFF_HEADS, nq),
        in_specs=[pl.BlockSpec(memory_space=pltpu.SMEM),
                  pl.BlockSpec((4, DIFF_D), lambda b, h, i: (0, 0)),
                  pl.BlockSpec((blk, w), lambda b, h, i: (b * nq + i, _DQ // w + h)),
                  pl.BlockSpec((seq, w), lambda b, h, i: (b, _DK // w + h)),
                  pl.BlockSpec((seq, w), lambda b, h, i: (b, _DV // w + h)),
                  pl.BlockSpec((1, 2, blk, blk), lambda b, h, i: (h, 0, 0, 0)),
                  pl.BlockSpec((1, w), lambda b, h, i: (0, 0))],
        out_specs=pl.BlockSpec((blk, w), lambda b, h, i: (b * nq + i, h)),
        out_shape=jax.ShapeDtypeStruct((batch * seq, DIFF_HEADS * w), BF16),
        scratch_shapes=[pltpu.VMEM((2, blk, LANE), F32),
                        pltpu.VMEM((2, blk, w + LANE), F32)],
        compiler_params=_cparams(("arbitrary", "arbitrary", "arbitrary")),
        name="diff_attn",
    )(bias_far, diff_lambda, z, z, z, bias_tiles, norm_g)


def _ret_kernel(q_ref, k_ref, v_ref, g_ref, sin_ref, cos_ref, inner_ref, cross_ref, sdec_ref, cdec_ref,
                o_ref, state_ref, *, n_chunks):
    c_len = RET_CHUNK
    hk = RET_HEADS * RET_DK
    hv = RET_HEADS * RET_DV

    @pl.when(pl.program_id(1) == 0)
    def _():
        state_ref[...] = jnp.zeros_like(state_ref)

    even = (lax.broadcasted_iota(jnp.int32, (1, hk), 1) % 2) == 0
    lane_k = lax.broadcasted_iota(jnp.int32, (1, hk), 1) // RET_DK
    lane_v = lax.broadcasted_iota(jnp.int32, (1, hv), 1) // RET_DV
    sr = lax.broadcasted_iota(jnp.int32, (hv, hk), 0) // RET_DV
    sc = lax.broadcasted_iota(jnp.int32, (hv, hk), 1) // RET_DK
    vr = lax.broadcasted_iota(jnp.int32, (hv, hv), 0) // RET_DV
    vc = lax.broadcasted_iota(jnp.int32, (hv, hv), 1) // RET_DV
    seg = (vr == vc).astype(BF16)

    st = state_ref[...]
    for c in range(n_chunks):
        rows = pl.ds(c * c_len, c_len)
        sin = sin_ref[rows, :]
        cos = cos_ref[rows, :]

        def rotary(t):
            nxt = pltpu.roll(t, hk - 1, 1)
            prv = pltpu.roll(t, 1, 1)
            return t * cos + jnp.where(even, -nxt, prv) * sin

        qf = rotary(q_ref[rows, :].astype(F32))
        kf = rotary(k_ref[rows, :].astype(F32)) * (RET_DK ** -0.5)
        v = v_ref[rows, :]

        q4 = jnp.concatenate([jnp.where(lane_k == h, qf, 0.0) for h in range(RET_HEADS)], axis=0).astype(BF16)
        s4 = _dot_nt(q4, kf.astype(BF16)) * inner_ref[...]
        pv = _dot(s4.astype(BF16), v)
        acc = jnp.zeros((c_len, hv), F32)
        for h in range(RET_HEADS):
            acc = acc + jnp.where(lane_v == h, pv[h * c_len:(h + 1) * c_len, :], 0.0)

        o = acc + _dot_nt(qf.astype(BF16), st.astype(BF16)) * cross_ref[...]
        ks = (kf * sdec_ref[...]).astype(BF16)
        st = jnp.where(sr == sc, st * cdec_ref[...] + _dot_tn(v, ks), 0.0)

        ms = _seg_sum(o * o, seg) * (1.0 / RET_DV)
        g = g_ref[rows, :].astype(F32)
        o_ref[rows, :] = (o * lax.rsqrt(ms + HEAD_NORM_EPS) * (g * jax.nn.sigmoid(g))).astype(o_ref.dtype)
    state_ref[...] = st


def _ret_tables(seq):
    c_len = RET_CHUNK
    pos = jnp.arange(seq, dtype=F32)
    angle = 1.0 / (10000.0 ** jnp.linspace(0.0, 1.0, RET_DK // 2, dtype=F32))
    angle = jnp.repeat(angle, 2)
    sin = jnp.tile(jnp.sin(pos[:, None] * angle), (1, RET_HEADS))
    cos = jnp.tile(jnp.cos(pos[:, None] * angle), (1, RET_HEADS))
    log_g = jnp.log1p(-jnp.exp2(-5.0 - jnp.arange(RET_HEADS, dtype=F32)))
    idx = jnp.arange(c_len, dtype=F32)
    rel = idx[:, None] - idx[None, :]
    inner = jnp.where(rel[None] >= 0, jnp.exp(jnp.maximum(rel, 0.0)[None] * log_g[:, None, None]), 0.0)
    inner = inner.reshape(RET_HEADS * c_len, c_len)
    cross = jnp.exp((idx + 1.0)[None] * log_g[:, None])
    cross = jnp.repeat(cross.T, RET_DV, axis=1)
    sdec = jnp.exp((c_len - 1.0 - idx)[None] * log_g[:, None])
    sdec = jnp.repeat(sdec.T, RET_DK, axis=1)
    cdec = jnp.repeat(jnp.exp(c_len * log_g), RET_DK)[None, :]
    return sin, cos, inner, cross, sdec, cdec


def _ret(z, tables, batch, seq, tr=512):
    c_len = RET_CHUNK
    tr = min(tr, seq)
    nt = seq // tr
    hk = RET_HEADS * RET_DK
    hv = RET_HEADS * RET_DV
    sin, cos, inner, cross, sdec, cdec = tables

    def rowmap(cb):
        return lambda b, t: (b * nt + t, cb)

    const = lambda b, t: (0, 0)
    return pl.pallas_call(
        functools.partial(_ret_kernel, n_chunks=tr // c_len),
        grid=(batch, nt),
        in_specs=[pl.BlockSpec((tr, hk), rowmap(_RQ // hk)),
                  pl.BlockSpec((tr, hk), rowmap(_RK // hk)),
                  pl.BlockSpec((tr, hv), rowmap(_RV // hv)),
                  pl.BlockSpec((tr, hv), rowmap(_RG // hv)),
                  pl.BlockSpec((tr, hk), lambda b, t: (t, 0)),
                  pl.BlockSpec((tr, hk), lambda b, t: (t, 0)),
                  pl.BlockSpec((RET_HEADS * c_len, c_len), const),
                  pl.BlockSpec((c_len, hv), const),
                  pl.BlockSpec((c_len, hk), const),
                  pl.BlockSpec((1, hk), const)],
        out_specs=pl.BlockSpec((tr, hv), lambda b, t: (b * nt + t, 0)),
        out_shape=jax.ShapeDtypeStruct((batch * seq, hv), BF16),
        scratch_shapes=[pltpu.VMEM((hv, hk), F32)],
        compiler_params=_cparams(("arbitrary", "arbitrary")),
        name="retention",
    )(z, z, z, z, sin, cos, inner, cross, sdec, cdec)


def _layer_norm(t, g, b):
    mu = jnp.mean(t, axis=-1, keepdims=True)
    var = jnp.mean(jnp.square(t - mu), axis=-1, keepdims=True)
    return (t - mu) * lax.rsqrt(var + LN_EPS) * g + b


def _out_ln_kernel(gla_ref, dif_ref, ret_ref, x_ref, wo_ref, g_ref, b_ref, wr_ref, br_ref, earlier_ref,
                   x1_ref, x1b_ref, x1pa_ref, x1pb_ref, gate_ref, idx_ref, cnt_ref, *, alpha):
    w0 = gla_ref.shape[1]
    w1 = w0 + dif_ref.shape[1]
    a = (_dot(gla_ref[...], wo_ref[0:w0, :]) + _dot(dif_ref[...], wo_ref[w0:w1, :])
         + _dot(ret_ref[...], wo_ref[w1:, :]))
    x1 = _layer_norm(alpha * x_ref[...] + a, g_ref[...], b_ref[...])
    x1_ref[...] = x1
    x1b_ref[...] = x1.astype(BF16)
    x1p = _pack_bf16_pairs(x1)
    x1pa_ref[...] = x1p[:, :SC_ROW_WORDS]
    x1pb_ref[...] = x1p[:, SC_ROW_WORDS:]

    hi, lo = _split_hi_lo(x1)
    tm = x1.shape[0]
    parts = _dot(jnp.concatenate([hi, lo], axis=0), wr_ref[...])
    logits = (parts[:tm, :LANE] + parts[tm:, :LANE]) + (parts[:tm, LANE:] + parts[tm:, LANE:]) + br_ref[...]
    lane = lax.broadcasted_iota(jnp.int32, logits.shape, 1)
    work = jnp.where(lane < N_EXPERTS, logits, -jnp.inf)
    vals = jnp.zeros(logits.shape, F32)
    idxs = jnp.zeros(logits.shape, jnp.int32)
    chosen = jnp.zeros(logits.shape, F32)
    picks = []
    top0 = None
    for r in range(TOP_K):
        mx = jnp.max(work, axis=-1, keepdims=True)
        am = jnp.min(jnp.where(work == mx, lane, LANE), axis=-1, keepdims=True)
        if r == 0:
            top0 = mx
        vals = jnp.where(lane == r, jnp.exp(mx - top0), vals)
        idxs = jnp.where(lane == r, am, idxs)
        chosen = jnp.where(lane == am, 1.0, chosen)
        work = jnp.where(lane == am, -jnp.inf, work)
        picks.append(am)
    gate_ref[...] = vals / jnp.sum(vals, axis=-1, keepdims=True)

    @pl.when(pl.program_id(0) == 0)
    def _():
        cnt_ref[...] = jnp.zeros_like(cnt_ref)

    before = _dot(earlier_ref[...], chosen.astype(BF16)) + cnt_ref[...]
    for r in range(TOP_K):
        rank = jnp.sum(jnp.where(lane == picks[r], before, 0.0), axis=-1, keepdims=True)
        idxs = jnp.where(lane == TOP_K + r, rank.astype(jnp.int32), idxs)
    idx_ref[...] = idxs
    cnt_ref[...] = cnt_ref[...] + jnp.sum(chosen, axis=0, keepdims=True)


def _out_ln(gla_o, dif_o, ret_o, x, w_out, ln_g, ln_b, wr, b_router, alpha, tm=512):
    n, d = x.shape
    tm = min(tm, n)
    earlier = jnp.asarray(np.tril(np.ones((tm, tm), np.float32), -1), BF16)
    row = lambda i: (i, 0)
    const = lambda i: (0, 0)
    return pl.pallas_call(
        functools.partial(_out_ln_kernel, alpha=alpha),
        grid=(n // tm,),
        in_specs=[pl.BlockSpec((tm, gla_o.shape[1]), row),
                  pl.BlockSpec((tm, dif_o.shape[1]), row),
                  pl.BlockSpec((tm, ret_o.shape[1]), row),
                  pl.BlockSpec((tm, d), row),
                  pl.BlockSpec(w_out.shape, const),
                  pl.BlockSpec((1, d), const),
                  pl.BlockSpec((1, d), const),
                  pl.BlockSpec((d, 2 * LANE), const),
                  pl.BlockSpec((1, LANE), const),
                  pl.BlockSpec((tm, tm), const)],
        out_specs=[pl.BlockSpec((tm, d), row),
                   pl.BlockSpec((tm, d), row),
                   pl.BlockSpec((tm, SC_ROW_WORDS), row),
                   pl.BlockSpec((tm, SC_ROW_WORDS), row),
                   pl.BlockSpec((tm, LANE), row),
                   pl.BlockSpec((tm, LANE), row),
                   pl.BlockSpec((1, LANE), const)],
        out_shape=[jax.ShapeDtypeStruct((n, d), F32),
                   jax.ShapeDtypeStruct((n, d), BF16),
                   jax.ShapeDtypeStruct((n, SC_ROW_WORDS), jnp.uint32),
                   jax.ShapeDtypeStruct((n, SC_ROW_WORDS), jnp.uint32),
                   jax.ShapeDtypeStruct((n, LANE), F32),
                   jax.ShapeDtypeStruct((n, LANE), jnp.int32),
                   jax.ShapeDtypeStruct((1, LANE), F32)],
        compiler_params=_cparams(("arbitrary",)),
        name="out_ln_router",
    )(gla_o, dif_o, ret_o, x, w_out, ln_g, ln_b, wr, b_router, earlier)


GU_GROUP = 2 * LANE


def _deinterleave_perm():
    perm = np.zeros((GU_GROUP, GU_GROUP), np.float32)
    half = np.arange(LANE)
    perm[2 * half, half] = 1.0
    perm[2 * half + 1, LANE + half] = 1.0
    return jnp.asarray(perm, BF16)


def _moe_kernel(blk_e_ref, n_used_ref, n_valid_ref, xa_ref, xb_ref, wgu_ref, bgu_ref, wdn_ref, bdn_ref, perm_ref,
                oa_ref, ob_ref,
                wgu_b_ref, wdn_b_ref, *, rows_per_cast):
    i = pl.program_id(0)
    used = i < n_used_ref[0]
    new_expert = jnp.logical_or(i == 0, blk_e_ref[i] != blk_e_ref[jnp.maximum(i - 1, 0)])

    @pl.when(jnp.logical_and(used, new_expert))
    def _():
        d = wgu_ref.shape[0]
        for r in range(d // rows_per_cast):
            rows = slice(r * rows_per_cast, (r + 1) * rows_per_cast)
            wdn_b_ref[rows, :] = wdn_ref[rows, :].astype(BF16)
            for g in range(wgu_ref.shape[1] // GU_GROUP):
                cols = slice(g * GU_GROUP, (g + 1) * GU_GROUP)
                wgu_b_ref[rows, cols] = _dot(wgu_ref[rows, cols].astype(BF16), perm_ref[...]).astype(BF16)

    @pl.when(used)
    def _():
        x = _unpack_bf16_pairs(jnp.concatenate([xa_ref[...], xb_ref[...]], axis=1))
        row = lax.broadcasted_iota(jnp.int32, x.shape, 0)
        x = jnp.where(row < n_valid_ref[i], x, 0.0).astype(BF16)
        gu = _dot(x, wgu_b_ref[...]) + bgu_ref[...]
        acts = []
        for grp in range(gu.shape[1] // GU_GROUP):
            g = jnp.minimum(gu[:, grp * GU_GROUP:grp * GU_GROUP + LANE], SWIGLU_LIMIT)
            u = jnp.clip(gu[:, grp * GU_GROUP + LANE:(grp + 1) * GU_GROUP], -SWIGLU_LIMIT, SWIGLU_LIMIT)
            acts.append(((u + 1.0) * (g * jax.nn.sigmoid(g * SWIGLU_ALPHA))).astype(BF16))
        act = jnp.concatenate(acts, axis=1)
        y = _pack_bf16_pairs(_dot(act, wdn_b_ref[...]) + bdn_ref[...])
        oa_ref[...] = y[:, :SC_ROW_WORDS]
        ob_ref[...] = y[:, SC_ROW_WORDS:]

    @pl.when(jnp.logical_not(used))
    def _():
        oa_ref[...] = jnp.zeros_like(oa_ref)
        ob_ref[...] = jnp.zeros_like(ob_ref)


def _moe(xs, blk_e, n_used, n_valid, w_gu, b_gu, w_dn, b_dn, blk):
    rows = xs[0].shape[0]
    assert w_gu.shape[1] == 4 * SC_ROW_WORDS
    d = w_gu.shape[1]
    f2 = w_gu.shape[2]
    f = w_dn.shape[1]
    assert f == d
    grid_spec = pltpu.PrefetchScalarGridSpec(
        num_scalar_prefetch=3,
        grid=(rows // blk,),
        in_specs=[pl.BlockSpec((blk, SC_ROW_WORDS), lambda i, be, nu, nv: (i, 0)),
                  pl.BlockSpec((blk, SC_ROW_WORDS), lambda i, be, nu, nv: (i, 0)),
                  pl.BlockSpec((None, d, f2), lambda i, be, nu, nv: (be[i], 0, 0)),
                  pl.BlockSpec((None, 1, f2), lambda i, be, nu, nv: (be[i], 0, 0)),
                  pl.BlockSpec((None, f, d), lambda i, be, nu, nv: (be[i], 0, 0)),
                  pl.BlockSpec((None, 1, d), lambda i, be, nu, nv: (be[i], 0, 0)),
                  pl.BlockSpec((GU_GROUP, GU_GROUP), lambda i, be, nu, nv: (0, 0))],
        out_specs=[pl.BlockSpec((blk, SC_ROW_WORDS), lambda i, be, nu, nv: (i, 0)),
                   pl.BlockSpec((blk, SC_ROW_WORDS), lambda i, be, nu, nv: (i, 0))],
        scratch_shapes=[pltpu.VMEM((d, f2), BF16), pltpu.VMEM((f, d), BF16)],
    )
    return pl.pallas_call(
        functools.partial(_moe_kernel, rows_per_cast=256),
        grid_spec=grid_spec,
        out_shape=[jax.ShapeDtypeStruct((rows, SC_ROW_WORDS), jnp.uint32)] * 2,
        compiler_params=_cparams(("arbitrary",)),
        name="moe_experts",
    )(blk_e, n_used, n_valid, xs[0], xs[1], w_gu, b_gu, w_dn, b_dn, _deinterleave_perm())


def _combine_kernel(yga_ref, ygb_ref, gate_ref, x1_ref, x1b_ref, p_ref, wg_ref, bg_ref, wp_ref, g_ref, b_ref,
                    x2_ref, x2b_ref, *, alpha):
    gates = gate_ref[...]
    m = jnp.zeros(x1_ref.shape, F32)
    for r in range(TOP_K):
        m = m + _unpack_bf16_pairs(jnp.concatenate([yga_ref[r], ygb_ref[r]], axis=1)) * gates[:, r:r + 1]
    e = (jax.nn.sigmoid(_dot(x1b_ref[...], wg_ref[...]) + bg_ref[...])
         * _dot(p_ref[...].astype(BF16), wp_ref[...]))
    x2 = _layer_norm(alpha * x1_ref[...] + m + e, g_ref[...], b_ref[...])
    x2_ref[...] = x2
    x2b_ref[...] = x2.astype(BF16)


def _combine(yg, gates, x1, x1b, p, w_pg, b_pg, w_pp, ln_g, ln_b, alpha, tm=512):
    n, d = x1.shape
    tm = min(tm, n)
    row = lambda i: (i, 0)
    const = lambda i: (0, 0)
    return pl.pallas_call(
        functools.partial(_combine_kernel, alpha=alpha),
        grid=(n // tm,),
        in_specs=[pl.BlockSpec((TOP_K, tm, SC_ROW_WORDS), lambda i: (0, i, 0)),
                  pl.BlockSpec((TOP_K, tm, SC_ROW_WORDS), lambda i: (0, i, 0)),
                  pl.BlockSpec((tm, LANE), row),
                  pl.BlockSpec((tm, d), row),
                  pl.BlockSpec((tm, d), row),
                  pl.BlockSpec((tm, p.shape[1]), row),
                  pl.BlockSpec(w_pg.shape, const),
                  pl.BlockSpec((1, d), const),
                  pl.BlockSpec(w_pp.shape, const),
                  pl.BlockSpec((1, d), const),
                  pl.BlockSpec((1, d), const)],
        out_specs=[pl.BlockSpec((tm, d), row), pl.BlockSpec((tm, d), row)],
        out_shape=[jax.ShapeDtypeStruct((n, d), F32), jax.ShapeDtypeStruct((n, d), BF16)],
        compiler_params=_cparams(("arbitrary",)),
        name="combine_ln",
    )(yg[0], yg[1], gates, x1, x1b, p, w_pg, b_pg, w_pp, ln_g, ln_b)


def _route(experts, ranks, counts, blk):
    n_tok = experts.shape[0]
    n_assign = n_tok * TOP_K
    sizes = counts.astype(jnp.int32)
    padded = ((sizes + blk - 1) // blk) * blk
    pends = jnp.cumsum(padded)
    pstarts = pends - padded
    onehot = experts[:, :, None] == jnp.arange(N_EXPERTS, dtype=jnp.int32)
    slot = jnp.sum(jnp.where(onehot, pstarts, 0), axis=-1) + ranks
    buf_len = ((n_assign + N_EXPERTS * (blk - 1) + blk - 1) // blk) * blk
    n_blk = buf_len // blk
    first_row = jnp.arange(n_blk, dtype=jnp.int32) * blk
    blk_e = jnp.minimum(jnp.sum(pends[None, :] <= first_row[:, None], axis=1), N_EXPERTS - 1).astype(jnp.int32)
    n_used = (pends[-1] // blk).astype(jnp.int32).reshape(1)
    mine = blk_e[:, None] == jnp.arange(N_EXPERTS, dtype=jnp.int32)[None, :]
    group_rows_left = jnp.sum(jnp.where(mine, pstarts + sizes, 0), axis=1) - first_row
    n_valid = jnp.clip(group_rows_left, 0, blk).astype(jnp.int32)
    return slot, buf_len, blk_e, n_used, n_valid


def _regroup_w_in(w_in):
    depth, d, width = w_in.shape
    lr0 = _GV + GLA_HEADS * GLA_DV
    src = np.full((Z_WIDTH,), -1, np.int32)
    src[:lr0] = np.arange(lr0)
    src[lr0:width - GLA_GATE_RANK] = np.arange(lr0 + GLA_GATE_RANK, width)
    src[_LR:_LR + GLA_GATE_RANK] = np.arange(lr0, lr0 + GLA_GATE_RANK)
    select = (jnp.arange(width, dtype=jnp.int32)[:, None] == jnp.asarray(src)[None, :]).astype(BF16)
    out = _matmul(w_in.reshape(depth * d, width).astype(BF16), select, d, Z_WIDTH // 3, BF16, "regroup_w_in")
    return out.reshape(depth, d, Z_WIDTH)


def kernel(x, p, w_in, w_gla_gate, b_gla_gate, gla_norm_g, diff_lambda, diff_norm_g, w_out, rel_bias, ln1_g, ln1_b, w_router, b_router, w_gate_up, b_gate_up, w_down, b_down, w_ple_gate, b_ple_gate, w_ple_proj, ln2_g, ln2_b):
    batch, seq, d = x.shape
    depth = w_in.shape[0]
    n = batch * seq
    alpha = (2 * depth) ** 0.25
    moe_blk = 512
    diff_blk = min(512, seq)

    w_in_b = _regroup_w_in(w_in)
    wg = jnp.zeros((depth, LANE, GLA_HEADS * GLA_DK), F32).at[:, :GLA_GATE_RANK, :].set(w_gla_gate).astype(BF16)
    gla_g = jnp.tile(gla_norm_g, (1, GLA_HEADS))[:, None, :]
    w_out_b = w_out.astype(BF16)
    wr = jnp.zeros((depth, d, LANE), F32).at[:, :, :N_EXPERTS].set(w_router)
    wr_hi = wr.astype(BF16)
    wr_b = jnp.concatenate([wr_hi, (wr - wr_hi.astype(F32)).astype(BF16)], axis=-1)
    br = jnp.zeros((depth, 1, LANE), F32).at[:, 0, :N_EXPERTS].set(b_router)
    w_gu = w_gate_up.reshape((depth * N_EXPERTS,) + w_gate_up.shape[2:])
    b_gu = b_gate_up.reshape(depth * N_EXPERTS, -1, LANE, 2).swapaxes(-1, -2).reshape(depth * N_EXPERTS, 1, -1)
    w_dn = w_down.reshape((depth * N_EXPERTS,) + w_down.shape[2:])
    b_dn = b_down.reshape(depth * N_EXPERTS, 1, -1)
    w_pg_b = w_ple_gate.astype(BF16)
    w_pp_b = w_ple_proj.astype(BF16)
    bias_tiles = _bias_tiles(rel_bias, diff_blk)
    ret_tables = _ret_tables(seq)

    xf = x.reshape(n, d)
    xb = xf.astype(BF16)
    pf = p.reshape(depth, n, p.shape[-1])
    for i in range(depth):
        lam_init = 0.8 - 0.6 * math.exp(-0.3 * i)
        z = _matmul(xb, w_in_b[i], min(2048, n), Z_WIDTH // 3, BF16, "in_proj")
        gla_o = _gla(z, wg[i], b_gla_gate[i][None, :], gla_g[i], batch, seq)
        dif_o = _diff(z, diff_lambda[i], diff_norm_g[i][None, :], bias_tiles, batch, seq, lam_init, diff_blk)
        ret_o = _ret(z, ret_tables, batch, seq)
        x1, x1b, x1pa, x1pb, gates, idx, cnt = _out_ln(gla_o, dif_o, ret_o, xf, w_out_b[i], ln1_g[i][None, :],
                                                       ln1_b[i][None, :], wr_b[i], br[i], alpha)
        slot, buf_len, blk_e, n_used, n_valid = _route(idx[:, :TOP_K], idx[:, TOP_K:2 * TOP_K], cnt[0, :N_EXPERTS], moe_blk)
        slot_k = slot.T.reshape(-1)
        xs = [_sc_scatter(t, slot_k, buf_len) for t in (x1pa, x1pb)]
        yb = _moe(xs, blk_e + i * N_EXPERTS, n_used, n_valid, w_gu, b_gu, w_dn, b_dn, moe_blk)
        yg = [t.reshape(TOP_K, n, SC_ROW_WORDS) for t in _gather_rows(yb, slot_k)]
        xf, xb = _combine(yg, gates, x1, x1b, pf[i], w_pg_b[i], b_ple_gate[i][None, :], w_pp_b[i],
                          ln2_g[i][None, :], ln2_b[i][None, :], alpha)
    return xf.reshape(batch, seq, d)
```

```python
import functools
import math

import numpy as np
import jax
import jax.numpy as jnp
from jax import lax
from jax.experimental import pallas as pl
from jax.experimental.pallas import tpu as pltpu
from jax.experimental.pallas import tpu_sc as plsc

F32 = jnp.float32
BF16 = jnp.bfloat16

GLA_HEADS = 4
GLA_DK = 32
GLA_DV = 64
GLA_GATE_RANK = 16
GLA_TAU = 16.0
GLA_CHUNK = 64
DIFF_HEADS = 4
DIFF_D = 64
RET_HEADS = 4
RET_DK = 64
RET_DV = 64
RET_CHUNK = 128
T5_BUCKETS = 32
T5_MAX_DIST = 128
N_EXPERTS = 32
TOP_K = 4
SWIGLU_LIMIT = 7.0
SWIGLU_ALPHA = 1.702
LN_EPS = 1e-5
HEAD_NORM_EPS = 1e-5

LANE = 128
VMEM_LIMIT = 56 * 1024 * 1024

_GQ, _GK, _GV, _GO = 0, 128, 256, 512
_DQ, _DK, _DV = 768, 1280, 1792
_RQ, _RK, _RV, _RG = 2304, 2560, 2816, 3072
_LR = 3328
Z_WIDTH = 3456

GLA_FAST_MAX_DECAY = 60.0


def _cparams(sem):
    return pltpu.CompilerParams(dimension_semantics=sem, vmem_limit_bytes=VMEM_LIMIT)


def _split_hi_lo(a):
    hi = a.astype(BF16)
    lo = (a - hi.astype(F32)).astype(BF16)
    return hi, lo


def _dot(a, b):
    return jnp.dot(a, b, preferred_element_type=F32)


def _dot_nt(a, b):
    return lax.dot_general(a, b, (((1,), (1,)), ((), ())), preferred_element_type=F32)


def _dot_tn(a, b):
    return lax.dot_general(a, b, (((0,), (0,)), ((), ())), preferred_element_type=F32)


def _seg_sum(a, ind):
    hi, lo = _split_hi_lo(a)
    return _dot(hi, ind) + _dot(lo, ind)


def _pack_bf16_pairs(t):
    h = t.shape[1] // 2
    bits = lax.bitcast_convert_type(t.astype(BF16).astype(F32), jnp.uint32)
    return (bits[:, :h] >> 16) | (bits[:, h:] & jnp.uint32(0xFFFF0000))


def _unpack_bf16_pairs(packed):
    lo = lax.bitcast_convert_type(packed << 16, F32)
    hi = lax.bitcast_convert_type(packed & jnp.uint32(0xFFFF0000), F32)
    return jnp.concatenate([lo, hi], axis=1)


SC_WINDOW = 128
SC_ROW_WORDS = 256


def _sc_gather(table, indices):
    m = indices.shape[0]
    words = table.shape[1]
    assert words == SC_ROW_WORDS and m % SC_WINDOW == 0
    mesh = plsc.VectorSubcoreMesh(core_axis_name="core", subcore_axis_name="subcore")

    def body(x_hbm, i_hbm, o_hbm):
        def step(i_vmem, o_vmem):
            pltpu.sync_copy(x_hbm.at[i_vmem.at[0]], o_vmem)

        pltpu.emit_pipeline(
            step,
            grid=(m // SC_WINDOW,),
            in_specs=[pl.BlockSpec((1, SC_WINDOW), index_map=lambda i: (0, i))],
            out_specs=[pl.BlockSpec((SC_WINDOW, words), index_map=lambda i: (i, 0))],
            core_axis_name=("core", "subcore"),
            dimension_semantics=(pltpu.PARALLEL,),
        )(i_hbm, o_hbm)

    return pl.kernel(body, out_type=jax.ShapeDtypeStruct((m, words), table.dtype), mesh=mesh,
                     scratch_types=[])(table, indices.reshape(1, m))


def _sc_scatter(table, indices, out_rows):
    r, words = table.shape
    m = indices.shape[0]
    assert words == SC_ROW_WORDS and r % SC_WINDOW == 0 and m % r == 0
    mesh = plsc.VectorSubcoreMesh(core_axis_name="core", subcore_axis_name="subcore")
    steps_per_pass = r // SC_WINDOW

    def body(x_hbm, i_hbm, o_hbm):
        def step(x_vmem, i_vmem):
            pltpu.sync_copy(x_vmem, o_hbm.at[i_vmem.at[0]])

        pltpu.emit_pipeline(
            step,
            grid=(m // SC_WINDOW,),
            in_specs=[pl.BlockSpec((SC_WINDOW, words), index_map=lambda i: (i % steps_per_pass, 0)),
                      pl.BlockSpec((1, SC_WINDOW), index_map=lambda i: (0, i))],
            out_specs=[],
            core_axis_name=("core", "subcore"),
            dimension_semantics=(pltpu.PARALLEL,),
        )(x_hbm, i_hbm)

    return pl.kernel(body, out_type=jax.ShapeDtypeStruct((out_rows, words), table.dtype), mesh=mesh,
                     scratch_types=[])(table, indices.reshape(1, m))


def _gather_rows(tables, rows):
    return [_sc_gather(t, rows) for t in tables]


def _mm_kernel(x_ref, w_ref, o_ref):
    o_ref[...] = _dot(x_ref[...], w_ref[...]).astype(o_ref.dtype)


def _matmul(x, w, tm, tn, out_dtype, name):
    m, k = x.shape
    n = w.shape[1]
    return pl.pallas_call(
        _mm_kernel,
        grid=(n // tn, m // tm),
        in_specs=[pl.BlockSpec((tm, k), lambda j, i: (i, 0)),
                  pl.BlockSpec((k, tn), lambda j, i: (0, j))],
        out_specs=pl.BlockSpec((tm, tn), lambda j, i: (i, j)),
        out_shape=jax.ShapeDtypeStruct((m, n), out_dtype),
        compiler_params=_cparams(("arbitrary", "arbitrary")),
        name=name,
    )(x, w)


def _gla_kernel(q_ref, k_ref, v_ref, og_ref, lr_ref, wg_ref, bg_ref, ng_ref, o_ref,
                state_ref, la_ref, cum_ref, kf_ref, vf_ref, oacc_ref, *, n_chunks, fast_max_decay):
    c_len = GLA_CHUNK
    hk = GLA_HEADS * GLA_DK
    hv = GLA_HEADS * GLA_DV

    @pl.when(pl.program_id(1) == 0)
    def _():
        state_ref[...] = jnp.zeros_like(state_ref)

    zg = _dot(lr_ref[...], wg_ref[...]) + bg_ref[...]
    la = (jnp.minimum(zg, 0.0) - jnp.log1p(jnp.exp(-jnp.abs(zg)))) / GLA_TAU
    la_ref[...] = la
    chunk_tot = jnp.sum(la.reshape(n_chunks, c_len, hk), axis=1)
    fast = jnp.min(chunk_tot) > -fast_max_decay

    row = lax.broadcasted_iota(jnp.int32, (c_len, c_len), 0)
    col = lax.broadcasted_iota(jnp.int32, (c_len, c_len), 1)
    tril = (row >= col).astype(BF16)
    sr = lax.broadcasted_iota(jnp.int32, (hv, hk), 0)
    sc = lax.broadcasted_iota(jnp.int32, (hv, hk), 1)
    state_mask = ((sr // GLA_DV) == (sc // GLA_DK)).astype(F32)
    lane_k = lax.broadcasted_iota(jnp.int32, (1, hk), 1) // GLA_DK
    lane_v = lax.broadcasted_iota(jnp.int32, (1, hv), 1) // GLA_DV
    r4 = lax.broadcasted_iota(jnp.int32, (GLA_HEADS * c_len, c_len), 0) % c_len
    c4 = lax.broadcasted_iota(jnp.int32, (GLA_HEADS * c_len, c_len), 1)
    causal4 = r4 >= c4
    er = lax.broadcasted_iota(jnp.int32, (hk, hv), 0) // GLA_DK
    ec = lax.broadcasted_iota(jnp.int32, (hk, hv), 1) // GLA_DV
    head_expand = (er == ec).astype(BF16)
    trow = lax.broadcasted_iota(jnp.int32, (c_len, hv), 0)

    def load_chunk(rows):
        q = q_ref[rows, :].astype(F32) * (GLA_DK ** -0.5)
        k = k_ref[rows, :].astype(F32)
        v = v_ref[rows, :]
        la_hi, la_lo = _split_hi_lo(la_ref[rows, :])
        cum = _dot(tril, la_hi) + _dot(tril, la_lo)
        return q, k, v, cum

    def fast_intra(q, k, v, cum):
        mid = cum[c_len // 2:c_len // 2 + 1, :]
        qa = q * jnp.exp(cum - mid)
        kb = (k * jnp.exp(mid - cum)).astype(BF16)
        qa4 = jnp.concatenate(
            [jnp.where(lane_k == h, qa, 0.0) for h in range(GLA_HEADS)], axis=0).astype(BF16)
        s4 = jnp.where(causal4, _dot_nt(qa4, kb), 0.0)
        pv = _dot(s4.astype(BF16), v)
        acc = jnp.zeros((c_len, hv), F32)
        for h in range(GLA_HEADS):
            acc = acc + jnp.where(lane_v == h, pv[h * c_len:(h + 1) * c_len, :], 0.0)
        return acc

    def slow_intra(q, k, v, cum):
        cum_ref[...] = cum
        kf_ref[...] = k
        vf_ref[...] = v.astype(F32)

        def one_key(s, acc):
            ks = kf_ref[pl.ds(s, 1), :]
            vs = vf_ref[pl.ds(s, 1), :]
            cs = cum_ref[pl.ds(s, 1), :]
            w = q * ks * jnp.exp(jnp.minimum(cum - cs, 0.0))
            wsum = _dot(w.astype(BF16), head_expand)
            return acc + jnp.where(trow >= s, wsum, 0.0) * vs

        return lax.fori_loop(0, c_len, one_key, jnp.zeros((c_len, hv), F32))

    def finish_chunk(rows, q, k, v, cum, intra, st):
        last = cum[c_len - 1:c_len, :]
        qg = (q * jnp.exp(cum)).astype(BF16)
        oacc_ref[rows, :] = intra + _dot_nt(qg, st.astype(BF16))
        kg = (k * jnp.exp(last - cum)).astype(BF16)
        upd = _dot_tn(v, kg)
        return (st * jnp.exp(last) + upd) * state_mask

    @pl.when(fast)
    def _():
        st = state_ref[...]
        for c in range(n_chunks):
            rows = pl.ds(c * c_len, c_len)
            q, k, v, cum = load_chunk(rows)
            st = finish_chunk(rows, q, k, v, cum, fast_intra(q, k, v, cum), st)
        state_ref[...] = st

    @pl.when(jnp.logical_not(fast))
    def _():
        def chunk(c, carry):
            rows = pl.ds(pl.multiple_of(c * c_len, c_len), c_len)
            q, k, v, cum = load_chunk(rows)
            state_ref[...] = finish_chunk(rows, q, k, v, cum, slow_intra(q, k, v, cum), state_ref[...])
            return carry

        lax.fori_loop(0, n_chunks, chunk, 0)

    o = oacc_ref[...]
    vr = lax.broadcasted_iota(jnp.int32, (hv, hv), 0) // GLA_DV
    vc = lax.broadcasted_iota(jnp.int32, (hv, hv), 1) // GLA_DV
    seg = (vr == vc).astype(BF16)
    ms = _seg_sum(o * o, seg) * (1.0 / GLA_DV)
    y = o * lax.rsqrt(ms + HEAD_NORM_EPS) * ng_ref[...]
    g = og_ref[...].astype(F32)
    o_ref[...] = (y * (g * jax.nn.sigmoid(g))).astype(o_ref.dtype)


def _gla(z, w_gate, b_gate, norm_g, batch, seq, tg=1024, fast_max_decay=GLA_FAST_MAX_DECAY):
    tg = min(tg, seq)
    nt = seq // tg
    hk = GLA_HEADS * GLA_DK
    hv = GLA_HEADS * GLA_DV

    def rowmap(cb):
        return lambda b, t: (b * nt + t, cb)

    const = lambda b, t: (0, 0)
    return pl.pallas_call(
        functools.partial(_gla_kernel, n_chunks=tg // GLA_CHUNK, fast_max_decay=fast_max_decay),
        grid=(batch, nt),
        in_specs=[pl.BlockSpec((tg, hk), rowmap(_GQ // hk)),
                  pl.BlockSpec((tg, hk), rowmap(_GK // hk)),
                  pl.BlockSpec((tg, hv), rowmap(_GV // hv)),
                  pl.BlockSpec((tg, hv), rowmap(_GO // hv)),
                  pl.BlockSpec((tg, LANE), rowmap(_LR // LANE)),
                  pl.BlockSpec((LANE, hk), const),
                  pl.BlockSpec((1, hk), const),
                  pl.BlockSpec((1, hv), const)],
        out_specs=pl.BlockSpec((tg, hv), lambda b, t: (b * nt + t, 0)),
        out_shape=jax.ShapeDtypeStruct((batch * seq, hv), BF16),
        scratch_shapes=[pltpu.VMEM((hv, hk), F32),
                        pltpu.VMEM((tg, hk), F32),
                        pltpu.VMEM((GLA_CHUNK, hk), F32),
                        pltpu.VMEM((GLA_CHUNK, hk), F32),
                        pltpu.VMEM((GLA_CHUNK, hv), F32),
                        pltpu.VMEM((tg, hv), F32)],
        compiler_params=_cparams(("arbitrary", "arbitrary")),
        name="gla",
    )(z, z, z, z, z, w_gate, b_gate, norm_g)


LOG2E = math.log2(math.e)


def _diff_kernel(far_ref, lam_ref, q_ref, k_ref, v_ref, bias_ref, ng_ref, o_ref,
                 m_ref, acc_ref, *, blk, lam_init):
    qi = pl.program_id(2)
    d = DIFF_D
    w = 2 * DIFF_D
    q = (q_ref[...].astype(F32) * (DIFF_D ** -0.5 * LOG2E)).astype(BF16)
    lane_q = lax.broadcasted_iota(jnp.int32, q.shape, 1)
    q2 = jnp.concatenate([jnp.where(lane_q < d, q, jnp.zeros_like(q)),
                          jnp.where(lane_q >= d, q, jnp.zeros_like(q))], axis=0)
    m_ref[...] = jnp.full_like(m_ref, -jnp.inf)
    acc_ref[...] = jnp.zeros_like(acc_ref)

    def kv_step(j, bias, shift):
        rows = pl.ds(pl.multiple_of(j * blk, blk), blk)
        vj = v_ref[rows, :]
        s_both = _dot_nt(q2, k_ref[rows, :])
        for m in range(2):
            s = s_both[m * blk:(m + 1) * blk, :]
            if bias is not None:
                s = s + bias
            m_cur = jnp.max(s, axis=-1, keepdims=True)
            if shift is not None:
                m_cur = m_cur + shift
            m_old = m_ref[m]
            m_new = jnp.maximum(m_old, m_cur)
            alpha = jnp.exp2(m_old - m_new)
            sub = m_new if shift is None else m_new - shift
            p = jnp.exp2(s - jnp.concatenate([sub] * (blk // LANE), axis=1))
            row_sum = jnp.broadcast_to(jnp.sum(p, axis=-1, keepdims=True), (blk, LANE))
            acc_ref[m] = (jnp.concatenate([alpha, alpha], axis=1) * acc_ref[m]
                          + jnp.concatenate([_dot(p.astype(BF16), vj), row_sum], axis=1))
            m_ref[m] = m_new

    far = far_ref[pl.program_id(1)]

    def far_step(j, carry):
        kv_step(j, None, far)
        return carry

    lax.fori_loop(0, jnp.maximum(qi - 1, 0), far_step, 0)

    @pl.when(qi >= 1)
    def _():
        kv_step(qi - 1, bias_ref[0, 1], None)

    kv_step(qi, bias_ref[0, 0], None)

    lf = lam_ref[...]
    lam = (jnp.exp(jnp.sum(lf[0:1] * lf[1:2], axis=-1, keepdims=True))
           - jnp.exp(jnp.sum(lf[2:3] * lf[3:4], axis=-1, keepdims=True)) + lam_init)
    a0 = acc_ref[0]
    a1 = acc_ref[1]
    o = a0[:, :w] / a0[:, w:] - lam * (a1[:, :w] / a1[:, w:])
    ms = jnp.mean(o * o, axis=-1, keepdims=True)
    y = o * lax.rsqrt(ms + HEAD_NORM_EPS) * ng_ref[...] * (1.0 - lam_init)
    o_ref[...] = y.astype(o_ref.dtype)


def _t5_bucket(rel):
    n = jnp.maximum(-rel, 0)
    max_exact = T5_BUCKETS // 2
    nf = jnp.maximum(n, 1).astype(F32)
    large = max_exact + (jnp.log(nf / max_exact) / math.log(T5_MAX_DIST / max_exact)
                         * (T5_BUCKETS - max_exact)).astype(jnp.int32)
    large = jnp.minimum(large, T5_BUCKETS - 1)
    return jnp.where(n < max_exact, n, large)


def _bias_tiles(rel_bias, blk):
    assert blk + 1 >= T5_MAX_DIST
    m = jnp.arange(2 * blk)
    col_minus_row = jnp.where(m < blk, m, m - 2 * blk)
    tiles = []
    for delta in range(2):
        rel = col_minus_row - delta * blk
        w = rel_bias[_t5_bucket(rel)].astype(F32).T
        if delta == 0:
            w = jnp.where(rel <= 0, w, -jnp.inf)
        flat = jnp.tile(w, (1, blk))[:, :blk * (2 * blk - 1)]
        tiles.append(flat.reshape(-1, blk, 2 * blk - 1)[:, :, :blk])
    far = rel_bias[_t5_bucket(jnp.full((1,), -(blk + 1)))[0]].astype(F32)
    return jnp.stack(tiles, axis=1) * LOG2E, far * LOG2E


def _diff(z, diff_lambda, norm_g, bias, batch, seq, lam_init, blk):
    blk = min(blk, seq)
    nq = seq // blk
    w = 2 * DIFF_D
    bias_tiles, bias_far = bias
    return pl.pallas_call(
        functools.partial(_diff_kernel, blk=blk, lam_init=lam_init),
        grid=(batch, DIFF_HEADS, nq),
        in_specs=[pl.BlockSpec(memory_space=pltpu.SMEM),
                  pl.BlockSpec((4, DIFF_D), lambda b, h, i: (0, 0)),
                  pl.BlockSpec((blk, w), lambda b, h, i: (b * nq + i, _DQ // w + h)),
                  pl.BlockSpec((seq, w), lambda b, h, i: (b, _DK // w + h)),
                  pl.BlockSpec((seq, w), lambda b, h, i: (b, _DV // w + h)),
                  pl.BlockSpec((1, 2, blk, blk), lambda b, h, i: (h, 0, 0, 0)),
                  pl.BlockSpec((1, w), lambda b, h, i: (0, 0))],
        out_specs=pl.BlockSpec((blk, w), lambda b, h, i: (b * nq + i, h)),
        out_shape=jax.ShapeDtypeStruct((batch * seq, DIFF_HEADS * w), BF16),
        scratch_shapes=[pltpu.VMEM((2, blk, LANE), F32),
                        pltpu.VMEM((2, blk, w + LANE), F32)],
        compiler_params=_cparams(("arbitrary", "arbitrary", "arbitrary")),
        name="diff_attn",
    )(bias_far, diff_lambda, z, z, z, bias_tiles, norm_g)


def _ret_kernel(q_ref, k_ref, v_ref, g_ref, sin_ref, cos_ref, inner_ref, cross_ref, sdec_ref, cdec_ref,
                o_ref, state_ref, *, n_chunks):
    c_len = RET_CHUNK
    hk = RET_HEADS * RET_DK
    hv = RET_HEADS * RET_DV

    @pl.when(pl.program_id(1) == 0)
    def _():
        state_ref[...] = jnp.zeros_like(state_ref)

    even = (lax.broadcasted_iota(jnp.int32, (1, hk), 1) % 2) == 0
    lane_k = lax.broadcasted_iota(jnp.int32, (1, hk), 1) // RET_DK
    lane_v = lax.broadcasted_iota(jnp.int32, (1, hv), 1) // RET_DV
    sr = lax.broadcasted_iota(jnp.int32, (hv, hk), 0) // RET_DV
    sc = lax.broadcasted_iota(jnp.int32, (hv, hk), 1) // RET_DK
    vr = lax.broadcasted_iota(jnp.int32, (hv, hv), 0) // RET_DV
    vc = lax.broadcasted_iota(jnp.int32, (hv, hv), 1) // RET_DV
    seg = (vr == vc).astype(BF16)

    st = state_ref[...]
    for c in range(n_chunks):
        rows = pl.ds(c * c_len, c_len)
        sin = sin_ref[rows, :]
        cos = cos_ref[rows, :]

        def rotary(t):
            nxt = pltpu.roll(t, hk - 1, 1)
            prv = pltpu.roll(t, 1, 1)
            return t * cos + jnp.where(even, -nxt, prv) * sin

        qf = rotary(q_ref[rows, :].astype(F32))
        kf = rotary(k_ref[rows, :].astype(F32)) * (RET_DK ** -0.5)
        v = v_ref[rows, :]

        q4 = jnp.concatenate([jnp.where(lane_k == h, qf, 0.0) for h in range(RET_HEADS)], axis=0).astype(BF16)
        s4 = _dot_nt(q4, kf.astype(BF16)) * inner_ref[...]
        pv = _dot(s4.astype(BF16), v)
        acc = jnp.zeros((c_len, hv), F32)
        for h in range(RET_HEADS):
            acc = acc + jnp.where(lane_v == h, pv[h * c_len:(h + 1) * c_len, :], 0.0)

        o = acc + _dot_nt(qf.astype(BF16), st.astype(BF16)) * cross_ref[...]
        ks = (kf * sdec_ref[...]).astype(BF16)
        st = jnp.where(sr == sc, st * cdec_ref[...] + _dot_tn(v, ks), 0.0)

        ms = _seg_sum(o * o, seg) * (1.0 / RET_DV)
        g = g_ref[rows, :].astype(F32)
        o_ref[rows, :] = (o * lax.rsqrt(ms + HEAD_NORM_EPS) * (g * jax.nn.sigmoid(g))).astype(o_ref.dtype)
    state_ref[...] = st


def _ret_tables(seq):
    c_len = RET_CHUNK
    pos = jnp.arange(seq, dtype=F32)
    angle = 1.0 / (10000.0 ** jnp.linspace(0.0, 1.0, RET_DK // 2, dtype=F32))
    angle = jnp.repeat(angle, 2)
    sin = jnp.tile(jnp.sin(pos[:, None] * angle), (1, RET_HEADS))
    cos = jnp.tile(jnp.cos(pos[:, None] * angle), (1, RET_HEADS))
    log_g = jnp.log1p(-jnp.exp2(-5.0 - jnp.arange(RET_HEADS, dtype=F32)))
    idx = jnp.arange(c_len, dtype=F32)
    rel = idx[:, None] - idx[None, :]
    inner = jnp.where(rel[None] >= 0, jnp.exp(jnp.maximum(rel, 0.0)[None] * log_g[:, None, None]), 0.0)
    inner = inner.reshape(RET_HEADS * c_len, c_len)
    cross = jnp.exp((idx + 1.0)[None] * log_g[:, None])
    cross = jnp.repeat(cross.T, RET_DV, axis=1)
    sdec = jnp.exp((c_len - 1.0 - idx)[None] * log_g[:, None])
    sdec = jnp.repeat(sdec.T, RET_DK, axis=1)
    cdec = jnp.repeat(jnp.exp(c_len * log_g), RET_DK)[None, :]
    return sin, cos, inner, cross, sdec, cdec


def _ret(z, tables, batch, seq, tr=512):
    c_len = RET_CHUNK
    tr = min(tr, seq)
    nt = seq // tr
    hk = RET_HEADS * RET_DK
    hv = RET_HEADS * RET_DV
    sin, cos, inner, cross, sdec, cdec = tables

    def rowmap(cb):
        return lambda b, t: (b * nt + t, cb)

    const = lambda b, t: (0, 0)
    return pl.pallas_call(
        functools.partial(_ret_kernel, n_chunks=tr // c_len),
        grid=(batch, nt),
        in_specs=[pl.BlockSpec((tr, hk), rowmap(_RQ // hk)),
                  pl.BlockSpec((tr, hk), rowmap(_RK // hk)),
                  pl.BlockSpec((tr, hv), rowmap(_RV // hv)),
                  pl.BlockSpec((tr, hv), rowmap(_RG // hv)),
                  pl.BlockSpec((tr, hk), lambda b, t: (t, 0)),
                  pl.BlockSpec((tr, hk), lambda b, t: (t, 0)),
                  pl.BlockSpec((RET_HEADS * c_len, c_len), const),
                  pl.BlockSpec((c_len, hv), const),
                  pl.BlockSpec((c_len, hk), const),
                  pl.BlockSpec((1, hk), const)],
        out_specs=pl.BlockSpec((tr, hv), lambda b, t: (b * nt + t, 0)),
        out_shape=jax.ShapeDtypeStruct((batch * seq, hv), BF16),
        scratch_shapes=[pltpu.VMEM((hv, hk), F32)],
        compiler_params=_cparams(("arbitrary", "arbitrary")),
        name="retention",
    )(z, z, z, z, sin, cos, inner, cross, sdec, cdec)


def _layer_norm(t, g, b):
    mu = jnp.mean(t, axis=-1, keepdims=True)
    var = jnp.mean(jnp.square(t - mu), axis=-1, keepdims=True)
    return (t - mu) * lax.rsqrt(var + LN_EPS) * g + b


def _out_ln_kernel(gla_ref, dif_ref, ret_ref, x_ref, wo_ref, g_ref, b_ref, wr_ref, br_ref, earlier_ref,
                   x1_ref, x1b_ref, x1pa_ref, x1pb_ref, gate_ref, idx_ref, cnt_ref, *, alpha):
    w0 = gla_ref.shape[1]
    w1 = w0 + dif_ref.shape[1]
    a = (_dot(gla_ref[...], wo_ref[0:w0, :]) + _dot(dif_ref[...], wo_ref[w0:w1, :])
         + _dot(ret_ref[...], wo_ref[w1:, :]))
    x1 = _layer_norm(alpha * x_ref[...] + a, g_ref[...], b_ref[...])
    x1_ref[...] = x1
    x1b_ref[...] = x1.astype(BF16)
    x1p = _pack_bf16_pairs(x1)
    x1pa_ref[...] = x1p[:, :SC_ROW_WORDS]
    x1pb_ref[...] = x1p[:, SC_ROW_WORDS:]

    hi, lo = _split_hi_lo(x1)
    tm = x1.shape[0]
    parts = _dot(jnp.concatenate([hi, lo], axis=0), wr_ref[...])
    logits = (parts[:tm, :LANE] + parts[tm:, :LANE]) + (parts[:tm, LANE:] + parts[tm:, LANE:]) + br_ref[...]
    lane = lax.broadcasted_iota(jnp.int32, logits.shape, 1)
    work = jnp.where(lane < N_EXPERTS, logits, -jnp.inf)
    vals = jnp.zeros(logits.shape, F32)
    idxs = jnp.zeros(logits.shape, jnp.int32)
    chosen = jnp.zeros(logits.shape, F32)
    picks = []
    top0 = None
    for r in range(TOP_K):
        mx = jnp.max(work, axis=-1, keepdims=True)
        am = jnp.min(jnp.where(work == mx, lane, LANE), axis=-1, keepdims=True)
        if r == 0:
            top0 = mx
        vals = jnp.where(lane == r, jnp.exp(mx - top0), vals)
        idxs = jnp.where(lane == r, am, idxs)
        chosen = jnp.where(lane == am, 1.0, chosen)
        work = jnp.where(lane == am, -jnp.inf, work)
        picks.append(am)
    gate_ref[...] = vals / jnp.sum(vals, axis=-1, keepdims=True)

    @pl.when(pl.program_id(0) == 0)
    def _():
        cnt_ref[...] = jnp.zeros_like(cnt_ref)

    before = _dot(earlier_ref[...], chosen.astype(BF16)) + cnt_ref[...]
    for r in range(TOP_K):
        rank = jnp.sum(jnp.where(lane == picks[r], before, 0.0), axis=-1, keepdims=True)
        idxs = jnp.where(lane == TOP_K + r, rank.astype(jnp.int32), idxs)
    idx_ref[...] = idxs
    cnt_ref[...] = cnt_ref[...] + jnp.sum(chosen, axis=0, keepdims=True)


def _out_ln(gla_o, dif_o, ret_o, x, w_out, ln_g, ln_b, wr, b_router, alpha, tm=512):
    n, d = x.shape
    tm = min(tm, n)
    earlier = jnp.asarray(np.tril(np.ones((tm, tm), np.float32), -1), BF16)
    row = lambda i: (i, 0)
    const = lambda i: (0, 0)
    return pl.pallas_call(
        functools.partial(_out_ln_kernel, alpha=alpha),
        grid=(n // tm,),
        in_specs=[pl.BlockSpec((tm, gla_o.shape[1]), row),
                  pl.BlockSpec((tm, dif_o.shape[1]), row),
                  pl.BlockSpec((tm, ret_o.shape[1]), row),
                  pl.BlockSpec((tm, d), row),
                  pl.BlockSpec(w_out.shape, const),
                  pl.BlockSpec((1, d), const),
                  pl.BlockSpec((1, d), const),
                  pl.BlockSpec((d, 2 * LANE), const),
                  pl.BlockSpec((1, LANE), const),
                  pl.BlockSpec((tm, tm), const)],
        out_specs=[pl.BlockSpec((tm, d), row),
                   pl.BlockSpec((tm, d), row),
                   pl.BlockSpec((tm, SC_ROW_WORDS), row),
                   pl.BlockSpec((tm, SC_ROW_WORDS), row),
                   pl.BlockSpec((tm, LANE), row),
                   pl.BlockSpec((tm, LANE), row),
                   pl.BlockSpec((1, LANE), const)],
        out_shape=[jax.ShapeDtypeStruct((n, d), F32),
                   jax.ShapeDtypeStruct((n, d), BF16),
                   jax.ShapeDtypeStruct((n, SC_ROW_WORDS), jnp.uint32),
                   jax.ShapeDtypeStruct((n, SC_ROW_WORDS), jnp.uint32),
                   jax.ShapeDtypeStruct((n, LANE), F32),
                   jax.ShapeDtypeStruct((n, LANE), jnp.int32),
                   jax.ShapeDtypeStruct((1, LANE), F32)],
        compiler_params=_cparams(("arbitrary",)),
        name="out_ln_router",
    )(gla_o, dif_o, ret_o, x, w_out, ln_g, ln_b, wr, b_router, earlier)


GU_GROUP = 2 * LANE


def _deinterleave_perm():
    perm = np.zeros((GU_GROUP, GU_GROUP), np.float32)
    half = np.arange(LANE)
    perm[2 * half, half] = 1.0
    perm[2 * half + 1, LANE + half] = 1.0
    return jnp.asarray(perm, BF16)


def _moe_kernel(blk_e_ref, n_used_ref, n_valid_ref, xa_ref, xb_ref, wgu_ref, bgu_ref, wdn_ref, bdn_ref, perm_ref,
                oa_ref, ob_ref,
                wgu_b_ref, wdn_b_ref, *, rows_per_cast):
    i = pl.program_id(0)
    used = i < n_used_ref[0]
    new_expert = jnp.logical_or(i == 0, blk_e_ref[i] != blk_e_ref[jnp.maximum(i - 1, 0)])

    @pl.when(jnp.logical_and(used, new_expert))
    def _():
        d = wgu_ref.shape[0]
        for r in range(d // rows_per_cast):
            rows = slice(r * rows_per_cast, (r + 1) * rows_per_cast)
            wdn_b_ref[rows, :] = wdn_ref[rows, :].astype(BF16)
            for g in range(wgu_ref.shape[1] // GU_GROUP):
                cols = slice(g * GU_GROUP, (g + 1) * GU_GROUP)
                wgu_b_ref[rows, cols] = _dot(wgu_ref[rows, cols].astype(BF16), perm_ref[...]).astype(BF16)

    @pl.when(used)
    def _():
        x = _unpack_bf16_pairs(jnp.concatenate([xa_ref[...], xb_ref[...]], axis=1))
        row = lax.broadcasted_iota(jnp.int32, x.shape, 0)
        x = jnp.where(row < n_valid_ref[i], x, 0.0).astype(BF16)
        gu = _dot(x, wgu_b_ref[...]) + bgu_ref[...]
        acts = []
        for grp in range(gu.shape[1] // GU_GROUP):
            g = jnp.minimum(gu[:, grp * GU_GROUP:grp * GU_GROUP + LANE], SWIGLU_LIMIT)
            u = jnp.clip(gu[:, grp * GU_GROUP + LANE:(grp + 1) * GU_GROUP], -SWIGLU_LIMIT, SWIGLU_LIMIT)
            acts.append(((u + 1.0) * (g * jax.nn.sigmoid(g * SWIGLU_ALPHA))).astype(BF16))
        act = jnp.concatenate(acts, axis=1)
        y = _pack_bf16_pairs(_dot(act, wdn_b_ref[...]) + bdn_ref[...])
        oa_ref[...] = y[:, :SC_ROW_WORDS]
        ob_ref[...] = y[:, SC_ROW_WORDS:]

    @pl.when(jnp.logical_not(used))
    def _():
        oa_ref[...] = jnp.zeros_like(oa_ref)
        ob_ref[...] = jnp.zeros_like(ob_ref)


def _moe(xs, blk_e, n_used, n_valid, w_gu, b_gu, w_dn, b_dn, blk):
    rows = xs[0].shape[0]
    assert w_gu.shape[1] == 4 * SC_ROW_WORDS
    d = w_gu.shape[1]
    f2 = w_gu.shape[2]
    f = w_dn.shape[1]
    assert f == d
    grid_spec = pltpu.PrefetchScalarGridSpec(
        num_scalar_prefetch=3,
        grid=(rows // blk,),
        in_specs=[pl.BlockSpec((blk, SC_ROW_WORDS), lambda i, be, nu, nv: (i, 0)),
                  pl.BlockSpec((blk, SC_ROW_WORDS), lambda i, be, nu, nv: (i, 0)),
                  pl.BlockSpec((None, d, f2), lambda i, be, nu, nv: (be[i], 0, 0)),
                  pl.BlockSpec((None, 1, f2), lambda i, be, nu, nv: (be[i], 0, 0)),
                  pl.BlockSpec((None, f, d), lambda i, be, nu, nv: (be[i], 0, 0)),
                  pl.BlockSpec((None, 1, d), lambda i, be, nu, nv: (be[i], 0, 0)),
                  pl.BlockSpec((GU_GROUP, GU_GROUP), lambda i, be, nu, nv: (0, 0))],
        out_specs=[pl.BlockSpec((blk, SC_ROW_WORDS), lambda i, be, nu, nv: (i, 0)),
                   pl.BlockSpec((blk, SC_ROW_WORDS), lambda i, be, nu, nv: (i, 0))],
        scratch_shapes=[pltpu.VMEM((d, f2), BF16), pltpu.VMEM((f, d), BF16)],
    )
    return pl.pallas_call(
        functools.partial(_moe_kernel, rows_per_cast=256),
        grid_spec=grid_spec,
        out_shape=[jax.ShapeDtypeStruct((rows, SC_ROW_WORDS), jnp.uint32)] * 2,
        compiler_params=_cparams(("arbitrary",)),
        name="moe_experts",
    )(blk_e, n_used, n_valid, xs[0], xs[1], w_gu, b_gu, w_dn, b_dn, _deinterleave_perm())


def _combine_kernel(yga_ref, ygb_ref, gate_ref, x1_ref, x1b_ref, p_ref, wg_ref, bg_ref, wp_ref, g_ref, b_ref,
                    x2_ref, x2b_ref, *, alpha):
    gates = gate_ref[...]
    m = jnp.zeros(x1_ref.shape, F32)
    for r in range(TOP_K):
        m = m + _unpack_bf16_pairs(jnp.concatenate([yga_ref[r], ygb_ref[r]], axis=1)) * gates[:, r:r + 1]
    e = (jax.nn.sigmoid(_dot(x1b_ref[...], wg_ref[...]) + bg_ref[...])
         * _dot(p_ref[...].astype(BF16), wp_ref[...]))
    x2 = _layer_norm(alpha * x1_ref[...] + m + e, g_ref[...], b_ref[...])
    x2_ref[...] = x2
    x2b_ref[...] = x2.astype(BF16)


def _combine(yg, gates, x1, x1b, p, layer, w_pg, b_pg, w_pp, ln_g, ln_b, alpha, tm=512):
    n, d = x1.shape
    tm = min(tm, n)
    p_block0 = layer * (n // tm)
    row = lambda i: (i, 0)
    const = lambda i: (0, 0)
    return pl.pallas_call(
        functools.partial(_combine_kernel, alpha=alpha),
        grid=(n // tm,),
        in_specs=[pl.BlockSpec((TOP_K, tm, SC_ROW_WORDS), lambda i: (0, i, 0)),
                  pl.BlockSpec((TOP_K, tm, SC_ROW_WORDS), lambda i: (0, i, 0)),
                  pl.BlockSpec((tm, LANE), row),
                  pl.BlockSpec((tm, d), row),
                  pl.BlockSpec((tm, d), row),
                  pl.BlockSpec((tm, p.shape[1]), lambda i: (p_block0 + i, 0)),
                  pl.BlockSpec(w_pg.shape, const),
                  pl.BlockSpec((1, d), const),
                  pl.BlockSpec(w_pp.shape, const),
                  pl.BlockSpec((1, d), const),
                  pl.BlockSpec((1, d), const)],
        out_specs=[pl.BlockSpec((tm, d), row), pl.BlockSpec((tm, d), row)],
        out_shape=[jax.ShapeDtypeStruct((n, d), F32), jax.ShapeDtypeStruct((n, d), BF16)],
        compiler_params=_cparams(("arbitrary",)),
        name="combine_ln",
    )(yg[0], yg[1], gates, x1, x1b, p, w_pg, b_pg, w_pp, ln_g, ln_b)


def _route(experts, ranks, counts, blk):
    n_tok = experts.shape[0]
    n_assign = n_tok * TOP_K
    sizes = counts.astype(jnp.int32)
    padded = ((sizes + blk - 1) // blk) * blk
    pends = jnp.cumsum(padded)
    pstarts = pends - padded
    onehot = experts[:, :, None] == jnp.arange(N_EXPERTS, dtype=jnp.int32)
    slot = jnp.sum(jnp.where(onehot, pstarts, 0), axis=-1) + ranks
    buf_len = ((n_assign + N_EXPERTS * (blk - 1) + blk - 1) // blk) * blk
    n_blk = buf_len // blk
    first_row = jnp.arange(n_blk, dtype=jnp.int32) * blk
    blk_e = jnp.minimum(jnp.sum(pends[None, :] <= first_row[:, None], axis=1), N_EXPERTS - 1).astype(jnp.int32)
    n_used = (pends[-1] // blk).astype(jnp.int32).reshape(1)
    mine = blk_e[:, None] == jnp.arange(N_EXPERTS, dtype=jnp.int32)[None, :]
    group_rows_left = jnp.sum(jnp.where(mine, pstarts + sizes, 0), axis=1) - first_row
    n_valid = jnp.clip(group_rows_left, 0, blk).astype(jnp.int32)
    return slot, buf_len, blk_e, n_used, n_valid


def _regroup_w_in(w_in):
    depth, d, width = w_in.shape
    lr0 = _GV + GLA_HEADS * GLA_DV
    src = np.full((Z_WIDTH,), -1, np.int32)
    src[:lr0] = np.arange(lr0)
    src[lr0:width - GLA_GATE_RANK] = np.arange(lr0 + GLA_GATE_RANK, width)
    src[_LR:_LR + GLA_GATE_RANK] = np.arange(lr0, lr0 + GLA_GATE_RANK)
    select = (jnp.arange(width, dtype=jnp.int32)[:, None] == jnp.asarray(src)[None, :]).astype(BF16)
    out = _matmul(w_in.reshape(depth * d, width).astype(BF16), select, d, Z_WIDTH // 3, BF16, "regroup_w_in")
    return out.reshape(depth, d, Z_WIDTH)


def kernel(x, p, w_in, w_gla_gate, b_gla_gate, gla_norm_g, diff_lambda, diff_norm_g, w_out, rel_bias, ln1_g, ln1_b, w_router, b_router, w_gate_up, b_gate_up, w_down, b_down, w_ple_gate, b_ple_gate, w_ple_proj, ln2_g, ln2_b):
    batch, seq, d = x.shape
    depth = w_in.shape[0]
    n = batch * seq
    alpha = (2 * depth) ** 0.25
    moe_blk = 512
    diff_blk = min(512, seq)

    w_in_b = _regroup_w_in(w_in)
    wg = jnp.zeros((depth, LANE, GLA_HEADS * GLA_DK), F32).at[:, :GLA_GATE_RANK, :].set(w_gla_gate).astype(BF16)
    gla_g = jnp.tile(gla_norm_g, (1, GLA_HEADS))[:, None, :]
    w_out_b = w_out.astype(BF16)
    wr = jnp.zeros((depth, d, LANE), F32).at[:, :, :N_EXPERTS].set(w_router)
    wr_hi = wr.astype(BF16)
    wr_b = jnp.concatenate([wr_hi, (wr - wr_hi.astype(F32)).astype(BF16)], axis=-1)
    br = jnp.zeros((depth, 1, LANE), F32).at[:, 0, :N_EXPERTS].set(b_router)
    w_gu = w_gate_up.reshape((depth * N_EXPERTS,) + w_gate_up.shape[2:])
    b_gu = b_gate_up.reshape(depth * N_EXPERTS, -1, LANE, 2).swapaxes(-1, -2).reshape(depth * N_EXPERTS, 1, -1)
    w_dn = w_down.reshape((depth * N_EXPERTS,) + w_down.shape[2:])
    b_dn = b_down.reshape(depth * N_EXPERTS, 1, -1)
    w_pg_b = w_ple_gate.astype(BF16)
    w_pp_b = w_ple_proj.astype(BF16)
    bias_tiles = _bias_tiles(rel_bias, diff_blk)
    ret_tables = _ret_tables(seq)

    xf = x.reshape(n, d)
    xb = xf.astype(BF16)
    pf = p.reshape(depth * n, p.shape[-1])
    for i in range(depth):
        lam_init = 0.8 - 0.6 * math.exp(-0.3 * i)
        z = _matmul(xb, w_in_b[i], min(2048, n), Z_WIDTH // 3, BF16, "in_proj")
        gla_o = _gla(z, wg[i], b_gla_gate[i][None, :], gla_g[i], batch, seq)
        dif_o = _diff(z, diff_lambda[i], diff_norm_g[i][None, :], bias_tiles, batch, seq, lam_init, diff_blk)
        ret_o = _ret(z, ret_tables, batch, seq)
        x1, x1b, x1pa, x1pb, gates, idx, cnt = _out_ln(gla_o, dif_o, ret_o, xf, w_out_b[i], ln1_g[i][None, :],
                                                       ln1_b[i][None, :], wr_b[i], br[i], alpha)
        slot, buf_len, blk_e, n_used, n_valid = _route(idx[:, :TOP_K], idx[:, TOP_K:2 * TOP_K], cnt[0, :N_EXPERTS], moe_blk)
        slot_k = slot.T.reshape(-1)
        xs = [_sc_scatter(t, slot_k, buf_len) for t in (x1pa, x1pb)]
        yb = _moe(xs, blk_e + i * N_EXPERTS, n_used, n_valid, w_gu, b_gu, w_dn, b_dn, moe_blk)
        yg = [t.reshape(TOP_K, n, SC_ROW_WORDS) for t in _gather_rows(yb, slot_k)]
        xf, xb = _combine(yg, gates, x1, x1b, pf, i, w_pg_b[i], b_ple_gate[i][None, :], w_pp_b[i],
                          ln2_g[i][None, :], ln2_b[i][None, :], alpha)
    return xf.reshape(batch, seq, d)
```

```python
import functools
import math

import numpy as np
import jax
import jax.numpy as jnp
from jax import lax
from jax.experimental import pallas as pl
from jax.experimental.pallas import tpu as pltpu
from jax.experimental.pallas import tpu_sc as plsc

F32 = jnp.float32
BF16 = jnp.bfloat16

GLA_HEADS = 4
GLA_DK = 32
GLA_DV = 64
GLA_GATE_RANK = 16
GLA_TAU = 16.0
GLA_CHUNK = 64
DIFF_HEADS = 4
DIFF_D = 64
RET_HEADS = 4
RET_DK = 64
RET_DV = 64
RET_CHUNK = 128
T5_BUCKETS = 32
T5_MAX_DIST = 128
N_EXPERTS = 32
TOP_K = 4
SWIGLU_LIMIT = 7.0
SWIGLU_ALPHA = 1.702
LN_EPS = 1e-5
HEAD_NORM_EPS = 1e-5

LANE = 128
VMEM_LIMIT = 56 * 1024 * 1024

_GQ, _GK, _GV, _GO = 0, 128, 256, 512
_DQ, _DK, _DV = 768, 1280, 1792
_RQ, _RK, _RV, _RG = 2304, 2560, 2816, 3072
_LR = 3328
Z_WIDTH = 3456

GLA_FAST_MAX_DECAY = 60.0


def _cparams(sem):
    return pltpu.CompilerParams(dimension_semantics=sem, vmem_limit_bytes=VMEM_LIMIT)


def _split_hi_lo(a):
    hi = a.astype(BF16)
    lo = (a - hi.astype(F32)).astype(BF16)
    return hi, lo


def _dot(a, b):
    return jnp.dot(a, b, preferred_element_type=F32)


def _dot_nt(a, b):
    return lax.dot_general(a, b, (((1,), (1,)), ((), ())), preferred_element_type=F32)


def _dot_tn(a, b):
    return lax.dot_general(a, b, (((0,), (0,)), ((), ())), preferred_element_type=F32)


def _seg_sum(a, ind):
    hi, lo = _split_hi_lo(a)
    return _dot(hi, ind) + _dot(lo, ind)


def _pack_bf16_pairs(t):
    h = t.shape[1] // 2
    bits = lax.bitcast_convert_type(t.astype(BF16).astype(F32), jnp.uint32)
    return (bits[:, :h] >> 16) | (bits[:, h:] & jnp.uint32(0xFFFF0000))


def _unpack_bf16_pairs(packed):
    lo = lax.bitcast_convert_type(packed << 16, F32)
    hi = lax.bitcast_convert_type(packed & jnp.uint32(0xFFFF0000), F32)
    return jnp.concatenate([lo, hi], axis=1)


SC_WINDOW = 128
SC_ROW_WORDS = 256


def _sc_gather(table, indices):
    m = indices.shape[0]
    words = table.shape[1]
    assert words == SC_ROW_WORDS and m % SC_WINDOW == 0
    mesh = plsc.VectorSubcoreMesh(core_axis_name="core", subcore_axis_name="subcore")

    def body(x_hbm, i_hbm, o_hbm):
        def step(i_vmem, o_vmem):
            pltpu.sync_copy(x_hbm.at[i_vmem.at[0]], o_vmem)

        pltpu.emit_pipeline(
            step,
            grid=(m // SC_WINDOW,),
            in_specs=[pl.BlockSpec((1, SC_WINDOW), index_map=lambda i: (0, i))],
            out_specs=[pl.BlockSpec((SC_WINDOW, words), index_map=lambda i: (i, 0))],
            core_axis_name=("core", "subcore"),
            dimension_semantics=(pltpu.PARALLEL,),
        )(i_hbm, o_hbm)

    return pl.kernel(body, out_type=jax.ShapeDtypeStruct((m, words), table.dtype), mesh=mesh,
                     scratch_types=[])(table, indices.reshape(1, m))


def _sc_scatter(table, indices, out_rows):
    r, words = table.shape
    m = indices.shape[0]
    assert words == SC_ROW_WORDS and r % SC_WINDOW == 0 and m % r == 0
    mesh = plsc.VectorSubcoreMesh(core_axis_name="core", subcore_axis_name="subcore")
    steps_per_pass = r // SC_WINDOW

    def body(x_hbm, i_hbm, o_hbm):
        def step(x_vmem, i_vmem):
            pltpu.sync_copy(x_vmem, o_hbm.at[i_vmem.at[0]])

        pltpu.emit_pipeline(
            step,
            grid=(m // SC_WINDOW,),
            in_specs=[pl.BlockSpec((SC_WINDOW, words), index_map=lambda i: (i % steps_per_pass, 0)),
                      pl.BlockSpec((1, SC_WINDOW), index_map=lambda i: (0, i))],
            out_specs=[],
            core_axis_name=("core", "subcore"),
            dimension_semantics=(pltpu.PARALLEL,),
        )(x_hbm, i_hbm)

    return pl.kernel(body, out_type=jax.ShapeDtypeStruct((out_rows, words), table.dtype), mesh=mesh,
                     scratch_types=[])(table, indices.reshape(1, m))


def _gather_rows(tables, rows):
    return [_sc_gather(t, rows) for t in tables]


def _mm_kernel(x_ref, w_ref, o_ref):
    o_ref[...] = _dot(x_ref[...], w_ref[...]).astype(o_ref.dtype)


def _matmul(x, w, tm, tn, out_dtype, name):
    m, k = x.shape
    n = w.shape[1]
    return pl.pallas_call(
        _mm_kernel,
        grid=(n // tn, m // tm),
        in_specs=[pl.BlockSpec((tm, k), lambda j, i: (i, 0)),
                  pl.BlockSpec((k, tn), lambda j, i: (0, j))],
        out_specs=pl.BlockSpec((tm, tn), lambda j, i: (i, j)),
        out_shape=jax.ShapeDtypeStruct((m, n), out_dtype),
        compiler_params=_cparams(("arbitrary", "arbitrary")),
        name=name,
    )(x, w)


def _gla_kernel(q_ref, k_ref, v_ref, og_ref, lr_ref, wg_ref, bg_ref, ng_ref, o_ref,
                state_ref, la_ref, cum_ref, kf_ref, vf_ref, oacc_ref, *, n_chunks, fast_max_decay):
    c_len = GLA_CHUNK
    hk = GLA_HEADS * GLA_DK
    hv = GLA_HEADS * GLA_DV

    @pl.when(pl.program_id(1) == 0)
    def _():
        state_ref[...] = jnp.zeros_like(state_ref)

    zg = _dot(lr_ref[...], wg_ref[...]) + bg_ref[...]
    la = (jnp.minimum(zg, 0.0) - jnp.log1p(jnp.exp(-jnp.abs(zg)))) / GLA_TAU
    la_ref[...] = la
    chunk_tot = jnp.sum(la.reshape(n_chunks, c_len, hk), axis=1)
    fast = jnp.min(chunk_tot) > -fast_max_decay

    row = lax.broadcasted_iota(jnp.int32, (c_len, c_len), 0)
    col = lax.broadcasted_iota(jnp.int32, (c_len, c_len), 1)
    tril = (row >= col).astype(BF16)
    sr = lax.broadcasted_iota(jnp.int32, (hv, hk), 0)
    sc = lax.broadcasted_iota(jnp.int32, (hv, hk), 1)
    state_mask = ((sr // GLA_DV) == (sc // GLA_DK)).astype(F32)
    lane_k = lax.broadcasted_iota(jnp.int32, (1, hk), 1) // GLA_DK
    lane_v = lax.broadcasted_iota(jnp.int32, (1, hv), 1) // GLA_DV
    r4 = lax.broadcasted_iota(jnp.int32, (GLA_HEADS * c_len, c_len), 0) % c_len
    c4 = lax.broadcasted_iota(jnp.int32, (GLA_HEADS * c_len, c_len), 1)
    causal4 = r4 >= c4
    er = lax.broadcasted_iota(jnp.int32, (hk, hv), 0) // GLA_DK
    ec = lax.broadcasted_iota(jnp.int32, (hk, hv), 1) // GLA_DV
    head_expand = (er == ec).astype(BF16)
    trow = lax.broadcasted_iota(jnp.int32, (c_len, hv), 0)

    def load_chunk(rows):
        q = q_ref[rows, :].astype(F32) * (GLA_DK ** -0.5)
        k = k_ref[rows, :].astype(F32)
        v = v_ref[rows, :]
        la_hi, la_lo = _split_hi_lo(la_ref[rows, :])
        cum = _dot(tril, la_hi) + _dot(tril, la_lo)
        return q, k, v, cum

    def fast_intra(q, k, v, cum):
        mid = cum[c_len // 2:c_len // 2 + 1, :]
        qa = q * jnp.exp(cum - mid)
        kb = (k * jnp.exp(mid - cum)).astype(BF16)
        qa4 = jnp.concatenate(
            [jnp.where(lane_k == h, qa, 0.0) for h in range(GLA_HEADS)], axis=0).astype(BF16)
        s4 = jnp.where(causal4, _dot_nt(qa4, kb), 0.0)
        pv = _dot(s4.astype(BF16), v)
        acc = jnp.zeros((c_len, hv), F32)
        for h in range(GLA_HEADS):
            acc = acc + jnp.where(lane_v == h, pv[h * c_len:(h + 1) * c_len, :], 0.0)
        return acc

    def slow_intra(q, k, v, cum):
        cum_ref[...] = cum
        kf_ref[...] = k
        vf_ref[...] = v.astype(F32)

        def one_key(s, acc):
            ks = kf_ref[pl.ds(s, 1), :]
            vs = vf_ref[pl.ds(s, 1), :]
            cs = cum_ref[pl.ds(s, 1), :]
            w = q * ks * jnp.exp(jnp.minimum(cum - cs, 0.0))
            wsum = _dot(w.astype(BF16), head_expand)
            return acc + jnp.where(trow >= s, wsum, 0.0) * vs

        return lax.fori_loop(0, c_len, one_key, jnp.zeros((c_len, hv), F32))

    def finish_chunk(rows, q, k, v, cum, intra, st):
        last = cum[c_len - 1:c_len, :]
        qg = (q * jnp.exp(cum)).astype(BF16)
        oacc_ref[rows, :] = intra + _dot_nt(qg, st.astype(BF16))
        kg = (k * jnp.exp(last - cum)).astype(BF16)
        upd = _dot_tn(v, kg)
        return (st * jnp.exp(last) + upd) * state_mask

    @pl.when(fast)
    def _():
        st = state_ref[...]
        for c in range(n_chunks):
            rows = pl.ds(c * c_len, c_len)
            q, k, v, cum = load_chunk(rows)
            st = finish_chunk(rows, q, k, v, cum, fast_intra(q, k, v, cum), st)
        state_ref[...] = st

    @pl.when(jnp.logical_not(fast))
    def _():
        def chunk(c, carry):
            rows = pl.ds(pl.multiple_of(c * c_len, c_len), c_len)
            q, k, v, cum = load_chunk(rows)
            state_ref[...] = finish_chunk(rows, q, k, v, cum, slow_intra(q, k, v, cum), state_ref[...])
            return carry

        lax.fori_loop(0, n_chunks, chunk, 0)

    o = oacc_ref[...]
    vr = lax.broadcasted_iota(jnp.int32, (hv, hv), 0) // GLA_DV
    vc = lax.broadcasted_iota(jnp.int32, (hv, hv), 1) // GLA_DV
    seg = (vr == vc).astype(BF16)
    ms = _seg_sum(o * o, seg) * (1.0 / GLA_DV)
    y = o * lax.rsqrt(ms + HEAD_NORM_EPS) * ng_ref[...]
    g = og_ref[...].astype(F32)
    o_ref[...] = (y * (g * jax.nn.sigmoid(g))).astype(o_ref.dtype)


def _gla(z, w_gate, b_gate, norm_g, batch, seq, tg=1024, fast_max_decay=GLA_FAST_MAX_DECAY):
    tg = min(tg, seq)
    nt = seq // tg
    hk = GLA_HEADS * GLA_DK
    hv = GLA_HEADS * GLA_DV

    def rowmap(cb):
        return lambda b, t: (b * nt + t, cb)

    const = lambda b, t: (0, 0)
    return pl.pallas_call(
        functools.partial(_gla_kernel, n_chunks=tg // GLA_CHUNK, fast_max_decay=fast_max_decay),
        grid=(batch, nt),
        in_specs=[pl.BlockSpec((tg, hk), rowmap(_GQ // hk)),
                  pl.BlockSpec((tg, hk), rowmap(_GK // hk)),
                  pl.BlockSpec((tg, hv), rowmap(_GV // hv)),
                  pl.BlockSpec((tg, hv), rowmap(_GO // hv)),
                  pl.BlockSpec((tg, LANE), rowmap(_LR // LANE)),
                  pl.BlockSpec((LANE, hk), const),
                  pl.BlockSpec((1, hk), const),
                  pl.BlockSpec((1, hv), const)],
        out_specs=pl.BlockSpec((tg, hv), lambda b, t: (b * nt + t, 0)),
        out_shape=jax.ShapeDtypeStruct((batch * seq, hv), BF16),
        scratch_shapes=[pltpu.VMEM((hv, hk), F32),
                        pltpu.VMEM((tg, hk), F32),
                        pltpu.VMEM((GLA_CHUNK, hk), F32),
                        pltpu.VMEM((GLA_CHUNK, hk), F32),
                        pltpu.VMEM((GLA_CHUNK, hv), F32),
                        pltpu.VMEM((tg, hv), F32)],
        compiler_params=_cparams(("arbitrary", "arbitrary")),
        name="gla",
    )(z, z, z, z, z, w_gate, b_gate, norm_g)


LOG2E = math.log2(math.e)


def _diff_kernel(far_ref, lam_ref, q_ref, k_ref, v_ref, bias_ref, ng_ref, o_ref,
                 m_ref, acc_ref, *, blk, lam_init):
    qi = pl.program_id(2)
    d = DIFF_D
    w = 2 * DIFF_D
    q = (q_ref[...].astype(F32) * (DIFF_D ** -0.5 * LOG2E)).astype(BF16)
    lane_q = lax.broadcasted_iota(jnp.int32, q.shape, 1)
    q2 = jnp.concatenate([jnp.where(lane_q < d, q, jnp.zeros_like(q)),
                          jnp.where(lane_q >= d, q, jnp.zeros_like(q))], axis=0)
    m_ref[...] = jnp.full_like(m_ref, -jnp.inf)
    acc_ref[...] = jnp.zeros_like(acc_ref)

    def kv_step(j, bias, shift):
        rows = pl.ds(pl.multiple_of(j * blk, blk), blk)
        vj = v_ref[rows, :]
        s_both = _dot_nt(q2, k_ref[rows, :])
        for m in range(2):
            s = s_both[m * blk:(m + 1) * blk, :]
            if bias is not None:
                s = s + bias
            m_cur = jnp.max(s, axis=-1, keepdims=True)
            if shift is not None:
                m_cur = m_cur + shift
            m_old = m_ref[m]
            m_new = jnp.maximum(m_old, m_cur)
            alpha = jnp.exp2(m_old - m_new)
            sub = m_new if shift is None else m_new - shift
            p = jnp.exp2(s - jnp.concatenate([sub] * (blk // LANE), axis=1))
            row_sum = jnp.broadcast_to(jnp.sum(p, axis=-1, keepdims=True), (blk, LANE))
            acc_ref[m] = (jnp.concatenate([alpha, alpha], axis=1) * acc_ref[m]
                          + jnp.concatenate([_dot(p.astype(BF16), vj), row_sum], axis=1))
            m_ref[m] = m_new

    far = far_ref[pl.program_id(1)]

    def far_step(j, carry):
        kv_step(j, None, far)
        return carry

    lax.fori_loop(0, jnp.maximum(qi - 1, 0), far_step, 0)

    @pl.when(qi >= 1)
    def _():
        kv_step(qi - 1, bias_ref[0, 1], None)

    kv_step(qi, bias_ref[0, 0], None)

    lf = lam_ref[...]
    lam = (jnp.exp(jnp.sum(lf[0:1] * lf[1:2], axis=-1, keepdims=True))
           - jnp.exp(jnp.sum(lf[2:3] * lf[3:4], axis=-1, keepdims=True)) + lam_init)
    a0 = acc_ref[0]
    a1 = acc_ref[1]
    o = a0[:, :w] / a0[:, w:] - lam * (a1[:, :w] / a1[:, w:])
    ms = jnp.mean(o * o, axis=-1, keepdims=True)
    y = o * lax.rsqrt(ms + HEAD_NORM_EPS) * ng_ref[...] * (1.0 - lam_init)
    o_ref[...] = y.astype(o_ref.dtype)


def _t5_bucket(rel):
    n = jnp.maximum(-rel, 0)
    max_exact = T5_BUCKETS // 2
    nf = jnp.maximum(n, 1).astype(F32)
    large = max_exact + (jnp.log(nf / max_exact) / math.log(T5_MAX_DIST / max_exact)
                         * (T5_BUCKETS - max_exact)).astype(jnp.int32)
    large = jnp.minimum(large, T5_BUCKETS - 1)
    return jnp.where(n < max_exact, n, large)


def _bias_tiles(rel_bias, blk):
    assert blk + 1 >= T5_MAX_DIST
    m = jnp.arange(2 * blk)
    col_minus_row = jnp.where(m < blk, m, m - 2 * blk)
    tiles = []
    for delta in range(2):
        rel = col_minus_row - delta * blk
        w = rel_bias[_t5_bucket(rel)].astype(F32).T
        if delta == 0:
            w = jnp.where(rel <= 0, w, -jnp.inf)
        flat = jnp.tile(w, (1, blk))[:, :blk * (2 * blk - 1)]
        tiles.append(flat.reshape(-1, blk, 2 * blk - 1)[:, :, :blk])
    far = rel_bias[_t5_bucket(jnp.full((1,), -(blk + 1)))[0]].astype(F32)
    return jnp.stack(tiles, axis=1) * LOG2E, far * LOG2E


def _diff(z, diff_lambda, norm_g, bias, batch, seq, lam_init, blk):
    blk = min(blk, seq)
    nq = seq // blk
    w = 2 * DIFF_D
    bias_tiles, bias_far = bias
    return pl.pallas_call(
        functools.partial(_diff_kernel, blk=blk, lam_init=lam_init),
        grid=(batch, DIFF_HEADS, nq),
        in_specs=[pl.BlockSpec(memory_space=pltpu.SMEM),
                  pl.BlockSpec((4, DIFF_D), lambda b, h, i: (0, 0)),
                  pl.BlockSpec((blk, w), lambda b, h, i: (b * nq + i, _DQ // w + h)),
                  pl.BlockSpec((seq, w), lambda b, h, i: (b, _DK // w + h)),
                  pl.BlockSpec((seq, w), lambda b, h, i: (b, _DV // w + h)),
                  pl.BlockSpec((1, 2, blk, blk), lambda b, h, i: (h, 0, 0, 0)),
                  pl.BlockSpec((1, w), lambda b, h, i: (0, 0))],
        out_specs=pl.BlockSpec((blk, w), lambda b, h, i: (b * nq + i, h)),
        out_shape=jax.ShapeDtypeStruct((batch * seq, DIFF_HEADS * w), BF16),
        scratch_shapes=[pltpu.VMEM((2, blk, LANE), F32),
                        pltpu.VMEM((2, blk, w + LANE), F32)],
        compiler_params=_cparams(("arbitrary", "arbitrary", "arbitrary")),
        name="diff_attn",
    )(bias_far, diff_lambda, z, z, z, bias_tiles, norm_g)


def _ret_kernel(q_ref, k_ref, v_ref, g_ref, sin_ref, cos_ref, inner_ref, cross_ref, sdec_ref, cdec_ref,
                o_ref, state_ref, *, n_chunks):
    c_len = RET_CHUNK
    hk = RET_HEADS * RET_DK
    hv = RET_HEADS * RET_DV

    @pl.when(pl.program_id(1) == 0)
    def _():
        state_ref[...] = jnp.zeros_like(state_ref)

    even = (lax.broadcasted_iota(jnp.int32, (1, hk), 1) % 2) == 0
    lane_k = lax.broadcasted_iota(jnp.int32, (1, hk), 1) // RET_DK
    lane_v = lax.broadcasted_iota(jnp.int32, (1, hv), 1) // RET_DV
    sr = lax.broadcasted_iota(jnp.int32, (hv, hk), 0) // RET_DV
    sc = lax.broadcasted_iota(jnp.int32, (hv, hk), 1) // RET_DK
    vr = lax.broadcasted_iota(jnp.int32, (hv, hv), 0) // RET_DV
    vc = lax.broadcasted_iota(jnp.int32, (hv, hv), 1) // RET_DV
    seg = (vr == vc).astype(BF16)

    st = state_ref[...]
    for c in range(n_chunks):
        rows = pl.ds(c * c_len, c_len)
        sin = sin_ref[rows, :]
        cos = cos_ref[rows, :]

        def rotary(t):
            nxt = pltpu.roll(t, hk - 1, 1)
            prv = pltpu.roll(t, 1, 1)
            return t * cos + jnp.where(even, -nxt, prv) * sin

        qf = rotary(q_ref[rows, :].astype(F32))
        kf = rotary(k_ref[rows, :].astype(F32)) * (RET_DK ** -0.5)
        v = v_ref[rows, :]

        q4 = jnp.concatenate([jnp.where(lane_k == h, qf, 0.0) for h in range(RET_HEADS)], axis=0).astype(BF16)
        s4 = _dot_nt(q4, kf.astype(BF16)) * inner_ref[...]
        pv = _dot(s4.astype(BF16), v)
        acc = jnp.zeros((c_len, hv), F32)
        for h in range(RET_HEADS):
            acc = acc + jnp.where(lane_v == h, pv[h * c_len:(h + 1) * c_len, :], 0.0)

        o = acc + _dot_nt(qf.astype(BF16), st.astype(BF16)) * cross_ref[...]
        ks = (kf * sdec_ref[...]).astype(BF16)
        st = jnp.where(sr == sc, st * cdec_ref[...] + _dot_tn(v, ks), 0.0)

        ms = _seg_sum(o * o, seg) * (1.0 / RET_DV)
        g = g_ref[rows, :].astype(F32)
        o_ref[rows, :] = (o * lax.rsqrt(ms + HEAD_NORM_EPS) * (g * jax.nn.sigmoid(g))).astype(o_ref.dtype)
    state_ref[...] = st


def _ret_tables(seq):
    c_len = RET_CHUNK
    pos = jnp.arange(seq, dtype=F32)
    angle = 1.0 / (10000.0 ** jnp.linspace(0.0, 1.0, RET_DK // 2, dtype=F32))
    angle = jnp.repeat(angle, 2)
    sin = jnp.tile(jnp.sin(pos[:, None] * angle), (1, RET_HEADS))
    cos = jnp.tile(jnp.cos(pos[:, None] * angle), (1, RET_HEADS))
    log_g = jnp.log1p(-jnp.exp2(-5.0 - jnp.arange(RET_HEADS, dtype=F32)))
    idx = jnp.arange(c_len, dtype=F32)
    rel = idx[:, None] - idx[None, :]
    inner = jnp.where(rel[None] >= 0, jnp.exp(jnp.maximum(rel, 0.0)[None] * log_g[:, None, None]), 0.0)
    inner = inner.reshape(RET_HEADS * c_len, c_len)
    cross = jnp.exp((idx + 1.0)[None] * log_g[:, None])
    cross = jnp.repeat(cross.T, RET_DV, axis=1)
    sdec = jnp.exp((c_len - 1.0 - idx)[None] * log_g[:, None])
    sdec = jnp.repeat(sdec.T, RET_DK, axis=1)
    cdec = jnp.repeat(jnp.exp(c_len * log_g), RET_DK)[None, :]
    return sin, cos, inner, cross, sdec, cdec


def _ret(z, tables, batch, seq, tr=512):
    c_len = RET_CHUNK
    tr = min(tr, seq)
    nt = seq // tr
    hk = RET_HEADS * RET_DK
    hv = RET_HEADS * RET_DV
    sin, cos, inner, cross, sdec, cdec = tables

    def rowmap(cb):
        return lambda b, t: (b * nt + t, cb)

    const = lambda b, t: (0, 0)
    return pl.pallas_call(
        functools.partial(_ret_kernel, n_chunks=tr // c_len),
        grid=(batch, nt),
        in_specs=[pl.BlockSpec((tr, hk), rowmap(_RQ // hk)),
                  pl.BlockSpec((tr, hk), rowmap(_RK // hk)),
                  pl.BlockSpec((tr, hv), rowmap(_RV // hv)),
                  pl.BlockSpec((tr, hv), rowmap(_RG // hv)),
                  pl.BlockSpec((tr, hk), lambda b, t: (t, 0)),
                  pl.BlockSpec((tr, hk), lambda b, t: (t, 0)),
                  pl.BlockSpec((RET_HEADS * c_len, c_len), const),
                  pl.BlockSpec((c_len, hv), const),
                  pl.BlockSpec((c_len, hk), const),
                  pl.BlockSpec((1, hk), const)],
        out_specs=pl.BlockSpec((tr, hv), lambda b, t: (b * nt + t, 0)),
        out_shape=jax.ShapeDtypeStruct((batch * seq, hv), BF16),
        scratch_shapes=[pltpu.VMEM((hv, hk), F32)],
        compiler_params=_cparams(("arbitrary", "arbitrary")),
        name="retention",
    )(z, z, z, z, sin, cos, inner, cross, sdec, cdec)


def _layer_norm(t, g, b):
    mu = jnp.mean(t, axis=-1, keepdims=True)
    var = jnp.mean(jnp.square(t - mu), axis=-1, keepdims=True)
    return (t - mu) * lax.rsqrt(var + LN_EPS) * g + b


def _out_ln_kernel(gla_ref, dif_ref, ret_ref, x_ref, wo_ref, g_ref, b_ref, wr_ref, br_ref, earlier_ref,
                   x1_ref, x1b_ref, x1pa_ref, x1pb_ref, gate_ref, idx_ref, cnt_ref, *, alpha):
    w0 = gla_ref.shape[1]
    w1 = w0 + dif_ref.shape[1]
    a = (_dot(gla_ref[...], wo_ref[0:w0, :]) + _dot(dif_ref[...], wo_ref[w0:w1, :])
         + _dot(ret_ref[...], wo_ref[w1:, :]))
    x1 = _layer_norm(alpha * x_ref[...] + a, g_ref[...], b_ref[...])
    x1_ref[...] = x1
    x1b_ref[...] = x1.astype(BF16)
    x1p = _pack_bf16_pairs(x1)
    x1pa_ref[...] = x1p[:, :SC_ROW_WORDS]
    x1pb_ref[...] = x1p[:, SC_ROW_WORDS:]

    hi, lo = _split_hi_lo(x1)
    tm = x1.shape[0]
    parts = _dot(jnp.concatenate([hi, lo], axis=0), wr_ref[...])
    logits = (parts[:tm, :LANE] + parts[tm:, :LANE]) + (parts[:tm, LANE:] + parts[tm:, LANE:]) + br_ref[...]
    lane = lax.broadcasted_iota(jnp.int32, logits.shape, 1)
    work = jnp.where(lane < N_EXPERTS, logits, -jnp.inf)
    vals = jnp.zeros(logits.shape, F32)
    idxs = jnp.zeros(logits.shape, jnp.int32)
    chosen = jnp.zeros(logits.shape, F32)
    picks = []
    top0 = None
    for r in range(TOP_K):
        mx = jnp.max(work, axis=-1, keepdims=True)
        am = jnp.min(jnp.where(work == mx, lane, LANE), axis=-1, keepdims=True)
        if r == 0:
            top0 = mx
        vals = jnp.where(lane == r, jnp.exp(mx - top0), vals)
        idxs = jnp.where(lane == r, am, idxs)
        chosen = jnp.where(lane == am, 1.0, chosen)
        work = jnp.where(lane == am, -jnp.inf, work)
        picks.append(am)
    gate_ref[...] = vals / jnp.sum(vals, axis=-1, keepdims=True)

    @pl.when(pl.program_id(0) == 0)
    def _():
        cnt_ref[...] = jnp.zeros_like(cnt_ref)

    before = _dot(earlier_ref[...], chosen.astype(BF16)) + cnt_ref[...]
    for r in range(TOP_K):
        rank = jnp.sum(jnp.where(lane == picks[r], before, 0.0), axis=-1, keepdims=True)
        idxs = jnp.where(lane == TOP_K + r, rank.astype(jnp.int32), idxs)
    idx_ref[...] = idxs
    cnt_ref[...] = cnt_ref[...] + jnp.sum(chosen, axis=0, keepdims=True)


def _out_ln(gla_o, dif_o, ret_o, x, w_out, ln_g, ln_b, wr, b_router, alpha, tm=512):
    n, d = x.shape
    tm = min(tm, n)
    earlier = jnp.asarray(np.tril(np.ones((tm, tm), np.float32), -1), BF16)
    row = lambda i: (i, 0)
    const = lambda i: (0, 0)
    return pl.pallas_call(
        functools.partial(_out_ln_kernel, alpha=alpha),
        grid=(n // tm,),
        in_specs=[pl.BlockSpec((tm, gla_o.shape[1]), row),
                  pl.BlockSpec((tm, dif_o.shape[1]), row),
                  pl.BlockSpec((tm, ret_o.shape[1]), row),
                  pl.BlockSpec((tm, d), row),
                  pl.BlockSpec(w_out.shape, const),
                  pl.BlockSpec((1, d), const),
                  pl.BlockSpec((1, d), const),
                  pl.BlockSpec((d, 2 * LANE), const),
                  pl.BlockSpec((1, LANE), const),
                  pl.BlockSpec((tm, tm), const)],
        out_specs=[pl.BlockSpec((tm, d), row),
                   pl.BlockSpec((tm, d), row),
                   pl.BlockSpec((tm, SC_ROW_WORDS), row),
                   pl.BlockSpec((tm, SC_ROW_WORDS), row),
                   pl.BlockSpec((tm, LANE), row),
                   pl.BlockSpec((tm, LANE), row),
                   pl.BlockSpec((1, LANE), const)],
        out_shape=[jax.ShapeDtypeStruct((n, d), F32),
                   jax.ShapeDtypeStruct((n, d), BF16),
                   jax.ShapeDtypeStruct((n, SC_ROW_WORDS), jnp.uint32),
                   jax.ShapeDtypeStruct((n, SC_ROW_WORDS), jnp.uint32),
                   jax.ShapeDtypeStruct((n, LANE), F32),
                   jax.ShapeDtypeStruct((n, LANE), jnp.int32),
                   jax.ShapeDtypeStruct((1, LANE), F32)],
        compiler_params=_cparams(("arbitrary",)),
        name="out_ln_router",
    )(gla_o, dif_o, ret_o, x, w_out, ln_g, ln_b, wr, b_router, earlier)


GU_GROUP = 2 * LANE


def _deinterleave_perm():
    perm = np.zeros((GU_GROUP, GU_GROUP), np.float32)
    half = np.arange(LANE)
    perm[2 * half, half] = 1.0
    perm[2 * half + 1, LANE + half] = 1.0
    return jnp.asarray(perm, BF16)


def _moe_kernel(blk_e_ref, n_used_ref, n_valid_ref, xa_ref, xb_ref, wgu_ref, bgu_ref, wdn_ref, bdn_ref, perm_ref,
                oa_ref, ob_ref,
                wgu_b_ref, wdn_b_ref, *, rows_per_cast):
    i = pl.program_id(0)
    used = i < n_used_ref[0]
    new_expert = jnp.logical_or(i == 0, blk_e_ref[i] != blk_e_ref[jnp.maximum(i - 1, 0)])

    @pl.when(jnp.logical_and(used, new_expert))
    def _():
        d = wgu_ref.shape[0]
        for r in range(d // rows_per_cast):
            rows = slice(r * rows_per_cast, (r + 1) * rows_per_cast)
            wdn_b_ref[rows, :] = wdn_ref[rows, :].astype(BF16)
            for g in range(wgu_ref.shape[1] // GU_GROUP):
                cols = slice(g * GU_GROUP, (g + 1) * GU_GROUP)
                wgu_b_ref[rows, cols] = _dot(wgu_ref[rows, cols].astype(BF16), perm_ref[...]).astype(BF16)

    @pl.when(used)
    def _():
        x = _unpack_bf16_pairs(jnp.concatenate([xa_ref[...], xb_ref[...]], axis=1))
        row = lax.broadcasted_iota(jnp.int32, x.shape, 0)
        x = jnp.where(row < n_valid_ref[i], x, 0.0).astype(BF16)
        gu = _dot(x, wgu_b_ref[...]) + bgu_ref[...]
        acts = []
        for grp in range(gu.shape[1] // GU_GROUP):
            g = jnp.minimum(gu[:, grp * GU_GROUP:grp * GU_GROUP + LANE], SWIGLU_LIMIT)
            u = jnp.clip(gu[:, grp * GU_GROUP + LANE:(grp + 1) * GU_GROUP], -SWIGLU_LIMIT, SWIGLU_LIMIT)
            acts.append(((u + 1.0) * (g * jax.nn.sigmoid(g * SWIGLU_ALPHA))).astype(BF16))
        act = jnp.concatenate(acts, axis=1)
        y = _pack_bf16_pairs(_dot(act, wdn_b_ref[...]) + bdn_ref[...])
        oa_ref[...] = y[:, :SC_ROW_WORDS]
        ob_ref[...] = y[:, SC_ROW_WORDS:]

    @pl.when(jnp.logical_not(used))
    def _():
        oa_ref[...] = jnp.zeros_like(oa_ref)
        ob_ref[...] = jnp.zeros_like(ob_ref)


def _moe(xs, blk_e, n_used, n_valid, w_gu, b_gu, w_dn, b_dn, blk):
    rows = xs[0].shape[0]
    assert w_gu.shape[1] == 4 * SC_ROW_WORDS
    d = w_gu.shape[1]
    f2 = w_gu.shape[2]
    f = w_dn.shape[1]
    assert f == d
    grid_spec = pltpu.PrefetchScalarGridSpec(
        num_scalar_prefetch=3,
        grid=(rows // blk,),
        in_specs=[pl.BlockSpec((blk, SC_ROW_WORDS), lambda i, be, nu, nv: (i, 0)),
                  pl.BlockSpec((blk, SC_ROW_WORDS), lambda i, be, nu, nv: (i, 0)),
                  pl.BlockSpec((None, d, f2), lambda i, be, nu, nv: (be[i], 0, 0)),
                  pl.BlockSpec((None, 1, f2), lambda i, be, nu, nv: (be[i], 0, 0)),
                  pl.BlockSpec((None, f, d), lambda i, be, nu, nv: (be[i], 0, 0)),
                  pl.BlockSpec((None, 1, d), lambda i, be, nu, nv: (be[i], 0, 0)),
                  pl.BlockSpec((GU_GROUP, GU_GROUP), lambda i, be, nu, nv: (0, 0))],
        out_specs=[pl.BlockSpec((blk, SC_ROW_WORDS), lambda i, be, nu, nv: (i, 0)),
                   pl.BlockSpec((blk, SC_ROW_WORDS), lambda i, be, nu, nv: (i, 0))],
        scratch_shapes=[pltpu.VMEM((d, f2), BF16), pltpu.VMEM((f, d), BF16)],
    )
    return pl.pallas_call(
        functools.partial(_moe_kernel, rows_per_cast=256),
        grid_spec=grid_spec,
        out_shape=[jax.ShapeDtypeStruct((rows, SC_ROW_WORDS), jnp.uint32)] * 2,
        compiler_params=_cparams(("arbitrary",)),
        name="moe_experts",
    )(blk_e, n_used, n_valid, xs[0], xs[1], w_gu, b_gu, w_dn, b_dn, _deinterleave_perm())


def _ple_kernel(x1b_ref, p_ref, wg_ref, bg_ref, wp_ref, e_ref):
    e = (jax.nn.sigmoid(_dot(x1b_ref[...], wg_ref[...]) + bg_ref[...])
         * _dot(p_ref[...].astype(BF16), wp_ref[...]))
    e_ref[...] = e.astype(e_ref.dtype)


def _ple(x1b, p, layer, w_pg, b_pg, w_pp, tm=512):
    n, d = x1b.shape
    tm = min(tm, n)
    p_block0 = layer * (n // tm)
    row = lambda i: (i, 0)
    const = lambda i: (0, 0)
    return pl.pallas_call(
        _ple_kernel,
        grid=(n // tm,),
        in_specs=[pl.BlockSpec((tm, d), row),
                  pl.BlockSpec((tm, p.shape[1]), lambda i: (p_block0 + i, 0)),
                  pl.BlockSpec(w_pg.shape, const),
                  pl.BlockSpec((1, d), const),
                  pl.BlockSpec(w_pp.shape, const)],
        out_specs=pl.BlockSpec((tm, d), row),
        out_shape=jax.ShapeDtypeStruct((n, d), BF16),
        compiler_params=_cparams(("arbitrary",)),
        name="embedding_gate",
    )(x1b, p, w_pg, b_pg, w_pp)


def _combine_kernel(yga_ref, ygb_ref, gate_ref, x1_ref, e_ref, g_ref, b_ref, x2_ref, x2b_ref, *, alpha):
    gates = gate_ref[...]
    m = jnp.zeros(x1_ref.shape, F32)
    for r in range(TOP_K):
        m = m + _unpack_bf16_pairs(jnp.concatenate([yga_ref[r], ygb_ref[r]], axis=1)) * gates[:, r:r + 1]
    x2 = _layer_norm(alpha * x1_ref[...] + m + e_ref[...].astype(F32), g_ref[...], b_ref[...])
    x2_ref[...] = x2
    x2b_ref[...] = x2.astype(BF16)


def _combine(yg, gates, x1, e, ln_g, ln_b, alpha, tm=512):
    n, d = x1.shape
    tm = min(tm, n)
    row = lambda i: (i, 0)
    const = lambda i: (0, 0)
    return pl.pallas_call(
        functools.partial(_combine_kernel, alpha=alpha),
        grid=(n // tm,),
        in_specs=[pl.BlockSpec((TOP_K, tm, SC_ROW_WORDS), lambda i: (0, i, 0)),
                  pl.BlockSpec((TOP_K, tm, SC_ROW_WORDS), lambda i: (0, i, 0)),
                  pl.BlockSpec((tm, LANE), row),
                  pl.BlockSpec((tm, d), row),
                  pl.BlockSpec((tm, d), row),
                  pl.BlockSpec((1, d), const),
                  pl.BlockSpec((1, d), const)],
        out_specs=[pl.BlockSpec((tm, d), row), pl.BlockSpec((tm, d), row)],
        out_shape=[jax.ShapeDtypeStruct((n, d), F32), jax.ShapeDtypeStruct((n, d), BF16)],
        compiler_params=_cparams(("arbitrary",)),
        name="combine_ln",
    )(yg[0], yg[1], gates, x1, e, ln_g, ln_b)


def _route(experts, ranks, counts, blk):
    n_tok = experts.shape[0]
    n_assign = n_tok * TOP_K
    sizes = counts.astype(jnp.int32)
    padded = ((sizes + blk - 1) // blk) * blk
    pends = jnp.cumsum(padded)
    pstarts = pends - padded
    onehot = experts[:, :, None] == jnp.arange(N_EXPERTS, dtype=jnp.int32)
    slot = jnp.sum(jnp.where(onehot, pstarts, 0), axis=-1) + ranks
    buf_len = ((n_assign + N_EXPERTS * (blk - 1) + blk - 1) // blk) * blk
    n_blk = buf_len // blk
    first_row = jnp.arange(n_blk, dtype=jnp.int32) * blk
    blk_e = jnp.minimum(jnp.sum(pends[None, :] <= first_row[:, None], axis=1), N_EXPERTS - 1).astype(jnp.int32)
    n_used = (pends[-1] // blk).astype(jnp.int32).reshape(1)
    mine = blk_e[:, None] == jnp.arange(N_EXPERTS, dtype=jnp.int32)[None, :]
    group_rows_left = jnp.sum(jnp.where(mine, pstarts + sizes, 0), axis=1) - first_row
    n_valid = jnp.clip(group_rows_left, 0, blk).astype(jnp.int32)
    return slot, buf_len, blk_e, n_used, n_valid


def _regroup_w_in(w_in):
    depth, d, width = w_in.shape
    lr0 = _GV + GLA_HEADS * GLA_DV
    src = np.full((Z_WIDTH,), -1, np.int32)
    src[:lr0] = np.arange(lr0)
    src[lr0:width - GLA_GATE_RANK] = np.arange(lr0 + GLA_GATE_RANK, width)
    src[_LR:_LR + GLA_GATE_RANK] = np.arange(lr0, lr0 + GLA_GATE_RANK)
    select = (jnp.arange(width, dtype=jnp.int32)[:, None] == jnp.asarray(src)[None, :]).astype(BF16)
    out = _matmul(w_in.reshape(depth * d, width).astype(BF16), select, d, Z_WIDTH // 3, BF16, "regroup_w_in")
    return out.reshape(depth, d, Z_WIDTH)


def kernel(x, p, w_in, w_gla_gate, b_gla_gate, gla_norm_g, diff_lambda, diff_norm_g, w_out, rel_bias, ln1_g, ln1_b, w_router, b_router, w_gate_up, b_gate_up, w_down, b_down, w_ple_gate, b_ple_gate, w_ple_proj, ln2_g, ln2_b):
    batch, seq, d = x.shape
    depth = w_in.shape[0]
    n = batch * seq
    alpha = (2 * depth) ** 0.25
    moe_blk = 512
    diff_blk = min(512, seq)

    w_in_b = _regroup_w_in(w_in)
    wg = jnp.zeros((depth, LANE, GLA_HEADS * GLA_DK), F32).at[:, :GLA_GATE_RANK, :].set(w_gla_gate).astype(BF16)
    gla_g = jnp.tile(gla_norm_g, (1, GLA_HEADS))[:, None, :]
    w_out_b = w_out.astype(BF16)
    wr = jnp.zeros((depth, d, LANE), F32).at[:, :, :N_EXPERTS].set(w_router)
    wr_hi = wr.astype(BF16)
    wr_b = jnp.concatenate([wr_hi, (wr - wr_hi.astype(F32)).astype(BF16)], axis=-1)
    br = jnp.zeros((depth, 1, LANE), F32).at[:, 0, :N_EXPERTS].set(b_router)
    w_gu = w_gate_up.reshape((depth * N_EXPERTS,) + w_gate_up.shape[2:])
    b_gu = b_gate_up.reshape(depth * N_EXPERTS, -1, LANE, 2).swapaxes(-1, -2).reshape(depth * N_EXPERTS, 1, -1)
    w_dn = w_down.reshape((depth * N_EXPERTS,) + w_down.shape[2:])
    b_dn = b_down.reshape(depth * N_EXPERTS, 1, -1)
    w_pg_b = w_ple_gate.astype(BF16)
    w_pp_b = w_ple_proj.astype(BF16)
    bias_tiles = _bias_tiles(rel_bias, diff_blk)
    ret_tables = _ret_tables(seq)

    xf = x.reshape(n, d)
    xb = xf.astype(BF16)
    pf = p.reshape(depth * n, p.shape[-1])
    for i in range(depth):
        lam_init = 0.8 - 0.6 * math.exp(-0.3 * i)
        z = _matmul(xb, w_in_b[i], min(2048, n), Z_WIDTH // 3, BF16, "in_proj")
        gla_o = _gla(z, wg[i], b_gla_gate[i][None, :], gla_g[i], batch, seq)
        dif_o = _diff(z, diff_lambda[i], diff_norm_g[i][None, :], bias_tiles, batch, seq, lam_init, diff_blk)
        ret_o = _ret(z, ret_tables, batch, seq)
        x1, x1b, x1pa, x1pb, gates, idx, cnt = _out_ln(gla_o, dif_o, ret_o, xf, w_out_b[i], ln1_g[i][None, :],
                                                       ln1_b[i][None, :], wr_b[i], br[i], alpha)
        slot, buf_len, blk_e, n_used, n_valid = _route(idx[:, :TOP_K], idx[:, TOP_K:2 * TOP_K], cnt[0, :N_EXPERTS], moe_blk)
        slot_k = slot.T.reshape(-1)
        xs = [_sc_scatter(t, slot_k, buf_len) for t in (x1pa, x1pb)]
        e = _ple(x1b, pf, i, w_pg_b[i], b_ple_gate[i][None, :], w_pp_b[i])
        yb = _moe(xs, blk_e + i * N_EXPERTS, n_used, n_valid, w_gu, b_gu, w_dn, b_dn, moe_blk)
        yg = [t.reshape(TOP_K, n, SC_ROW_WORDS) for t in _gather_rows(yb, slot_k)]
        xf, xb = _combine(yg, gates, x1, e, ln2_g[i][None, :], ln2_b[i][None, :], alpha)
    return xf.reshape(batch, seq, d)
```

```python
import functools
import math

import numpy as np
import jax
import jax.numpy as jnp
from jax import lax
from jax.experimental import pallas as pl
from jax.experimental.pallas import tpu as pltpu
from jax.experimental.pallas import tpu_sc as plsc

F32 = jnp.float32
BF16 = jnp.bfloat16

GLA_HEADS = 4
GLA_DK = 32
GLA_DV = 64
GLA_GATE_RANK = 16
GLA_TAU = 16.0
GLA_CHUNK = 64
DIFF_HEADS = 4
DIFF_D = 64
RET_HEADS = 4
RET_DK = 64
RET_DV = 64
RET_CHUNK = 128
T5_BUCKETS = 32
T5_MAX_DIST = 128
N_EXPERTS = 32
TOP_K = 4
SWIGLU_LIMIT = 7.0
SWIGLU_ALPHA = 1.702
LN_EPS = 1e-5
HEAD_NORM_EPS = 1e-5

LANE = 128
VMEM_LIMIT = 56 * 1024 * 1024

_GQ, _GK, _GV, _GO = 0, 128, 256, 512
_DQ, _DK, _DV = 768, 1280, 1792
_RQ, _RK, _RV, _RG = 2304, 2560, 2816, 3072
_LR = 3328
Z_WIDTH = 3456

GLA_FAST_MAX_DECAY = 60.0


def _cparams(sem):
    return pltpu.CompilerParams(dimension_semantics=sem, vmem_limit_bytes=VMEM_LIMIT)


def _split_hi_lo(a):
    hi = a.astype(BF16)
    lo = (a - hi.astype(F32)).astype(BF16)
    return hi, lo


def _dot(a, b):
    return jnp.dot(a, b, preferred_element_type=F32)


def _dot_nt(a, b):
    return lax.dot_general(a, b, (((1,), (1,)), ((), ())), preferred_element_type=F32)


def _dot_tn(a, b):
    return lax.dot_general(a, b, (((0,), (0,)), ((), ())), preferred_element_type=F32)


def _seg_sum(a, ind):
    hi, lo = _split_hi_lo(a)
    return _dot(hi, ind) + _dot(lo, ind)


def _pack_bf16_pairs(t):
    h = t.shape[1] // 2
    bits = lax.bitcast_convert_type(t.astype(BF16).astype(F32), jnp.uint32)
    return (bits[:, :h] >> 16) | (bits[:, h:] & jnp.uint32(0xFFFF0000))


def _unpack_bf16_pairs(packed):
    lo = lax.bitcast_convert_type(packed << 16, F32)
    hi = lax.bitcast_convert_type(packed & jnp.uint32(0xFFFF0000), F32)
    return jnp.concatenate([lo, hi], axis=1)


SC_WINDOW = 128
SC_ROW_WORDS = 256


def _sc_gather(table, indices):
    m = indices.shape[0]
    words = table.shape[1]
    assert words == SC_ROW_WORDS and m % SC_WINDOW == 0
    mesh = plsc.VectorSubcoreMesh(core_axis_name="core", subcore_axis_name="subcore")

    def body(x_hbm, i_hbm, o_hbm):
        def step(i_vmem, o_vmem):
            pltpu.sync_copy(x_hbm.at[i_vmem.at[0]], o_vmem)

        pltpu.emit_pipeline(
            step,
            grid=(m // SC_WINDOW,),
            in_specs=[pl.BlockSpec((1, SC_WINDOW), index_map=lambda i: (0, i))],
            out_specs=[pl.BlockSpec((SC_WINDOW, words), index_map=lambda i: (i, 0))],
            core_axis_name=("core", "subcore"),
            dimension_semantics=(pltpu.PARALLEL,),
        )(i_hbm, o_hbm)

    return pl.kernel(body, out_type=jax.ShapeDtypeStruct((m, words), table.dtype), mesh=mesh,
                     scratch_types=[])(table, indices.reshape(1, m))


def _sc_scatter(table, indices, out_rows):
    r, words = table.shape
    m = indices.shape[0]
    assert words == SC_ROW_WORDS and r % SC_WINDOW == 0 and m % r == 0
    mesh = plsc.VectorSubcoreMesh(core_axis_name="core", subcore_axis_name="subcore")
    steps_per_pass = r // SC_WINDOW

    def body(x_hbm, i_hbm, o_hbm):
        def step(x_vmem, i_vmem):
            pltpu.sync_copy(x_vmem, o_hbm.at[i_vmem.at[0]])

        pltpu.emit_pipeline(
            step,
            grid=(m // SC_WINDOW,),
            in_specs=[pl.BlockSpec((SC_WINDOW, words), index_map=lambda i: (i % steps_per_pass, 0)),
                      pl.BlockSpec((1, SC_WINDOW), index_map=lambda i: (0, i))],
            out_specs=[],
            core_axis_name=("core", "subcore"),
            dimension_semantics=(pltpu.PARALLEL,),
        )(x_hbm, i_hbm)

    return pl.kernel(body, out_type=jax.ShapeDtypeStruct((out_rows, words), table.dtype), mesh=mesh,
                     scratch_types=[])(table, indices.reshape(1, m))


def _gather_rows(tables, rows):
    return [_sc_gather(t, rows) for t in tables]


def _mm_kernel(x_ref, w_ref, o_ref):
    o_ref[...] = _dot(x_ref[...], w_ref[...]).astype(o_ref.dtype)


def _matmul(x, w, tm, tn, out_dtype, name):
    m, k = x.shape
    n = w.shape[1]
    return pl.pallas_call(
        _mm_kernel,
        grid=(n // tn, m // tm),
        in_specs=[pl.BlockSpec((tm, k), lambda j, i: (i, 0)),
                  pl.BlockSpec((k, tn), lambda j, i: (0, j))],
        out_specs=pl.BlockSpec((tm, tn), lambda j, i: (i, j)),
        out_shape=jax.ShapeDtypeStruct((m, n), out_dtype),
        compiler_params=_cparams(("arbitrary", "arbitrary")),
        name=name,
    )(x, w)


def _gla_kernel(q_ref, k_ref, v_ref, og_ref, lr_ref, wg_ref, bg_ref, ng_ref, o_ref,
                state_ref, la_ref, cum_ref, kf_ref, vf_ref, oacc_ref, *, n_chunks, fast_max_decay):
    c_len = GLA_CHUNK
    hk = GLA_HEADS * GLA_DK
    hv = GLA_HEADS * GLA_DV

    @pl.when(pl.program_id(1) == 0)
    def _():
        state_ref[...] = jnp.zeros_like(state_ref)

    zg = _dot(lr_ref[...], wg_ref[...]) + bg_ref[...]
    la = (jnp.minimum(zg, 0.0) - jnp.log1p(jnp.exp(-jnp.abs(zg)))) / GLA_TAU
    la_ref[...] = la
    chunk_tot = jnp.sum(la.reshape(n_chunks, c_len, hk), axis=1)
    fast = jnp.min(chunk_tot) > -fast_max_decay

    row = lax.broadcasted_iota(jnp.int32, (c_len, c_len), 0)
    col = lax.broadcasted_iota(jnp.int32, (c_len, c_len), 1)
    tril = (row >= col).astype(BF16)
    sr = lax.broadcasted_iota(jnp.int32, (hv, hk), 0)
    sc = lax.broadcasted_iota(jnp.int32, (hv, hk), 1)
    state_mask = ((sr // GLA_DV) == (sc // GLA_DK)).astype(F32)
    lane_k = lax.broadcasted_iota(jnp.int32, (1, hk), 1) // GLA_DK
    lane_v = lax.broadcasted_iota(jnp.int32, (1, hv), 1) // GLA_DV
    r4 = lax.broadcasted_iota(jnp.int32, (GLA_HEADS * c_len, c_len), 0) % c_len
    c4 = lax.broadcasted_iota(jnp.int32, (GLA_HEADS * c_len, c_len), 1)
    causal4 = r4 >= c4
    er = lax.broadcasted_iota(jnp.int32, (hk, hv), 0) // GLA_DK
    ec = lax.broadcasted_iota(jnp.int32, (hk, hv), 1) // GLA_DV
    head_expand = (er == ec).astype(BF16)
    trow = lax.broadcasted_iota(jnp.int32, (c_len, hv), 0)

    def load_chunk(rows):
        q = q_ref[rows, :].astype(F32) * (GLA_DK ** -0.5)
        k = k_ref[rows, :].astype(F32)
        v = v_ref[rows, :]
        la_hi, la_lo = _split_hi_lo(la_ref[rows, :])
        cum = _dot(tril, la_hi) + _dot(tril, la_lo)
        return q, k, v, cum

    def fast_intra(q, k, v, cum):
        mid = cum[c_len // 2:c_len // 2 + 1, :]
        qa = q * jnp.exp(cum - mid)
        kb = (k * jnp.exp(mid - cum)).astype(BF16)
        qa4 = jnp.concatenate(
            [jnp.where(lane_k == h, qa, 0.0) for h in range(GLA_HEADS)], axis=0).astype(BF16)
        s4 = jnp.where(causal4, _dot_nt(qa4, kb), 0.0)
        pv = _dot(s4.astype(BF16), v)
        acc = jnp.zeros((c_len, hv), F32)
        for h in range(GLA_HEADS):
            acc = acc + jnp.where(lane_v == h, pv[h * c_len:(h + 1) * c_len, :], 0.0)
        return acc

    def slow_intra(q, k, v, cum):
        cum_ref[...] = cum
        kf_ref[...] = k
        vf_ref[...] = v.astype(F32)

        def one_key(s, acc):
            ks = kf_ref[pl.ds(s, 1), :]
            vs = vf_ref[pl.ds(s, 1), :]
            cs = cum_ref[pl.ds(s, 1), :]
            w = q * ks * jnp.exp(jnp.minimum(cum - cs, 0.0))
            wsum = _dot(w.astype(BF16), head_expand)
            return acc + jnp.where(trow >= s, wsum, 0.0) * vs

        return lax.fori_loop(0, c_len, one_key, jnp.zeros((c_len, hv), F32))

    def finish_chunk(rows, q, k, v, cum, intra, st):
        last = cum[c_len - 1:c_len, :]
        qg = (q * jnp.exp(cum)).astype(BF16)
        oacc_ref[rows, :] = intra + _dot_nt(qg, st.astype(BF16))
        kg = (k * jnp.exp(last - cum)).astype(BF16)
        upd = _dot_tn(v, kg)
        return (st * jnp.exp(last) + upd) * state_mask

    @pl.when(fast)
    def _():
        st = state_ref[...]
        for c in range(n_chunks):
            rows = pl.ds(c * c_len, c_len)
            q, k, v, cum = load_chunk(rows)
            st = finish_chunk(rows, q, k, v, cum, fast_intra(q, k, v, cum), st)
        state_ref[...] = st

    @pl.when(jnp.logical_not(fast))
    def _():
        def chunk(c, carry):
            rows = pl.ds(pl.multiple_of(c * c_len, c_len), c_len)
            q, k, v, cum = load_chunk(rows)
            state_ref[...] = finish_chunk(rows, q, k, v, cum, slow_intra(q, k, v, cum), state_ref[...])
            return carry

        lax.fori_loop(0, n_chunks, chunk, 0)

    o = oacc_ref[...]
    vr = lax.broadcasted_iota(jnp.int32, (hv, hv), 0) // GLA_DV
    vc = lax.broadcasted_iota(jnp.int32, (hv, hv), 1) // GLA_DV
    seg = (vr == vc).astype(BF16)
    ms = _seg_sum(o * o, seg) * (1.0 / GLA_DV)
    y = o * lax.rsqrt(ms + HEAD_NORM_EPS) * ng_ref[...]
    g = og_ref[...].astype(F32)
    o_ref[...] = (y * (g * jax.nn.sigmoid(g))).astype(o_ref.dtype)


def _gla(z, w_gate, b_gate, norm_g, batch, seq, tg=1024, fast_max_decay=GLA_FAST_MAX_DECAY):
    tg = min(tg, seq)
    nt = seq // tg
    hk = GLA_HEADS * GLA_DK
    hv = GLA_HEADS * GLA_DV

    def rowmap(cb):
        return lambda b, t: (b * nt + t, cb)

    const = lambda b, t: (0, 0)
    return pl.pallas_call(
        functools.partial(_gla_kernel, n_chunks=tg // GLA_CHUNK, fast_max_decay=fast_max_decay),
        grid=(batch, nt),
        in_specs=[pl.BlockSpec((tg, hk), rowmap(_GQ // hk)),
                  pl.BlockSpec((tg, hk), rowmap(_GK // hk)),
                  pl.BlockSpec((tg, hv), rowmap(_GV // hv)),
                  pl.BlockSpec((tg, hv), rowmap(_GO // hv)),
                  pl.BlockSpec((tg, LANE), rowmap(_LR // LANE)),
                  pl.BlockSpec((LANE, hk), const),
                  pl.BlockSpec((1, hk), const),
                  pl.BlockSpec((1, hv), const)],
        out_specs=pl.BlockSpec((tg, hv), lambda b, t: (b * nt + t, 0)),
        out_shape=jax.ShapeDtypeStruct((batch * seq, hv), BF16),
        scratch_shapes=[pltpu.VMEM((hv, hk), F32),
                        pltpu.VMEM((tg, hk), F32),
                        pltpu.VMEM((GLA_CHUNK, hk), F32),
                        pltpu.VMEM((GLA_CHUNK, hk), F32),
                        pltpu.VMEM((GLA_CHUNK, hv), F32),
                        pltpu.VMEM((tg, hv), F32)],
        compiler_params=_cparams(("arbitrary", "arbitrary")),
        name="gla",
    )(z, z, z, z, z, w_gate, b_gate, norm_g)


LOG2E = math.log2(math.e)


def _diff_kernel(far_ref, lam_ref, q_ref, k_ref, v_ref, bias_ref, ng_ref, o_ref,
                 m_ref, acc_ref, *, blk, lam_init):
    qi = pl.program_id(2)
    d = DIFF_D
    w = 2 * DIFF_D
    q = (q_ref[...].astype(F32) * (DIFF_D ** -0.5 * LOG2E)).astype(BF16)
    lane_q = lax.broadcasted_iota(jnp.int32, q.shape, 1)
    q2 = jnp.concatenate([jnp.where(lane_q < d, q, jnp.zeros_like(q)),
                          jnp.where(lane_q >= d, q, jnp.zeros_like(q))], axis=0)
    m_ref[...] = jnp.full_like(m_ref, -jnp.inf)
    acc_ref[...] = jnp.zeros_like(acc_ref)

    def kv_step(j, bias, shift):
        rows = pl.ds(pl.multiple_of(j * blk, blk), blk)
        vj = v_ref[rows, :]
        s_both = _dot_nt(q2, k_ref[rows, :])
        for m in range(2):
            s = s_both[m * blk:(m + 1) * blk, :]
            if bias is not None:
                s = s + bias
            m_cur = jnp.max(s, axis=-1, keepdims=True)
            if shift is not None:
                m_cur = m_cur + shift
            m_old = m_ref[m]
            m_new = jnp.maximum(m_old, m_cur)
            alpha = jnp.exp2(m_old - m_new)
            sub = m_new if shift is None else m_new - shift
            p = jnp.exp2(s - jnp.concatenate([sub] * (blk // LANE), axis=1))
            row_sum = jnp.broadcast_to(jnp.sum(p, axis=-1, keepdims=True), (blk, LANE))
            acc_ref[m] = (jnp.concatenate([alpha, alpha], axis=1) * acc_ref[m]
                          + jnp.concatenate([_dot(p.astype(BF16), vj), row_sum], axis=1))
            m_ref[m] = m_new

    far = far_ref[pl.program_id(1)]

    def far_step(j, carry):
        kv_step(j, None, far)
        return carry

    lax.fori_loop(0, jnp.maximum(qi - 1, 0), far_step, 0)

    @pl.when(qi >= 1)
    def _():
        kv_step(qi - 1, bias_ref[0, 1], None)

    kv_step(qi, bias_ref[0, 0], None)

    lf = lam_ref[...]
    lam = (jnp.exp(jnp.sum(lf[0:1] * lf[1:2], axis=-1, keepdims=True))
           - jnp.exp(jnp.sum(lf[2:3] * lf[3:4], axis=-1, keepdims=True)) + lam_init)
    a0 = acc_ref[0]
    a1 = acc_ref[1]
    o = a0[:, :w] / a0[:, w:] - lam * (a1[:, :w] / a1[:, w:])
    ms = jnp.mean(o * o, axis=-1, keepdims=True)
    y = o * lax.rsqrt(ms + HEAD_NORM_EPS) * ng_ref[...] * (1.0 - lam_init)
    o_ref[...] = y.astype(o_ref.dtype)


def _t5_bucket(rel):
    n = jnp.maximum(-rel, 0)
    max_exact = T5_BUCKETS // 2
    nf = jnp.maximum(n, 1).astype(F32)
    large = max_exact + (jnp.log(nf / max_exact) / math.log(T5_MAX_DIST / max_exact)
                         * (T5_BUCKETS - max_exact)).astype(jnp.int32)
    large = jnp.minimum(large, T5_BUCKETS - 1)
    return jnp.where(n < max_exact, n, large)


def _bias_tiles(rel_bias, blk):
    assert blk + 1 >= T5_MAX_DIST
    m = jnp.arange(2 * blk)
    col_minus_row = jnp.where(m < blk, m, m - 2 * blk)
    tiles = []
    for delta in range(2):
        rel = col_minus_row - delta * blk
        w = rel_bias[_t5_bucket(rel)].astype(F32).T
        if delta == 0:
            w = jnp.where(rel <= 0, w, -jnp.inf)
        flat = jnp.tile(w, (1, blk))[:, :blk * (2 * blk - 1)]
        tiles.append(flat.reshape(-1, blk, 2 * blk - 1)[:, :, :blk])
    far = rel_bias[_t5_bucket(jnp.full((1,), -(blk + 1)))[0]].astype(F32)
    return jnp.stack(tiles, axis=1) * LOG2E, far * LOG2E


def _diff(z, diff_lambda, norm_g, bias, batch, seq, lam_init, blk):
    blk = min(blk, seq)
    nq = seq // blk
    w = 2 * DIFF_D
    bias_tiles, bias_far = bias
    return pl.pallas_call(
        functools.partial(_diff_kernel, blk=blk, lam_init=lam_init),
        grid=(batch, DIFF_HEADS, nq),
        in_specs=[pl.BlockSpec(memory_space=pltpu.SMEM),
                  pl.BlockSpec((4, DIFF_D), lambda b, h, i: (0, 0)),
                  pl.BlockSpec((blk, w), lambda b, h, i: (b * nq + i, _DQ // w + h)),
                  pl.BlockSpec((seq, w), lambda b, h, i: (b, _DK // w + h)),
                  pl.BlockSpec((seq, w), lambda b, h, i: (b, _DV // w + h)),
                  pl.BlockSpec((1, 2, blk, blk), lambda b, h, i: (h, 0, 0, 0)),
                  pl.BlockSpec((1, w), lambda b, h, i: (0, 0))],
        out_specs=pl.BlockSpec((blk, w), lambda b, h, i: (b * nq + i, h)),
        out_shape=jax.ShapeDtypeStruct((batch * seq, DIFF_HEADS * w), BF16),
        scratch_shapes=[pltpu.VMEM((2, blk, LANE), F32),
                        pltpu.VMEM((2, blk, w + LANE), F32)],
        compiler_params=_cparams(("arbitrary", "arbitrary", "arbitrary")),
        name="diff_attn",
    )(bias_far, diff_lambda, z, z, z, bias_tiles, norm_g)


def _ret_kernel(q_ref, k_ref, v_ref, g_ref, sin_ref, cos_ref, inner_ref, cross_ref, sdec_ref, cdec_ref,
                o_ref, state_ref, *, n_chunks):
    c_len = RET_CHUNK
    hk = RET_HEADS * RET_DK
    hv = RET_HEADS * RET_DV

    @pl.when(pl.program_id(1) == 0)
    def _():
        state_ref[...] = jnp.zeros_like(state_ref)

    even = (lax.broadcasted_iota(jnp.int32, (1, hk), 1) % 2) == 0
    lane_k = lax.broadcasted_iota(jnp.int32, (1, hk), 1) // RET_DK
    lane_v = lax.broadcasted_iota(jnp.int32, (1, hv), 1) // RET_DV
    sr = lax.broadcasted_iota(jnp.int32, (hv, hk), 0) // RET_DV
    sc = lax.broadcasted_iota(jnp.int32, (hv, hk), 1) // RET_DK
    vr = lax.broadcasted_iota(jnp.int32, (hv, hv), 0) // RET_DV
    vc = lax.broadcasted_iota(jnp.int32, (hv, hv), 1) // RET_DV
    seg = (vr == vc).astype(BF16)

    st = state_ref[...]
    for c in range(n_chunks):
        rows = pl.ds(c * c_len, c_len)
        sin = sin_ref[rows, :]
        cos = cos_ref[rows, :]

        def rotary(t):
            nxt = pltpu.roll(t, hk - 1, 1)
            prv = pltpu.roll(t, 1, 1)
            return t * cos + jnp.where(even, -nxt, prv) * sin

        qf = rotary(q_ref[rows, :].astype(F32))
        kf = rotary(k_ref[rows, :].astype(F32)) * (RET_DK ** -0.5)
        v = v_ref[rows, :]

        q4 = jnp.concatenate([jnp.where(lane_k == h, qf, 0.0) for h in range(RET_HEADS)], axis=0).astype(BF16)
        s4 = _dot_nt(q4, kf.astype(BF16)) * inner_ref[...]
        pv = _dot(s4.astype(BF16), v)
        acc = jnp.zeros((c_len, hv), F32)
        for h in range(RET_HEADS):
            acc = acc + jnp.where(lane_v == h, pv[h * c_len:(h + 1) * c_len, :], 0.0)

        o = acc + _dot_nt(qf.astype(BF16), st.astype(BF16)) * cross_ref[...]
        ks = (kf * sdec_ref[...]).astype(BF16)
        st = jnp.where(sr == sc, st * cdec_ref[...] + _dot_tn(v, ks), 0.0)

        ms = _seg_sum(o * o, seg) * (1.0 / RET_DV)
        g = g_ref[rows, :].astype(F32)
        o_ref[rows, :] = (o * lax.rsqrt(ms + HEAD_NORM_EPS) * (g * jax.nn.sigmoid(g))).astype(o_ref.dtype)
    state_ref[...] = st


def _ret_tables(seq):
    c_len = RET_CHUNK
    pos = jnp.arange(seq, dtype=F32)
    angle = 1.0 / (10000.0 ** jnp.linspace(0.0, 1.0, RET_DK // 2, dtype=F32))
    angle = jnp.repeat(angle, 2)
    sin = jnp.tile(jnp.sin(pos[:, None] * angle), (1, RET_HEADS))
    cos = jnp.tile(jnp.cos(pos[:, None] * angle), (1, RET_HEADS))
    log_g = jnp.log1p(-jnp.exp2(-5.0 - jnp.arange(RET_HEADS, dtype=F32)))
    idx = jnp.arange(c_len, dtype=F32)
    rel = idx[:, None] - idx[None, :]
    inner = jnp.where(rel[None] >= 0, jnp.exp(jnp.maximum(rel, 0.0)[None] * log_g[:, None, None]), 0.0)
    inner = inner.reshape(RET_HEADS * c_len, c_len)
    cross = jnp.exp((idx + 1.0)[None] * log_g[:, None])
    cross = jnp.repeat(cross.T, RET_DV, axis=1)
    sdec = jnp.exp((c_len - 1.0 - idx)[None] * log_g[:, None])
    sdec = jnp.repeat(sdec.T, RET_DK, axis=1)
    cdec = jnp.repeat(jnp.exp(c_len * log_g), RET_DK)[None, :]
    return sin, cos, inner, cross, sdec, cdec


def _ret(z, tables, batch, seq, tr=512):
    c_len = RET_CHUNK
    tr = min(tr, seq)
    nt = seq // tr
    hk = RET_HEADS * RET_DK
    hv = RET_HEADS * RET_DV
    sin, cos, inner, cross, sdec, cdec = tables

    def rowmap(cb):
        return lambda b, t: (b * nt + t, cb)

    const = lambda b, t: (0, 0)
    return pl.pallas_call(
        functools.partial(_ret_kernel, n_chunks=tr // c_len),
        grid=(batch, nt),
        in_specs=[pl.BlockSpec((tr, hk), rowmap(_RQ // hk)),
                  pl.BlockSpec((tr, hk), rowmap(_RK // hk)),
                  pl.BlockSpec((tr, hv), rowmap(_RV // hv)),
                  pl.BlockSpec((tr, hv), rowmap(_RG // hv)),
                  pl.BlockSpec((tr, hk), lambda b, t: (t, 0)),
                  pl.BlockSpec((tr, hk), lambda b, t: (t, 0)),
                  pl.BlockSpec((RET_HEADS * c_len, c_len), const),
                  pl.BlockSpec((c_len, hv), const),
                  pl.BlockSpec((c_len, hk), const),
                  pl.BlockSpec((1, hk), const)],
        out_specs=pl.BlockSpec((tr, hv), lambda b, t: (b * nt + t, 0)),
        out_shape=jax.ShapeDtypeStruct((batch * seq, hv), BF16),
        scratch_shapes=[pltpu.VMEM((hv, hk), F32)],
        compiler_params=_cparams(("arbitrary", "arbitrary")),
        name="retention",
    )(z, z, z, z, sin, cos, inner, cross, sdec, cdec)


def _layer_norm(t, g, b):
    mu = jnp.mean(t, axis=-1, keepdims=True)
    var = jnp.mean(jnp.square(t - mu), axis=-1, keepdims=True)
    return (t - mu) * lax.rsqrt(var + LN_EPS) * g + b


def _out_ln_kernel(gla_ref, dif_ref, ret_ref, x_ref, wo_ref, g_ref, b_ref, wr_ref, br_ref, earlier_ref,
                   x1_ref, x1b_ref, x1pa_ref, x1pb_ref, gate_ref, idx_ref, cnt_ref, *, alpha):
    w0 = gla_ref.shape[1]
    w1 = w0 + dif_ref.shape[1]
    a = (_dot(gla_ref[...], wo_ref[0:w0, :]) + _dot(dif_ref[...], wo_ref[w0:w1, :])
         + _dot(ret_ref[...], wo_ref[w1:, :]))
    x1 = _layer_norm(alpha * x_ref[...] + a, g_ref[...], b_ref[...])
    x1_ref[...] = x1
    x1b_ref[...] = x1.astype(BF16)
    x1p = _pack_bf16_pairs(x1)
    x1pa_ref[...] = x1p[:, :SC_ROW_WORDS]
    x1pb_ref[...] = x1p[:, SC_ROW_WORDS:]

    hi, lo = _split_hi_lo(x1)
    tm = x1.shape[0]
    parts = _dot(jnp.concatenate([hi, lo], axis=0), wr_ref[...])
    logits = (parts[:tm, :LANE] + parts[tm:, :LANE]) + (parts[:tm, LANE:] + parts[tm:, LANE:]) + br_ref[...]
    lane = lax.broadcasted_iota(jnp.int32, logits.shape, 1)
    work = jnp.where(lane < N_EXPERTS, logits, -jnp.inf)
    vals = jnp.zeros(logits.shape, F32)
    idxs = jnp.zeros(logits.shape, jnp.int32)
    chosen = jnp.zeros(logits.shape, F32)
    picks = []
    top0 = None
    for r in range(TOP_K):
        mx = jnp.max(work, axis=-1, keepdims=True)
        am = jnp.min(jnp.where(work == mx, lane, LANE), axis=-1, keepdims=True)
        if r == 0:
            top0 = mx
        vals = jnp.where(lane == r, jnp.exp(mx - top0), vals)
        idxs = jnp.where(lane == r, am, idxs)
        chosen = jnp.where(lane == am, 1.0, chosen)
        work = jnp.where(lane == am, -jnp.inf, work)
        picks.append(am)
    gate_ref[...] = vals / jnp.sum(vals, axis=-1, keepdims=True)

    @pl.when(pl.program_id(0) == 0)
    def _():
        cnt_ref[...] = jnp.zeros_like(cnt_ref)

    before = _dot(earlier_ref[...], chosen.astype(BF16)) + cnt_ref[...]
    for r in range(TOP_K):
        rank = jnp.sum(jnp.where(lane == picks[r], before, 0.0), axis=-1, keepdims=True)
        idxs = jnp.where(lane == TOP_K + r, rank.astype(jnp.int32), idxs)
    idx_ref[...] = idxs
    cnt_ref[...] = cnt_ref[...] + jnp.sum(chosen, axis=0, keepdims=True)


def _out_ln(gla_o, dif_o, ret_o, x, w_out, ln_g, ln_b, wr, b_router, alpha, tm=1024):
    n, d = x.shape
    tm = min(tm, n)
    earlier = jnp.asarray(np.tril(np.ones((tm, tm), np.float32), -1), BF16)
    row = lambda i: (i, 0)
    const = lambda i: (0, 0)
    return pl.pallas_call(
        functools.partial(_out_ln_kernel, alpha=alpha),
        grid=(n // tm,),
        in_specs=[pl.BlockSpec((tm, gla_o.shape[1]), row),
                  pl.BlockSpec((tm, dif_o.shape[1]), row),
                  pl.BlockSpec((tm, ret_o.shape[1]), row),
                  pl.BlockSpec((tm, d), row),
                  pl.BlockSpec(w_out.shape, const),
                  pl.BlockSpec((1, d), const),
                  pl.BlockSpec((1, d), const),
                  pl.BlockSpec((d, 2 * LANE), const),
                  pl.BlockSpec((1, LANE), const),
                  pl.BlockSpec((tm, tm), const)],
        out_specs=[pl.BlockSpec((tm, d), row),
                   pl.BlockSpec((tm, d), row),
                   pl.BlockSpec((tm, SC_ROW_WORDS), row),
                   pl.BlockSpec((tm, SC_ROW_WORDS), row),
                   pl.BlockSpec((tm, LANE), row),
                   pl.BlockSpec((tm, LANE), row),
                   pl.BlockSpec((1, LANE), const)],
        out_shape=[jax.ShapeDtypeStruct((n, d), F32),
                   jax.ShapeDtypeStruct((n, d), BF16),
                   jax.ShapeDtypeStruct((n, SC_ROW_WORDS), jnp.uint32),
                   jax.ShapeDtypeStruct((n, SC_ROW_WORDS), jnp.uint32),
                   jax.ShapeDtypeStruct((n, LANE), F32),
                   jax.ShapeDtypeStruct((n, LANE), jnp.int32),
                   jax.ShapeDtypeStruct((1, LANE), F32)],
        compiler_params=_cparams(("arbitrary",)),
        name="out_ln_router",
    )(gla_o, dif_o, ret_o, x, w_out, ln_g, ln_b, wr, b_router, earlier)


GU_GROUP = 2 * LANE


def _deinterleave_perm():
    perm = np.zeros((GU_GROUP, GU_GROUP), np.float32)
    half = np.arange(LANE)
    perm[2 * half, half] = 1.0
    perm[2 * half + 1, LANE + half] = 1.0
    return jnp.asarray(perm, BF16)


def _moe_kernel(blk_e_ref, n_used_ref, n_valid_ref, xa_ref, xb_ref, wgu_ref, bgu_ref, wdn_ref, bdn_ref, perm_ref,
                oa_ref, ob_ref,
                wgu_b_ref, wdn_b_ref, *, rows_per_cast):
    i = pl.program_id(0)
    used = i < n_used_ref[0]
    new_expert = jnp.logical_or(i == 0, blk_e_ref[i] != blk_e_ref[jnp.maximum(i - 1, 0)])

    @pl.when(jnp.logical_and(used, new_expert))
    def _():
        d = wgu_ref.shape[0]
        for r in range(d // rows_per_cast):
            rows = slice(r * rows_per_cast, (r + 1) * rows_per_cast)
            wdn_b_ref[rows, :] = wdn_ref[rows, :].astype(BF16)
            for g in range(wgu_ref.shape[1] // GU_GROUP):
                cols = slice(g * GU_GROUP, (g + 1) * GU_GROUP)
                wgu_b_ref[rows, cols] = _dot(wgu_ref[rows, cols].astype(BF16), perm_ref[...]).astype(BF16)

    @pl.when(used)
    def _():
        x = _unpack_bf16_pairs(jnp.concatenate([xa_ref[...], xb_ref[...]], axis=1))
        row = lax.broadcasted_iota(jnp.int32, x.shape, 0)
        x = jnp.where(row < n_valid_ref[i], x, 0.0).astype(BF16)
        gu = _dot(x, wgu_b_ref[...]) + bgu_ref[...]
        acts = []
        for grp in range(gu.shape[1] // GU_GROUP):
            g = jnp.minimum(gu[:, grp * GU_GROUP:grp * GU_GROUP + LANE], SWIGLU_LIMIT)
            u = jnp.clip(gu[:, grp * GU_GROUP + LANE:(grp + 1) * GU_GROUP], -SWIGLU_LIMIT, SWIGLU_LIMIT)
            acts.append(((u + 1.0) * (g * jax.nn.sigmoid(g * SWIGLU_ALPHA))).astype(BF16))
        act = jnp.concatenate(acts, axis=1)
        y = _pack_bf16_pairs(_dot(act, wdn_b_ref[...]) + bdn_ref[...])
        oa_ref[...] = y[:, :SC_ROW_WORDS]
        ob_ref[...] = y[:, SC_ROW_WORDS:]

    @pl.when(jnp.logical_not(used))
    def _():
        oa_ref[...] = jnp.zeros_like(oa_ref)
        ob_ref[...] = jnp.zeros_like(ob_ref)


def _moe(xs, blk_e, n_used, n_valid, w_gu, b_gu, w_dn, b_dn, blk):
    rows = xs[0].shape[0]
    assert w_gu.shape[1] == 4 * SC_ROW_WORDS
    d = w_gu.shape[1]
    f2 = w_gu.shape[2]
    f = w_dn.shape[1]
    assert f == d
    grid_spec = pltpu.PrefetchScalarGridSpec(
        num_scalar_prefetch=3,
        grid=(rows // blk,),
        in_specs=[pl.BlockSpec((blk, SC_ROW_WORDS), lambda i, be, nu, nv: (i, 0)),
                  pl.BlockSpec((blk, SC_ROW_WORDS), lambda i, be, nu, nv: (i, 0)),
                  pl.BlockSpec((None, d, f2), lambda i, be, nu, nv: (be[i], 0, 0)),
                  pl.BlockSpec((None, 1, f2), lambda i, be, nu, nv: (be[i], 0, 0)),
                  pl.BlockSpec((None, f, d), lambda i, be, nu, nv: (be[i], 0, 0)),
                  pl.BlockSpec((None, 1, d), lambda i, be, nu, nv: (be[i], 0, 0)),
                  pl.BlockSpec((GU_GROUP, GU_GROUP), lambda i, be, nu, nv: (0, 0))],
        out_specs=[pl.BlockSpec((blk, SC_ROW_WORDS), lambda i, be, nu, nv: (i, 0)),
                   pl.BlockSpec((blk, SC_ROW_WORDS), lambda i, be, nu, nv: (i, 0))],
        scratch_shapes=[pltpu.VMEM((d, f2), BF16), pltpu.VMEM((f, d), BF16)],
    )
    return pl.pallas_call(
        functools.partial(_moe_kernel, rows_per_cast=256),
        grid_spec=grid_spec,
        out_shape=[jax.ShapeDtypeStruct((rows, SC_ROW_WORDS), jnp.uint32)] * 2,
        compiler_params=_cparams(("arbitrary",)),
        name="moe_experts",
    )(blk_e, n_used, n_valid, xs[0], xs[1], w_gu, b_gu, w_dn, b_dn, _deinterleave_perm())


def _combine_kernel(yga_ref, ygb_ref, gate_ref, x1_ref, x1b_ref, p_ref, wg_ref, bg_ref, wp_ref, g_ref, b_ref,
                    x2_ref, x2b_ref, *, alpha):
    gates = gate_ref[...]
    m = jnp.zeros(x1_ref.shape, F32)
    for r in range(TOP_K):
        m = m + _unpack_bf16_pairs(jnp.concatenate([yga_ref[r], ygb_ref[r]], axis=1)) * gates[:, r:r + 1]
    e = (jax.nn.sigmoid(_dot(x1b_ref[...], wg_ref[...]) + bg_ref[...])
         * _dot(p_ref[...].astype(BF16), wp_ref[...]))
    x2 = _layer_norm(alpha * x1_ref[...] + m + e, g_ref[...], b_ref[...])
    x2_ref[...] = x2
    x2b_ref[...] = x2.astype(BF16)


def _combine(yg, gates, x1, x1b, p, layer, w_pg, b_pg, w_pp, ln_g, ln_b, alpha, tm=512):
    n, d = x1.shape
    tm = min(tm, n)
    p_block0 = layer * (n // tm)
    row = lambda i: (i, 0)
    const = lambda i: (0, 0)
    return pl.pallas_call(
        functools.partial(_combine_kernel, alpha=alpha),
        grid=(n // tm,),
        in_specs=[pl.BlockSpec((TOP_K, tm, SC_ROW_WORDS), lambda i: (0, i, 0)),
                  pl.BlockSpec((TOP_K, tm, SC_ROW_WORDS), lambda i: (0, i, 0)),
                  pl.BlockSpec((tm, LANE), row),
                  pl.BlockSpec((tm, d), row),
                  pl.BlockSpec((tm, d), row),
                  pl.BlockSpec((tm, p.shape[1]), lambda i: (p_block0 + i, 0)),
                  pl.BlockSpec(w_pg.shape, const),
                  pl.BlockSpec((1, d), const),
                  pl.BlockSpec(w_pp.shape, const),
                  pl.BlockSpec((1, d), const),
                  pl.BlockSpec((1, d), const)],
        out_specs=[pl.BlockSpec((tm, d), row), pl.BlockSpec((tm, d), row)],
        out_shape=[jax.ShapeDtypeStruct((n, d), F32), jax.ShapeDtypeStruct((n, d), BF16)],
        compiler_params=_cparams(("arbitrary",)),
        name="combine_ln",
    )(yg[0], yg[1], gates, x1, x1b, p, w_pg, b_pg, w_pp, ln_g, ln_b)


def _route(experts, ranks, counts, blk):
    n_tok = experts.shape[0]
    n_assign = n_tok * TOP_K
    sizes = counts.astype(jnp.int32)
    padded = ((sizes + blk - 1) // blk) * blk
    pends = jnp.cumsum(padded)
    pstarts = pends - padded
    onehot = experts[:, :, None] == jnp.arange(N_EXPERTS, dtype=jnp.int32)
    slot = jnp.sum(jnp.where(onehot, pstarts, 0), axis=-1) + ranks
    buf_len = ((n_assign + N_EXPERTS * (blk - 1) + blk - 1) // blk) * blk
    n_blk = buf_len // blk
    first_row = jnp.arange(n_blk, dtype=jnp.int32) * blk
    blk_e = jnp.minimum(jnp.sum(pends[None, :] <= first_row[:, None], axis=1), N_EXPERTS - 1).astype(jnp.int32)
    n_used = (pends[-1] // blk).astype(jnp.int32).reshape(1)
    mine = blk_e[:, None] == jnp.arange(N_EXPERTS, dtype=jnp.int32)[None, :]
    group_rows_left = jnp.sum(jnp.where(mine, pstarts + sizes, 0), axis=1) - first_row
    n_valid = jnp.clip(group_rows_left, 0, blk).astype(jnp.int32)
    return slot, buf_len, blk_e, n_used, n_valid


def _regroup_w_in(w_in):
    depth, d, width = w_in.shape
    lr0 = _GV + GLA_HEADS * GLA_DV
    src = np.full((Z_WIDTH,), -1, np.int32)
    src[:lr0] = np.arange(lr0)
    src[lr0:width - GLA_GATE_RANK] = np.arange(lr0 + GLA_GATE_RANK, width)
    src[_LR:_LR + GLA_GATE_RANK] = np.arange(lr0, lr0 + GLA_GATE_RANK)
    select = (jnp.arange(width, dtype=jnp.int32)[:, None] == jnp.asarray(src)[None, :]).astype(BF16)
    out = _matmul(w_in.reshape(depth * d, width).astype(BF16), select, d, Z_WIDTH // 3, BF16, "regroup_w_in")
    return out.reshape(depth, d, Z_WIDTH)


def kernel(x, p, w_in, w_gla_gate, b_gla_gate, gla_norm_g, diff_lambda, diff_norm_g, w_out, rel_bias, ln1_g, ln1_b, w_router, b_router, w_gate_up, b_gate_up, w_down, b_down, w_ple_gate, b_ple_gate, w_ple_proj, ln2_g, ln2_b):
    batch, seq, d = x.shape
    depth = w_in.shape[0]
    n = batch * seq
    alpha = (2 * depth) ** 0.25
    moe_blk = 512
    diff_blk = min(512, seq)

    w_in_b = _regroup_w_in(w_in)
    wg = jnp.zeros((depth, LANE, GLA_HEADS * GLA_DK), F32).at[:, :GLA_GATE_RANK, :].set(w_gla_gate).astype(BF16)
    gla_g = jnp.tile(gla_norm_g, (1, GLA_HEADS))[:, None, :]
    w_out_b = w_out.astype(BF16)
    wr = jnp.zeros((depth, d, LANE), F32).at[:, :, :N_EXPERTS].set(w_router)
    wr_hi = wr.astype(BF16)
    wr_b = jnp.concatenate([wr_hi, (wr - wr_hi.astype(F32)).astype(BF16)], axis=-1)
    br = jnp.zeros((depth, 1, LANE), F32).at[:, 0, :N_EXPERTS].set(b_router)
    w_gu = w_gate_up.reshape((depth * N_EXPERTS,) + w_gate_up.shape[2:])
    b_gu = b_gate_up.reshape(depth * N_EXPERTS, -1, LANE, 2).swapaxes(-1, -2).reshape(depth * N_EXPERTS, 1, -1)
    w_dn = w_down.reshape((depth * N_EXPERTS,) + w_down.shape[2:])
    b_dn = b_down.reshape(depth * N_EXPERTS, 1, -1)
    w_pg_b = w_ple_gate.astype(BF16)
    w_pp_b = w_ple_proj.astype(BF16)
    bias_tiles = _bias_tiles(rel_bias, diff_blk)
    ret_tables = _ret_tables(seq)

    xf = x.reshape(n, d)
    xb = xf.astype(BF16)
    pf = p.reshape(depth * n, p.shape[-1])
    for i in range(depth):
        lam_init = 0.8 - 0.6 * math.exp(-0.3 * i)
        z = _matmul(xb, w_in_b[i], min(2048, n), Z_WIDTH // 3, BF16, "in_proj")
        gla_o = _gla(z, wg[i], b_gla_gate[i][None, :], gla_g[i], batch, seq)
        dif_o = _diff(z, diff_lambda[i], diff_norm_g[i][None, :], bias_tiles, batch, seq, lam_init, diff_blk)
        ret_o = _ret(z, ret_tables, batch, seq)
        x1, x1b, x1pa, x1pb, gates, idx, cnt = _out_ln(gla_o, dif_o, ret_o, xf, w_out_b[i], ln1_g[i][None, :],
                                                       ln1_b[i][None, :], wr_b[i], br[i], alpha)
        slot, buf_len, blk_e, n_used, n_valid = _route(idx[:, :TOP_K], idx[:, TOP_K:2 * TOP_K], cnt[0, :N_EXPERTS], moe_blk)
        slot_k = slot.T.reshape(-1)
        xs = [_sc_scatter(t, slot_k, buf_len) for t in (x1pa, x1pb)]
        yb = _moe(xs, blk_e + i * N_EXPERTS, n_used, n_valid, w_gu, b_gu, w_dn, b_dn, moe_blk)
        yg = [t.reshape(TOP_K, n, SC_ROW_WORDS) for t in _gather_rows(yb, slot_k)]
        xf, xb = _combine(yg, gates, x1, x1b, pf, i, w_pg_b[i], b_ple_gate[i][None, :], w_pp_b[i],
                          ln2_g[i][None, :], ln2_b[i][None, :], alpha)
    return xf.reshape(batch, seq, d)
```

```python
import functools
import math

import numpy as np
import jax
import jax.numpy as jnp
from jax import lax
from jax.experimental import pallas as pl
from jax.experimental.pallas import tpu as pltpu
from jax.experimental.pallas import tpu_sc as plsc

F32 = jnp.float32
BF16 = jnp.bfloat16

GLA_HEADS = 4
GLA_DK = 32
GLA_DV = 64
GLA_GATE_RANK = 16
GLA_TAU = 16.0
GLA_CHUNK = 64
DIFF_HEADS = 4
DIFF_D = 64
RET_HEADS = 4
RET_DK = 64
RET_DV = 64
RET_CHUNK = 128
T5_BUCKETS = 32
T5_MAX_DIST = 128
N_EXPERTS = 32
TOP_K = 4
SWIGLU_LIMIT = 7.0
SWIGLU_ALPHA = 1.702
LN_EPS = 1e-5
HEAD_NORM_EPS = 1e-5

LANE = 128
VMEM_LIMIT = 56 * 1024 * 1024

_GQ, _GK, _GV, _GO = 0, 128, 256, 512
_DQ, _DK, _DV = 768, 1280, 1792
_RQ, _RK, _RV, _RG = 2304, 2560, 2816, 3072
_LR = 3328
Z_WIDTH = 3456

GLA_FAST_MAX_DECAY = 60.0


def _cparams(sem):
    return pltpu.CompilerParams(dimension_semantics=sem, vmem_limit_bytes=VMEM_LIMIT)


def _split_hi_lo(a):
    hi = a.astype(BF16)
    lo = (a - hi.astype(F32)).astype(BF16)
    return hi, lo


def _dot(a, b):
    return jnp.dot(a, b, preferred_element_type=F32)


def _dot_nt(a, b):
    return lax.dot_general(a, b, (((1,), (1,)), ((), ())), preferred_element_type=F32)


def _dot_tn(a, b):
    return lax.dot_general(a, b, (((0,), (0,)), ((), ())), preferred_element_type=F32)


def _seg_sum(a, ind):
    hi, lo = _split_hi_lo(a)
    return _dot(hi, ind) + _dot(lo, ind)


def _pack_bf16_pairs(t):
    h = t.shape[1] // 2
    bits = lax.bitcast_convert_type(t.astype(BF16).astype(F32), jnp.uint32)
    return (bits[:, :h] >> 16) | (bits[:, h:] & jnp.uint32(0xFFFF0000))


def _unpack_bf16_pairs(packed):
    lo = lax.bitcast_convert_type(packed << 16, F32)
    hi = lax.bitcast_convert_type(packed & jnp.uint32(0xFFFF0000), F32)
    return jnp.concatenate([lo, hi], axis=1)


SC_WINDOW = 128
SC_ROW_WORDS = 256


def _sc_gather(table, indices):
    m = indices.shape[0]
    words = table.shape[1]
    assert words == SC_ROW_WORDS and m % SC_WINDOW == 0
    mesh = plsc.VectorSubcoreMesh(core_axis_name="core", subcore_axis_name="subcore")

    def body(x_hbm, i_hbm, o_hbm):
        def step(i_vmem, o_vmem):
            pltpu.sync_copy(x_hbm.at[i_vmem.at[0]], o_vmem)

        pltpu.emit_pipeline(
            step,
            grid=(m // SC_WINDOW,),
            in_specs=[pl.BlockSpec((1, SC_WINDOW), index_map=lambda i: (0, i))],
            out_specs=[pl.BlockSpec((SC_WINDOW, words), index_map=lambda i: (i, 0))],
            core_axis_name=("core", "subcore"),
            dimension_semantics=(pltpu.PARALLEL,),
        )(i_hbm, o_hbm)

    return pl.kernel(body, out_type=jax.ShapeDtypeStruct((m, words), table.dtype), mesh=mesh,
                     scratch_types=[])(table, indices.reshape(1, m))


def _sc_scatter(table, indices, out_rows):
    r, words = table.shape
    m = indices.shape[0]
    assert words == SC_ROW_WORDS and r % SC_WINDOW == 0 and m % r == 0
    mesh = plsc.VectorSubcoreMesh(core_axis_name="core", subcore_axis_name="subcore")
    steps_per_pass = r // SC_WINDOW

    def body(x_hbm, i_hbm, o_hbm):
        def step(x_vmem, i_vmem):
            pltpu.sync_copy(x_vmem, o_hbm.at[i_vmem.at[0]])

        pltpu.emit_pipeline(
            step,
            grid=(m // SC_WINDOW,),
            in_specs=[pl.BlockSpec((SC_WINDOW, words), index_map=lambda i: (i % steps_per_pass, 0)),
                      pl.BlockSpec((1, SC_WINDOW), index_map=lambda i: (0, i))],
            out_specs=[],
            core_axis_name=("core", "subcore"),
            dimension_semantics=(pltpu.PARALLEL,),
        )(x_hbm, i_hbm)

    return pl.kernel(body, out_type=jax.ShapeDtypeStruct((out_rows, words), table.dtype), mesh=mesh,
                     scratch_types=[])(table, indices.reshape(1, m))


def _gather_rows(tables, rows):
    return [_sc_gather(t, rows) for t in tables]


def _mm_kernel(x_ref, w_ref, o_ref):
    o_ref[...] = _dot(x_ref[...], w_ref[...]).astype(o_ref.dtype)


def _matmul(x, w, tm, tn, out_dtype, name):
    m, k = x.shape
    n = w.shape[1]
    return pl.pallas_call(
        _mm_kernel,
        grid=(n // tn, m // tm),
        in_specs=[pl.BlockSpec((tm, k), lambda j, i: (i, 0)),
                  pl.BlockSpec((k, tn), lambda j, i: (0, j))],
        out_specs=pl.BlockSpec((tm, tn), lambda j, i: (i, j)),
        out_shape=jax.ShapeDtypeStruct((m, n), out_dtype),
        compiler_params=_cparams(("arbitrary", "arbitrary")),
        name=name,
    )(x, w)


def _gla_kernel(q_ref, k_ref, v_ref, og_ref, lr_ref, wg_ref, bg_ref, ng_ref, o_ref,
                state_ref, la_ref, cum_ref, kf_ref, vf_ref, oacc_ref, *, n_chunks, fast_max_decay):
    c_len = GLA_CHUNK
    hk = GLA_HEADS * GLA_DK
    hv = GLA_HEADS * GLA_DV

    @pl.when(pl.program_id(1) == 0)
    def _():
        state_ref[...] = jnp.zeros_like(state_ref)

    zg = _dot(lr_ref[...], wg_ref[...]) + bg_ref[...]
    la = (jnp.minimum(zg, 0.0) - jnp.log1p(jnp.exp(-jnp.abs(zg)))) / GLA_TAU
    la_ref[...] = la
    chunk_tot = jnp.sum(la.reshape(n_chunks, c_len, hk), axis=1)
    fast = jnp.min(chunk_tot) > -fast_max_decay

    row = lax.broadcasted_iota(jnp.int32, (c_len, c_len), 0)
    col = lax.broadcasted_iota(jnp.int32, (c_len, c_len), 1)
    tril = (row >= col).astype(BF16)
    sr = lax.broadcasted_iota(jnp.int32, (hv, hk), 0)
    sc = lax.broadcasted_iota(jnp.int32, (hv, hk), 1)
    state_mask = ((sr // GLA_DV) == (sc // GLA_DK)).astype(F32)
    lane_k = lax.broadcasted_iota(jnp.int32, (1, hk), 1) // GLA_DK
    lane_v = lax.broadcasted_iota(jnp.int32, (1, hv), 1) // GLA_DV
    r4 = lax.broadcasted_iota(jnp.int32, (GLA_HEADS * c_len, c_len), 0) % c_len
    c4 = lax.broadcasted_iota(jnp.int32, (GLA_HEADS * c_len, c_len), 1)
    causal4 = r4 >= c4
    er = lax.broadcasted_iota(jnp.int32, (hk, hv), 0) // GLA_DK
    ec = lax.broadcasted_iota(jnp.int32, (hk, hv), 1) // GLA_DV
    head_expand = (er == ec).astype(BF16)
    trow = lax.broadcasted_iota(jnp.int32, (c_len, hv), 0)

    def load_chunk(rows):
        q = q_ref[rows, :].astype(F32) * (GLA_DK ** -0.5)
        k = k_ref[rows, :].astype(F32)
        v = v_ref[rows, :]
        la_hi, la_lo = _split_hi_lo(la_ref[rows, :])
        cum = _dot(tril, la_hi) + _dot(tril, la_lo)
        return q, k, v, cum

    def fast_intra(q, k, v, cum):
        mid = cum[c_len // 2:c_len // 2 + 1, :]
        qa = q * jnp.exp(cum - mid)
        kb = (k * jnp.exp(mid - cum)).astype(BF16)
        qa4 = jnp.concatenate(
            [jnp.where(lane_k == h, qa, 0.0) for h in range(GLA_HEADS)], axis=0).astype(BF16)
        s4 = jnp.where(causal4, _dot_nt(qa4, kb), 0.0)
        pv = _dot(s4.astype(BF16), v)
        acc = jnp.zeros((c_len, hv), F32)
        for h in range(GLA_HEADS):
            acc = acc + jnp.where(lane_v == h, pv[h * c_len:(h + 1) * c_len, :], 0.0)
        return acc

    def slow_intra(q, k, v, cum):
        cum_ref[...] = cum
        kf_ref[...] = k
        vf_ref[...] = v.astype(F32)

        def one_key(s, acc):
            ks = kf_ref[pl.ds(s, 1), :]
            vs = vf_ref[pl.ds(s, 1), :]
            cs = cum_ref[pl.ds(s, 1), :]
            w = q * ks * jnp.exp(jnp.minimum(cum - cs, 0.0))
            wsum = _dot(w.astype(BF16), head_expand)
            return acc + jnp.where(trow >= s, wsum, 0.0) * vs

        return lax.fori_loop(0, c_len, one_key, jnp.zeros((c_len, hv), F32))

    def finish_chunk(rows, q, k, v, cum, intra, st):
        last = cum[c_len - 1:c_len, :]
        qg = (q * jnp.exp(cum)).astype(BF16)
        oacc_ref[rows, :] = intra + _dot_nt(qg, st.astype(BF16))
        kg = (k * jnp.exp(last - cum)).astype(BF16)
        upd = _dot_tn(v, kg)
        return (st * jnp.exp(last) + upd) * state_mask

    @pl.when(fast)
    def _():
        st = state_ref[...]
        for c in range(n_chunks):
            rows = pl.ds(c * c_len, c_len)
            q, k, v, cum = load_chunk(rows)
            st = finish_chunk(rows, q, k, v, cum, fast_intra(q, k, v, cum), st)
        state_ref[...] = st

    @pl.when(jnp.logical_not(fast))
    def _():
        def chunk(c, carry):
            rows = pl.ds(pl.multiple_of(c * c_len, c_len), c_len)
            q, k, v, cum = load_chunk(rows)
            state_ref[...] = finish_chunk(rows, q, k, v, cum, slow_intra(q, k, v, cum), state_ref[...])
            return carry

        lax.fori_loop(0, n_chunks, chunk, 0)

    o = oacc_ref[...]
    vr = lax.broadcasted_iota(jnp.int32, (hv, hv), 0) // GLA_DV
    vc = lax.broadcasted_iota(jnp.int32, (hv, hv), 1) // GLA_DV
    seg = (vr == vc).astype(BF16)
    ms = _seg_sum(o * o, seg) * (1.0 / GLA_DV)
    y = o * lax.rsqrt(ms + HEAD_NORM_EPS) * ng_ref[...]
    g = og_ref[...].astype(F32)
    o_ref[...] = (y * (g * jax.nn.sigmoid(g))).astype(o_ref.dtype)


def _gla(z, w_gate, b_gate, norm_g, batch, seq, tg=1024, fast_max_decay=GLA_FAST_MAX_DECAY):
    tg = min(tg, seq)
    nt = seq // tg
    hk = GLA_HEADS * GLA_DK
    hv = GLA_HEADS * GLA_DV

    def rowmap(cb):
        return lambda b, t: (b * nt + t, cb)

    const = lambda b, t: (0, 0)
    return pl.pallas_call(
        functools.partial(_gla_kernel, n_chunks=tg // GLA_CHUNK, fast_max_decay=fast_max_decay),
        grid=(batch, nt),
        in_specs=[pl.BlockSpec((tg, hk), rowmap(_GQ // hk)),
                  pl.BlockSpec((tg, hk), rowmap(_GK // hk)),
                  pl.BlockSpec((tg, hv), rowmap(_GV // hv)),
                  pl.BlockSpec((tg, hv), rowmap(_GO // hv)),
                  pl.BlockSpec((tg, LANE), rowmap(_LR // LANE)),
                  pl.BlockSpec((LANE, hk), const),
                  pl.BlockSpec((1, hk), const),
                  pl.BlockSpec((1, hv), const)],
        out_specs=pl.BlockSpec((tg, hv), lambda b, t: (b * nt + t, 0)),
        out_shape=jax.ShapeDtypeStruct((batch * seq, hv), BF16),
        scratch_shapes=[pltpu.VMEM((hv, hk), F32),
                        pltpu.VMEM((tg, hk), F32),
                        pltpu.VMEM((GLA_CHUNK, hk), F32),
                        pltpu.VMEM((GLA_CHUNK, hk), F32),
                        pltpu.VMEM((GLA_CHUNK, hv), F32),
                        pltpu.VMEM((tg, hv), F32)],
        compiler_params=_cparams(("arbitrary", "arbitrary")),
        name="gla",
    )(z, z, z, z, z, w_gate, b_gate, norm_g)


LOG2E = math.log2(math.e)


def _diff_kernel(far_ref, lam_ref, q_ref, k_ref, v_ref, bias_ref, ng_ref, o_ref,
                 m_ref, acc_ref, *, blk, lam_init):
    qi = pl.program_id(2)
    d = DIFF_D
    w = 2 * DIFF_D
    q = (q_ref[...].astype(F32) * (DIFF_D ** -0.5 * LOG2E)).astype(BF16)
    lane_q = lax.broadcasted_iota(jnp.int32, q.shape, 1)
    q2 = jnp.concatenate([jnp.where(lane_q < d, q, jnp.zeros_like(q)),
                          jnp.where(lane_q >= d, q, jnp.zeros_like(q))], axis=0)
    m_ref[...] = jnp.full_like(m_ref, -jnp.inf)
    acc_ref[...] = jnp.zeros_like(acc_ref)

    def kv_step(j, bias, shift):
        rows = pl.ds(pl.multiple_of(j * blk, blk), blk)
        vj = v_ref[rows, :]
        s_both = _dot_nt(q2, k_ref[rows, :])
        for m in range(2):
            s = s_both[m * blk:(m + 1) * blk, :]
            if bias is not None:
                s = s + bias
            m_cur = jnp.max(s, axis=-1, keepdims=True)
            if shift is not None:
                m_cur = m_cur + shift
            m_old = m_ref[m]
            m_new = jnp.maximum(m_old, m_cur)
            alpha = jnp.exp2(m_old - m_new)
            sub = m_new if shift is None else m_new - shift
            p = jnp.exp2(s - jnp.concatenate([sub] * (blk // LANE), axis=1))
            row_sum = jnp.broadcast_to(jnp.sum(p, axis=-1, keepdims=True), (blk, LANE))
            acc_ref[m] = (jnp.concatenate([alpha, alpha], axis=1) * acc_ref[m]
                          + jnp.concatenate([_dot(p.astype(BF16), vj), row_sum], axis=1))
            m_ref[m] = m_new

    far = far_ref[pl.program_id(1)]

    def far_step(j, carry):
        kv_step(j, None, far)
        return carry

    lax.fori_loop(0, jnp.maximum(qi - 1, 0), far_step, 0)

    @pl.when(qi >= 1)
    def _():
        kv_step(qi - 1, bias_ref[0, 1], None)

    kv_step(qi, bias_ref[0, 0], None)

    lf = lam_ref[...]
    lam = (jnp.exp(jnp.sum(lf[0:1] * lf[1:2], axis=-1, keepdims=True))
           - jnp.exp(jnp.sum(lf[2:3] * lf[3:4], axis=-1, keepdims=True)) + lam_init)
    a0 = acc_ref[0]
    a1 = acc_ref[1]
    o = a0[:, :w] / a0[:, w:] - lam * (a1[:, :w] / a1[:, w:])
    ms = jnp.mean(o * o, axis=-1, keepdims=True)
    y = o * lax.rsqrt(ms + HEAD_NORM_EPS) * ng_ref[...] * (1.0 - lam_init)
    o_ref[...] = y.astype(o_ref.dtype)


def _t5_bucket(rel):
    n = jnp.maximum(-rel, 0)
    max_exact = T5_BUCKETS // 2
    nf = jnp.maximum(n, 1).astype(F32)
    large = max_exact + (jnp.log(nf / max_exact) / math.log(T5_MAX_DIST / max_exact)
                         * (T5_BUCKETS - max_exact)).astype(jnp.int32)
    large = jnp.minimum(large, T5_BUCKETS - 1)
    return jnp.where(n < max_exact, n, large)


def _bias_tiles(rel_bias, blk):
    assert blk + 1 >= T5_MAX_DIST
    m = jnp.arange(2 * blk)
    col_minus_row = jnp.where(m < blk, m, m - 2 * blk)
    tiles = []
    for delta in range(2):
        rel = col_minus_row - delta * blk
        w = rel_bias[_t5_bucket(rel)].astype(F32).T
        if delta == 0:
            w = jnp.where(rel <= 0, w, -jnp.inf)
        flat = jnp.tile(w, (1, blk))[:, :blk * (2 * blk - 1)]
        tiles.append(flat.reshape(-1, blk, 2 * blk - 1)[:, :, :blk])
    far = rel_bias[_t5_bucket(jnp.full((1,), -(blk + 1)))[0]].astype(F32)
    return jnp.stack(tiles, axis=1) * LOG2E, far * LOG2E


def _diff(z, diff_lambda, norm_g, bias, batch, seq, lam_init, blk):
    blk = min(blk, seq)
    nq = seq // blk
    w = 2 * DIFF_D
    bias_tiles, bias_far = bias
    return pl.pallas_call(
        functools.partial(_diff_kernel, blk=blk, lam_init=lam_init),
        grid=(batch, DIFF_HEADS, nq),
        in_specs=[pl.BlockSpec(memory_space=pltpu.SMEM),
                  pl.BlockSpec((4, DIFF_D), lambda b, h, i: (0, 0)),
                  pl.BlockSpec((blk, w), lambda b, h, i: (b * nq + i, _DQ // w + h)),
                  pl.BlockSpec((seq, w), lambda b, h, i: (b, _DK // w + h)),
                  pl.BlockSpec((seq, w), lambda b, h, i: (b, _DV // w + h)),
                  pl.BlockSpec((1, 2, blk, blk), lambda b, h, i: (h, 0, 0, 0)),
                  pl.BlockSpec((1, w), lambda b, h, i: (0, 0))],
        out_specs=pl.BlockSpec((blk, w), lambda b, h, i: (b * nq + i, h)),
        out_shape=jax.ShapeDtypeStruct((batch * seq, DIFF_HEADS * w), BF16),
        scratch_shapes=[pltpu.VMEM((2, blk, LANE), F32),
                        pltpu.VMEM((2, blk, w + LANE), F32)],
        compiler_params=_cparams(("arbitrary", "arbitrary", "arbitrary")),
        name="diff_attn",
    )(bias_far, diff_lambda, z, z, z, bias_tiles, norm_g)


def _ret_kernel(q_ref, k_ref, v_ref, g_ref, sin_ref, cos_ref, inner_ref, cross_ref, sdec_ref, cdec_ref,
                o_ref, state_ref, *, n_chunks):
    c_len = RET_CHUNK
    hk = RET_HEADS * RET_DK
    hv = RET_HEADS * RET_DV

    @pl.when(pl.program_id(1) == 0)
    def _():
        state_ref[...] = jnp.zeros_like(state_ref)

    even = (lax.broadcasted_iota(jnp.int32, (1, hk), 1) % 2) == 0
    lane_k = lax.broadcasted_iota(jnp.int32, (1, hk), 1) // RET_DK
    lane_v = lax.broadcasted_iota(jnp.int32, (1, hv), 1) // RET_DV
    sr = lax.broadcasted_iota(jnp.int32, (hv, hk), 0) // RET_DV
    sc = lax.broadcasted_iota(jnp.int32, (hv, hk), 1) // RET_DK
    vr = lax.broadcasted_iota(jnp.int32, (hv, hv), 0) // RET_DV
    vc = lax.broadcasted_iota(jnp.int32, (hv, hv), 1) // RET_DV
    seg = (vr == vc).astype(BF16)

    st = state_ref[...]
    for c in range(n_chunks):
        rows = pl.ds(c * c_len, c_len)
        sin = sin_ref[rows, :]
        cos = cos_ref[rows, :]

        def rotary(t):
            nxt = pltpu.roll(t, hk - 1, 1)
            prv = pltpu.roll(t, 1, 1)
            return t * cos + jnp.where(even, -nxt, prv) * sin

        qf = rotary(q_ref[rows, :].astype(F32))
        kf = rotary(k_ref[rows, :].astype(F32)) * (RET_DK ** -0.5)
        v = v_ref[rows, :]

        q4 = jnp.concatenate([jnp.where(lane_k == h, qf, 0.0) for h in range(RET_HEADS)], axis=0).astype(BF16)
        s4 = _dot_nt(q4, kf.astype(BF16)) * inner_ref[...]
        pv = _dot(s4.astype(BF16), v)
        acc = jnp.zeros((c_len, hv), F32)
        for h in range(RET_HEADS):
            acc = acc + jnp.where(lane_v == h, pv[h * c_len:(h + 1) * c_len, :], 0.0)

        o = acc + _dot_nt(qf.astype(BF16), st.astype(BF16)) * cross_ref[...]
        ks = (kf * sdec_ref[...]).astype(BF16)
        st = jnp.where(sr == sc, st * cdec_ref[...] + _dot_tn(v, ks), 0.0)

        ms = _seg_sum(o * o, seg) * (1.0 / RET_DV)
        g = g_ref[rows, :].astype(F32)
        o_ref[rows, :] = (o * lax.rsqrt(ms + HEAD_NORM_EPS) * (g * jax.nn.sigmoid(g))).astype(o_ref.dtype)
    state_ref[...] = st


def _ret_tables(seq):
    c_len = RET_CHUNK
    pos = jnp.arange(seq, dtype=F32)
    angle = 1.0 / (10000.0 ** jnp.linspace(0.0, 1.0, RET_DK // 2, dtype=F32))
    angle = jnp.repeat(angle, 2)
    sin = jnp.tile(jnp.sin(pos[:, None] * angle), (1, RET_HEADS))
    cos = jnp.tile(jnp.cos(pos[:, None] * angle), (1, RET_HEADS))
    log_g = jnp.log1p(-jnp.exp2(-5.0 - jnp.arange(RET_HEADS, dtype=F32)))
    idx = jnp.arange(c_len, dtype=F32)
    rel = idx[:, None] - idx[None, :]
    inner = jnp.where(rel[None] >= 0, jnp.exp(jnp.maximum(rel, 0.0)[None] * log_g[:, None, None]), 0.0)
    inner = inner.reshape(RET_HEADS * c_len, c_len)
    cross = jnp.exp((idx + 1.0)[None] * log_g[:, None])
    cross = jnp.repeat(cross.T, RET_DV, axis=1)
    sdec = jnp.exp((c_len - 1.0 - idx)[None] * log_g[:, None])
    sdec = jnp.repeat(sdec.T, RET_DK, axis=1)
    cdec = jnp.repeat(jnp.exp(c_len * log_g), RET_DK)[None, :]
    return sin, cos, inner, cross, sdec, cdec


def _ret(z, tables, batch, seq, tr=1024):
    c_len = RET_CHUNK
    tr = min(tr, seq)
    nt = seq // tr
    hk = RET_HEADS * RET_DK
    hv = RET_HEADS * RET_DV
    sin, cos, inner, cross, sdec, cdec = tables

    def rowmap(cb):
        return lambda b, t: (b * nt + t, cb)

    const = lambda b, t: (0, 0)
    return pl.pallas_call(
        functools.partial(_ret_kernel, n_chunks=tr // c_len),
        grid=(batch, nt),
        in_specs=[pl.BlockSpec((tr, hk), rowmap(_RQ // hk)),
                  pl.BlockSpec((tr, hk), rowmap(_RK // hk)),
                  pl.BlockSpec((tr, hv), rowmap(_RV // hv)),
                  pl.BlockSpec((tr, hv), rowmap(_RG // hv)),
                  pl.BlockSpec((tr, hk), lambda b, t: (t, 0)),
                  pl.BlockSpec((tr, hk), lambda b, t: (t, 0)),
                  pl.BlockSpec((RET_HEADS * c_len, c_len), const),
                  pl.BlockSpec((c_len, hv), const),
                  pl.BlockSpec((c_len, hk), const),
                  pl.BlockSpec((1, hk), const)],
        out_specs=pl.BlockSpec((tr, hv), lambda b, t: (b * nt + t, 0)),
        out_shape=jax.ShapeDtypeStruct((batch * seq, hv), BF16),
        scratch_shapes=[pltpu.VMEM((hv, hk), F32)],
        compiler_params=_cparams(("arbitrary", "arbitrary")),
        name="retention",
    )(z, z, z, z, sin, cos, inner, cross, sdec, cdec)


def _layer_norm(t, g, b):
    mu = jnp.mean(t, axis=-1, keepdims=True)
    var = jnp.mean(jnp.square(t - mu), axis=-1, keepdims=True)
    return (t - mu) * lax.rsqrt(var + LN_EPS) * g + b


def _out_ln_kernel(gla_ref, dif_ref, ret_ref, x_ref, wo_ref, g_ref, b_ref, wr_ref, br_ref, earlier_ref,
                   x1_ref, x1b_ref, x1pa_ref, x1pb_ref, gate_ref, idx_ref, cnt_ref, *, alpha):
    w0 = gla_ref.shape[1]
    w1 = w0 + dif_ref.shape[1]
    a = (_dot(gla_ref[...], wo_ref[0:w0, :]) + _dot(dif_ref[...], wo_ref[w0:w1, :])
         + _dot(ret_ref[...], wo_ref[w1:, :]))
    x1 = _layer_norm(alpha * x_ref[...] + a, g_ref[...], b_ref[...])
    x1_ref[...] = x1
    x1b_ref[...] = x1.astype(BF16)
    x1p = _pack_bf16_pairs(x1)
    x1pa_ref[...] = x1p[:, :SC_ROW_WORDS]
    x1pb_ref[...] = x1p[:, SC_ROW_WORDS:]

    hi, lo = _split_hi_lo(x1)
    tm = x1.shape[0]
    parts = _dot(jnp.concatenate([hi, lo], axis=0), wr_ref[...])
    logits = (parts[:tm, :LANE] + parts[tm:, :LANE]) + (parts[:tm, LANE:] + parts[tm:, LANE:]) + br_ref[...]
    lane = lax.broadcasted_iota(jnp.int32, logits.shape, 1)
    work = jnp.where(lane < N_EXPERTS, logits, -jnp.inf)
    vals = jnp.zeros(logits.shape, F32)
    idxs = jnp.zeros(logits.shape, jnp.int32)
    chosen = jnp.zeros(logits.shape, F32)
    picks = []
    top0 = None
    for r in range(TOP_K):
        mx = jnp.max(work, axis=-1, keepdims=True)
        am = jnp.min(jnp.where(work == mx, lane, LANE), axis=-1, keepdims=True)
        if r == 0:
            top0 = mx
        vals = jnp.where(lane == r, jnp.exp(mx - top0), vals)
        idxs = jnp.where(lane == r, am, idxs)
        chosen = jnp.where(lane == am, 1.0, chosen)
        work = jnp.where(lane == am, -jnp.inf, work)
        picks.append(am)
    gate_ref[...] = vals / jnp.sum(vals, axis=-1, keepdims=True)

    @pl.when(pl.program_id(0) == 0)
    def _():
        cnt_ref[...] = jnp.zeros_like(cnt_ref)

    before = _dot(earlier_ref[...], chosen.astype(BF16)) + cnt_ref[...]
    for r in range(TOP_K):
        rank = jnp.sum(jnp.where(lane == picks[r], before, 0.0), axis=-1, keepdims=True)
        idxs = jnp.where(lane == TOP_K + r, rank.astype(jnp.int32), idxs)
    idx_ref[...] = idxs
    cnt_ref[...] = cnt_ref[...] + jnp.sum(chosen, axis=0, keepdims=True)


def _out_ln(gla_o, dif_o, ret_o, x, w_out, ln_g, ln_b, wr, b_router, alpha, tm=1024):
    n, d = x.shape
    tm = min(tm, n)
    earlier = jnp.asarray(np.tril(np.ones((tm, tm), np.float32), -1), BF16)
    row = lambda i: (i, 0)
    const = lambda i: (0, 0)
    return pl.pallas_call(
        functools.partial(_out_ln_kernel, alpha=alpha),
        grid=(n // tm,),
        in_specs=[pl.BlockSpec((tm, gla_o.shape[1]), row),
                  pl.BlockSpec((tm, dif_o.shape[1]), row),
                  pl.BlockSpec((tm, ret_o.shape[1]), row),
                  pl.BlockSpec((tm, d), row),
                  pl.BlockSpec(w_out.shape, const),
                  pl.BlockSpec((1, d), const),
                  pl.BlockSpec((1, d), const),
                  pl.BlockSpec((d, 2 * LANE), const),
                  pl.BlockSpec((1, LANE), const),
                  pl.BlockSpec((tm, tm), const)],
        out_specs=[pl.BlockSpec((tm, d), row),
                   pl.BlockSpec((tm, d), row),
                   pl.BlockSpec((tm, SC_ROW_WORDS), row),
                   pl.BlockSpec((tm, SC_ROW_WORDS), row),
                   pl.BlockSpec((tm, LANE), row),
                   pl.BlockSpec((tm, LANE), row),
                   pl.BlockSpec((1, LANE), const)],
        out_shape=[jax.ShapeDtypeStruct((n, d), F32),
                   jax.ShapeDtypeStruct((n, d), BF16),
                   jax.ShapeDtypeStruct((n, SC_ROW_WORDS), jnp.uint32),
                   jax.ShapeDtypeStruct((n, SC_ROW_WORDS), jnp.uint32),
                   jax.ShapeDtypeStruct((n, LANE), F32),
                   jax.ShapeDtypeStruct((n, LANE), jnp.int32),
                   jax.ShapeDtypeStruct((1, LANE), F32)],
        compiler_params=_cparams(("arbitrary",)),
        name="out_ln_router",
    )(gla_o, dif_o, ret_o, x, w_out, ln_g, ln_b, wr, b_router, earlier)


GU_GROUP = 2 * LANE


def _deinterleave_perm():
    perm = np.zeros((GU_GROUP, GU_GROUP), np.float32)
    half = np.arange(LANE)
    perm[2 * half, half] = 1.0
    perm[2 * half + 1, LANE + half] = 1.0
    return jnp.asarray(perm, BF16)


def _moe_kernel(blk_e_ref, n_used_ref, n_valid_ref, xa_ref, xb_ref, wgu_ref, bgu_ref, wdn_ref, bdn_ref, perm_ref,
                oa_ref, ob_ref,
                wgu_b_ref, wdn_b_ref, *, rows_per_cast):
    i = pl.program_id(0)
    used = i < n_used_ref[0]
    new_expert = jnp.logical_or(i == 0, blk_e_ref[i] != blk_e_ref[jnp.maximum(i - 1, 0)])

    @pl.when(jnp.logical_and(used, new_expert))
    def _():
        d = wgu_ref.shape[0]
        for r in range(d // rows_per_cast):
            rows = slice(r * rows_per_cast, (r + 1) * rows_per_cast)
            wdn_b_ref[rows, :] = wdn_ref[rows, :].astype(BF16)
            for g in range(wgu_ref.shape[1] // GU_GROUP):
                cols = slice(g * GU_GROUP, (g + 1) * GU_GROUP)
                wgu_b_ref[rows, cols] = _dot(wgu_ref[rows, cols].astype(BF16), perm_ref[...]).astype(BF16)

    @pl.when(used)
    def _():
        x = _unpack_bf16_pairs(jnp.concatenate([xa_ref[...], xb_ref[...]], axis=1))
        row = lax.broadcasted_iota(jnp.int32, x.shape, 0)
        x = jnp.where(row < n_valid_ref[i], x, 0.0).astype(BF16)
        gu = _dot(x, wgu_b_ref[...]) + bgu_ref[...]
        acts = []
        for grp in range(gu.shape[1] // GU_GROUP):
            g = jnp.minimum(gu[:, grp * GU_GROUP:grp * GU_GROUP + LANE], SWIGLU_LIMIT)
            u = jnp.clip(gu[:, grp * GU_GROUP + LANE:(grp + 1) * GU_GROUP], -SWIGLU_LIMIT, SWIGLU_LIMIT)
            acts.append(((u + 1.0) * (g * jax.nn.sigmoid(g * SWIGLU_ALPHA))).astype(BF16))
        act = jnp.concatenate(acts, axis=1)
        y = _pack_bf16_pairs(_dot(act, wdn_b_ref[...]) + bdn_ref[...])
        oa_ref[...] = y[:, :SC_ROW_WORDS]
        ob_ref[...] = y[:, SC_ROW_WORDS:]

    @pl.when(jnp.logical_not(used))
    def _():
        oa_ref[...] = jnp.zeros_like(oa_ref)
        ob_ref[...] = jnp.zeros_like(ob_ref)


def _moe(xs, blk_e, n_used, n_valid, w_gu, b_gu, w_dn, b_dn, blk):
    rows = xs[0].shape[0]
    assert w_gu.shape[1] == 4 * SC_ROW_WORDS
    d = w_gu.shape[1]
    f2 = w_gu.shape[2]
    f = w_dn.shape[1]
    assert f == d
    grid_spec = pltpu.PrefetchScalarGridSpec(
        num_scalar_prefetch=3,
        grid=(rows // blk,),
        in_specs=[pl.BlockSpec((blk, SC_ROW_WORDS), lambda i, be, nu, nv: (i, 0)),
                  pl.BlockSpec((blk, SC_ROW_WORDS), lambda i, be, nu, nv: (i, 0)),
                  pl.BlockSpec((None, d, f2), lambda i, be, nu, nv: (be[i], 0, 0)),
                  pl.BlockSpec((None, 1, f2), lambda i, be, nu, nv: (be[i], 0, 0)),
                  pl.BlockSpec((None, f, d), lambda i, be, nu, nv: (be[i], 0, 0)),
                  pl.BlockSpec((None, 1, d), lambda i, be, nu, nv: (be[i], 0, 0)),
                  pl.BlockSpec((GU_GROUP, GU_GROUP), lambda i, be, nu, nv: (0, 0))],
        out_specs=[pl.BlockSpec((blk, SC_ROW_WORDS), lambda i, be, nu, nv: (i, 0)),
                   pl.BlockSpec((blk, SC_ROW_WORDS), lambda i, be, nu, nv: (i, 0))],
        scratch_shapes=[pltpu.VMEM((d, f2), BF16), pltpu.VMEM((f, d), BF16)],
    )
    return pl.pallas_call(
        functools.partial(_moe_kernel, rows_per_cast=256),
        grid_spec=grid_spec,
        out_shape=[jax.ShapeDtypeStruct((rows, SC_ROW_WORDS), jnp.uint32)] * 2,
        compiler_params=_cparams(("arbitrary",)),
        name="moe_experts",
    )(blk_e, n_used, n_valid, xs[0], xs[1], w_gu, b_gu, w_dn, b_dn, _deinterleave_perm())


def _combine_kernel(yga_ref, ygb_ref, gate_ref, x1_ref, x1b_ref, p_ref, wg_ref, bg_ref, wp_ref, g_ref, b_ref,
                    x2_ref, x2b_ref, *, alpha):
    gates = gate_ref[...]
    m = jnp.zeros(x1_ref.shape, F32)
    for r in range(TOP_K):
        m = m + _unpack_bf16_pairs(jnp.concatenate([yga_ref[r], ygb_ref[r]], axis=1)) * gates[:, r:r + 1]
    e = (jax.nn.sigmoid(_dot(x1b_ref[...], wg_ref[...]) + bg_ref[...])
         * _dot(p_ref[...].astype(BF16), wp_ref[...]))
    x2 = _layer_norm(alpha * x1_ref[...] + m + e, g_ref[...], b_ref[...])
    x2_ref[...] = x2
    x2b_ref[...] = x2.astype(BF16)


def _combine(yg, gates, x1, x1b, p, layer, w_pg, b_pg, w_pp, ln_g, ln_b, alpha, tm=512):
    n, d = x1.shape
    tm = min(tm, n)
    p_block0 = layer * (n // tm)
    row = lambda i: (i, 0)
    const = lambda i: (0, 0)
    return pl.pallas_call(
        functools.partial(_combine_kernel, alpha=alpha),
        grid=(n // tm,),
        in_specs=[pl.BlockSpec((TOP_K, tm, SC_ROW_WORDS), lambda i: (0, i, 0)),
                  pl.BlockSpec((TOP_K, tm, SC_ROW_WORDS), lambda i: (0, i, 0)),
                  pl.BlockSpec((tm, LANE), row),
                  pl.BlockSpec((tm, d), row),
                  pl.BlockSpec((tm, d), row),
                  pl.BlockSpec((tm, p.shape[1]), lambda i: (p_block0 + i, 0)),
                  pl.BlockSpec(w_pg.shape, const),
                  pl.BlockSpec((1, d), const),
                  pl.BlockSpec(w_pp.shape, const),
                  pl.BlockSpec((1, d), const),
                  pl.BlockSpec((1, d), const)],
        out_specs=[pl.BlockSpec((tm, d), row), pl.BlockSpec((tm, d), row)],
        out_shape=[jax.ShapeDtypeStruct((n, d), F32), jax.ShapeDtypeStruct((n, d), BF16)],
        compiler_params=_cparams(("arbitrary",)),
        name="combine_ln",
    )(yg[0], yg[1], gates, x1, x1b, p, w_pg, b_pg, w_pp, ln_g, ln_b)


def _route(experts, ranks, counts, blk):
    n_tok = experts.shape[0]
    n_assign = n_tok * TOP_K
    sizes = counts.astype(jnp.int32)
    padded = ((sizes + blk - 1) // blk) * blk
    pends = jnp.cumsum(padded)
    pstarts = pends - padded
    onehot = experts[:, :, None] == jnp.arange(N_EXPERTS, dtype=jnp.int32)
    slot = jnp.sum(jnp.where(onehot, pstarts, 0), axis=-1) + ranks
    buf_len = ((n_assign + N_EXPERTS * (blk - 1) + blk - 1) // blk) * blk
    n_blk = buf_len // blk
    first_row = jnp.arange(n_blk, dtype=jnp.int32) * blk
    blk_e = jnp.minimum(jnp.sum(pends[None, :] <= first_row[:, None], axis=1), N_EXPERTS - 1).astype(jnp.int32)
    n_used = (pends[-1] // blk).astype(jnp.int32).reshape(1)
    mine = blk_e[:, None] == jnp.arange(N_EXPERTS, dtype=jnp.int32)[None, :]
    group_rows_left = jnp.sum(jnp.where(mine, pstarts + sizes, 0), axis=1) - first_row
    n_valid = jnp.clip(group_rows_left, 0, blk).astype(jnp.int32)
    return slot, buf_len, blk_e, n_used, n_valid


def _regroup_w_in(w_in):
    depth, d, width = w_in.shape
    lr0 = _GV + GLA_HEADS * GLA_DV
    src = np.full((Z_WIDTH,), -1, np.int32)
    src[:lr0] = np.arange(lr0)
    src[lr0:width - GLA_GATE_RANK] = np.arange(lr0 + GLA_GATE_RANK, width)
    src[_LR:_LR + GLA_GATE_RANK] = np.arange(lr0, lr0 + GLA_GATE_RANK)
    select = (jnp.arange(width, dtype=jnp.int32)[:, None] == jnp.asarray(src)[None, :]).astype(BF16)
    out = _matmul(w_in.reshape(depth * d, width).astype(BF16), select, d, Z_WIDTH // 3, BF16, "regroup_w_in")
    return out.reshape(depth, d, Z_WIDTH)


def kernel(x, p, w_in, w_gla_gate, b_gla_gate, gla_norm_g, diff_lambda, diff_norm_g, w_out, rel_bias, ln1_g, ln1_b, w_router, b_router, w_gate_up, b_gate_up, w_down, b_down, w_ple_gate, b_ple_gate, w_ple_proj, ln2_g, ln2_b):
    batch, seq, d = x.shape
    depth = w_in.shape[0]
    n = batch * seq
    alpha = (2 * depth) ** 0.25
    moe_blk = 512
    diff_blk = min(512, seq)

    w_in_b = _regroup_w_in(w_in)
    wg = jnp.zeros((depth, LANE, GLA_HEADS * GLA_DK), F32).at[:, :GLA_GATE_RANK, :].set(w_gla_gate).astype(BF16)
    gla_g = jnp.tile(gla_norm_g, (1, GLA_HEADS))[:, None, :]
    w_out_b = w_out.astype(BF16)
    wr = jnp.zeros((depth, d, LANE), F32).at[:, :, :N_EXPERTS].set(w_router)
    wr_hi = wr.astype(BF16)
    wr_b = jnp.concatenate([wr_hi, (wr - wr_hi.astype(F32)).astype(BF16)], axis=-1)
    br = jnp.zeros((depth, 1, LANE), F32).at[:, 0, :N_EXPERTS].set(b_router)
    w_gu = w_gate_up.reshape((depth * N_EXPERTS,) + w_gate_up.shape[2:])
    b_gu = b_gate_up.reshape(depth * N_EXPERTS, -1, LANE, 2).swapaxes(-1, -2).reshape(depth * N_EXPERTS, 1, -1)
    w_dn = w_down.reshape((depth * N_EXPERTS,) + w_down.shape[2:])
    b_dn = b_down.reshape(depth * N_EXPERTS, 1, -1)
    w_pg_b = w_ple_gate.astype(BF16)
    w_pp_b = w_ple_proj.astype(BF16)
    bias_tiles = _bias_tiles(rel_bias, diff_blk)
    ret_tables = _ret_tables(seq)

    xf = x.reshape(n, d)
    xb = xf.astype(BF16)
    pf = p.reshape(depth * n, p.shape[-1])
    for i in range(depth):
        lam_init = 0.8 - 0.6 * math.exp(-0.3 * i)
        z = _matmul(xb, w_in_b[i], min(2048, n), Z_WIDTH // 3, BF16, "in_proj")
        gla_o = _gla(z, wg[i], b_gla_gate[i][None, :], gla_g[i], batch, seq)
        dif_o = _diff(z, diff_lambda[i], diff_norm_g[i][None, :], bias_tiles, batch, seq, lam_init, diff_blk)
        ret_o = _ret(z, ret_tables, batch, seq)
        x1, x1b, x1pa, x1pb, gates, idx, cnt = _out_ln(gla_o, dif_o, ret_o, xf, w_out_b[i], ln1_g[i][None, :],
                                                       ln1_b[i][None, :], wr_b[i], br[i], alpha)
        slot, buf_len, blk_e, n_used, n_valid = _route(idx[:, :TOP_K], idx[:, TOP_K:2 * TOP_K], cnt[0, :N_EXPERTS], moe_blk)
        slot_k = slot.T.reshape(-1)
        xs = [_sc_scatter(t, slot_k, buf_len) for t in (x1pa, x1pb)]
        yb = _moe(xs, blk_e + i * N_EXPERTS, n_used, n_valid, w_gu, b_gu, w_dn, b_dn, moe_blk)
        yg = [t.reshape(TOP_K, n, SC_ROW_WORDS) for t in _gather_rows(yb, slot_k)]
        xf, xb = _combine(yg, gates, x1, x1b, pf, i, w_pg_b[i], b_ple_gate[i][None, :], w_pp_b[i],
                          ln2_g[i][None, :], ln2_b[i][None, :], alpha)
    return xf.reshape(batch, seq, d)
```
